```python
import jax, jax.numpy as jnp
from jax import lax
import numpy as np

D_MODEL = 2048
BATCH = 1
SEQ = 8192
DEPTH = 2

CONV_CH = 1024
CONV_GROUPS = 16
CONV_WIDTH = 31
MLA_HEADS = 8
QK_NOPE = 128
QK_ROPE = 64
V_HEAD = 128
Q_RANK = 512
KV_RANK = 256
ROPE_THETA = 10000.0
Q_BLOCK = 128
EVEN_IN = 2 * CONV_CH + Q_RANK + KV_RANK + QK_ROPE
EVEN_MIX = CONV_CH + MLA_HEADS * V_HEAD
RNN_WIDTH = 2048
RNN_HEADS = 16
RNN_HEAD_DIM = RNN_WIDTH // RNN_HEADS
RNN_CONV_WIDTH = 4
RG_C = 8.0
N_GROUPS = 8
EXPERTS_PER_GROUP = 8
N_EXPERTS = N_GROUPS * EXPERTS_PER_GROUP
TOP_K = 2
D_EXPERT = 512
MOE_BLOCK = 256
PLE_DIM = 256
ALPHA = (2 * DEPTH) ** 0.25
BETA = (8 * DEPTH) ** -0.25
N_EVEN = (DEPTH + 1) // 2
N_ODD = DEPTH // 2
LN_EPS = 1e-5
RMS_EPS = 1e-6

kernel_name = 'hybrid_conv_mla_rglru_hmoe_deepnorm'


def layer_norm(x, g, b):
    xf = x.astype(jnp.float32)
    mu = xf.mean(-1, keepdims=True)
    var = jnp.square(xf - mu).mean(-1, keepdims=True)
    return ((xf - mu) * lax.rsqrt(var + LN_EPS) * g + b).astype(x.dtype)


def rms_norm(x, g):
    xf = x.astype(jnp.float32)
    return (xf * lax.rsqrt(jnp.square(xf).mean(-1, keepdims=True) + RMS_EPS) * g).astype(x.dtype)


def rope(x, cos, sin):
    half = x.shape[-1] // 2
    x1, x2 = x[..., :half], x[..., half:]
    return jnp.concatenate([x1 * cos - x2 * sin, x2 * cos + x1 * sin], axis=-1)


def causal_depthwise_conv(u, w, b):
    width = w.shape[0]
    y = lax.conv_general_dilated(u, w[:, None, :], window_strides=(1,), padding=[(width - 1, 0)],
                                 dimension_numbers=('NWC', 'WIO', 'NWC'),
                                 feature_group_count=u.shape[-1])
    return y + b


def mla_attention(c_q, c_kv, k_r, positions, qnorm_g, w_uq, kvnorm_g, w_ukv):
    B, S, _ = c_q.shape
    dt = c_q.dtype
    q = (rms_norm(c_q, qnorm_g) @ w_uq).reshape(B, S, MLA_HEADS, QK_NOPE + QK_ROPE)
    kv = (rms_norm(c_kv, kvnorm_g) @ w_ukv).reshape(B, S, MLA_HEADS, QK_NOPE + V_HEAD)
    inv_freq = 1.0 / (ROPE_THETA ** (jnp.arange(0, QK_ROPE, 2, dtype=jnp.float32) / QK_ROPE))
    ang = positions.astype(jnp.float32)[..., None] * inv_freq
    cos, sin = jnp.cos(ang).astype(dt), jnp.sin(ang).astype(dt)
    q_nope = q[..., :QK_NOPE]
    q_rope = rope(q[..., QK_NOPE:], cos[:, :, None], sin[:, :, None])
    k_nope, v = kv[..., :QK_NOPE], kv[..., QK_NOPE:]
    k_rope = rope(k_r, cos, sin)
    scale = (QK_NOPE + QK_ROPE) ** -0.5
    key_idx = jnp.arange(S)

    def attend(start):
        qn = lax.dynamic_slice_in_dim(q_nope, start, Q_BLOCK, axis=1)
        qr = lax.dynamic_slice_in_dim(q_rope, start, Q_BLOCK, axis=1)
        s = jnp.einsum('bqhd,bkhd->bhqk', qn, k_nope) + jnp.einsum('bqhd,bkd->bhqk', qr, k_rope)
        s = s.astype(jnp.float32) * scale
        mask = (start + jnp.arange(Q_BLOCK))[:, None] >= key_idx[None, :]
        w = jax.nn.softmax(jnp.where(mask, s, -jnp.inf), axis=-1).astype(v.dtype)
        return jnp.einsum('bhqk,bkhd->bqhd', w, v)

    out = lax.map(attend, jnp.arange(S // Q_BLOCK) * Q_BLOCK)
    return jnp.moveaxis(out, 0, 1).reshape(B, S, MLA_HEADS * V_HEAD)


def even_mixer(x, positions, w_in, conv_w, conv_b, cnorm_g, cnorm_b, qnorm_g, w_uq, kvnorm_g, w_ukv, w_out):
    B, S, _ = x.shape
    h = x @ w_in
    cuts = [CONV_CH, 2 * CONV_CH, 2 * CONV_CH + Q_RANK, 2 * CONV_CH + Q_RANK + KV_RANK]
    conv_val, conv_gate, c_q, c_kv, k_r = jnp.split(h, cuts, axis=-1)
    u = conv_val * jax.nn.sigmoid(conv_gate)
    u = causal_depthwise_conv(u, conv_w, conv_b)
    gsz = CONV_CH // CONV_GROUPS
    u = layer_norm(u.reshape(B, S, CONV_GROUPS, gsz), cnorm_g.reshape(CONV_GROUPS, gsz),
                   cnorm_b.reshape(CONV_GROUPS, gsz))
    u = jax.nn.silu(u.reshape(B, S, CONV_CH))
    a = mla_attention(c_q, c_kv, k_r, positions, qnorm_g, w_uq, kvnorm_g, w_ukv)
    return jnp.concatenate([u, a], axis=-1) @ w_out


def _linear_combine(left, right):
    a_l, b_l = left
    a_r, b_r = right
    return a_l * a_r, a_r * b_l + b_r


def odd_mixer(x, w_in, conv_w, conv_b, w_a, b_a, w_i, b_i, lam, w_out):
    B, S, _ = x.shape
    h = x @ w_in
    gate, xr = jnp.split(h, [RNN_WIDTH], axis=-1)
    xr = causal_depthwise_conv(xr, conv_w, conv_b)
    xb = xr.reshape(B, S, RNN_HEADS, RNN_HEAD_DIM)
    r = jax.nn.sigmoid((jnp.einsum('bshi,hij->bshj', xb, w_a).reshape(B, S, RNN_WIDTH) + b_a).astype(jnp.float32))
    i = jax.nn.sigmoid((jnp.einsum('bshi,hij->bshj', xb, w_i).reshape(B, S, RNN_WIDTH) + b_i).astype(jnp.float32))
    log_a = -RG_C * r * jax.nn.softplus(-lam.astype(jnp.float32))
    a = jnp.exp(log_a)
    bval = jnp.sqrt(-jnp.expm1(2.0 * log_a)) * (i * xr.astype(jnp.float32))
    _, hs = lax.associative_scan(_linear_combine, (a, bval), axis=1)
    y = jax.nn.gelu(gate) * hs.astype(x.dtype)
    return y @ w_out


def hierarchical_moe(x, w_grp, b_grp, w_exp, b_exp, w1, w3, w2):
    B, S, D = x.shape
    T = B * S
    A = T * TOP_K
    xf = x.reshape(T, D)
    grp_prob = jax.nn.softmax((xf @ w_grp + b_grp).astype(jnp.float32), axis=-1)
    g_prob, g_idx = lax.top_k(grp_prob, 1)
    e_logits = (xf @ w_exp + b_exp).astype(jnp.float32).reshape(T, N_GROUPS, EXPERTS_PER_GROUP)
    e_logits = jnp.take_along_axis(e_logits, g_idx[:, :, None], axis=1)[:, 0]
    top_p, top_i = lax.top_k(jax.nn.softmax(e_logits, axis=-1), TOP_K)
    gates = g_prob * top_p / top_p.sum(-1, keepdims=True)
    expert_ids = (g_idx * EXPERTS_PER_GROUP + top_i).reshape(A)
    token_ids = jnp.arange(A, dtype=jnp.int32) // TOP_K
    order = jnp.argsort(expert_ids)
    sorted_e = expert_ids[order]
    counts = jnp.bincount(expert_ids, length=N_EXPERTS).astype(jnp.int32)
    starts = jnp.cumsum(counts) - counts
    pcounts = (counts + MOE_BLOCK - 1) // MOE_BLOCK * MOE_BLOCK
    pends = jnp.cumsum(pcounts)
    pstarts = pends - pcounts
    dest = pstarts[sorted_e] + jnp.arange(A, dtype=jnp.int32) - starts[sorted_e]
    n_blocks = -(-A // MOE_BLOCK) + N_EXPERTS
    P = n_blocks * MOE_BLOCK
    row_token = jnp.full((P,), T, jnp.int32).at[dest].set(token_ids[order])
    row_gate = jnp.zeros((P,), jnp.float32).at[dest].set(gates.reshape(A)[order])
    block_expert = jnp.minimum(
        jnp.searchsorted(pends, jnp.arange(n_blocks, dtype=jnp.int32) * MOE_BLOCK, side='right'),
        N_EXPERTS - 1)
    x_pad = jnp.concatenate([xf, jnp.zeros((1, D), xf.dtype)], axis=0)
    xs = x_pad[row_token].reshape(n_blocks, MOE_BLOCK, D)

    def expert_block(args):
        xb, e = args
        hmid = jax.nn.silu(xb @ w1[e]) * (xb @ w3[e])
        return hmid @ w2[e]

    ys = lax.map(expert_block, (xs, block_expert)).reshape(P, D)
    out = jnp.zeros((T + 1, D), x.dtype).at[row_token].add(ys * row_gate[:, None].astype(x.dtype))
    return out[:T].reshape(B, S, D)


def setup_inputs(seed: int = 0) -> dict:
    key = jax.random.key(seed)
    ks = iter(jax.random.split(key, 64))

    def nrm(shape, scale):
        return jax.random.normal(next(ks), shape, jnp.float32) * scale

    def gain(shape):
        return 1.0 + nrm(shape, 0.01)

    positions = (jax.random.randint(next(ks), (BATCH, 1), 0, 1024, jnp.int32)
                 + jnp.arange(SEQ, dtype=jnp.int32)[None, :])
    a0 = jax.random.uniform(next(ks), (N_ODD, RNN_WIDTH), jnp.float32, 0.9, 0.999)
    s0 = a0 ** (1.0 / RG_C)
    lam = jnp.log(s0) - jnp.log1p(-s0)
    return {
        'x': nrm((BATCH, SEQ, D_MODEL), 1.0),
        'p': nrm((DEPTH, BATCH, SEQ, PLE_DIM), 1.0),
        'positions': positions,
        'ev_w_in': nrm((N_EVEN, D_MODEL, EVEN_IN), D_MODEL ** -0.5),
        'ev_conv_w': nrm((N_EVEN, CONV_WIDTH, CONV_CH), CONV_WIDTH ** -0.5),
        'ev_conv_b': nrm((N_EVEN, CONV_CH), 0.01),
        'ev_cnorm_g': gain((N_EVEN, CONV_CH)),
        'ev_cnorm_b': nrm((N_EVEN, CONV_CH), 0.01),
        'ev_qnorm_g': gain((N_EVEN, Q_RANK)),
        'ev_w_uq': nrm((N_EVEN, Q_RANK, MLA_HEADS * (QK_NOPE + QK_ROPE)), Q_RANK ** -0.5),
        'ev_kvnorm_g': gain((N_EVEN, KV_RANK)),
        'ev_w_ukv': nrm((N_EVEN, KV_RANK, MLA_HEADS * (QK_NOPE + V_HEAD)), KV_RANK ** -0.5),
        'ev_w_out': nrm((N_EVEN, EVEN_MIX, D_MODEL), BETA * EVEN_MIX ** -0.5),
        'od_w_in': nrm((N_ODD, D_MODEL, 2 * RNN_WIDTH), D_MODEL ** -0.5),
        'od_conv_w': nrm((N_ODD, RNN_CONV_WIDTH, RNN_WIDTH), RNN_CONV_WIDTH ** -0.5),
        'od_conv_b': nrm((N_ODD, RNN_WIDTH), 0.01),
        'od_w_a': nrm((N_ODD, RNN_HEADS, RNN_HEAD_DIM, RNN_HEAD_DIM), RNN_HEAD_DIM ** -0.5),
        'od_b_a': nrm((N_ODD, RNN_WIDTH), 0.01),
        'od_w_i': nrm((N_ODD, RNN_HEADS, RNN_HEAD_DIM, RNN_HEAD_DIM), RNN_HEAD_DIM ** -0.5),
        'od_b_i': nrm((N_ODD, RNN_WIDTH), 0.01),
        'od_lam': lam,
        'od_w_out': nrm((N_ODD, RNN_WIDTH, D_MODEL), BETA * RNN_WIDTH ** -0.5),
        'ln_mix_g': gain((DEPTH, D_MODEL)),
        'ln_mix_b': nrm((DEPTH, D_MODEL), 0.01),
        'ln_ffn_g': gain((DEPTH, D_MODEL)),
        'ln_ffn_b': nrm((DEPTH, D_MODEL), 0.01),
        'moe_w_grp': nrm((DEPTH, D_MODEL, N_GROUPS), D_MODEL ** -0.5),
        'moe_b_grp': nrm((DEPTH, N_GROUPS), 0.01),
        'moe_w_exp': nrm((DEPTH, D_MODEL, N_EXPERTS), D_MODEL ** -0.5),
        'moe_b_exp': nrm((DEPTH, N_EXPERTS), 0.01),
        'moe_w1': nrm((DEPTH, N_EXPERTS, D_MODEL, D_EXPERT), D_MODEL ** -0.5),
        'moe_w3': nrm((DEPTH, N_EXPERTS, D_MODEL, D_EXPERT), D_MODEL ** -0.5),
        'moe_w2': nrm((DEPTH, N_EXPERTS, D_EXPERT, D_MODEL), BETA * D_EXPERT ** -0.5),
        'ple_w_proj': nrm((DEPTH, PLE_DIM, D_MODEL), PLE_DIM ** -0.5),
        'ple_w_gate': nrm((DEPTH, D_MODEL, D_MODEL), D_MODEL ** -0.5),
        'ple_b_gate': nrm((DEPTH, D_MODEL), 0.01),
    }


def reference(x, p, positions, ev_w_in, ev_conv_w, ev_conv_b, ev_cnorm_g, ev_cnorm_b, ev_qnorm_g,
              ev_w_uq, ev_kvnorm_g, ev_w_ukv, ev_w_out, od_w_in, od_conv_w, od_conv_b, od_w_a, od_b_a,
              od_w_i, od_b_i, od_lam, od_w_out, ln_mix_g, ln_mix_b, ln_ffn_g, ln_ffn_b, moe_w_grp,
              moe_b_grp, moe_w_exp, moe_b_exp, moe_w1, moe_w3, moe_w2, ple_w_proj, ple_w_gate,
              ple_b_gate):
    for i in range(DEPTH):
        j = i // 2
        if i % 2 == 0:
            m = even_mixer(x, positions, ev_w_in[j], ev_conv_w[j], ev_conv_b[j], ev_cnorm_g[j],
                           ev_cnorm_b[j], ev_qnorm_g[j], ev_w_uq[j], ev_kvnorm_g[j], ev_w_ukv[j],
                           ev_w_out[j])
        else:
            m = odd_mixer(x, od_w_in[j], od_conv_w[j], od_conv_b[j], od_w_a[j], od_b_a[j],
                          od_w_i[j], od_b_i[j], od_lam[j], od_w_out[j])
        x = layer_norm(ALPHA * x + m, ln_mix_g[i], ln_mix_b[i])
        f = hierarchical_moe(x, moe_w_grp[i], moe_b_grp[i], moe_w_exp[i], moe_b_exp[i],
                             moe_w1[i], moe_w3[i], moe_w2[i])
        x = layer_norm(ALPHA * x + f, ln_ffn_g[i], ln_ffn_b[i])
        x = x + jax.nn.sigmoid(x @ ple_w_gate[i] + ple_b_gate[i]) * (p[i] @ ple_w_proj[i])
    return x
```

```python
import functools
import math

import jax
import jax.numpy as jnp
from jax import lax
from jax.experimental import pallas as pl
from jax.experimental.pallas import tpu as pltpu

F32 = jnp.float32
BF16 = jnp.bfloat16

D_MODEL = 2048
SEQ = 8192
DEPTH = 2
CONV_CH = 1024
CONV_GROUPS = 16
CONV_WIDTH = 31
MLA_HEADS = 8
QK_NOPE = 128
QK_ROPE = 64
V_HEAD = 128
Q_RANK = 512
KV_RANK = 256
ROPE_THETA = 10000.0
RNN_WIDTH = 2048
RNN_HEADS = 16
RNN_HEAD_DIM = RNN_WIDTH // RNN_HEADS
RNN_CONV_WIDTH = 4
RG_C = 8.0
N_GROUPS = 8
EXPERTS_PER_GROUP = 8
N_EXPERTS = N_GROUPS * EXPERTS_PER_GROUP
TOP_K = 2
D_EXPERT = 512
PLE_DIM = 256
ALPHA = (2 * DEPTH) ** 0.25
LN_EPS = 1e-5
RMS_EPS = 1e-6

LANES = 128
CONV_HALO = 32
RNN_HALO = 8
MOE_ROWS = 256
ROUTE_LANES = 128
MIB = 2 ** 20


def _params(semantics, vmem_mib):
    return pltpu.CompilerParams(dimension_semantics=semantics, vmem_limit_bytes=vmem_mib * MIB)


def _layer_norm_rows(z, g, b):
    mu = jnp.mean(z, axis=-1, keepdims=True)
    d = z - mu
    var = jnp.mean(d * d, axis=-1, keepdims=True)
    return d * lax.rsqrt(var + LN_EPS) * g + b


def _split_bf16(v):
    hi = v.astype(BF16)
    lo = (v - hi.astype(F32)).astype(BF16)
    return hi, lo


def _gelu_tanh(x):
    c = math.sqrt(2.0 / math.pi)
    return 0.5 * x * (1.0 + jnp.tanh(c * (x + 0.044715 * (x * x * x))))


def _proj_kernel(x_ref, w_ref, o_ref, *, act):
    y = jnp.dot(x_ref[...], w_ref[...], preferred_element_type=F32)
    if act == "gelu":
        y = _gelu_tanh(y)
    o_ref[...] = y.astype(o_ref.dtype)


def _proj(x, w, *, act, out_dtype, tm=512, tn=512):
    m, k = x.shape
    n = w.shape[1]
    return pl.pallas_call(
        functools.partial(_proj_kernel, act=act),
        grid=(n // tn, m // tm),
        in_specs=[pl.BlockSpec((tm, k), lambda j, i: (i, 0)),
                  pl.BlockSpec((k, tn), lambda j, i: (0, j))],
        out_specs=pl.BlockSpec((tm, tn), lambda j, i: (i, j)),
        out_shape=jax.ShapeDtypeStruct((m, n), out_dtype),
        compiler_params=_params(("parallel", "parallel"), 40),
        name="proj_" + str(act),
    )(x, w)


def _glu_kernel(x_ref, wv_ref, wg_ref, o_ref):
    x = x_ref[...]
    v = jnp.dot(x, wv_ref[...], preferred_element_type=F32)
    g = jnp.dot(x, wg_ref[...], preferred_element_type=F32)
    o_ref[...] = v * jax.nn.sigmoid(g)


def _glu_proj(x, w, *, tm=512, tn=512):
    m, k = x.shape
    n = w.shape[1] // 2
    nb = n // tn
    return pl.pallas_call(
        _glu_kernel,
        grid=(nb, m // tm),
        in_specs=[pl.BlockSpec((tm, k), lambda j, i: (i, 0)),
                  pl.BlockSpec((k, tn), lambda j, i: (0, j)),
                  pl.BlockSpec((k, tn), lambda j, i: (0, j + nb))],
        out_specs=pl.BlockSpec((tm, tn), lambda j, i: (i, j)),
        out_shape=jax.ShapeDtypeStruct((m, n), F32),
        compiler_params=_params(("parallel", "parallel"), 40),
        name="glu_proj",
    )(x, w, w)


def _conv_gln_kernel(cur_ref, prev_ref, w_ref, b_ref, g_ref, beta_ref, gm_ref, o_ref,
                     buf_ref, y_ref, *, tt, ch, width, rc, cc):
    i = pl.program_id(0)
    buf_ref[0:CONV_HALO, :] = jnp.where(i > 0, prev_ref[...], 0.0)
    buf_ref[CONV_HALO:, :] = cur_ref[...]
    off = CONV_HALO - (width - 1)
    for c0 in range(0, ch, cc):
        for r0 in range(0, tt, rc):
            acc = jnp.broadcast_to(b_ref[:, c0:c0 + cc], (rc, cc))
            for k in range(width):
                acc = acc + buf_ref[r0 + off + k:r0 + off + k + rc, c0:c0 + cc] * w_ref[k:k + 1, c0:c0 + cc]
            y_ref[r0:r0 + rc, c0:c0 + cc] = acc
    gm = gm_ref[...]

    def seg_mean(v):
        hi, lo = _split_bf16(v)
        return (jnp.dot(hi, gm, preferred_element_type=F32)
                + jnp.dot(lo, gm, preferred_element_type=F32))

    for c0 in range(0, ch, LANES):
        y = y_ref[:, c0:c0 + LANES]
        d = y - seg_mean(y)
        var = seg_mean(d * d)
        z = d * lax.rsqrt(var + LN_EPS) * g_ref[:, c0:c0 + LANES] + beta_ref[:, c0:c0 + LANES]
        o_ref[:, c0:c0 + LANES] = (z * jax.nn.sigmoid(z)).astype(o_ref.dtype)


def _conv_gln(u, w, b, g, beta, *, groups, tt=256):
    s, ch = u.shape
    width = w.shape[0]
    gsz = ch // groups
    assert LANES % gsz == 0 and width - 1 <= CONV_HALO and tt % CONV_HALO == 0
    wpad = jnp.zeros((CONV_HALO, ch), F32).at[:width].set(w)
    lane = jnp.arange(LANES)
    gm = jnp.where((lane[:, None] // gsz) == (lane[None, :] // gsz), 1.0 / gsz, 0.0).astype(BF16)
    hb = tt // CONV_HALO
    row = lambda v: v.reshape(1, ch)
    return pl.pallas_call(
        functools.partial(_conv_gln_kernel, tt=tt, ch=ch, width=width, rc=32, cc=256),
        grid=(s // tt,),
        in_specs=[pl.BlockSpec((tt, ch), lambda i: (i, 0)),
                  pl.BlockSpec((CONV_HALO, ch), lambda i: (jnp.maximum(i * hb - 1, 0), 0)),
                  pl.BlockSpec((CONV_HALO, ch), lambda i: (0, 0)),
                  pl.BlockSpec((1, ch), lambda i: (0, 0)),
                  pl.BlockSpec((1, ch), lambda i: (0, 0)),
                  pl.BlockSpec((1, ch), lambda i: (0, 0)),
                  pl.BlockSpec((LANES, LANES), lambda i: (0, 0))],
        out_specs=pl.BlockSpec((tt, ch), lambda i: (i, 0)),
        out_shape=jax.ShapeDtypeStruct((s, ch), BF16),
        scratch_shapes=[pltpu.VMEM((tt + CONV_HALO, ch), F32), pltpu.VMEM((tt, ch), F32)],
        compiler_params=_params(("parallel",), 32),
        name="conv_gln",
    )(u, u, wpad, row(b), row(g), row(beta), gm)


def _rope_table_kernel(pos_ref, invf_ref, cos_ref, sin_ref):
    ang = pos_ref[...].astype(F32) * invf_ref[...]
    cos_ref[...] = jnp.cos(ang)
    sin_ref[...] = jnp.sin(ang)


def _rope_tables(positions):
    s = positions.shape[0]
    half = QK_ROPE // 2
    per_row = LANES // half
    inv_freq = 1.0 / (ROPE_THETA ** (jnp.arange(0, QK_ROPE, 2, dtype=F32) / QK_ROPE))
    pos_rep = jnp.repeat(positions.reshape(s // per_row, per_row), half, axis=1)
    invf = jnp.tile(inv_freq, per_row).reshape(1, LANES)
    cos, sin = pl.pallas_call(
        _rope_table_kernel,
        out_shape=[jax.ShapeDtypeStruct((s // per_row, LANES), F32)] * 2,
        name="rope_table",
    )(pos_rep, invf)
    cos = cos.reshape(s, half)
    sin = sin.reshape(s, half)
    zero = jnp.zeros((s, LANES - QK_ROPE), F32)
    return jnp.concatenate([cos, cos, zero], axis=1), jnp.concatenate([-sin, sin, zero], axis=1)


def _mla_proj_kernel(x_ref, wm_ref, qg_ref, kvg_ref, wq_ref, wkv_ref, c_ref, s_ref,
                     qn_ref, qr_ref, kn_ref, kr_ref, v_ref, *, scale):
    nq = MLA_HEADS * QK_NOPE
    c = jnp.dot(x_ref[...], wm_ref[...], preferred_element_type=F32)
    cq = c[:, :Q_RANK]
    ckv = c[:, Q_RANK:Q_RANK + KV_RANK]
    kr_pad = c[:, Q_RANK + KV_RANK:Q_RANK + KV_RANK + LANES]
    kr_rot = c[:, Q_RANK + KV_RANK + LANES:]
    cqn = cq * lax.rsqrt(jnp.mean(cq * cq, axis=-1, keepdims=True) + RMS_EPS) * qg_ref[...]
    ckvn = ckv * lax.rsqrt(jnp.mean(ckv * ckv, axis=-1, keepdims=True) + RMS_EPS) * kvg_ref[...]
    q = jnp.dot(cqn.astype(BF16), wq_ref[...], preferred_element_type=F32)
    kv = jnp.dot(ckvn.astype(BF16), wkv_ref[...], preferred_element_type=F32)
    cosm = c_ref[...]
    sinm = s_ref[...]
    qn_ref[...] = (q[:, :nq] * scale).astype(BF16)
    for h in range(MLA_HEADS):
        lo = nq + h * LANES
        rot = q[:, lo:lo + LANES] * cosm + q[:, lo + nq:lo + nq + LANES] * sinm
        qr_ref[:, h * LANES:(h + 1) * LANES] = (rot * scale).astype(BF16)
    kn_ref[...] = kv[:, :nq].astype(BF16)
    v_ref[...] = kv[:, nq:].astype(BF16)
    kr_ref[...] = (kr_pad * cosm + kr_rot * sinm).astype(BF16)


def _rope_swap(w):
    half = QK_ROPE // 2
    return jnp.concatenate([w[..., half:], w[..., :half]], axis=-1)


def _mla_proj(xb, w_in_mla, qnorm_g, w_uq, kvnorm_g, w_ukv, cosm, sinm, *, tm=256):
    s, d = xb.shape
    nq = MLA_HEADS * QK_NOPE
    pad = lambda w: jnp.concatenate([w, jnp.zeros(w.shape[:-1] + (LANES - QK_ROPE,), w.dtype)], axis=-1)
    w_kr = w_in_mla[:, Q_RANK + KV_RANK:]
    wm = jnp.concatenate([w_in_mla[:, :Q_RANK + KV_RANK], pad(w_kr), pad(_rope_swap(w_kr))], axis=1).astype(BF16)
    wq3 = w_uq.reshape(Q_RANK, MLA_HEADS, QK_NOPE + QK_ROPE)
    wq_rope = wq3[:, :, QK_NOPE:]
    wq = jnp.concatenate([wq3[:, :, :QK_NOPE].reshape(Q_RANK, nq),
                          pad(wq_rope).reshape(Q_RANK, MLA_HEADS * LANES),
                          pad(_rope_swap(wq_rope)).reshape(Q_RANK, MLA_HEADS * LANES)], axis=1).astype(BF16)
    wkv3 = w_ukv.reshape(KV_RANK, MLA_HEADS, QK_NOPE + V_HEAD)
    wkv = jnp.concatenate([wkv3[:, :, :QK_NOPE].reshape(KV_RANK, nq),
                           wkv3[:, :, QK_NOPE:].reshape(KV_RANK, MLA_HEADS * V_HEAD)], axis=1).astype(BF16)
    scale = (QK_NOPE + QK_ROPE) ** -0.5
    full = lambda a: pl.BlockSpec(a.shape, lambda i: (0,) * a.ndim)
    rows = lambda n: pl.BlockSpec((tm, n), lambda i: (i, 0))
    qg = qnorm_g.reshape(1, Q_RANK)
    kvg = kvnorm_g.reshape(1, KV_RANK)
    return pl.pallas_call(
        functools.partial(_mla_proj_kernel, scale=scale),
        grid=(s // tm,),
        in_specs=[rows(d), full(wm), full(qg), full(kvg), full(wq), full(wkv), rows(LANES), rows(LANES)],
        out_specs=[rows(nq), rows(MLA_HEADS * LANES), rows(nq), rows(LANES), rows(MLA_HEADS * V_HEAD)],
        out_shape=[jax.ShapeDtypeStruct((s, nq), BF16),
                   jax.ShapeDtypeStruct((s, MLA_HEADS * LANES), BF16),
                   jax.ShapeDtypeStruct((s, nq), BF16),
                   jax.ShapeDtypeStruct((s, LANES), BF16),
                   jax.ShapeDtypeStruct((s, MLA_HEADS * V_HEAD), BF16)],
        compiler_params=_params(("parallel",), 48),
        name="mla_proj",
    )(xb, wm, qg, kvg, wq, wkv, cosm, sinm)


def _attn_kernel(qn_ref, qr_ref, kn_ref, kr_ref, v_ref, o_ref, m_ref, l_ref, acc_ref, *, tq):
    qi = pl.program_id(1)
    q = jnp.concatenate([qn_ref[...], qr_ref[...]], axis=1)
    m_ref[...] = jnp.full(m_ref.shape, -jnp.inf, F32)
    l_ref[...] = jnp.zeros(l_ref.shape, F32)
    acc_ref[...] = jnp.zeros(acc_ref.shape, F32)

    def chunk(j, masked):
        rows = pl.ds(pl.multiple_of(j * tq, tq), tq)
        k = jnp.concatenate([kn_ref[rows, :], kr_ref[rows, :]], axis=1)
        s = lax.dot_general(q, k, (((1,), (1,)), ((), ())), preferred_element_type=F32)
        if masked:
            keep = lax.broadcasted_iota(jnp.int32, s.shape, 0) >= lax.broadcasted_iota(jnp.int32, s.shape, 1)
            s = jnp.where(keep, s, -jnp.inf)
        m_old = m_ref[...]
        m_new = jnp.maximum(m_old, jnp.max(s, axis=-1, keepdims=True))
        alpha = jnp.exp(m_old - m_new)
        p = jnp.exp(s - m_new)
        l_ref[...] = alpha * l_ref[...] + jnp.sum(p, axis=-1, keepdims=True)
        acc_ref[...] = alpha * acc_ref[...] + jnp.dot(p.astype(BF16), v_ref[rows, :], preferred_element_type=F32)
        m_ref[...] = m_new

    def body(j, carry):
        chunk(j, False)
        return carry

    lax.fori_loop(0, qi, body, 0)
    chunk(qi, True)
    o_ref[...] = (acc_ref[...] / l_ref[...]).astype(o_ref.dtype)


def _attention(qn, qr, kn, kr, v, *, tq=512):
    s = qn.shape[0]
    head = lambda: pl.BlockSpec((tq, LANES), lambda h, i: (i, h))
    keys = lambda per_head: pl.BlockSpec((s, LANES), (lambda h, i: (0, h)) if per_head else (lambda h, i: (0, 0)))
    return pl.pallas_call(
        functools.partial(_attn_kernel, tq=tq),
        grid=(MLA_HEADS, s // tq),
        in_specs=[head(), head(), keys(True), keys(False), keys(True)],
        out_specs=head(),
        out_shape=jax.ShapeDtypeStruct((s, MLA_HEADS * V_HEAD), BF16),
        scratch_shapes=[pltpu.VMEM((tq, 1), F32), pltpu.VMEM((tq, 1), F32), pltpu.VMEM((tq, V_HEAD), F32)],
        compiler_params=_params(("parallel", "parallel"), 40),
        name="mla_attention",
    )(qn, qr, kn, kr, v)


def _route(logits):
    lane = lax.broadcasted_iota(jnp.int32, logits.shape, 1).astype(F32)
    big = float(2 * ROUTE_LANES)
    neg = -jnp.inf
    gl = jnp.where(lane < N_GROUPS, logits, neg)
    gmax = jnp.max(gl, axis=-1, keepdims=True)
    gsum = jnp.sum(jnp.exp(gl - gmax), axis=-1, keepdims=True)
    g_prob = 1.0 / gsum
    g_idx = jnp.min(jnp.where(gl == gmax, lane, big), axis=-1, keepdims=True)
    lo = N_GROUPS + g_idx * EXPERTS_PER_GROUP
    el = jnp.where((lane >= lo) & (lane < lo + EXPERTS_PER_GROUP), logits, neg)
    emax = jnp.max(el, axis=-1, keepdims=True)
    esum = jnp.sum(jnp.exp(el - emax), axis=-1, keepdims=True)
    i0 = jnp.min(jnp.where(el == emax, lane, big), axis=-1, keepdims=True)
    el2 = jnp.where(lane == i0, neg, el)
    emax2 = jnp.max(el2, axis=-1, keepdims=True)
    i1 = jnp.min(jnp.where(el2 == emax2, lane, big), axis=-1, keepdims=True)
    p0 = 1.0 / esum
    p1 = jnp.exp(emax2 - emax) / esum
    g0 = g_prob * p0 / (p0 + p1)
    g1 = g_prob * p1 / (p0 + p1)
    e0 = i0 - N_GROUPS
    e1 = i1 - N_GROUPS
    return jnp.where(lane == 0, e0, jnp.where(lane == 1, e1, jnp.where(lane == 2, g0, jnp.where(lane == 3, g1, 0.0))))


def _out_ln_route_kernel(a1_ref, a2_ref, w_ref, x_ref, g_ref, b_ref, wrh_ref, wrl_ref, br_ref,
                         x1_ref, route_ref):
    half = a1_ref.shape[1]
    m = (jnp.dot(a1_ref[...], w_ref[:half, :], preferred_element_type=F32)
         + jnp.dot(a2_ref[...], w_ref[half:, :], preferred_element_type=F32))
    x1 = _layer_norm_rows(ALPHA * x_ref[...] + m, g_ref[...], b_ref[...])
    x1_ref[...] = x1
    xh, xl = _split_bf16(x1)
    wrh = wrh_ref[...]
    logits = (jnp.dot(xh, wrh, preferred_element_type=F32) + jnp.dot(xl, wrh, preferred_element_type=F32)
              + jnp.dot(xh, wrl_ref[...], preferred_element_type=F32) + br_ref[...])
    route_ref[...] = _route(logits)


def _out_ln_route(a1, a2, w, x, ln_g, ln_b, w_grp, b_grp, w_exp, b_exp, *, col1=0, col2=0, tm=256):
    s, d = x.shape
    half = w.shape[0] // 2
    pad = ROUTE_LANES - N_GROUPS - N_EXPERTS
    wr = jnp.concatenate([w_grp, w_exp, jnp.zeros((d, pad), F32)], axis=1)
    br = jnp.concatenate([b_grp, b_exp, jnp.zeros((pad,), F32)]).reshape(1, ROUTE_LANES)
    wrh = wr.astype(BF16)
    wrl = (wr - wrh.astype(F32)).astype(BF16)
    full = lambda a: pl.BlockSpec(a.shape, lambda i: (0,) * a.ndim)
    rows = lambda n: pl.BlockSpec((tm, n), lambda i: (i, 0))
    g2 = ln_g.reshape(1, d)
    b2 = ln_b.reshape(1, d)
    return pl.pallas_call(
        _out_ln_route_kernel,
        grid=(s // tm,),
        in_specs=[pl.BlockSpec((tm, half), lambda i: (i, col1)), pl.BlockSpec((tm, half), lambda i: (i, col2)),
                  full(w), rows(d), full(g2), full(b2), full(wrh), full(wrl), full(br)],
        out_specs=[rows(d), rows(ROUTE_LANES)],
        out_shape=[jax.ShapeDtypeStruct((s, d), F32), jax.ShapeDtypeStruct((s, ROUTE_LANES), F32)],
        compiler_params=_params(("parallel",), 48),
        name="out_ln_route",
    )(a1, a2, w, x, g2, b2, wrh, wrl, br)


def _moe_plan(route):
    s = route.shape[0]
    a = s * TOP_K
    eid = route[:, :TOP_K].astype(jnp.int32).reshape(a)
    order = jnp.argsort(eid)
    sorted_e = eid[order]
    counts = jnp.zeros((N_EXPERTS,), jnp.int32).at[eid].add(1)
    starts = jnp.cumsum(counts) - counts
    pcounts = (counts + MOE_ROWS - 1) // MOE_ROWS * MOE_ROWS
    pends = jnp.cumsum(pcounts)
    pstarts = pends - pcounts
    dest_sorted = pstarts[sorted_e] + jnp.arange(a, dtype=jnp.int32) - starts[sorted_e]
    dest = jnp.zeros((a,), jnp.int32).at[order].set(dest_sorted)
    n_blocks = a // MOE_ROWS + N_EXPERTS
    row_token = jnp.zeros((n_blocks * MOE_ROWS,), jnp.int32).at[dest_sorted].set(order // TOP_K)
    n_used = (pends[-1] // MOE_ROWS).astype(jnp.int32)
    blk = jnp.minimum(jnp.arange(n_blocks, dtype=jnp.int32), n_used - 1) * MOE_ROWS
    block_expert = jnp.minimum(jnp.searchsorted(pends, blk, side="right"), N_EXPERTS - 1).astype(jnp.int32)
    pos = dest.reshape(s, TOP_K).T.reshape(TOP_K * s)
    return row_token, block_expert, n_used.reshape(1), pos


def _row_copy(src_hbm, dst_ref, src_row, dst_row, sem):
    return pltpu.make_async_copy(src_hbm.at[pl.ds(src_row, 1), :], dst_ref.at[pl.ds(dst_row, 1), :], sem)


def _gather_kernel(tok_ref, nused_ref, x_hbm, o_ref, sem):
    b = pl.program_id(0)

    @pl.when(b < nused_ref[0])
    def _():
        base = b * MOE_ROWS

        def issue(r, carry):
            _row_copy(x_hbm, o_ref, tok_ref[base + r], r, sem).start()
            return carry

        lax.fori_loop(0, MOE_ROWS, issue, 0)

        def drain(r, carry):
            _row_copy(x_hbm, o_ref, 0, r, sem).wait()
            return carry

        lax.fori_loop(0, MOE_ROWS, drain, 0)


def _gather_rows(x, row_token, n_used):
    d = x.shape[1]
    blk = lambda b, tok, nu: (jnp.minimum(b, nu[0] - 1), 0)
    return pl.pallas_call(
        _gather_kernel,
        grid_spec=pltpu.PrefetchScalarGridSpec(
            num_scalar_prefetch=2,
            grid=(row_token.shape[0] // MOE_ROWS,),
            in_specs=[pl.BlockSpec(memory_space=pl.ANY)],
            out_specs=pl.BlockSpec((MOE_ROWS, d), blk),
            scratch_shapes=[pltpu.SemaphoreType.DMA(())]),
        out_shape=jax.ShapeDtypeStruct((row_token.shape[0], d), x.dtype),
        compiler_params=_params(("arbitrary",), 32),
        name="moe_gather",
    )(row_token, n_used, x)


def _ffn_kernel(be_ref, nused_ref, xs_ref, w1_ref, w3_ref, w2_ref, ys_ref):
    b = pl.program_id(0)

    @pl.when(b < nused_ref[0])
    def _():
        x = xs_ref[...].astype(BF16)
        h1 = jnp.dot(x, w1_ref[0].astype(BF16), preferred_element_type=F32)
        h3 = jnp.dot(x, w3_ref[0].astype(BF16), preferred_element_type=F32)
        hmid = (h1 * jax.nn.sigmoid(h1) * h3).astype(BF16)
        ys_ref[...] = jnp.dot(hmid, w2_ref[0].astype(BF16), preferred_element_type=F32)


def _grouped_ffn(xs, w1, w3, w2, block_expert, n_used):
    d = xs.shape[1]
    de = w1.shape[2]
    blk = lambda b, be, nu: (jnp.minimum(b, nu[0] - 1), 0)
    wsel = lambda b, be, nu: (be[b], 0, 0)
    return pl.pallas_call(
        _ffn_kernel,
        grid_spec=pltpu.PrefetchScalarGridSpec(
            num_scalar_prefetch=2,
            grid=(xs.shape[0] // MOE_ROWS,),
            in_specs=[pl.BlockSpec((MOE_ROWS, d), blk),
                      pl.BlockSpec((1, d, de), wsel),
                      pl.BlockSpec((1, d, de), wsel),
                      pl.BlockSpec((1, de, d), wsel)],
            out_specs=pl.BlockSpec((MOE_ROWS, d), blk)),
        out_shape=jax.ShapeDtypeStruct(xs.shape, F32),
        compiler_params=_params(("arbitrary",), 56),
        name="moe_ffn",
    )(block_expert, n_used, xs, w1, w3, w2)


def _combine_ln_kernel(pos_ref, ys_hbm, x1_ref, route_ref, g_ref, b_ref, x2_ref, x2b_ref,
                       y0_ref, y1_ref, sems, *, tm, s):
    t0 = pl.program_id(0) * tm

    def issue(r, carry):
        _row_copy(ys_hbm, y0_ref, pos_ref[t0 + r], r, sems.at[0]).start()
        _row_copy(ys_hbm, y1_ref, pos_ref[s + t0 + r], r, sems.at[1]).start()
        return carry

    lax.fori_loop(0, tm, issue, 0)

    def drain(r, carry):
        _row_copy(ys_hbm, y0_ref, 0, r, sems.at[0]).wait()
        _row_copy(ys_hbm, y1_ref, 0, r, sems.at[1]).wait()
        return carry

    lax.fori_loop(0, tm, drain, 0)
    route = route_ref[...]
    f = route[:, 2:3] * y0_ref[...] + route[:, 3:4] * y1_ref[...]
    x2 = _layer_norm_rows(ALPHA * x1_ref[...] + f, g_ref[...], b_ref[...])
    x2_ref[...] = x2
    x2b_ref[...] = x2.astype(BF16)


def _combine_ln(ys, pos, x1, route, ln_g, ln_b, *, tm=256):
    s, d = x1.shape
    rows = lambda n: pl.BlockSpec((tm, n), lambda i, pos: (i, 0))
    full = lambda n: pl.BlockSpec((1, n), lambda i, pos: (0, 0))
    return pl.pallas_call(
        functools.partial(_combine_ln_kernel, tm=tm, s=s),
        grid_spec=pltpu.PrefetchScalarGridSpec(
            num_scalar_prefetch=1,
            grid=(s // tm,),
            in_specs=[pl.BlockSpec(memory_space=pl.ANY), rows(d), rows(ROUTE_LANES), full(d), full(d)],
            out_specs=[rows(d), rows(d)],
            scratch_shapes=[pltpu.VMEM((tm, d), F32), pltpu.VMEM((tm, d), F32),
                            pltpu.SemaphoreType.DMA((2,))]),
        out_shape=[jax.ShapeDtypeStruct((s, d), F32), jax.ShapeDtypeStruct((s, d), BF16)],
        compiler_params=_params(("arbitrary",), 40),
        name="moe_combine_ln",
    )(pos, ys, x1, route, ln_g.reshape(1, d), ln_b.reshape(1, d))


def _ple_kernel(x_ref, xb_ref, wg_ref, bg_ref, p_ref, wp_ref, o_ref, ob_ref):
    gate = jax.nn.sigmoid(jnp.dot(xb_ref[...], wg_ref[...], preferred_element_type=F32) + bg_ref[...])
    emb = jnp.dot(p_ref[...].astype(BF16), wp_ref[...], preferred_element_type=F32)
    out = x_ref[...] + gate * emb
    o_ref[...] = out
    ob_ref[...] = out.astype(BF16)


def _ple(x, xb, w_gate, b_gate, p, w_proj, *, tm=256):
    s, d = x.shape
    pd = p.shape[1]
    rows = lambda n: pl.BlockSpec((tm, n), lambda i: (i, 0))
    full = lambda a: pl.BlockSpec(a.shape, lambda i: (0,) * a.ndim)
    bg = b_gate.reshape(1, d)
    return pl.pallas_call(
        _ple_kernel,
        grid=(s // tm,),
        in_specs=[rows(d), rows(d), full(w_gate), full(bg), rows(pd), full(w_proj)],
        out_specs=[rows(d), rows(d)],
        out_shape=[jax.ShapeDtypeStruct((s, d), F32), jax.ShapeDtypeStruct((s, d), BF16)],
        compiler_params=_params(("parallel",), 48),
        name="ple",
    )(x, xb, w_gate, bg, p, w_proj)


def _rglru_kernel(cur_ref, prev_ref, gate_ref, cw_ref, cb_ref, wa_ref, ba_ref, wi_ref, bi_ref, lam_ref,
                  o_ref, buf_ref, a_ref, b_ref, h_ref, carry_ref, *, tt, tc, width):
    i = pl.program_id(1)

    @pl.when(i == 0)
    def _():
        carry_ref[...] = jnp.zeros(carry_ref.shape, F32)

    buf_ref[0:RNN_HALO, :] = jnp.where(i > 0, prev_ref[...], 0.0)
    buf_ref[RNN_HALO:, :] = cur_ref[...]
    off = RNN_HALO - (width - 1)
    xc = jnp.broadcast_to(cb_ref[...], (tt, tc))
    for k in range(width):
        xc = xc + buf_ref[off + k:off + k + tt, :] * cw_ref[k:k + 1, :]
    z = -lam_ref[...]
    softplus = jnp.log1p(jnp.exp(-jnp.abs(z))) + jnp.maximum(z, 0.0)
    for hh in range(tc // RNN_HEAD_DIM):
        cols = slice(hh * RNN_HEAD_DIM, (hh + 1) * RNN_HEAD_DIM)
        xh = xc[:, cols]
        xhb = xh.astype(BF16)
        r = jax.nn.sigmoid(jnp.dot(xhb, wa_ref[hh], preferred_element_type=F32) + ba_ref[:, cols])
        ig = jax.nn.sigmoid(jnp.dot(xhb, wi_ref[hh], preferred_element_type=F32) + bi_ref[:, cols])
        log_a = (-RG_C) * r * softplus[:, cols]
        a = jnp.exp(log_a)
        bval = jnp.sqrt(-jnp.tanh(log_a) * (a * a + 1.0)) * (ig * xh)
        a_ref[:, cols] = a
        b_ref[:, cols] = bval

    def group(gi, h):
        base = pl.multiple_of(gi * 8, 8)
        for r in range(8):
            row = pl.ds(base + r, 1)
            h = a_ref[row, :] * h + b_ref[row, :]
            h_ref[row, :] = h
        return h

    carry_ref[...] = lax.fori_loop(0, tt // 8, group, carry_ref[...])
    o_ref[...] = (gate_ref[...].astype(F32) * h_ref[...]).astype(o_ref.dtype)


def _rglru(xr, gate, conv_w, conv_b, w_a, b_a, w_i, b_i, lam, *, tt=256, tc=512):
    s, wd = xr.shape
    width = conv_w.shape[0]
    hpc = tc // RNN_HEAD_DIM
    hb = tt // RNN_HALO
    row = lambda v: v.reshape(1, wd)
    cvec = lambda: pl.BlockSpec((1, tc), lambda c, i: (0, c))
    heads = lambda: pl.BlockSpec((hpc, RNN_HEAD_DIM, RNN_HEAD_DIM), lambda c, i: (c, 0, 0))
    tile = lambda: pl.BlockSpec((tt, tc), lambda c, i: (i, c))
    return pl.pallas_call(
        functools.partial(_rglru_kernel, tt=tt, tc=tc, width=width),
        grid=(wd // tc, s // tt),
        in_specs=[tile(),
                  pl.BlockSpec((RNN_HALO, tc), lambda c, i: (jnp.maximum(i * hb - 1, 0), c)),
                  tile(),
                  pl.BlockSpec((width, tc), lambda c, i: (0, c)),
                  cvec(), heads(), cvec(), heads(), cvec(), cvec()],
        out_specs=tile(),
        out_shape=jax.ShapeDtypeStruct((s, wd), BF16),
        scratch_shapes=[pltpu.VMEM((tt + RNN_HALO, tc), F32), pltpu.VMEM((tt, tc), F32),
                        pltpu.VMEM((tt, tc), F32), pltpu.VMEM((tt, tc), F32), pltpu.VMEM((1, tc), F32)],
        compiler_params=_params(("parallel", "arbitrary"), 32),
        name="rglru",
    )(xr, xr, gate, conv_w, row(conv_b), w_a.astype(BF16), row(b_a), w_i.astype(BF16), row(b_i), row(lam))


def _moe_and_ple(x1, route, i, p, ln_ffn_g, ln_ffn_b, moe_w1, moe_w3, moe_w2, ple_w_proj, ple_w_gate, ple_b_gate):
    row_token, block_expert, n_used, pos = _moe_plan(route)
    xs = _gather_rows(x1, row_token, n_used)
    ys = _grouped_ffn(xs, moe_w1[i], moe_w3[i], moe_w2[i], block_expert, n_used)
    x2, x2b = _combine_ln(ys, pos, x1, route, ln_ffn_g[i], ln_ffn_b[i])
    return _ple(x2, x2b, ple_w_gate[i].astype(BF16), ple_b_gate[i], p[i, 0], ple_w_proj[i].astype(BF16))


def kernel(x, p, positions, ev_w_in, ev_conv_w, ev_conv_b, ev_cnorm_g, ev_cnorm_b, ev_qnorm_g, ev_w_uq, ev_kvnorm_g, ev_w_ukv, ev_w_out, od_w_in, od_conv_w, od_conv_b, od_w_a, od_b_a, od_w_i, od_b_i, od_lam, od_w_out, ln_mix_g, ln_mix_b, ln_ffn_g, ln_ffn_b, moe_w_grp, moe_b_grp, moe_w_exp, moe_b_exp, moe_w1, moe_w3, moe_w2, ple_w_proj, ple_w_gate, ple_b_gate):
    x0 = x[0]
    x0b = x0.astype(BF16)
    moe = functools.partial(_moe_and_ple, p=p, ln_ffn_g=ln_ffn_g, ln_ffn_b=ln_ffn_b, moe_w1=moe_w1,
                            moe_w3=moe_w3, moe_w2=moe_w2, ple_w_proj=ple_w_proj, ple_w_gate=ple_w_gate,
                            ple_b_gate=ple_b_gate)

    w_in = ev_w_in[0]
    u = _glu_proj(x0b, w_in[:, :2 * CONV_CH].astype(BF16))
    ub = _conv_gln(u, ev_conv_w[0], ev_conv_b[0], ev_cnorm_g[0], ev_cnorm_b[0], groups=CONV_GROUPS)
    cosm, sinm = _rope_tables(positions[0])
    qn, qr, kn, kr, v = _mla_proj(x0b, w_in[:, 2 * CONV_CH:], ev_qnorm_g[0], ev_w_uq[0], ev_kvnorm_g[0],
                                  ev_w_ukv[0], cosm, sinm)
    att = _attention(qn, qr, kn, kr, v)
    x1, route = _out_ln_route(ub, att, ev_w_out[0].astype(BF16), x0, ln_mix_g[0], ln_mix_b[0],
                              moe_w_grp[0], moe_b_grp[0], moe_w_exp[0], moe_b_exp[0])
    x3, x3b = moe(x1, route, 0)

    w_in1 = od_w_in[0]
    gate = _proj(x3b, w_in1[:, :RNN_WIDTH].astype(BF16), act="gelu", out_dtype=BF16)
    xr = _proj(x3b, w_in1[:, RNN_WIDTH:].astype(BF16), act=None, out_dtype=F32)
    y = _rglru(xr, gate, od_conv_w[0], od_conv_b[0], od_w_a[0], od_b_a[0], od_w_i[0], od_b_i[0], od_lam[0])
    x4, route1 = _out_ln_route(y, y, od_w_out[0].astype(BF16), x3, ln_mix_g[1], ln_mix_b[1],
                               moe_w_grp[1], moe_b_grp[1], moe_w_exp[1], moe_b_exp[1], col1=0, col2=1)
    x6, _ = moe(x4, route1, 1)
    return x6[None]
```

```python
import functools
import math

import jax
import jax.numpy as jnp
from jax import lax
from jax.experimental import pallas as pl
from jax.experimental.pallas import tpu as pltpu

F32 = jnp.float32
BF16 = jnp.bfloat16

D_MODEL = 2048
SEQ = 8192
DEPTH = 2
CONV_CH = 1024
CONV_GROUPS = 16
CONV_WIDTH = 31
MLA_HEADS = 8
QK_NOPE = 128
QK_ROPE = 64
V_HEAD = 128
Q_RANK = 512
KV_RANK = 256
ROPE_THETA = 10000.0
RNN_WIDTH = 2048
RNN_HEADS = 16
RNN_HEAD_DIM = RNN_WIDTH // RNN_HEADS
RNN_CONV_WIDTH = 4
RG_C = 8.0
N_GROUPS = 8
EXPERTS_PER_GROUP = 8
N_EXPERTS = N_GROUPS * EXPERTS_PER_GROUP
TOP_K = 2
D_EXPERT = 512
PLE_DIM = 256
ALPHA = (2 * DEPTH) ** 0.25
LN_EPS = 1e-5
RMS_EPS = 1e-6

LANES = 128
CONV_HALO = 32
RNN_HALO = 8
MOE_ROWS = 256
ROUTE_LANES = 128
MIB = 2 ** 20


def _params(semantics, vmem_mib):
    return pltpu.CompilerParams(dimension_semantics=semantics, vmem_limit_bytes=vmem_mib * MIB)


def _layer_norm_rows(z, g, b):
    mu = jnp.mean(z, axis=-1, keepdims=True)
    d = z - mu
    var = jnp.mean(d * d, axis=-1, keepdims=True)
    return d * lax.rsqrt(var + LN_EPS) * g + b


def _split_bf16(v):
    hi = v.astype(BF16)
    lo = (v - hi.astype(F32)).astype(BF16)
    return hi, lo


def _gelu_tanh(x):
    c = math.sqrt(2.0 / math.pi)
    return 0.5 * x * (1.0 + jnp.tanh(c * (x + 0.044715 * (x * x * x))))


def _proj_kernel(x_ref, w_ref, o_ref, *, act):
    y = jnp.dot(x_ref[...], w_ref[...], preferred_element_type=F32)
    if act == "gelu":
        y = _gelu_tanh(y)
    o_ref[...] = y.astype(o_ref.dtype)


def _proj(x, w, *, act, out_dtype, tm=512, tn=512):
    m, k = x.shape
    n = w.shape[1]
    return pl.pallas_call(
        functools.partial(_proj_kernel, act=act),
        grid=(n // tn, m // tm),
        in_specs=[pl.BlockSpec((tm, k), lambda j, i: (i, 0)),
                  pl.BlockSpec((k, tn), lambda j, i: (0, j))],
        out_specs=pl.BlockSpec((tm, tn), lambda j, i: (i, j)),
        out_shape=jax.ShapeDtypeStruct((m, n), out_dtype),
        compiler_params=_params(("parallel", "parallel"), 40),
        name="proj_" + str(act),
    )(x, w)


def _glu_kernel(x_ref, wv_ref, wg_ref, o_ref):
    x = x_ref[...]
    v = jnp.dot(x, wv_ref[...], preferred_element_type=F32)
    g = jnp.dot(x, wg_ref[...], preferred_element_type=F32)
    o_ref[...] = v * jax.nn.sigmoid(g)


def _glu_proj(x, w, *, tm=512, tn=512):
    m, k = x.shape
    n = w.shape[1] // 2
    nb = n // tn
    return pl.pallas_call(
        _glu_kernel,
        grid=(nb, m // tm),
        in_specs=[pl.BlockSpec((tm, k), lambda j, i: (i, 0)),
                  pl.BlockSpec((k, tn), lambda j, i: (0, j)),
                  pl.BlockSpec((k, tn), lambda j, i: (0, j + nb))],
        out_specs=pl.BlockSpec((tm, tn), lambda j, i: (i, j)),
        out_shape=jax.ShapeDtypeStruct((m, n), F32),
        compiler_params=_params(("parallel", "parallel"), 40),
        name="glu_proj",
    )(x, w, w)


def _conv_gln_kernel(cur_ref, prev_ref, w_ref, b_ref, g_ref, beta_ref, gm_ref, o_ref,
                     buf_ref, y_ref, *, tt, ch, width, rc, cc):
    i = pl.program_id(0)
    buf_ref[0:CONV_HALO, :] = jnp.where(i > 0, prev_ref[...], 0.0)
    buf_ref[CONV_HALO:, :] = cur_ref[...]
    off = CONV_HALO - (width - 1)
    for c0 in range(0, ch, cc):
        for r0 in range(0, tt, rc):
            acc = jnp.broadcast_to(b_ref[:, c0:c0 + cc], (rc, cc))
            for k in range(width):
                acc = acc + buf_ref[r0 + off + k:r0 + off + k + rc, c0:c0 + cc] * w_ref[k:k + 1, c0:c0 + cc]
            y_ref[r0:r0 + rc, c0:c0 + cc] = acc
    gm = gm_ref[...]

    def seg_mean(v):
        hi, lo = _split_bf16(v)
        return (jnp.dot(hi, gm, preferred_element_type=F32)
                + jnp.dot(lo, gm, preferred_element_type=F32))

    for c0 in range(0, ch, LANES):
        y = y_ref[:, c0:c0 + LANES]
        d = y - seg_mean(y)
        var = seg_mean(d * d)
        z = d * lax.rsqrt(var + LN_EPS) * g_ref[:, c0:c0 + LANES] + beta_ref[:, c0:c0 + LANES]
        o_ref[:, c0:c0 + LANES] = (z * jax.nn.sigmoid(z)).astype(o_ref.dtype)


def _conv_gln(u, w, b, g, beta, *, groups, tt=256):
    s, ch = u.shape
    width = w.shape[0]
    gsz = ch // groups
    assert LANES % gsz == 0 and width - 1 <= CONV_HALO and tt % CONV_HALO == 0
    wpad = jnp.zeros((CONV_HALO, ch), F32).at[:width].set(w)
    lane = jnp.arange(LANES)
    gm = jnp.where((lane[:, None] // gsz) == (lane[None, :] // gsz), 1.0 / gsz, 0.0).astype(BF16)
    hb = tt // CONV_HALO
    row = lambda v: v.reshape(1, ch)
    return pl.pallas_call(
        functools.partial(_conv_gln_kernel, tt=tt, ch=ch, width=width, rc=32, cc=256),
        grid=(s // tt,),
        in_specs=[pl.BlockSpec((tt, ch), lambda i: (i, 0)),
                  pl.BlockSpec((CONV_HALO, ch), lambda i: (jnp.maximum(i * hb - 1, 0), 0)),
                  pl.BlockSpec((CONV_HALO, ch), lambda i: (0, 0)),
                  pl.BlockSpec((1, ch), lambda i: (0, 0)),
                  pl.BlockSpec((1, ch), lambda i: (0, 0)),
                  pl.BlockSpec((1, ch), lambda i: (0, 0)),
                  pl.BlockSpec((LANES, LANES), lambda i: (0, 0))],
        out_specs=pl.BlockSpec((tt, ch), lambda i: (i, 0)),
        out_shape=jax.ShapeDtypeStruct((s, ch), BF16),
        scratch_shapes=[pltpu.VMEM((tt + CONV_HALO, ch), F32), pltpu.VMEM((tt, ch), F32)],
        compiler_params=_params(("parallel",), 32),
        name="conv_gln",
    )(u, u, wpad, row(b), row(g), row(beta), gm)


def _rope_table_kernel(pos_ref, invf_ref, cos_ref, sin_ref):
    ang = pos_ref[...].astype(F32) * invf_ref[...]
    cos_ref[...] = jnp.cos(ang)
    sin_ref[...] = jnp.sin(ang)


def _rope_tables(positions):
    s = positions.shape[0]
    half = QK_ROPE // 2
    per_row = LANES // half
    inv_freq = 1.0 / (ROPE_THETA ** (jnp.arange(0, QK_ROPE, 2, dtype=F32) / QK_ROPE))
    pos_rep = jnp.repeat(positions.reshape(s // per_row, per_row), half, axis=1)
    invf = jnp.tile(inv_freq, per_row).reshape(1, LANES)
    cos, sin = pl.pallas_call(
        _rope_table_kernel,
        out_shape=[jax.ShapeDtypeStruct((s // per_row, LANES), F32)] * 2,
        name="rope_table",
    )(pos_rep, invf)
    cos = cos.reshape(s, half)
    sin = sin.reshape(s, half)
    zero = jnp.zeros((s, LANES - QK_ROPE), F32)
    return jnp.concatenate([cos, cos, zero], axis=1), jnp.concatenate([-sin, sin, zero], axis=1)


def _mla_proj_kernel(x_ref, wm_ref, qg_ref, kvg_ref, wq_ref, wkv_ref, c_ref, s_ref,
                     qn_ref, qr_ref, kn_ref, kr_ref, v_ref, *, scale):
    nq = MLA_HEADS * QK_NOPE
    c = jnp.dot(x_ref[...], wm_ref[...], preferred_element_type=F32)
    cq = c[:, :Q_RANK]
    ckv = c[:, Q_RANK:Q_RANK + KV_RANK]
    kr_pad = c[:, Q_RANK + KV_RANK:Q_RANK + KV_RANK + LANES]
    kr_rot = c[:, Q_RANK + KV_RANK + LANES:]
    cqn = cq * lax.rsqrt(jnp.mean(cq * cq, axis=-1, keepdims=True) + RMS_EPS) * qg_ref[...]
    ckvn = ckv * lax.rsqrt(jnp.mean(ckv * ckv, axis=-1, keepdims=True) + RMS_EPS) * kvg_ref[...]
    q = jnp.dot(cqn.astype(BF16), wq_ref[...], preferred_element_type=F32)
    kv = jnp.dot(ckvn.astype(BF16), wkv_ref[...], preferred_element_type=F32)
    cosm = c_ref[...]
    sinm = s_ref[...]
    qn_ref[...] = (q[:, :nq] * scale).astype(BF16)
    for h in range(MLA_HEADS):
        lo = nq + h * LANES
        rot = q[:, lo:lo + LANES] * cosm + q[:, lo + nq:lo + nq + LANES] * sinm
        qr_ref[:, h * LANES:(h + 1) * LANES] = (rot * scale).astype(BF16)
    kn_ref[...] = kv[:, :nq].astype(BF16)
    v_ref[...] = kv[:, nq:].astype(BF16)
    kr_ref[...] = (kr_pad * cosm + kr_rot * sinm).astype(BF16)


def _rope_swap(w):
    half = QK_ROPE // 2
    return jnp.concatenate([w[..., half:], w[..., :half]], axis=-1)


def _mla_proj(xb, w_in_mla, qnorm_g, w_uq, kvnorm_g, w_ukv, cosm, sinm, *, tm=256):
    s, d = xb.shape
    nq = MLA_HEADS * QK_NOPE
    pad = lambda w: jnp.concatenate([w, jnp.zeros(w.shape[:-1] + (LANES - QK_ROPE,), w.dtype)], axis=-1)
    w_kr = w_in_mla[:, Q_RANK + KV_RANK:]
    wm = jnp.concatenate([w_in_mla[:, :Q_RANK + KV_RANK], pad(w_kr), pad(_rope_swap(w_kr))], axis=1).astype(BF16)
    wq3 = w_uq.reshape(Q_RANK, MLA_HEADS, QK_NOPE + QK_ROPE)
    wq_rope = wq3[:, :, QK_NOPE:]
    wq = jnp.concatenate([wq3[:, :, :QK_NOPE].reshape(Q_RANK, nq),
                          pad(wq_rope).reshape(Q_RANK, MLA_HEADS * LANES),
                          pad(_rope_swap(wq_rope)).reshape(Q_RANK, MLA_HEADS * LANES)], axis=1).astype(BF16)
    wkv3 = w_ukv.reshape(KV_RANK, MLA_HEADS, QK_NOPE + V_HEAD)
    wkv = jnp.concatenate([wkv3[:, :, :QK_NOPE].reshape(KV_RANK, nq),
                           wkv3[:, :, QK_NOPE:].reshape(KV_RANK, MLA_HEADS * V_HEAD)], axis=1).astype(BF16)
    scale = (QK_NOPE + QK_ROPE) ** -0.5 * math.log2(math.e)
    full = lambda a: pl.BlockSpec(a.shape, lambda i: (0,) * a.ndim)
    rows = lambda n: pl.BlockSpec((tm, n), lambda i: (i, 0))
    qg = qnorm_g.reshape(1, Q_RANK)
    kvg = kvnorm_g.reshape(1, KV_RANK)
    return pl.pallas_call(
        functools.partial(_mla_proj_kernel, scale=scale),
        grid=(s // tm,),
        in_specs=[rows(d), full(wm), full(qg), full(kvg), full(wq), full(wkv), rows(LANES), rows(LANES)],
        out_specs=[rows(nq), rows(MLA_HEADS * LANES), rows(nq), rows(LANES), rows(MLA_HEADS * V_HEAD)],
        out_shape=[jax.ShapeDtypeStruct((s, nq), BF16),
                   jax.ShapeDtypeStruct((s, MLA_HEADS * LANES), BF16),
                   jax.ShapeDtypeStruct((s, nq), BF16),
                   jax.ShapeDtypeStruct((s, LANES), BF16),
                   jax.ShapeDtypeStruct((s, MLA_HEADS * V_HEAD), BF16)],
        compiler_params=_params(("parallel",), 48),
        name="mla_proj",
    )(xb, wm, qg, kvg, wq, wkv, cosm, sinm)


def _attn_kernel(qn_ref, qr_ref, kn_ref, kr_ref, v_ref, o_ref, m_ref, acc_ref, *, tq, tk, hb):
    qi = pl.program_id(1)
    per_q = tq // tk
    m_ref[...] = jnp.full(m_ref.shape, -jnp.inf, F32)
    acc_ref[...] = jnp.zeros(acc_ref.shape, F32)
    ones = jnp.ones((tk, LANES), BF16)
    head_cols = [slice(h * LANES, (h + 1) * LANES) for h in range(hb)]
    qs = [jnp.concatenate([qn_ref[:, c], qr_ref[:, c]], axis=1) for c in head_cols]

    def chunk(j, diag):
        rows = pl.ds(pl.multiple_of(j * tk, tk), tk)
        k_rope = kr_ref[rows, :]
        for h, c in enumerate(head_cols):
            k = jnp.concatenate([kn_ref[rows, c], k_rope], axis=1)
            s = lax.dot_general(qs[h], k, (((1,), (1,)), ((), ())), preferred_element_type=F32)
            if diag is not None:
                keep = (lax.broadcasted_iota(jnp.int32, s.shape, 0)
                        >= lax.broadcasted_iota(jnp.int32, s.shape, 1) + diag * tk)
                s = jnp.where(keep, s, -jnp.inf)
            m_old = m_ref[h]
            m_new = jnp.maximum(m_old, jnp.max(s, axis=-1, keepdims=True))
            alpha = jnp.exp2(m_old - m_new)
            p = jnp.exp2(s - jnp.concatenate([m_new] * (tk // LANES), axis=1))
            v_ext = jnp.concatenate([v_ref[rows, c], ones], axis=1)
            acc_ref[h] = (jnp.concatenate([alpha, alpha], axis=1) * acc_ref[h]
                          + jnp.dot(p.astype(BF16), v_ext, preferred_element_type=F32))
            m_ref[h] = m_new

    def body(j, carry):
        chunk(j, None)
        return carry

    lax.fori_loop(0, qi * per_q, body, 0)
    for d in range(per_q):
        chunk(qi * per_q + d, d)
    for h, c in enumerate(head_cols):
        acc = acc_ref[h]
        o_ref[:, c] = (acc[:, :V_HEAD] / acc[:, V_HEAD:]).astype(o_ref.dtype)


def _attention(qn, qr, kn, kr, v, *, tq=1024, tk=512, hb=2):
    s = qn.shape[0]
    wide = hb * LANES
    tile = lambda: pl.BlockSpec((tq, wide), lambda h, i: (i, h))
    keys = lambda: pl.BlockSpec((s, wide), lambda h, i: (0, h))
    return pl.pallas_call(
        functools.partial(_attn_kernel, tq=tq, tk=tk, hb=hb),
        grid=(MLA_HEADS // hb, s // tq),
        in_specs=[tile(), tile(), keys(), pl.BlockSpec((s, LANES), lambda h, i: (0, 0)), keys()],
        out_specs=tile(),
        out_shape=jax.ShapeDtypeStruct((s, MLA_HEADS * V_HEAD), BF16),
        scratch_shapes=[pltpu.VMEM((hb, tq, LANES), F32), pltpu.VMEM((hb, tq, 2 * V_HEAD), F32)],
        compiler_params=_params(("parallel", "parallel"), 48),
        name="mla_attention",
    )(qn, qr, kn, kr, v)


def _route(logits):
    lane = lax.broadcasted_iota(jnp.int32, logits.shape, 1).astype(F32)
    big = float(2 * ROUTE_LANES)
    neg = -jnp.inf
    gl = jnp.where(lane < N_GROUPS, logits, neg)
    gmax = jnp.max(gl, axis=-1, keepdims=True)
    gsum = jnp.sum(jnp.exp(gl - gmax), axis=-1, keepdims=True)
    g_prob = 1.0 / gsum
    g_idx = jnp.min(jnp.where(gl == gmax, lane, big), axis=-1, keepdims=True)
    lo = N_GROUPS + g_idx * EXPERTS_PER_GROUP
    el = jnp.where((lane >= lo) & (lane < lo + EXPERTS_PER_GROUP), logits, neg)
    emax = jnp.max(el, axis=-1, keepdims=True)
    esum = jnp.sum(jnp.exp(el - emax), axis=-1, keepdims=True)
    i0 = jnp.min(jnp.where(el == emax, lane, big), axis=-1, keepdims=True)
    el2 = jnp.where(lane == i0, neg, el)
    emax2 = jnp.max(el2, axis=-1, keepdims=True)
    i1 = jnp.min(jnp.where(el2 == emax2, lane, big), axis=-1, keepdims=True)
    p0 = 1.0 / esum
    p1 = jnp.exp(emax2 - emax) / esum
    g0 = g_prob * p0 / (p0 + p1)
    g1 = g_prob * p1 / (p0 + p1)
    e0 = i0 - N_GROUPS
    e1 = i1 - N_GROUPS
    return jnp.where(lane == 0, e0, jnp.where(lane == 1, e1, jnp.where(lane == 2, g0, jnp.where(lane == 3, g1, 0.0))))


def _out_ln_route_kernel(a1_ref, a2_ref, w_ref, x_ref, g_ref, b_ref, wrh_ref, wrl_ref, br_ref,
                         x1_ref, route_ref):
    half = a1_ref.shape[1]
    m = (jnp.dot(a1_ref[...], w_ref[:half, :], preferred_element_type=F32)
         + jnp.dot(a2_ref[...], w_ref[half:, :], preferred_element_type=F32))
    x1 = _layer_norm_rows(ALPHA * x_ref[...] + m, g_ref[...], b_ref[...])
    x1_ref[...] = x1
    xh, xl = _split_bf16(x1)
    wrh = wrh_ref[...]
    logits = (jnp.dot(xh, wrh, preferred_element_type=F32) + jnp.dot(xl, wrh, preferred_element_type=F32)
              + jnp.dot(xh, wrl_ref[...], preferred_element_type=F32) + br_ref[...])
    route_ref[...] = _route(logits)


def _out_ln_route(a1, a2, w, x, ln_g, ln_b, w_grp, b_grp, w_exp, b_exp, *, col1=0, col2=0, tm=256):
    s, d = x.shape
    half = w.shape[0] // 2
    pad = ROUTE_LANES - N_GROUPS - N_EXPERTS
    wr = jnp.concatenate([w_grp, w_exp, jnp.zeros((d, pad), F32)], axis=1)
    br = jnp.concatenate([b_grp, b_exp, jnp.zeros((pad,), F32)]).reshape(1, ROUTE_LANES)
    wrh = wr.astype(BF16)
    wrl = (wr - wrh.astype(F32)).astype(BF16)
    full = lambda a: pl.BlockSpec(a.shape, lambda i: (0,) * a.ndim)
    rows = lambda n: pl.BlockSpec((tm, n), lambda i: (i, 0))
    g2 = ln_g.reshape(1, d)
    b2 = ln_b.reshape(1, d)
    return pl.pallas_call(
        _out_ln_route_kernel,
        grid=(s // tm,),
        in_specs=[pl.BlockSpec((tm, half), lambda i: (i, col1)), pl.BlockSpec((tm, half), lambda i: (i, col2)),
                  full(w), rows(d), full(g2), full(b2), full(wrh), full(wrl), full(br)],
        out_specs=[rows(d), rows(ROUTE_LANES)],
        out_shape=[jax.ShapeDtypeStruct((s, d), F32), jax.ShapeDtypeStruct((s, ROUTE_LANES), F32)],
        compiler_params=_params(("parallel",), 48),
        name="out_ln_route",
    )(a1, a2, w, x, g2, b2, wrh, wrl, br)


def _rank_kernel(route_ref, tri_ref, rank_ref, cnt_ref, carry_ref):
    @pl.when(pl.program_id(0) == 0)
    def _():
        carry_ref[...] = jnp.zeros(carry_ref.shape, F32)

    route = route_ref[...]
    lane = lax.broadcasted_iota(jnp.int32, route.shape, 1).astype(F32)
    oh0 = jnp.where(lane == route[:, 0:1], 1.0, 0.0)
    oh1 = jnp.where(lane == route[:, 1:2], 1.0, 0.0)
    both = oh0 + oh1
    before = jnp.dot(tri_ref[...], both.astype(BF16), preferred_element_type=F32) + carry_ref[...]
    r0 = jnp.sum(before * oh0, axis=-1, keepdims=True)
    r1 = jnp.sum(before * oh1, axis=-1, keepdims=True)
    rank_ref[...] = jnp.where(lane == 0.0, r0, jnp.where(lane == 1.0, r1, 0.0))
    carry_ref[...] = carry_ref[...] + jnp.sum(both, axis=0, keepdims=True)
    cnt_ref[...] = carry_ref[...]


def _dest_kernel(route_ref, rank_ref, pstart_ref, dest_ref):
    route = route_ref[...]
    rank = rank_ref[...]
    lane = lax.broadcasted_iota(jnp.int32, route.shape, 1).astype(F32)
    ps = pstart_ref[...]
    d0 = jnp.sum(jnp.where(lane == route[:, 0:1], ps, 0.0), axis=-1, keepdims=True) + rank[:, 0:1]
    d1 = jnp.sum(jnp.where(lane == route[:, 1:2], ps, 0.0), axis=-1, keepdims=True) + rank[:, 1:2]
    dest_ref[...] = jnp.where(lane == 0.0, d0, jnp.where(lane == 1.0, d1, 0.0)).astype(jnp.int32)


def _moe_plan(route, *, tb=512):
    s = route.shape[0]
    tri = (jnp.arange(tb)[:, None] > jnp.arange(tb)[None, :]).astype(BF16)
    rows = pl.BlockSpec((tb, ROUTE_LANES), lambda i: (i, 0))
    one = lambda: pl.BlockSpec((1, ROUTE_LANES), lambda i: (0, 0))
    rank, cnt = pl.pallas_call(
        _rank_kernel,
        grid=(s // tb,),
        in_specs=[rows, pl.BlockSpec((tb, tb), lambda i: (0, 0))],
        out_specs=[rows, one()],
        out_shape=[jax.ShapeDtypeStruct((s, ROUTE_LANES), F32), jax.ShapeDtypeStruct((1, ROUTE_LANES), F32)],
        scratch_shapes=[pltpu.VMEM((1, ROUTE_LANES), F32)],
        compiler_params=_params(("arbitrary",), 32),
        name="moe_rank",
    )(route, tri)
    counts = cnt[0, :N_EXPERTS].astype(jnp.int32)
    pcounts = (counts + MOE_ROWS - 1) // MOE_ROWS * MOE_ROWS
    pends = jnp.cumsum(pcounts)
    pstarts = pends - pcounts
    n_blocks = s * TOP_K // MOE_ROWS + N_EXPERTS
    n_used = pends[-1] // MOE_ROWS
    blk = jnp.minimum(jnp.arange(n_blocks, dtype=jnp.int32), n_used - 1) * MOE_ROWS
    block_expert = jnp.minimum(jnp.sum(pends[None, :] <= blk[:, None], axis=1), N_EXPERTS - 1).astype(jnp.int32)
    ps = jnp.zeros((1, ROUTE_LANES), F32).at[0, :N_EXPERTS].set(pstarts.astype(F32))
    dest = pl.pallas_call(
        _dest_kernel,
        grid=(s // tb,),
        in_specs=[rows, rows, one()],
        out_specs=rows,
        out_shape=jax.ShapeDtypeStruct((s, ROUTE_LANES), jnp.int32),
        compiler_params=_params(("parallel",), 32),
        name="moe_dest",
    )(route, rank, ps)
    dest = dest[:, :TOP_K].T.reshape(TOP_K * s)
    return dest, block_expert, n_used.astype(jnp.int32).reshape(1)


ROW_UNROLL = 8


def _dispatch_kernel(dest_ref, x_ref, xs_hbm, sems, *, tm, s):
    t0 = pl.program_id(0) * tm

    def issue(g, carry):
        for u in range(ROW_UNROLL):
            r = g * ROW_UNROLL + u
            src = x_ref.at[pl.ds(r, 1), :]
            pltpu.make_async_copy(src, xs_hbm.at[pl.ds(dest_ref[t0 + r], 1), :], sems.at[0]).start()
            pltpu.make_async_copy(src, xs_hbm.at[pl.ds(dest_ref[s + t0 + r], 1), :], sems.at[1]).start()
        return carry

    lax.fori_loop(0, tm // ROW_UNROLL, issue, 0)
    tile = xs_hbm.at[pl.ds(0, tm), :]
    pltpu.make_async_copy(x_ref, tile, sems.at[0]).wait()
    pltpu.make_async_copy(x_ref, tile, sems.at[1]).wait()


def _dispatch_rows(x, dest, n_blocks, *, tm=256):
    s, d = x.shape
    return pl.pallas_call(
        functools.partial(_dispatch_kernel, tm=tm, s=s),
        grid_spec=pltpu.PrefetchScalarGridSpec(
            num_scalar_prefetch=1,
            grid=(s // tm,),
            in_specs=[pl.BlockSpec((tm, d), lambda i, dest: (i, 0))],
            out_specs=pl.BlockSpec(memory_space=pl.ANY),
            scratch_shapes=[pltpu.SemaphoreType.DMA((2,))]),
        out_shape=jax.ShapeDtypeStruct((n_blocks * MOE_ROWS, d), x.dtype),
        compiler_params=_params(("arbitrary",), 32),
        name="moe_dispatch",
    )(dest, x)


def _ffn_kernel(be_ref, nused_ref, xs_ref, w1_ref, w3_ref, w2_ref, ys_ref):
    b = pl.program_id(0)

    @pl.when(b < nused_ref[0])
    def _():
        x = xs_ref[...].astype(BF16)
        h1 = jnp.dot(x, w1_ref[0].astype(BF16), preferred_element_type=F32)
        h3 = jnp.dot(x, w3_ref[0].astype(BF16), preferred_element_type=F32)
        hmid = (h1 * jax.nn.sigmoid(h1) * h3).astype(BF16)
        ys_ref[...] = jnp.dot(hmid, w2_ref[0].astype(BF16), preferred_element_type=F32)


def _grouped_ffn(xs, w1, w3, w2, layer, block_expert, n_used):
    d = xs.shape[1]
    de = w1.shape[3]
    blk = lambda b, be, nu: (jnp.minimum(b, nu[0] - 1), 0)
    wsel = lambda b, be, nu: (layer, be[b], 0, 0)
    return pl.pallas_call(
        _ffn_kernel,
        grid_spec=pltpu.PrefetchScalarGridSpec(
            num_scalar_prefetch=2,
            grid=(xs.shape[0] // MOE_ROWS,),
            in_specs=[pl.BlockSpec((MOE_ROWS, d), blk),
                      pl.BlockSpec((None, 1, d, de), wsel),
                      pl.BlockSpec((None, 1, d, de), wsel),
                      pl.BlockSpec((None, 1, de, d), wsel)],
            out_specs=pl.BlockSpec((MOE_ROWS, d), blk)),
        out_shape=jax.ShapeDtypeStruct(xs.shape, F32),
        compiler_params=_params(("arbitrary",), 56),
        name="moe_ffn",
    )(block_expert, n_used, xs, w1, w3, w2)


def _combine_ln_kernel(pos_ref, ys_hbm, x1_ref, route_ref, g_ref, b_ref, x2_ref, x2b_ref,
                       y0_ref, y1_ref, sems, *, tm, s):
    t0 = pl.program_id(0) * tm

    def issue(g, carry):
        for u in range(ROW_UNROLL):
            r = g * ROW_UNROLL + u
            dst = pl.ds(r, 1)
            pltpu.make_async_copy(ys_hbm.at[pl.ds(pos_ref[t0 + r], 1), :], y0_ref.at[dst, :], sems.at[0]).start()
            pltpu.make_async_copy(ys_hbm.at[pl.ds(pos_ref[s + t0 + r], 1), :], y1_ref.at[dst, :], sems.at[1]).start()
        return carry

    lax.fori_loop(0, tm // ROW_UNROLL, issue, 0)
    tile = ys_hbm.at[pl.ds(0, tm), :]
    pltpu.make_async_copy(tile, y0_ref, sems.at[0]).wait()
    pltpu.make_async_copy(tile, y1_ref, sems.at[1]).wait()
    route = route_ref[...]
    f = route[:, 2:3] * y0_ref[...] + route[:, 3:4] * y1_ref[...]
    x2 = _layer_norm_rows(ALPHA * x1_ref[...] + f, g_ref[...], b_ref[...])
    x2_ref[...] = x2
    x2b_ref[...] = x2.astype(BF16)


def _combine_ln(ys, pos, x1, route, ln_g, ln_b, *, tm=256):
    s, d = x1.shape
    rows = lambda n: pl.BlockSpec((tm, n), lambda i, pos: (i, 0))
    full = lambda n: pl.BlockSpec((1, n), lambda i, pos: (0, 0))
    return pl.pallas_call(
        functools.partial(_combine_ln_kernel, tm=tm, s=s),
        grid_spec=pltpu.PrefetchScalarGridSpec(
            num_scalar_prefetch=1,
            grid=(s // tm,),
            in_specs=[pl.BlockSpec(memory_space=pl.ANY), rows(d), rows(ROUTE_LANES), full(d), full(d)],
            out_specs=[rows(d), rows(d)],
            scratch_shapes=[pltpu.VMEM((tm, d), F32), pltpu.VMEM((tm, d), F32),
                            pltpu.SemaphoreType.DMA((2,))]),
        out_shape=[jax.ShapeDtypeStruct((s, d), F32), jax.ShapeDtypeStruct((s, d), BF16)],
        compiler_params=_params(("arbitrary",), 40),
        name="moe_combine_ln",
    )(pos, ys, x1, route, ln_g.reshape(1, d), ln_b.reshape(1, d))


def _ple_kernel(x_ref, xb_ref, wg_ref, bg_ref, p_ref, wp_ref, o_ref, ob_ref):
    gate = jax.nn.sigmoid(jnp.dot(xb_ref[...], wg_ref[...], preferred_element_type=F32) + bg_ref[...])
    emb = jnp.dot(p_ref[...].astype(BF16), wp_ref[...], preferred_element_type=F32)
    out = x_ref[...] + gate * emb
    o_ref[...] = out
    ob_ref[...] = out.astype(BF16)


def _ple(x, xb, w_gate, b_gate, p, w_proj, *, tm=256):
    s, d = x.shape
    pd = p.shape[1]
    rows = lambda n: pl.BlockSpec((tm, n), lambda i: (i, 0))
    full = lambda a: pl.BlockSpec(a.shape, lambda i: (0,) * a.ndim)
    bg = b_gate.reshape(1, d)
    return pl.pallas_call(
        _ple_kernel,
        grid=(s // tm,),
        in_specs=[rows(d), rows(d), full(w_gate), full(bg), rows(pd), full(w_proj)],
        out_specs=[rows(d), rows(d)],
        out_shape=[jax.ShapeDtypeStruct((s, d), F32), jax.ShapeDtypeStruct((s, d), BF16)],
        compiler_params=_params(("parallel",), 48),
        name="ple",
    )(x, xb, w_gate, bg, p, w_proj)


def _rglru_kernel(cur_ref, prev_ref, gate_ref, cw_ref, cb_ref, wa_ref, ba_ref, wi_ref, bi_ref, lam_ref,
                  o_ref, buf_ref, a_ref, b_ref, h_ref, carry_ref, *, tt, tc, width):
    i = pl.program_id(1)

    @pl.when(i == 0)
    def _():
        carry_ref[...] = jnp.zeros(carry_ref.shape, F32)

    buf_ref[0:RNN_HALO, :] = jnp.where(i > 0, prev_ref[...], 0.0)
    buf_ref[RNN_HALO:, :] = cur_ref[...]
    off = RNN_HALO - (width - 1)
    xc = jnp.broadcast_to(cb_ref[...], (tt, tc))
    for k in range(width):
        xc = xc + buf_ref[off + k:off + k + tt, :] * cw_ref[k:k + 1, :]
    z = -lam_ref[...]
    softplus = jnp.log1p(jnp.exp(-jnp.abs(z))) + jnp.maximum(z, 0.0)
    for hh in range(tc // RNN_HEAD_DIM):
        cols = slice(hh * RNN_HEAD_DIM, (hh + 1) * RNN_HEAD_DIM)
        xh = xc[:, cols]
        xhb = xh.astype(BF16)
        r = jax.nn.sigmoid(jnp.dot(xhb, wa_ref[hh], preferred_element_type=F32) + ba_ref[:, cols])
        ig = jax.nn.sigmoid(jnp.dot(xhb, wi_ref[hh], preferred_element_type=F32) + bi_ref[:, cols])
        log_a = (-RG_C) * r * softplus[:, cols]
        a = jnp.exp(log_a)
        bval = jnp.sqrt(-jnp.tanh(log_a) * (a * a + 1.0)) * (ig * xh)
        a_ref[:, cols] = a
        b_ref[:, cols] = bval

    def group(gi, h):
        base = pl.multiple_of(gi * 8, 8)
        for r in range(8):
            row = pl.ds(base + r, 1)
            h = a_ref[row, :] * h + b_ref[row, :]
            h_ref[row, :] = h
        return h

    carry_ref[...] = lax.fori_loop(0, tt // 8, group, carry_ref[...])
    o_ref[...] = (gate_ref[...].astype(F32) * h_ref[...]).astype(o_ref.dtype)


def _rglru(xr, gate, conv_w, conv_b, w_a, b_a, w_i, b_i, lam, *, tt=256, tc=512):
    s, wd = xr.shape
    width = conv_w.shape[0]
    hpc = tc // RNN_HEAD_DIM
    hb = tt // RNN_HALO
    row = lambda v: v.reshape(1, wd)
    cvec = lambda: pl.BlockSpec((1, tc), lambda c, i: (0, c))
    heads = lambda: pl.BlockSpec((hpc, RNN_HEAD_DIM, RNN_HEAD_DIM), lambda c, i: (c, 0, 0))
    tile = lambda: pl.BlockSpec((tt, tc), lambda c, i: (i, c))
    return pl.pallas_call(
        functools.partial(_rglru_kernel, tt=tt, tc=tc, width=width),
        grid=(wd // tc, s // tt),
        in_specs=[tile(),
                  pl.BlockSpec((RNN_HALO, tc), lambda c, i: (jnp.maximum(i * hb - 1, 0), c)),
                  tile(),
                  pl.BlockSpec((width, tc), lambda c, i: (0, c)),
                  cvec(), heads(), cvec(), heads(), cvec(), cvec()],
        out_specs=tile(),
        out_shape=jax.ShapeDtypeStruct((s, wd), BF16),
        scratch_shapes=[pltpu.VMEM((tt + RNN_HALO, tc), F32), pltpu.VMEM((tt, tc), F32),
                        pltpu.VMEM((tt, tc), F32), pltpu.VMEM((tt, tc), F32), pltpu.VMEM((1, tc), F32)],
        compiler_params=_params(("parallel", "arbitrary"), 32),
        name="rglru",
    )(xr, xr, gate, conv_w, row(conv_b), w_a.astype(BF16), row(b_a), w_i.astype(BF16), row(b_i), row(lam))


def _moe_and_ple(x1, route, i, p, ln_ffn_g, ln_ffn_b, moe_w1, moe_w3, moe_w2, ple_w_proj, ple_w_gate, ple_b_gate):
    dest, block_expert, n_used = _moe_plan(route)
    xs = _dispatch_rows(x1, dest, block_expert.shape[0])
    ys = _grouped_ffn(xs, moe_w1, moe_w3, moe_w2, i, block_expert, n_used)
    x2, x2b = _combine_ln(ys, dest, x1, route, ln_ffn_g[i], ln_ffn_b[i])
    return _ple(x2, x2b, ple_w_gate[i].astype(BF16), ple_b_gate[i], p[i, 0], ple_w_proj[i].astype(BF16))


def kernel(x, p, positions, ev_w_in, ev_conv_w, ev_conv_b, ev_cnorm_g, ev_cnorm_b, ev_qnorm_g, ev_w_uq, ev_kvnorm_g, ev_w_ukv, ev_w_out, od_w_in, od_conv_w, od_conv_b, od_w_a, od_b_a, od_w_i, od_b_i, od_lam, od_w_out, ln_mix_g, ln_mix_b, ln_ffn_g, ln_ffn_b, moe_w_grp, moe_b_grp, moe_w_exp, moe_b_exp, moe_w1, moe_w3, moe_w2, ple_w_proj, ple_w_gate, ple_b_gate):
    x0 = x[0]
    x0b = x0.astype(BF16)
    moe = functools.partial(_moe_and_ple, p=p, ln_ffn_g=ln_ffn_g, ln_ffn_b=ln_ffn_b, moe_w1=moe_w1,
                            moe_w3=moe_w3, moe_w2=moe_w2, ple_w_proj=ple_w_proj, ple_w_gate=ple_w_gate,
                            ple_b_gate=ple_b_gate)

    w_in = ev_w_in[0]
    u = _glu_proj(x0b, w_in[:, :2 * CONV_CH].astype(BF16))
    ub = _conv_gln(u, ev_conv_w[0], ev_conv_b[0], ev_cnorm_g[0], ev_cnorm_b[0], groups=CONV_GROUPS)
    cosm, sinm = _rope_tables(positions[0])
    qn, qr, kn, kr, v = _mla_proj(x0b, w_in[:, 2 * CONV_CH:], ev_qnorm_g[0], ev_w_uq[0], ev_kvnorm_g[0],
                                  ev_w_ukv[0], cosm, sinm)
    att = _attention(qn, qr, kn, kr, v)
    x1, route = _out_ln_route(ub, att, ev_w_out[0].astype(BF16), x0, ln_mix_g[0], ln_mix_b[0],
                              moe_w_grp[0], moe_b_grp[0], moe_w_exp[0], moe_b_exp[0])
    x3, x3b = moe(x1, route, 0)

    w_in1 = od_w_in[0]
    gate = _proj(x3b, w_in1[:, :RNN_WIDTH].astype(BF16), act="gelu", out_dtype=BF16)
    xr = _proj(x3b, w_in1[:, RNN_WIDTH:].astype(BF16), act=None, out_dtype=F32)
    y = _rglru(xr, gate, od_conv_w[0], od_conv_b[0], od_w_a[0], od_b_a[0], od_w_i[0], od_b_i[0], od_lam[0])
    x4, route1 = _out_ln_route(y, y, od_w_out[0].astype(BF16), x3, ln_mix_g[1], ln_mix_b[1],
                               moe_w_grp[1], moe_b_grp[1], moe_w_exp[1], moe_b_exp[1], col1=0, col2=1)
    x6, _ = moe(x4, route1, 1)
    return x6[None]
```

```python
import functools
import math

import jax
import jax.numpy as jnp
from jax import lax
from jax.experimental import pallas as pl
from jax.experimental.pallas import tpu as pltpu

F32 = jnp.float32
BF16 = jnp.bfloat16

D_MODEL = 2048
SEQ = 8192
DEPTH = 2
CONV_CH = 1024
CONV_GROUPS = 16
CONV_WIDTH = 31
MLA_HEADS = 8
QK_NOPE = 128
QK_ROPE = 64
V_HEAD = 128
Q_RANK = 512
KV_RANK = 256
ROPE_THETA = 10000.0
RNN_WIDTH = 2048
RNN_HEADS = 16
RNN_HEAD_DIM = RNN_WIDTH // RNN_HEADS
RNN_CONV_WIDTH = 4
RG_C = 8.0
N_GROUPS = 8
EXPERTS_PER_GROUP = 8
N_EXPERTS = N_GROUPS * EXPERTS_PER_GROUP
TOP_K = 2
D_EXPERT = 512
PLE_DIM = 256
ALPHA = (2 * DEPTH) ** 0.25
LN_EPS = 1e-5
RMS_EPS = 1e-6

LANES = 128
CONV_HALO = 32
RNN_HALO = 8
MOE_ROWS = 256
ROUTE_LANES = 128
MIB = 2 ** 20


def _params(semantics, vmem_mib):
    return pltpu.CompilerParams(dimension_semantics=semantics, vmem_limit_bytes=vmem_mib * MIB)


def _layer_norm_rows(z, g, b):
    mu = jnp.mean(z, axis=-1, keepdims=True)
    d = z - mu
    var = jnp.mean(d * d, axis=-1, keepdims=True)
    return d * lax.rsqrt(var + LN_EPS) * g + b


def _pack_bf16_pair(a, b):
    def rounded_bits(v):
        u = lax.bitcast_convert_type(v, jnp.uint32)
        return u + jnp.uint32(0x7FFF) + ((u >> 16) & jnp.uint32(1))

    return (rounded_bits(a) & jnp.uint32(0xFFFF0000)) | (rounded_bits(b) >> 16)


def _unpack_bf16_pair(u):
    a = lax.bitcast_convert_type(u & jnp.uint32(0xFFFF0000), F32)
    b = lax.bitcast_convert_type(u << 16, F32)
    return a, b


def _split_bf16(v):
    hi = v.astype(BF16)
    lo = (v - hi.astype(F32)).astype(BF16)
    return hi, lo


def _gelu_tanh(x):
    c = math.sqrt(2.0 / math.pi)
    return 0.5 * x * (1.0 + jnp.tanh(c * (x + 0.044715 * (x * x * x))))


def _proj_kernel(x_ref, w_ref, o_ref, *, act):
    y = jnp.dot(x_ref[...], w_ref[...], preferred_element_type=F32)
    if act == "gelu":
        y = _gelu_tanh(y)
    o_ref[...] = y.astype(o_ref.dtype)


def _proj(x, w, *, act, out_dtype, tm=512, tn=512):
    m, k = x.shape
    n = w.shape[1]
    return pl.pallas_call(
        functools.partial(_proj_kernel, act=act),
        grid=(n // tn, m // tm),
        in_specs=[pl.BlockSpec((tm, k), lambda j, i: (i, 0)),
                  pl.BlockSpec((k, tn), lambda j, i: (0, j))],
        out_specs=pl.BlockSpec((tm, tn), lambda j, i: (i, j)),
        out_shape=jax.ShapeDtypeStruct((m, n), out_dtype),
        compiler_params=_params(("parallel", "parallel"), 40),
        name="proj_" + str(act),
    )(x, w)


def _glu_kernel(x_ref, wv_ref, wg_ref, o_ref):
    x = x_ref[...]
    v = jnp.dot(x, wv_ref[...], preferred_element_type=F32)
    g = jnp.dot(x, wg_ref[...], preferred_element_type=F32)
    o_ref[...] = v * jax.nn.sigmoid(g)


def _glu_proj(x, w, *, tm=512, tn=512):
    m, k = x.shape
    n = w.shape[1] // 2
    nb = n // tn
    return pl.pallas_call(
        _glu_kernel,
        grid=(nb, m // tm),
        in_specs=[pl.BlockSpec((tm, k), lambda j, i: (i, 0)),
                  pl.BlockSpec((k, tn), lambda j, i: (0, j)),
                  pl.BlockSpec((k, tn), lambda j, i: (0, j + nb))],
        out_specs=pl.BlockSpec((tm, tn), lambda j, i: (i, j)),
        out_shape=jax.ShapeDtypeStruct((m, n), F32),
        compiler_params=_params(("parallel", "parallel"), 40),
        name="glu_proj",
    )(x, w, w)


def _conv_gln_kernel(cur_ref, prev_ref, w_ref, b_ref, g_ref, beta_ref, gm_ref, o_ref,
                     buf_ref, y_ref, *, tt, ch, width, rc, cc):
    i = pl.program_id(0)
    buf_ref[0:CONV_HALO, :] = jnp.where(i > 0, prev_ref[...], 0.0)
    buf_ref[CONV_HALO:, :] = cur_ref[...]
    off = CONV_HALO - (width - 1)
    for c0 in range(0, ch, cc):
        for r0 in range(0, tt, rc):
            acc = jnp.broadcast_to(b_ref[:, c0:c0 + cc], (rc, cc))
            for k in range(width):
                acc = acc + buf_ref[r0 + off + k:r0 + off + k + rc, c0:c0 + cc] * w_ref[k:k + 1, c0:c0 + cc]
            y_ref[r0:r0 + rc, c0:c0 + cc] = acc
    gm = gm_ref[...]

    def seg_mean(v):
        hi, lo = _split_bf16(v)
        return (jnp.dot(hi, gm, preferred_element_type=F32)
                + jnp.dot(lo, gm, preferred_element_type=F32))

    for c0 in range(0, ch, LANES):
        y = y_ref[:, c0:c0 + LANES]
        d = y - seg_mean(y)
        var = seg_mean(d * d)
        z = d * lax.rsqrt(var + LN_EPS) * g_ref[:, c0:c0 + LANES] + beta_ref[:, c0:c0 + LANES]
        o_ref[:, c0:c0 + LANES] = (z * jax.nn.sigmoid(z)).astype(o_ref.dtype)


def _conv_gln(u, w, b, g, beta, *, groups, tt=256):
    s, ch = u.shape
    width = w.shape[0]
    gsz = ch // groups
    assert LANES % gsz == 0 and width - 1 <= CONV_HALO and tt % CONV_HALO == 0
    wpad = jnp.zeros((CONV_HALO, ch), F32).at[:width].set(w)
    lane = jnp.arange(LANES)
    gm = jnp.where((lane[:, None] // gsz) == (lane[None, :] // gsz), 1.0 / gsz, 0.0).astype(BF16)
    hb = tt // CONV_HALO
    row = lambda v: v.reshape(1, ch)
    return pl.pallas_call(
        functools.partial(_conv_gln_kernel, tt=tt, ch=ch, width=width, rc=32, cc=256),
        grid=(s // tt,),
        in_specs=[pl.BlockSpec((tt, ch), lambda i: (i, 0)),
                  pl.BlockSpec((CONV_HALO, ch), lambda i: (jnp.maximum(i * hb - 1, 0), 0)),
                  pl.BlockSpec((CONV_HALO, ch), lambda i: (0, 0)),
                  pl.BlockSpec((1, ch), lambda i: (0, 0)),
                  pl.BlockSpec((1, ch), lambda i: (0, 0)),
                  pl.BlockSpec((1, ch), lambda i: (0, 0)),
                  pl.BlockSpec((LANES, LANES), lambda i: (0, 0))],
        out_specs=pl.BlockSpec((tt, ch), lambda i: (i, 0)),
        out_shape=jax.ShapeDtypeStruct((s, ch), BF16),
        scratch_shapes=[pltpu.VMEM((tt + CONV_HALO, ch), F32), pltpu.VMEM((tt, ch), F32)],
        compiler_params=_params(("parallel",), 32),
        name="conv_gln",
    )(u, u, wpad, row(b), row(g), row(beta), gm)


def _rope_table_kernel(pos_ref, invf_ref, cos_ref, sin_ref):
    ang = pos_ref[...].astype(F32) * invf_ref[...]
    cos_ref[...] = jnp.cos(ang)
    sin_ref[...] = jnp.sin(ang)


def _rope_tables(positions):
    s = positions.shape[0]
    half = QK_ROPE // 2
    per_row = LANES // half
    inv_freq = 1.0 / (ROPE_THETA ** (jnp.arange(0, QK_ROPE, 2, dtype=F32) / QK_ROPE))
    pos_rep = jnp.repeat(positions.reshape(s // per_row, per_row), half, axis=1)
    invf = jnp.tile(inv_freq, per_row).reshape(1, LANES)
    cos, sin = pl.pallas_call(
        _rope_table_kernel,
        out_shape=[jax.ShapeDtypeStruct((s // per_row, LANES), F32)] * 2,
        name="rope_table",
    )(pos_rep, invf)
    cos = cos.reshape(s, half)
    sin = sin.reshape(s, half)
    zero = jnp.zeros((s, LANES - QK_ROPE), F32)
    return jnp.concatenate([cos, cos, zero], axis=1), jnp.concatenate([-sin, sin, zero], axis=1)


def _mla_proj_kernel(x_ref, wm_ref, qg_ref, kvg_ref, wq_ref, wkv_ref, c_ref, s_ref,
                     qn_ref, qr_ref, kn_ref, kr_ref, v_ref, *, scale):
    nq = MLA_HEADS * QK_NOPE
    c = jnp.dot(x_ref[...], wm_ref[...], preferred_element_type=F32)
    cq = c[:, :Q_RANK]
    ckv = c[:, Q_RANK:Q_RANK + KV_RANK]
    kr_pad = c[:, Q_RANK + KV_RANK:Q_RANK + KV_RANK + LANES]
    kr_rot = c[:, Q_RANK + KV_RANK + LANES:]
    cqn = cq * lax.rsqrt(jnp.mean(cq * cq, axis=-1, keepdims=True) + RMS_EPS) * qg_ref[...]
    ckvn = ckv * lax.rsqrt(jnp.mean(ckv * ckv, axis=-1, keepdims=True) + RMS_EPS) * kvg_ref[...]
    q = jnp.dot(cqn.astype(BF16), wq_ref[...], preferred_element_type=F32)
    kv = jnp.dot(ckvn.astype(BF16), wkv_ref[...], preferred_element_type=F32)
    cosm = c_ref[...]
    sinm = s_ref[...]
    qn_ref[...] = (q[:, :nq] * scale).astype(BF16)
    for h in range(MLA_HEADS):
        lo = nq + h * LANES
        rot = q[:, lo:lo + LANES] * cosm + q[:, lo + nq:lo + nq + LANES] * sinm
        qr_ref[:, h * LANES:(h + 1) * LANES] = (rot * scale).astype(BF16)
    kn_ref[...] = kv[:, :nq].astype(BF16)
    v_ref[...] = kv[:, nq:].astype(BF16)
    kr_ref[...] = (kr_pad * cosm + kr_rot * sinm).astype(BF16)


def _rope_swap(w):
    half = QK_ROPE // 2
    return jnp.concatenate([w[..., half:], w[..., :half]], axis=-1)


def _mla_proj(xb, w_in_mla, qnorm_g, w_uq, kvnorm_g, w_ukv, cosm, sinm, *, tm=256):
    s, d = xb.shape
    nq = MLA_HEADS * QK_NOPE
    pad = lambda w: jnp.concatenate([w, jnp.zeros(w.shape[:-1] + (LANES - QK_ROPE,), w.dtype)], axis=-1)
    w_kr = w_in_mla[:, Q_RANK + KV_RANK:]
    wm = jnp.concatenate([w_in_mla[:, :Q_RANK + KV_RANK], pad(w_kr), pad(_rope_swap(w_kr))], axis=1).astype(BF16)
    wq3 = w_uq.reshape(Q_RANK, MLA_HEADS, QK_NOPE + QK_ROPE)
    wq_rope = wq3[:, :, QK_NOPE:]
    wq = jnp.concatenate([wq3[:, :, :QK_NOPE].reshape(Q_RANK, nq),
                          pad(wq_rope).reshape(Q_RANK, MLA_HEADS * LANES),
                          pad(_rope_swap(wq_rope)).reshape(Q_RANK, MLA_HEADS * LANES)], axis=1).astype(BF16)
    wkv3 = w_ukv.reshape(KV_RANK, MLA_HEADS, QK_NOPE + V_HEAD)
    wkv = jnp.concatenate([wkv3[:, :, :QK_NOPE].reshape(KV_RANK, nq),
                           wkv3[:, :, QK_NOPE:].reshape(KV_RANK, MLA_HEADS * V_HEAD)], axis=1).astype(BF16)
    scale = (QK_NOPE + QK_ROPE) ** -0.5 * math.log2(math.e)
    full = lambda a: pl.BlockSpec(a.shape, lambda i: (0,) * a.ndim)
    rows = lambda n: pl.BlockSpec((tm, n), lambda i: (i, 0))
    qg = qnorm_g.reshape(1, Q_RANK)
    kvg = kvnorm_g.reshape(1, KV_RANK)
    return pl.pallas_call(
        functools.partial(_mla_proj_kernel, scale=scale),
        grid=(s // tm,),
        in_specs=[rows(d), full(wm), full(qg), full(kvg), full(wq), full(wkv), rows(LANES), rows(LANES)],
        out_specs=[rows(nq), rows(MLA_HEADS * LANES), rows(nq), rows(LANES), rows(MLA_HEADS * V_HEAD)],
        out_shape=[jax.ShapeDtypeStruct((s, nq), BF16),
                   jax.ShapeDtypeStruct((s, MLA_HEADS * LANES), BF16),
                   jax.ShapeDtypeStruct((s, nq), BF16),
                   jax.ShapeDtypeStruct((s, LANES), BF16),
                   jax.ShapeDtypeStruct((s, MLA_HEADS * V_HEAD), BF16)],
        compiler_params=_params(("parallel",), 48),
        name="mla_proj",
    )(xb, wm, qg, kvg, wq, wkv, cosm, sinm)


def _attn_kernel(qn_ref, qr_ref, kn_ref, kr_ref, v_ref, o_ref, m_ref, acc_ref, *, tq, tk, hb):
    qi = pl.program_id(1)
    per_q = tq // tk
    m_ref[...] = jnp.full(m_ref.shape, -jnp.inf, F32)
    acc_ref[...] = jnp.zeros(acc_ref.shape, F32)
    ones = jnp.ones((tk, LANES), BF16)
    head_cols = [slice(h * LANES, (h + 1) * LANES) for h in range(hb)]
    qs = [jnp.concatenate([qn_ref[:, c], qr_ref[:, c]], axis=1) for c in head_cols]

    def chunk(j, diag):
        rows = pl.ds(pl.multiple_of(j * tk, tk), tk)
        k_rope = kr_ref[rows, :]
        for h, c in enumerate(head_cols):
            k = jnp.concatenate([kn_ref[rows, c], k_rope], axis=1)
            s = lax.dot_general(qs[h], k, (((1,), (1,)), ((), ())), preferred_element_type=F32)
            if diag is not None:
                keep = (lax.broadcasted_iota(jnp.int32, s.shape, 0)
                        >= lax.broadcasted_iota(jnp.int32, s.shape, 1) + diag * tk)
                s = jnp.where(keep, s, -jnp.inf)
            m_old = m_ref[h]
            m_new = jnp.maximum(m_old, jnp.max(s, axis=-1, keepdims=True))
            alpha = jnp.exp2(m_old - m_new)
            p = jnp.exp2(s - jnp.concatenate([m_new] * (tk // LANES), axis=1))
            v_ext = jnp.concatenate([v_ref[rows, c], ones], axis=1)
            acc_ref[h] = (jnp.concatenate([alpha, alpha], axis=1) * acc_ref[h]
                          + jnp.dot(p.astype(BF16), v_ext, preferred_element_type=F32))
            m_ref[h] = m_new

    def body(j, carry):
        chunk(j, None)
        return carry

    lax.fori_loop(0, qi * per_q, body, 0)
    for d in range(per_q):
        chunk(qi * per_q + d, d)
    for h, c in enumerate(head_cols):
        acc = acc_ref[h]
        o_ref[:, c] = (acc[:, :V_HEAD] / acc[:, V_HEAD:]).astype(o_ref.dtype)


def _attention(qn, qr, kn, kr, v, *, tq=1024, tk=512, hb=2):
    s = qn.shape[0]
    wide = hb * LANES
    tile = lambda: pl.BlockSpec((tq, wide), lambda h, i: (i, h))
    keys = lambda: pl.BlockSpec((s, wide), lambda h, i: (0, h))
    return pl.pallas_call(
        functools.partial(_attn_kernel, tq=tq, tk=tk, hb=hb),
        grid=(MLA_HEADS // hb, s // tq),
        in_specs=[tile(), tile(), keys(), pl.BlockSpec((s, LANES), lambda h, i: (0, 0)), keys()],
        out_specs=tile(),
        out_shape=jax.ShapeDtypeStruct((s, MLA_HEADS * V_HEAD), BF16),
        scratch_shapes=[pltpu.VMEM((hb, tq, LANES), F32), pltpu.VMEM((hb, tq, 2 * V_HEAD), F32)],
        compiler_params=_params(("parallel", "parallel"), 48),
        name="mla_attention",
    )(qn, qr, kn, kr, v)


def _route(logits):
    lane = lax.broadcasted_iota(jnp.int32, logits.shape, 1).astype(F32)
    big = float(2 * ROUTE_LANES)
    neg = -jnp.inf
    gl = jnp.where(lane < N_GROUPS, logits, neg)
    gmax = jnp.max(gl, axis=-1, keepdims=True)
    gsum = jnp.sum(jnp.exp(gl - gmax), axis=-1, keepdims=True)
    g_prob = 1.0 / gsum
    g_idx = jnp.min(jnp.where(gl == gmax, lane, big), axis=-1, keepdims=True)
    lo = N_GROUPS + g_idx * EXPERTS_PER_GROUP
    el = jnp.where((lane >= lo) & (lane < lo + EXPERTS_PER_GROUP), logits, neg)
    emax = jnp.max(el, axis=-1, keepdims=True)
    esum = jnp.sum(jnp.exp(el - emax), axis=-1, keepdims=True)
    i0 = jnp.min(jnp.where(el == emax, lane, big), axis=-1, keepdims=True)
    el2 = jnp.where(lane == i0, neg, el)
    emax2 = jnp.max(el2, axis=-1, keepdims=True)
    i1 = jnp.min(jnp.where(el2 == emax2, lane, big), axis=-1, keepdims=True)
    p0 = 1.0 / esum
    p1 = jnp.exp(emax2 - emax) / esum
    g0 = g_prob * p0 / (p0 + p1)
    g1 = g_prob * p1 / (p0 + p1)
    e0 = i0 - N_GROUPS
    e1 = i1 - N_GROUPS
    return jnp.where(lane == 0, e0, jnp.where(lane == 1, e1, jnp.where(lane == 2, g0, jnp.where(lane == 3, g1, 0.0))))


def _out_ln_route_kernel(a1_ref, a2_ref, w_ref, x_ref, g_ref, b_ref, wrh_ref, wrl_ref, br_ref,
                         x1_ref, x1p_ref, route_ref):
    half = a1_ref.shape[1]
    m = (jnp.dot(a1_ref[...], w_ref[:half, :], preferred_element_type=F32)
         + jnp.dot(a2_ref[...], w_ref[half:, :], preferred_element_type=F32))
    x1 = _layer_norm_rows(ALPHA * x_ref[...] + m, g_ref[...], b_ref[...])
    x1_ref[...] = x1
    hd = x1.shape[1] // 2
    x1p_ref[...] = _pack_bf16_pair(x1[:, :hd], x1[:, hd:])
    xh, xl = _split_bf16(x1)
    wrh = wrh_ref[...]
    logits = (jnp.dot(xh, wrh, preferred_element_type=F32) + jnp.dot(xl, wrh, preferred_element_type=F32)
              + jnp.dot(xh, wrl_ref[...], preferred_element_type=F32) + br_ref[...])
    route_ref[...] = _route(logits)


def _out_ln_route(a1, a2, w, x, ln_g, ln_b, w_grp, b_grp, w_exp, b_exp, *, col1=0, col2=0, tm=256):
    s, d = x.shape
    half = w.shape[0] // 2
    pad = ROUTE_LANES - N_GROUPS - N_EXPERTS
    wr = jnp.concatenate([w_grp, w_exp, jnp.zeros((d, pad), F32)], axis=1)
    br = jnp.concatenate([b_grp, b_exp, jnp.zeros((pad,), F32)]).reshape(1, ROUTE_LANES)
    wrh = wr.astype(BF16)
    wrl = (wr - wrh.astype(F32)).astype(BF16)
    full = lambda a: pl.BlockSpec(a.shape, lambda i: (0,) * a.ndim)
    rows = lambda n: pl.BlockSpec((tm, n), lambda i: (i, 0))
    g2 = ln_g.reshape(1, d)
    b2 = ln_b.reshape(1, d)
    return pl.pallas_call(
        _out_ln_route_kernel,
        grid=(s // tm,),
        in_specs=[pl.BlockSpec((tm, half), lambda i: (i, col1)), pl.BlockSpec((tm, half), lambda i: (i, col2)),
                  full(w), rows(d), full(g2), full(b2), full(wrh), full(wrl), full(br)],
        out_specs=[rows(d), rows(d // 2), rows(ROUTE_LANES)],
        out_shape=[jax.ShapeDtypeStruct((s, d), F32), jax.ShapeDtypeStruct((s, d // 2), jnp.uint32),
                   jax.ShapeDtypeStruct((s, ROUTE_LANES), F32)],
        compiler_params=_params(("parallel",), 48),
        name="out_ln_route",
    )(a1, a2, w, x, g2, b2, wrh, wrl, br)


def _rank_kernel(route_ref, tri_ref, rank_ref, cnt_ref, carry_ref):
    @pl.when(pl.program_id(0) == 0)
    def _():
        carry_ref[...] = jnp.zeros(carry_ref.shape, F32)

    route = route_ref[...]
    lane = lax.broadcasted_iota(jnp.int32, route.shape, 1).astype(F32)
    oh0 = jnp.where(lane == route[:, 0:1], 1.0, 0.0)
    oh1 = jnp.where(lane == route[:, 1:2], 1.0, 0.0)
    both = oh0 + oh1
    before = jnp.dot(tri_ref[...], both.astype(BF16), preferred_element_type=F32) + carry_ref[...]
    r0 = jnp.sum(before * oh0, axis=-1, keepdims=True)
    r1 = jnp.sum(before * oh1, axis=-1, keepdims=True)
    rank_ref[...] = jnp.where(lane == 0.0, r0, jnp.where(lane == 1.0, r1, 0.0))
    carry_ref[...] = carry_ref[...] + jnp.sum(both, axis=0, keepdims=True)
    cnt_ref[...] = carry_ref[...]


def _dest_kernel(route_ref, rank_ref, pstart_ref, dest_ref):
    route = route_ref[...]
    rank = rank_ref[...]
    lane = lax.broadcasted_iota(jnp.int32, route.shape, 1).astype(F32)
    ps = pstart_ref[...]
    d0 = jnp.sum(jnp.where(lane == route[:, 0:1], ps, 0.0), axis=-1, keepdims=True) + rank[:, 0:1]
    d1 = jnp.sum(jnp.where(lane == route[:, 1:2], ps, 0.0), axis=-1, keepdims=True) + rank[:, 1:2]
    dest_ref[...] = jnp.where(lane == 0.0, d0, jnp.where(lane == 1.0, d1, 0.0)).astype(jnp.int32)


def _moe_plan(route, *, tb=512):
    s = route.shape[0]
    tri = (jnp.arange(tb)[:, None] > jnp.arange(tb)[None, :]).astype(BF16)
    rows = pl.BlockSpec((tb, ROUTE_LANES), lambda i: (i, 0))
    one = lambda: pl.BlockSpec((1, ROUTE_LANES), lambda i: (0, 0))
    rank, cnt = pl.pallas_call(
        _rank_kernel,
        grid=(s // tb,),
        in_specs=[rows, pl.BlockSpec((tb, tb), lambda i: (0, 0))],
        out_specs=[rows, one()],
        out_shape=[jax.ShapeDtypeStruct((s, ROUTE_LANES), F32), jax.ShapeDtypeStruct((1, ROUTE_LANES), F32)],
        scratch_shapes=[pltpu.VMEM((1, ROUTE_LANES), F32)],
        compiler_params=_params(("arbitrary",), 32),
        name="moe_rank",
    )(route, tri)
    counts = cnt[0, :N_EXPERTS].astype(jnp.int32)
    pcounts = (counts + MOE_ROWS - 1) // MOE_ROWS * MOE_ROWS
    pends = jnp.cumsum(pcounts)
    pstarts = pends - pcounts
    n_blocks = s * TOP_K // MOE_ROWS + N_EXPERTS
    n_used = pends[-1] // MOE_ROWS
    blk = jnp.minimum(jnp.arange(n_blocks, dtype=jnp.int32), n_used - 1) * MOE_ROWS
    block_expert = jnp.minimum(jnp.sum(pends[None, :] <= blk[:, None], axis=1), N_EXPERTS - 1).astype(jnp.int32)
    ps = jnp.zeros((1, ROUTE_LANES), F32).at[0, :N_EXPERTS].set(pstarts.astype(F32))
    dest = pl.pallas_call(
        _dest_kernel,
        grid=(s // tb,),
        in_specs=[rows, rows, one()],
        out_specs=rows,
        out_shape=jax.ShapeDtypeStruct((s, ROUTE_LANES), jnp.int32),
        compiler_params=_params(("parallel",), 32),
        name="moe_dest",
    )(route, rank, ps)
    dest = dest[:, :TOP_K].T.reshape(TOP_K * s)
    return dest, block_expert, n_used.astype(jnp.int32).reshape(1)


ROW_UNROLL = 8


def _dispatch_kernel(dest_ref, x_ref, xs_hbm, sems, *, tm, s):
    t0 = pl.program_id(0) * tm

    def issue(g, carry):
        for u in range(ROW_UNROLL):
            r = g * ROW_UNROLL + u
            src = x_ref.at[pl.ds(r, 1), :]
            pltpu.make_async_copy(src, xs_hbm.at[pl.ds(dest_ref[t0 + r], 1), :], sems.at[0]).start()
            pltpu.make_async_copy(src, xs_hbm.at[pl.ds(dest_ref[s + t0 + r], 1), :], sems.at[1]).start()
        return carry

    lax.fori_loop(0, tm // ROW_UNROLL, issue, 0)
    tile = xs_hbm.at[pl.ds(0, tm), :]
    pltpu.make_async_copy(x_ref, tile, sems.at[0]).wait()
    pltpu.make_async_copy(x_ref, tile, sems.at[1]).wait()


def _dispatch_rows(x, dest, n_blocks, *, tm=256):
    s, d = x.shape
    return pl.pallas_call(
        functools.partial(_dispatch_kernel, tm=tm, s=s),
        grid_spec=pltpu.PrefetchScalarGridSpec(
            num_scalar_prefetch=1,
            grid=(s // tm,),
            in_specs=[pl.BlockSpec((tm, d), lambda i, dest: (i, 0))],
            out_specs=pl.BlockSpec(memory_space=pl.ANY),
            scratch_shapes=[pltpu.SemaphoreType.DMA((2,))]),
        out_shape=jax.ShapeDtypeStruct((n_blocks * MOE_ROWS, d), x.dtype),
        compiler_params=_params(("arbitrary",), 32),
        name="moe_dispatch",
    )(dest, x)


def _ffn_kernel(be_ref, nused_ref, xs_ref, w1_ref, w3_ref, w2_ref, ys_ref):
    b = pl.program_id(0)

    @pl.when(b < nused_ref[0])
    def _():
        x_lo, x_hi = _unpack_bf16_pair(xs_ref[...])
        x_lo = x_lo.astype(BF16)
        x_hi = x_hi.astype(BF16)
        hd = x_lo.shape[1]

        def up(w_ref):
            return (jnp.dot(x_lo, w_ref[0, :hd, :].astype(BF16), preferred_element_type=F32)
                    + jnp.dot(x_hi, w_ref[0, hd:, :].astype(BF16), preferred_element_type=F32))

        h1 = up(w1_ref)
        h3 = up(w3_ref)
        hmid = (h1 * jax.nn.sigmoid(h1) * h3).astype(BF16)
        y = jnp.dot(hmid, w2_ref[0].astype(BF16), preferred_element_type=F32)
        ys_ref[...] = _pack_bf16_pair(y[:, :hd], y[:, hd:])


def _grouped_ffn(xs, w1, w3, w2, layer, block_expert, n_used):
    d = xs.shape[1]
    dm = w1.shape[2]
    de = w1.shape[3]
    blk = lambda b, be, nu: (jnp.minimum(b, nu[0] - 1), 0)
    wsel = lambda b, be, nu: (layer, be[b], 0, 0)
    return pl.pallas_call(
        _ffn_kernel,
        grid_spec=pltpu.PrefetchScalarGridSpec(
            num_scalar_prefetch=2,
            grid=(xs.shape[0] // MOE_ROWS,),
            in_specs=[pl.BlockSpec((MOE_ROWS, d), blk),
                      pl.BlockSpec((None, 1, dm, de), wsel),
                      pl.BlockSpec((None, 1, dm, de), wsel),
                      pl.BlockSpec((None, 1, de, dm), wsel)],
            out_specs=pl.BlockSpec((MOE_ROWS, d), blk)),
        out_shape=jax.ShapeDtypeStruct(xs.shape, jnp.uint32),
        compiler_params=_params(("arbitrary",), 56),
        name="moe_ffn",
    )(block_expert, n_used, xs, w1, w3, w2)


def _combine_ple_kernel(pos_ref, ys_hbm, x1_ref, route_ref, g_ref, b_ref, wg_ref, bg_ref, p_ref, wp_ref,
                        o_ref, ob_ref, ybuf_ref, sems, *, tm, s):
    i = pl.program_id(0)
    n = pl.num_programs(0)
    slot = i % 2

    def start_gathers(tile_idx, buf):
        t0 = tile_idx * tm

        def issue(g, carry):
            for u in range(ROW_UNROLL):
                r = g * ROW_UNROLL + u
                for k in range(TOP_K):
                    pltpu.make_async_copy(ys_hbm.at[pl.ds(pos_ref[k * s + t0 + r], 1), :],
                                          ybuf_ref.at[buf, k, pl.ds(r, 1), :], sems.at[buf, k]).start()
            return carry

        lax.fori_loop(0, tm // ROW_UNROLL, issue, 0)

    @pl.when(i == 0)
    def _():
        start_gathers(0, 0)

    @pl.when(i + 1 < n)
    def _():
        start_gathers(i + 1, 1 - slot)

    tile = ys_hbm.at[pl.ds(0, tm), :]
    for k in range(TOP_K):
        pltpu.make_async_copy(tile, ybuf_ref.at[slot, k], sems.at[slot, k]).wait()

    route = route_ref[...]
    f = None
    for k in range(TOP_K):
        lo, hi = _unpack_bf16_pair(ybuf_ref[slot, k])
        fk = route[:, TOP_K + k:TOP_K + k + 1] * jnp.concatenate([lo, hi], axis=1)
        f = fk if f is None else f + fk
    x2 = _layer_norm_rows(ALPHA * x1_ref[...] + f, g_ref[...], b_ref[...])
    gate = jax.nn.sigmoid(jnp.dot(x2.astype(BF16), wg_ref[...], preferred_element_type=F32) + bg_ref[...])
    emb = jnp.dot(p_ref[...].astype(BF16), wp_ref[...], preferred_element_type=F32)
    out = x2 + gate * emb
    o_ref[...] = out
    ob_ref[...] = out.astype(BF16)


def _combine_ple(ys, pos, x1, route, ln_g, ln_b, w_gate, b_gate, p, w_proj, *, tm=256):
    s, d = x1.shape
    pd = p.shape[1]
    rows = lambda n: pl.BlockSpec((tm, n), lambda i, pos: (i, 0))
    full = lambda a: pl.BlockSpec(a.shape, lambda i, pos: (0,) * a.ndim)
    g2, b2, bg = ln_g.reshape(1, d), ln_b.reshape(1, d), b_gate.reshape(1, d)
    return pl.pallas_call(
        functools.partial(_combine_ple_kernel, tm=tm, s=s),
        grid_spec=pltpu.PrefetchScalarGridSpec(
            num_scalar_prefetch=1,
            grid=(s // tm,),
            in_specs=[pl.BlockSpec(memory_space=pl.ANY), rows(d), rows(ROUTE_LANES), full(g2), full(b2),
                      full(w_gate), full(bg), rows(pd), full(w_proj)],
            out_specs=[rows(d), rows(d)],
            scratch_shapes=[pltpu.VMEM((2, TOP_K, tm, d // 2), jnp.uint32),
                            pltpu.SemaphoreType.DMA((2, TOP_K))]),
        out_shape=[jax.ShapeDtypeStruct((s, d), F32), jax.ShapeDtypeStruct((s, d), BF16)],
        compiler_params=_params(("arbitrary",), 56),
        name="moe_combine_ple",
    )(pos, ys, x1, route, g2, b2, w_gate, bg, p, w_proj)


def _rglru_kernel(cur_ref, prev_ref, gate_ref, cw_ref, cb_ref, wa_ref, ba_ref, wi_ref, bi_ref, lam_ref,
                  o_ref, buf_ref, a_ref, b_ref, h_ref, carry_ref, *, tt, tc, width):
    i = pl.program_id(1)

    @pl.when(i == 0)
    def _():
        carry_ref[...] = jnp.zeros(carry_ref.shape, F32)

    buf_ref[0:RNN_HALO, :] = jnp.where(i > 0, prev_ref[...], 0.0)
    buf_ref[RNN_HALO:, :] = cur_ref[...]
    off = RNN_HALO - (width - 1)
    xc = jnp.broadcast_to(cb_ref[...], (tt, tc))
    for k in range(width):
        xc = xc + buf_ref[off + k:off + k + tt, :] * cw_ref[k:k + 1, :]
    z = -lam_ref[...]
    softplus = jnp.log1p(jnp.exp(-jnp.abs(z))) + jnp.maximum(z, 0.0)
    for hh in range(tc // RNN_HEAD_DIM):
        cols = slice(hh * RNN_HEAD_DIM, (hh + 1) * RNN_HEAD_DIM)
        xh = xc[:, cols]
        xhb = xh.astype(BF16)
        r = jax.nn.sigmoid(jnp.dot(xhb, wa_ref[hh], preferred_element_type=F32) + ba_ref[:, cols])
        ig = jax.nn.sigmoid(jnp.dot(xhb, wi_ref[hh], preferred_element_type=F32) + bi_ref[:, cols])
        log_a = (-RG_C) * r * softplus[:, cols]
        a = jnp.exp(log_a)
        bval = jnp.sqrt(-jnp.tanh(log_a) * (a * a + 1.0)) * (ig * xh)
        a_ref[:, cols] = a
        b_ref[:, cols] = bval

    def group(gi, h):
        base = pl.multiple_of(gi * 8, 8)
        for r in range(8):
            row = pl.ds(base + r, 1)
            h = a_ref[row, :] * h + b_ref[row, :]
            h_ref[row, :] = h
        return h

    carry_ref[...] = lax.fori_loop(0, tt // 8, group, carry_ref[...])
    o_ref[...] = (gate_ref[...].astype(F32) * h_ref[...]).astype(o_ref.dtype)


def _rglru(xr, gate, conv_w, conv_b, w_a, b_a, w_i, b_i, lam, *, tt=256, tc=512):
    s, wd = xr.shape
    width = conv_w.shape[0]
    hpc = tc // RNN_HEAD_DIM
    hb = tt // RNN_HALO
    row = lambda v: v.reshape(1, wd)
    cvec = lambda: pl.BlockSpec((1, tc), lambda c, i: (0, c))
    heads = lambda: pl.BlockSpec((hpc, RNN_HEAD_DIM, RNN_HEAD_DIM), lambda c, i: (c, 0, 0))
    tile = lambda: pl.BlockSpec((tt, tc), lambda c, i: (i, c))
    return pl.pallas_call(
        functools.partial(_rglru_kernel, tt=tt, tc=tc, width=width),
        grid=(wd // tc, s // tt),
        in_specs=[tile(),
                  pl.BlockSpec((RNN_HALO, tc), lambda c, i: (jnp.maximum(i * hb - 1, 0), c)),
                  tile(),
                  pl.BlockSpec((width, tc), lambda c, i: (0, c)),
                  cvec(), heads(), cvec(), heads(), cvec(), cvec()],
        out_specs=tile(),
        out_shape=jax.ShapeDtypeStruct((s, wd), BF16),
        scratch_shapes=[pltpu.VMEM((tt + RNN_HALO, tc), F32), pltpu.VMEM((tt, tc), F32),
                        pltpu.VMEM((tt, tc), F32), pltpu.VMEM((tt, tc), F32), pltpu.VMEM((1, tc), F32)],
        compiler_params=_params(("parallel", "arbitrary"), 32),
        name="rglru",
    )(xr, xr, gate, conv_w, row(conv_b), w_a.astype(BF16), row(b_a), w_i.astype(BF16), row(b_i), row(lam))


def _moe_and_ple(mixed, i, p, ln_ffn_g, ln_ffn_b, moe_w1, moe_w3, moe_w2, ple_w_proj, ple_w_gate, ple_b_gate):
    x1, x1_pairs, route = mixed
    dest, block_expert, n_used = _moe_plan(route)
    xs = _dispatch_rows(x1_pairs, dest, block_expert.shape[0])
    ys = _grouped_ffn(xs, moe_w1, moe_w3, moe_w2, i, block_expert, n_used)
    return _combine_ple(ys, dest, x1, route, ln_ffn_g[i], ln_ffn_b[i], ple_w_gate[i].astype(BF16),
                        ple_b_gate[i], p[i, 0], ple_w_proj[i].astype(BF16))


def kernel(x, p, positions, ev_w_in, ev_conv_w, ev_conv_b, ev_cnorm_g, ev_cnorm_b, ev_qnorm_g, ev_w_uq, ev_kvnorm_g, ev_w_ukv, ev_w_out, od_w_in, od_conv_w, od_conv_b, od_w_a, od_b_a, od_w_i, od_b_i, od_lam, od_w_out, ln_mix_g, ln_mix_b, ln_ffn_g, ln_ffn_b, moe_w_grp, moe_b_grp, moe_w_exp, moe_b_exp, moe_w1, moe_w3, moe_w2, ple_w_proj, ple_w_gate, ple_b_gate):
    x0 = x[0]
    x0b = x0.astype(BF16)
    moe = functools.partial(_moe_and_ple, p=p, ln_ffn_g=ln_ffn_g, ln_ffn_b=ln_ffn_b, moe_w1=moe_w1,
                            moe_w3=moe_w3, moe_w2=moe_w2, ple_w_proj=ple_w_proj, ple_w_gate=ple_w_gate,
                            ple_b_gate=ple_b_gate)

    w_in = ev_w_in[0]
    u = _glu_proj(x0b, w_in[:, :2 * CONV_CH].astype(BF16))
    ub = _conv_gln(u, ev_conv_w[0], ev_conv_b[0], ev_cnorm_g[0], ev_cnorm_b[0], groups=CONV_GROUPS)
    cosm, sinm = _rope_tables(positions[0])
    qn, qr, kn, kr, v = _mla_proj(x0b, w_in[:, 2 * CONV_CH:], ev_qnorm_g[0], ev_w_uq[0], ev_kvnorm_g[0],
                                  ev_w_ukv[0], cosm, sinm)
    att = _attention(qn, qr, kn, kr, v)
    mixed0 = _out_ln_route(ub, att, ev_w_out[0].astype(BF16), x0, ln_mix_g[0], ln_mix_b[0],
                           moe_w_grp[0], moe_b_grp[0], moe_w_exp[0], moe_b_exp[0])
    x3, x3b = moe(mixed0, 0)

    w_in1 = od_w_in[0]
    gate = _proj(x3b, w_in1[:, :RNN_WIDTH].astype(BF16), act="gelu", out_dtype=BF16)
    xr = _proj(x3b, w_in1[:, RNN_WIDTH:].astype(BF16), act=None, out_dtype=F32)
    y = _rglru(xr, gate, od_conv_w[0], od_conv_b[0], od_w_a[0], od_b_a[0], od_w_i[0], od_b_i[0], od_lam[0])
    mixed1 = _out_ln_route(y, y, od_w_out[0].astype(BF16), x3, ln_mix_g[1], ln_mix_b[1],
                           moe_w_grp[1], moe_b_grp[1], moe_w_exp[1], moe_b_exp[1], col1=0, col2=1)
    x6, _ = moe(mixed1, 1)
    return x6[None]
```

```python
import functools
import math

import jax
import jax.numpy as jnp
from jax import lax
from jax.experimental import pallas as pl
from jax.experimental.pallas import tpu as pltpu

F32 = jnp.float32
BF16 = jnp.bfloat16

D_MODEL = 2048
SEQ = 8192
DEPTH = 2
CONV_CH = 1024
CONV_GROUPS = 16
CONV_WIDTH = 31
MLA_HEADS = 8
QK_NOPE = 128
QK_ROPE = 64
V_HEAD = 128
Q_RANK = 512
KV_RANK = 256
ROPE_THETA = 10000.0
RNN_WIDTH = 2048
RNN_HEADS = 16
RNN_HEAD_DIM = RNN_WIDTH // RNN_HEADS
RNN_CONV_WIDTH = 4
RG_C = 8.0
N_GROUPS = 8
EXPERTS_PER_GROUP = 8
N_EXPERTS = N_GROUPS * EXPERTS_PER_GROUP
TOP_K = 2
D_EXPERT = 512
PLE_DIM = 256
ALPHA = (2 * DEPTH) ** 0.25
LN_EPS = 1e-5
RMS_EPS = 1e-6

LANES = 128
SUBLANES = 8
CONV_HALO = 32
RNN_HALO = 8
MOE_ROWS = 256
ROUTE_LANES = 128
MIB = 2 ** 20


def _params(semantics, vmem_mib):
    return pltpu.CompilerParams(dimension_semantics=semantics, vmem_limit_bytes=vmem_mib * MIB)


def _layer_norm_rows(z, g, b):
    mu = jnp.mean(z, axis=-1, keepdims=True)
    d = z - mu
    var = jnp.mean(d * d, axis=-1, keepdims=True)
    return d * lax.rsqrt(var + LN_EPS) * g + b


def _pack_bf16_pair(a, b):
    def rounded_bits(v):
        u = lax.bitcast_convert_type(v, jnp.uint32)
        return u + jnp.uint32(0x7FFF) + ((u >> 16) & jnp.uint32(1))

    return (rounded_bits(a) & jnp.uint32(0xFFFF0000)) | (rounded_bits(b) >> 16)


def _unpack_bf16_pair(u):
    a = lax.bitcast_convert_type(u & jnp.uint32(0xFFFF0000), F32)
    b = lax.bitcast_convert_type(u << 16, F32)
    return a, b


def _split_bf16(v):
    hi = v.astype(BF16)
    lo = (v - hi.astype(F32)).astype(BF16)
    return hi, lo


def _gelu_tanh(x):
    c = math.sqrt(2.0 / math.pi)
    return 0.5 * x * (1.0 + jnp.tanh(c * (x + 0.044715 * (x * x * x))))


def _proj_kernel(x_ref, w_ref, o_ref, *, act):
    y = jnp.dot(x_ref[...], w_ref[...], preferred_element_type=F32)
    if act == "gelu":
        y = _gelu_tanh(y)
    o_ref[...] = y.astype(o_ref.dtype)


def _proj(x, w, *, act, out_dtype, tm=512, tn=512):
    m, k = x.shape
    n = w.shape[1]
    return pl.pallas_call(
        functools.partial(_proj_kernel, act=act),
        grid=(n // tn, m // tm),
        in_specs=[pl.BlockSpec((tm, k), lambda j, i: (i, 0)),
                  pl.BlockSpec((k, tn), lambda j, i: (0, j))],
        out_specs=pl.BlockSpec((tm, tn), lambda j, i: (i, j)),
        out_shape=jax.ShapeDtypeStruct((m, n), out_dtype),
        compiler_params=_params(("parallel", "parallel"), 40),
        name="proj_" + str(act),
    )(x, w)


def _glu_kernel(x_ref, wv_ref, wg_ref, o_ref):
    x = x_ref[...]
    v = jnp.dot(x, wv_ref[...], preferred_element_type=F32)
    g = jnp.dot(x, wg_ref[...], preferred_element_type=F32)
    o_ref[...] = v * jax.nn.sigmoid(g)


def _glu_proj(x, w, *, tm=512, tn=512):
    m, k = x.shape
    n = w.shape[1] // 2
    nb = n // tn
    return pl.pallas_call(
        _glu_kernel,
        grid=(nb, m // tm),
        in_specs=[pl.BlockSpec((tm, k), lambda j, i: (i, 0)),
                  pl.BlockSpec((k, tn), lambda j, i: (0, j)),
                  pl.BlockSpec((k, tn), lambda j, i: (0, j + nb))],
        out_specs=pl.BlockSpec((tm, tn), lambda j, i: (i, j)),
        out_shape=jax.ShapeDtypeStruct((m, n), F32),
        compiler_params=_params(("parallel", "parallel"), 40),
        name="glu_proj",
    )(x, w, w)


def _conv_gln_kernel(cur_ref, prev_ref, w_ref, b_ref, g_ref, beta_ref, gm_ref, o_ref,
                     buf_ref, sh_ref, y_ref, *, tt, ch, width, rc, cc):
    i = pl.program_id(0)
    buf_ref[0:CONV_HALO, :] = jnp.where(i > 0, prev_ref[...], 0.0)
    buf_ref[CONV_HALO:, :] = cur_ref[...]
    off = CONV_HALO - (width - 1)
    sh_rows = sh_ref.shape[1]
    step = 7 * SUBLANES
    assert sh_rows % step == 0
    for b in range(1, SUBLANES):
        for c0 in range(0, ch, cc):
            for r0 in range(0, sh_rows, step):
                sh_ref[b - 1, r0:r0 + step, c0:c0 + cc] = buf_ref[r0 + b:r0 + b + step, c0:c0 + cc]
    for c0 in range(0, ch, cc):
        for r0 in range(0, tt, rc):
            acc = jnp.broadcast_to(b_ref[:, c0:c0 + cc], (rc, cc))
            for k in range(width):
                q, b = divmod(off + k, SUBLANES)
                lo = r0 + q * SUBLANES
                src = buf_ref[lo:lo + rc, c0:c0 + cc] if b == 0 else sh_ref[b - 1, lo:lo + rc, c0:c0 + cc]
                acc = acc + src * w_ref[k:k + 1, c0:c0 + cc]
            y_ref[r0:r0 + rc, c0:c0 + cc] = acc
    gm = gm_ref[...]

    def seg_mean(v):
        hi, lo = _split_bf16(v)
        return (jnp.dot(hi, gm, preferred_element_type=F32)
                + jnp.dot(lo, gm, preferred_element_type=F32))

    for c0 in range(0, ch, LANES):
        y = y_ref[:, c0:c0 + LANES]
        d = y - seg_mean(y)
        var = seg_mean(d * d)
        z = d * lax.rsqrt(var + LN_EPS) * g_ref[:, c0:c0 + LANES] + beta_ref[:, c0:c0 + LANES]
        o_ref[:, c0:c0 + LANES] = (z * jax.nn.sigmoid(z)).astype(o_ref.dtype)


def _conv_gln(u, w, b, g, beta, *, groups, tt=256):
    s, ch = u.shape
    width = w.shape[0]
    gsz = ch // groups
    assert LANES % gsz == 0 and width - 1 <= CONV_HALO and tt % CONV_HALO == 0
    wpad = jnp.zeros((CONV_HALO, ch), F32).at[:width].set(w)
    lane = jnp.arange(LANES)
    gm = jnp.where((lane[:, None] // gsz) == (lane[None, :] // gsz), 1.0 / gsz, 0.0).astype(BF16)
    hb = tt // CONV_HALO
    row = lambda v: v.reshape(1, ch)
    return pl.pallas_call(
        functools.partial(_conv_gln_kernel, tt=tt, ch=ch, width=width, rc=32, cc=256),
        grid=(s // tt,),
        in_specs=[pl.BlockSpec((tt, ch), lambda i: (i, 0)),
                  pl.BlockSpec((CONV_HALO, ch), lambda i: (jnp.maximum(i * hb - 1, 0), 0)),
                  pl.BlockSpec((CONV_HALO, ch), lambda i: (0, 0)),
                  pl.BlockSpec((1, ch), lambda i: (0, 0)),
                  pl.BlockSpec((1, ch), lambda i: (0, 0)),
                  pl.BlockSpec((1, ch), lambda i: (0, 0)),
                  pl.BlockSpec((LANES, LANES), lambda i: (0, 0))],
        out_specs=pl.BlockSpec((tt, ch), lambda i: (i, 0)),
        out_shape=jax.ShapeDtypeStruct((s, ch), BF16),
        scratch_shapes=[pltpu.VMEM((tt + CONV_HALO, ch), F32),
                        pltpu.VMEM((SUBLANES - 1, tt + CONV_HALO - SUBLANES, ch), F32),
                        pltpu.VMEM((tt, ch), F32)],
        compiler_params=_params(("parallel",), 40),
        name="conv_gln",
    )(u, u, wpad, row(b), row(g), row(beta), gm)


def _rope_table_kernel(pos_ref, invf_ref, cos_ref, sin_ref):
    ang = pos_ref[...].astype(F32) * invf_ref[...]
    cos_ref[...] = jnp.cos(ang)
    sin_ref[...] = jnp.sin(ang)


def _rope_tables(positions):
    s = positions.shape[0]
    half = QK_ROPE // 2
    per_row = LANES // half
    inv_freq = 1.0 / (ROPE_THETA ** (jnp.arange(0, QK_ROPE, 2, dtype=F32) / QK_ROPE))
    pos_rep = jnp.repeat(positions.reshape(s // per_row, per_row), half, axis=1)
    invf = jnp.tile(inv_freq, per_row).reshape(1, LANES)
    cos, sin = pl.pallas_call(
        _rope_table_kernel,
        out_shape=[jax.ShapeDtypeStruct((s // per_row, LANES), F32)] * 2,
        name="rope_table",
    )(pos_rep, invf)
    cos = cos.reshape(s, half)
    sin = sin.reshape(s, half)
    zero = jnp.zeros((s, LANES - QK_ROPE), F32)
    return jnp.concatenate([cos, cos, zero], axis=1), jnp.concatenate([-sin, sin, zero], axis=1)


def _mla_proj_kernel(x_ref, wm_ref, qg_ref, kvg_ref, wq_ref, wkv_ref, c_ref, s_ref,
                     qn_ref, qr_ref, kn_ref, kr_ref, v_ref, *, scale):
    nq = MLA_HEADS * QK_NOPE
    c = jnp.dot(x_ref[...], wm_ref[...], preferred_element_type=F32)
    cq = c[:, :Q_RANK]
    ckv = c[:, Q_RANK:Q_RANK + KV_RANK]
    kr_pad = c[:, Q_RANK + KV_RANK:Q_RANK + KV_RANK + LANES]
    kr_rot = c[:, Q_RANK + KV_RANK + LANES:]
    cqn = cq * lax.rsqrt(jnp.mean(cq * cq, axis=-1, keepdims=True) + RMS_EPS) * qg_ref[...]
    ckvn = ckv * lax.rsqrt(jnp.mean(ckv * ckv, axis=-1, keepdims=True) + RMS_EPS) * kvg_ref[...]
    q = jnp.dot(cqn.astype(BF16), wq_ref[...], preferred_element_type=F32)
    kv = jnp.dot(ckvn.astype(BF16), wkv_ref[...], preferred_element_type=F32)
    cosm = c_ref[...]
    sinm = s_ref[...]
    qn_ref[...] = (q[:, :nq] * scale).astype(BF16)
    for h in range(MLA_HEADS):
        lo = nq + h * LANES
        rot = q[:, lo:lo + LANES] * cosm + q[:, lo + nq:lo + nq + LANES] * sinm
        qr_ref[:, h * LANES:(h + 1) * LANES] = (rot * scale).astype(BF16)
    kn_ref[...] = kv[:, :nq].astype(BF16)
    v_ref[...] = kv[:, nq:].astype(BF16)
    kr_ref[...] = (kr_pad * cosm + kr_rot * sinm).astype(BF16)


def _rope_swap(w):
    half = QK_ROPE // 2
    return jnp.concatenate([w[..., half:], w[..., :half]], axis=-1)


def _mla_proj(xb, w_in_mla, qnorm_g, w_uq, kvnorm_g, w_ukv, cosm, sinm, *, tm=256):
    s, d = xb.shape
    nq = MLA_HEADS * QK_NOPE
    pad = lambda w: jnp.concatenate([w, jnp.zeros(w.shape[:-1] + (LANES - QK_ROPE,), w.dtype)], axis=-1)
    w_kr = w_in_mla[:, Q_RANK + KV_RANK:]
    wm = jnp.concatenate([w_in_mla[:, :Q_RANK + KV_RANK], pad(w_kr), pad(_rope_swap(w_kr))], axis=1).astype(BF16)
    wq3 = w_uq.reshape(Q_RANK, MLA_HEADS, QK_NOPE + QK_ROPE)
    wq_rope = wq3[:, :, QK_NOPE:]
    wq = jnp.concatenate([wq3[:, :, :QK_NOPE].reshape(Q_RANK, nq),
                          pad(wq_rope).reshape(Q_RANK, MLA_HEADS * LANES),
                          pad(_rope_swap(wq_rope)).reshape(Q_RANK, MLA_HEADS * LANES)], axis=1).astype(BF16)
    wkv3 = w_ukv.reshape(KV_RANK, MLA_HEADS, QK_NOPE + V_HEAD)
    wkv = jnp.concatenate([wkv3[:, :, :QK_NOPE].reshape(KV_RANK, nq),
                           wkv3[:, :, QK_NOPE:].reshape(KV_RANK, MLA_HEADS * V_HEAD)], axis=1).astype(BF16)
    scale = (QK_NOPE + QK_ROPE) ** -0.5 * math.log2(math.e)
    full = lambda a: pl.BlockSpec(a.shape, lambda i: (0,) * a.ndim)
    rows = lambda n: pl.BlockSpec((tm, n), lambda i: (i, 0))
    qg = qnorm_g.reshape(1, Q_RANK)
    kvg = kvnorm_g.reshape(1, KV_RANK)
    return pl.pallas_call(
        functools.partial(_mla_proj_kernel, scale=scale),
        grid=(s // tm,),
        in_specs=[rows(d), full(wm), full(qg), full(kvg), full(wq), full(wkv), rows(LANES), rows(LANES)],
        out_specs=[rows(nq), rows(MLA_HEADS * LANES), rows(nq), rows(LANES), rows(MLA_HEADS * V_HEAD)],
        out_shape=[jax.ShapeDtypeStruct((s, nq), BF16),
                   jax.ShapeDtypeStruct((s, MLA_HEADS * LANES), BF16),
                   jax.ShapeDtypeStruct((s, nq), BF16),
                   jax.ShapeDtypeStruct((s, LANES), BF16),
                   jax.ShapeDtypeStruct((s, MLA_HEADS * V_HEAD), BF16)],
        compiler_params=_params(("parallel",), 48),
        name="mla_proj",
    )(xb, wm, qg, kvg, wq, wkv, cosm, sinm)


def _attn_kernel(qn_ref, qr_ref, kn_ref, kr_ref, v_ref, o_ref, m_ref, acc_ref, *, tq, tk, hb):
    qi = pl.program_id(1)
    per_q = tq // tk
    m_ref[...] = jnp.full(m_ref.shape, -jnp.inf, F32)
    acc_ref[...] = jnp.zeros(acc_ref.shape, F32)
    ones = jnp.ones((tk, LANES), BF16)
    head_cols = [slice(h * LANES, (h + 1) * LANES) for h in range(hb)]
    qs = [jnp.concatenate([qn_ref[:, c], qr_ref[:, c]], axis=1) for c in head_cols]

    def chunk(j, diag):
        rows = pl.ds(pl.multiple_of(j * tk, tk), tk)
        k_rope = kr_ref[rows, :]
        for h, c in enumerate(head_cols):
            k = jnp.concatenate([kn_ref[rows, c], k_rope], axis=1)
            s = lax.dot_general(qs[h], k, (((1,), (1,)), ((), ())), preferred_element_type=F32)
            if diag is not None:
                keep = (lax.broadcasted_iota(jnp.int32, s.shape, 0)
                        >= lax.broadcasted_iota(jnp.int32, s.shape, 1) + diag * tk)
                s = jnp.where(keep, s, -jnp.inf)
            m_old = m_ref[h]
            m_new = jnp.maximum(m_old, jnp.max(s, axis=-1, keepdims=True))
            alpha = jnp.exp2(m_old - m_new)
            p = jnp.exp2(s - jnp.concatenate([m_new] * (tk // LANES), axis=1))
            v_ext = jnp.concatenate([v_ref[rows, c], ones], axis=1)
            acc_ref[h] = (jnp.concatenate([alpha, alpha], axis=1) * acc_ref[h]
                          + jnp.dot(p.astype(BF16), v_ext, preferred_element_type=F32))
            m_ref[h] = m_new

    def body(j, carry):
        chunk(j, None)
        return carry

    lax.fori_loop(0, qi * per_q, body, 0)
    for d in range(per_q):
        chunk(qi * per_q + d, d)
    for h, c in enumerate(head_cols):
        acc = acc_ref[h]
        o_ref[:, c] = (acc[:, :V_HEAD] / acc[:, V_HEAD:]).astype(o_ref.dtype)


def _attention(qn, qr, kn, kr, v, *, tq=1024, tk=512, hb=2):
    s = qn.shape[0]
    wide = hb * LANES
    tile = lambda: pl.BlockSpec((tq, wide), lambda h, i: (i, h))
    keys = lambda: pl.BlockSpec((s, wide), lambda h, i: (0, h))
    return pl.pallas_call(
        functools.partial(_attn_kernel, tq=tq, tk=tk, hb=hb),
        grid=(MLA_HEADS // hb, s // tq),
        in_specs=[tile(), tile(), keys(), pl.BlockSpec((s, LANES), lambda h, i: (0, 0)), keys()],
        out_specs=tile(),
        out_shape=jax.ShapeDtypeStruct((s, MLA_HEADS * V_HEAD), BF16),
        scratch_shapes=[pltpu.VMEM((hb, tq, LANES), F32), pltpu.VMEM((hb, tq, 2 * V_HEAD), F32)],
        compiler_params=_params(("parallel", "parallel"), 48),
        name="mla_attention",
    )(qn, qr, kn, kr, v)


def _route(logits):
    lane = lax.broadcasted_iota(jnp.int32, logits.shape, 1).astype(F32)
    big = float(2 * ROUTE_LANES)
    neg = -jnp.inf
    gl = jnp.where(lane < N_GROUPS, logits, neg)
    gmax = jnp.max(gl, axis=-1, keepdims=True)
    gsum = jnp.sum(jnp.exp(gl - gmax), axis=-1, keepdims=True)
    g_prob = 1.0 / gsum
    g_idx = jnp.min(jnp.where(gl == gmax, lane, big), axis=-1, keepdims=True)
    lo = N_GROUPS + g_idx * EXPERTS_PER_GROUP
    el = jnp.where((lane >= lo) & (lane < lo + EXPERTS_PER_GROUP), logits, neg)
    emax = jnp.max(el, axis=-1, keepdims=True)
    esum = jnp.sum(jnp.exp(el - emax), axis=-1, keepdims=True)
    i0 = jnp.min(jnp.where(el == emax, lane, big), axis=-1, keepdims=True)
    el2 = jnp.where(lane == i0, neg, el)
    emax2 = jnp.max(el2, axis=-1, keepdims=True)
    i1 = jnp.min(jnp.where(el2 == emax2, lane, big), axis=-1, keepdims=True)
    p0 = 1.0 / esum
    p1 = jnp.exp(emax2 - emax) / esum
    g0 = g_prob * p0 / (p0 + p1)
    g1 = g_prob * p1 / (p0 + p1)
    e0 = i0 - N_GROUPS
    e1 = i1 - N_GROUPS
    return jnp.where(lane == 0, e0, jnp.where(lane == 1, e1, jnp.where(lane == 2, g0, jnp.where(lane == 3, g1, 0.0))))


def _out_ln_route_kernel(a1_ref, a2_ref, w_ref, x_ref, g_ref, b_ref, wrh_ref, wrl_ref, br_ref,
                         x1_ref, x1p_ref, route_ref):
    half = a1_ref.shape[1]
    m = (jnp.dot(a1_ref[...], w_ref[:half, :], preferred_element_type=F32)
         + jnp.dot(a2_ref[...], w_ref[half:, :], preferred_element_type=F32))
    x1 = _layer_norm_rows(ALPHA * x_ref[...] + m, g_ref[...], b_ref[...])
    x1_ref[...] = x1
    hd = x1.shape[1] // 2
    x1p_ref[...] = _pack_bf16_pair(x1[:, :hd], x1[:, hd:])
    xh, xl = _split_bf16(x1)
    wrh = wrh_ref[...]
    logits = (jnp.dot(xh, wrh, preferred_element_type=F32) + jnp.dot(xl, wrh, preferred_element_type=F32)
              + jnp.dot(xh, wrl_ref[...], preferred_element_type=F32) + br_ref[...])
    route_ref[...] = _route(logits)


def _out_ln_route(a1, a2, w, x, ln_g, ln_b, w_grp, b_grp, w_exp, b_exp, *, col1=0, col2=0, tm=256):
    s, d = x.shape
    half = w.shape[0] // 2
    pad = ROUTE_LANES - N_GROUPS - N_EXPERTS
    wr = jnp.concatenate([w_grp, w_exp, jnp.zeros((d, pad), F32)], axis=1)
    br = jnp.concatenate([b_grp, b_exp, jnp.zeros((pad,), F32)]).reshape(1, ROUTE_LANES)
    wrh = wr.astype(BF16)
    wrl = (wr - wrh.astype(F32)).astype(BF16)
    full = lambda a: pl.BlockSpec(a.shape, lambda i: (0,) * a.ndim)
    rows = lambda n: pl.BlockSpec((tm, n), lambda i: (i, 0))
    g2 = ln_g.reshape(1, d)
    b2 = ln_b.reshape(1, d)
    return pl.pallas_call(
        _out_ln_route_kernel,
        grid=(s // tm,),
        in_specs=[pl.BlockSpec((tm, half), lambda i: (i, col1)), pl.BlockSpec((tm, half), lambda i: (i, col2)),
                  full(w), rows(d), full(g2), full(b2), full(wrh), full(wrl), full(br)],
        out_specs=[rows(d), rows(d // 2), rows(ROUTE_LANES)],
        out_shape=[jax.ShapeDtypeStruct((s, d), F32), jax.ShapeDtypeStruct((s, d // 2), jnp.uint32),
                   jax.ShapeDtypeStruct((s, ROUTE_LANES), F32)],
        compiler_params=_params(("parallel",), 48),
        name="out_ln_route",
    )(a1, a2, w, x, g2, b2, wrh, wrl, br)


def _rank_kernel(route_ref, tri_ref, rank_ref, cnt_ref, carry_ref):
    @pl.when(pl.program_id(0) == 0)
    def _():
        carry_ref[...] = jnp.zeros(carry_ref.shape, F32)

    route = route_ref[...]
    lane = lax.broadcasted_iota(jnp.int32, route.shape, 1).astype(F32)
    oh0 = jnp.where(lane == route[:, 0:1], 1.0, 0.0)
    oh1 = jnp.where(lane == route[:, 1:2], 1.0, 0.0)
    both = oh0 + oh1
    before = jnp.dot(tri_ref[...], both.astype(BF16), preferred_element_type=F32) + carry_ref[...]
    r0 = jnp.sum(before * oh0, axis=-1, keepdims=True)
    r1 = jnp.sum(before * oh1, axis=-1, keepdims=True)
    rank_ref[...] = jnp.where(lane == 0.0, r0, jnp.where(lane == 1.0, r1, 0.0))
    carry_ref[...] = carry_ref[...] + jnp.sum(both, axis=0, keepdims=True)
    cnt_ref[...] = carry_ref[...]


def _dest_kernel(route_ref, rank_ref, pstart_ref, dest_ref):
    route = route_ref[...]
    rank = rank_ref[...]
    lane = lax.broadcasted_iota(jnp.int32, route.shape, 1).astype(F32)
    ps = pstart_ref[...]
    d0 = jnp.sum(jnp.where(lane == route[:, 0:1], ps, 0.0), axis=-1, keepdims=True) + rank[:, 0:1]
    d1 = jnp.sum(jnp.where(lane == route[:, 1:2], ps, 0.0), axis=-1, keepdims=True) + rank[:, 1:2]
    dest_ref[...] = jnp.where(lane == 0.0, d0, jnp.where(lane == 1.0, d1, 0.0)).astype(jnp.int32)


def _moe_plan(route, *, tb=512):
    s = route.shape[0]
    tri = (jnp.arange(tb)[:, None] > jnp.arange(tb)[None, :]).astype(BF16)
    rows = pl.BlockSpec((tb, ROUTE_LANES), lambda i: (i, 0))
    one = lambda: pl.BlockSpec((1, ROUTE_LANES), lambda i: (0, 0))
    rank, cnt = pl.pallas_call(
        _rank_kernel,
        grid=(s // tb,),
        in_specs=[rows, pl.BlockSpec((tb, tb), lambda i: (0, 0))],
        out_specs=[rows, one()],
        out_shape=[jax.ShapeDtypeStruct((s, ROUTE_LANES), F32), jax.ShapeDtypeStruct((1, ROUTE_LANES), F32)],
        scratch_shapes=[pltpu.VMEM((1, ROUTE_LANES), F32)],
        compiler_params=_params(("arbitrary",), 32),
        name="moe_rank",
    )(route, tri)
    counts = cnt[0, :N_EXPERTS].astype(jnp.int32)
    pcounts = (counts + MOE_ROWS - 1) // MOE_ROWS * MOE_ROWS
    pends = jnp.cumsum(pcounts)
    pstarts = pends - pcounts
    n_blocks = s * TOP_K // MOE_ROWS + N_EXPERTS
    n_used = pends[-1] // MOE_ROWS
    blk = jnp.minimum(jnp.arange(n_blocks, dtype=jnp.int32), n_used - 1) * MOE_ROWS
    block_expert = jnp.minimum(jnp.sum(pends[None, :] <= blk[:, None], axis=1), N_EXPERTS - 1).astype(jnp.int32)
    ps = jnp.zeros((1, ROUTE_LANES), F32).at[0, :N_EXPERTS].set(pstarts.astype(F32))
    dest = pl.pallas_call(
        _dest_kernel,
        grid=(s // tb,),
        in_specs=[rows, rows, one()],
        out_specs=rows,
        out_shape=jax.ShapeDtypeStruct((s, ROUTE_LANES), jnp.int32),
        compiler_params=_params(("parallel",), 32),
        name="moe_dest",
    )(route, rank, ps)
    dest = dest[:, :TOP_K].T.reshape(TOP_K * s)
    return dest, block_expert, n_used.astype(jnp.int32).reshape(1)


ROW_UNROLL = 8


def _dispatch_kernel(dest_ref, x_ref, xs_hbm, sems, *, tm, s):
    t0 = pl.program_id(0) * tm

    def issue(g, carry):
        for u in range(ROW_UNROLL):
            r = g * ROW_UNROLL + u
            src = x_ref.at[pl.ds(r, 1), :]
            for k in range(TOP_K):
                pltpu.make_async_copy(src, xs_hbm.at[pl.ds(dest_ref[k * s + t0 + r], 1), :],
                                      sems.at[k]).start(priority=k)
        return carry

    lax.fori_loop(0, tm // ROW_UNROLL, issue, 0)
    tile = xs_hbm.at[pl.ds(0, tm), :]
    pltpu.make_async_copy(x_ref, tile, sems.at[0]).wait()
    pltpu.make_async_copy(x_ref, tile, sems.at[1]).wait()


def _dispatch_rows(x, dest, n_blocks, *, tm=256):
    s, d = x.shape
    return pl.pallas_call(
        functools.partial(_dispatch_kernel, tm=tm, s=s),
        grid_spec=pltpu.PrefetchScalarGridSpec(
            num_scalar_prefetch=1,
            grid=(s // tm,),
            in_specs=[pl.BlockSpec((tm, d), lambda i, dest: (i, 0))],
            out_specs=pl.BlockSpec(memory_space=pl.ANY),
            scratch_shapes=[pltpu.SemaphoreType.DMA((2,))]),
        out_shape=jax.ShapeDtypeStruct((n_blocks * MOE_ROWS, d), x.dtype),
        compiler_params=_params(("arbitrary",), 32),
        name="moe_dispatch",
    )(dest, x)


def _ffn_kernel(be_ref, nused_ref, xs_ref, w1_ref, w3_ref, w2_ref, ys_ref):
    b = pl.program_id(0)

    @pl.when(b < nused_ref[0])
    def _():
        x_lo, x_hi = _unpack_bf16_pair(xs_ref[...])
        x_lo = x_lo.astype(BF16)
        x_hi = x_hi.astype(BF16)
        hd = x_lo.shape[1]

        def up(w_ref):
            return (jnp.dot(x_lo, w_ref[0, :hd, :].astype(BF16), preferred_element_type=F32)
                    + jnp.dot(x_hi, w_ref[0, hd:, :].astype(BF16), preferred_element_type=F32))

        h1 = up(w1_ref)
        h3 = up(w3_ref)
        hmid = (h1 * jax.nn.sigmoid(h1) * h3).astype(BF16)
        y = jnp.dot(hmid, w2_ref[0].astype(BF16), preferred_element_type=F32)
        ys_ref[...] = _pack_bf16_pair(y[:, :hd], y[:, hd:])


def _grouped_ffn(xs, w1, w3, w2, layer, block_expert, n_used):
    d = xs.shape[1]
    dm = w1.shape[2]
    de = w1.shape[3]
    blk = lambda b, be, nu: (jnp.minimum(b, nu[0] - 1), 0)
    wsel = lambda b, be, nu: (layer, be[b], 0, 0)
    return pl.pallas_call(
        _ffn_kernel,
        grid_spec=pltpu.PrefetchScalarGridSpec(
            num_scalar_prefetch=2,
            grid=(xs.shape[0] // MOE_ROWS,),
            in_specs=[pl.BlockSpec((MOE_ROWS, d), blk),
                      pl.BlockSpec((None, 1, dm, de), wsel),
                      pl.BlockSpec((None, 1, dm, de), wsel),
                      pl.BlockSpec((None, 1, de, dm), wsel)],
            out_specs=pl.BlockSpec((MOE_ROWS, d), blk)),
        out_shape=jax.ShapeDtypeStruct(xs.shape, jnp.uint32),
        compiler_params=_params(("arbitrary",), 56),
        name="moe_ffn",
    )(block_expert, n_used, xs, w1, w3, w2)


def _combine_ple_kernel(pos_ref, ys_hbm, x1_ref, route_ref, g_ref, b_ref, wg_ref, bg_ref, p_ref, wp_ref,
                        o_ref, ob_ref, ybuf_ref, sems, *, tm, s):
    i = pl.program_id(0)
    n = pl.num_programs(0)
    slot = i % 2

    def start_row(t0, buf, r):
        for k in range(TOP_K):
            pltpu.make_async_copy(ys_hbm.at[pl.ds(pos_ref[k * s + t0 + r], 1), :],
                                  ybuf_ref.at[buf, k, pl.ds(r, 1), :], sems.at[buf, k]).start(priority=k)

    def wait_buffer(buf):
        tile = ys_hbm.at[pl.ds(0, tm), :]
        for k in range(TOP_K):
            pltpu.make_async_copy(tile, ybuf_ref.at[buf, k], sems.at[buf, k]).wait()

    @pl.when(i == 0)
    def _():
        def issue(g, carry):
            for u in range(ROW_UNROLL):
                start_row(0, 0, g * ROW_UNROLL + u)
            return carry

        lax.fori_loop(0, tm // ROW_UNROLL, issue, 0)

    wait_buffer(slot)
    nxt = jnp.minimum(i + 1, n - 1) * tm
    for r in range(tm):
        start_row(nxt, 1 - slot, r)

    route = route_ref[...]
    f = None
    for k in range(TOP_K):
        lo, hi = _unpack_bf16_pair(ybuf_ref[slot, k])
        fk = route[:, TOP_K + k:TOP_K + k + 1] * jnp.concatenate([lo, hi], axis=1)
        f = fk if f is None else f + fk
    x2 = _layer_norm_rows(ALPHA * x1_ref[...] + f, g_ref[...], b_ref[...])
    gate = jax.nn.sigmoid(jnp.dot(x2.astype(BF16), wg_ref[...], preferred_element_type=F32) + bg_ref[...])
    emb = jnp.dot(p_ref[...].astype(BF16), wp_ref[...], preferred_element_type=F32)
    out = x2 + gate * emb
    o_ref[...] = out
    ob_ref[...] = out.astype(BF16)

    @pl.when(i == n - 1)
    def _():
        wait_buffer(1 - slot)


def _combine_ple(ys, pos, x1, route, ln_g, ln_b, w_gate, b_gate, p, w_proj, *, tm=256):
    s, d = x1.shape
    pd = p.shape[1]
    rows = lambda n: pl.BlockSpec((tm, n), lambda i, pos: (i, 0))
    full = lambda a: pl.BlockSpec(a.shape, lambda i, pos: (0,) * a.ndim)
    g2, b2, bg = ln_g.reshape(1, d), ln_b.reshape(1, d), b_gate.reshape(1, d)
    return pl.pallas_call(
        functools.partial(_combine_ple_kernel, tm=tm, s=s),
        grid_spec=pltpu.PrefetchScalarGridSpec(
            num_scalar_prefetch=1,
            grid=(s // tm,),
            in_specs=[pl.BlockSpec(memory_space=pl.ANY), rows(d), rows(ROUTE_LANES), full(g2), full(b2),
                      full(w_gate), full(bg), rows(pd), full(w_proj)],
            out_specs=[rows(d), rows(d)],
            scratch_shapes=[pltpu.VMEM((2, TOP_K, tm, d // 2), jnp.uint32),
                            pltpu.SemaphoreType.DMA((2, TOP_K))]),
        out_shape=[jax.ShapeDtypeStruct((s, d), F32), jax.ShapeDtypeStruct((s, d), BF16)],
        compiler_params=_params(("arbitrary",), 56),
        name="moe_combine_ple",
    )(pos, ys, x1, route, g2, b2, w_gate, bg, p, w_proj)


def _rglru_kernel(cur_ref, prev_ref, gate_ref, cw_ref, cb_ref, wa_ref, ba_ref, wi_ref, bi_ref, lam_ref,
                  o_ref, buf_ref, a_ref, b_ref, h_ref, carry_ref, *, tt, tc, width):
    i = pl.program_id(1)

    @pl.when(i == 0)
    def _():
        carry_ref[...] = jnp.zeros(carry_ref.shape, F32)

    buf_ref[0:RNN_HALO, :] = jnp.where(i > 0, prev_ref[...], 0.0)
    buf_ref[RNN_HALO:, :] = cur_ref[...]
    off = RNN_HALO - (width - 1)
    z = -lam_ref[...]
    softplus = jnp.log1p(jnp.exp(-jnp.abs(z))) + jnp.maximum(z, 0.0)

    def sigmoid(v):
        return 0.5 * jnp.tanh(0.5 * v) + 0.5

    for hh in range(tc // RNN_HEAD_DIM):
        cols = slice(hh * RNN_HEAD_DIM, (hh + 1) * RNN_HEAD_DIM)
        xh = jnp.broadcast_to(cb_ref[:, cols], (tt, RNN_HEAD_DIM))
        for k in range(width):
            xh = xh + buf_ref[off + k:off + k + tt, cols] * cw_ref[k:k + 1, cols]
        xhb = xh.astype(BF16)
        r = sigmoid(jnp.dot(xhb, wa_ref[hh], preferred_element_type=F32) + ba_ref[:, cols])
        ig = sigmoid(jnp.dot(xhb, wi_ref[hh], preferred_element_type=F32) + bi_ref[:, cols])
        log_a = (-RG_C) * r * softplus[:, cols]
        a = jnp.exp(log_a)
        bval = jnp.sqrt(-jnp.tanh(log_a) * (a * a + 1.0)) * (ig * xh)
        a_ref[:, cols] = a
        b_ref[:, cols] = bval

    def group(gi, h):
        base = pl.multiple_of(gi * 8, 8)
        for r in range(8):
            row = pl.ds(base + r, 1)
            h = a_ref[row, :] * h + b_ref[row, :]
            h_ref[row, :] = h
        return h

    carry_ref[...] = lax.fori_loop(0, tt // 8, group, carry_ref[...])
    o_ref[...] = (gate_ref[...].astype(F32) * h_ref[...]).astype(o_ref.dtype)


def _rglru(xr, gate, conv_w, conv_b, w_a, b_a, w_i, b_i, lam, *, tt=256, tc=2048):
    s, wd = xr.shape
    width = conv_w.shape[0]
    hpc = tc // RNN_HEAD_DIM
    hb = tt // RNN_HALO
    row = lambda v: v.reshape(1, wd)
    cvec = lambda: pl.BlockSpec((1, tc), lambda c, i: (0, c))
    heads = lambda: pl.BlockSpec((hpc, RNN_HEAD_DIM, RNN_HEAD_DIM), lambda c, i: (c, 0, 0))
    tile = lambda: pl.BlockSpec((tt, tc), lambda c, i: (i, c))
    return pl.pallas_call(
        functools.partial(_rglru_kernel, tt=tt, tc=tc, width=width),
        grid=(wd // tc, s // tt),
        in_specs=[tile(),
                  pl.BlockSpec((RNN_HALO, tc), lambda c, i: (jnp.maximum(i * hb - 1, 0), c)),
                  tile(),
                  pl.BlockSpec((width, tc), lambda c, i: (0, c)),
                  cvec(), heads(), cvec(), heads(), cvec(), cvec()],
        out_specs=tile(),
        out_shape=jax.ShapeDtypeStruct((s, wd), BF16),
        scratch_shapes=[pltpu.VMEM((tt + RNN_HALO, tc), F32), pltpu.VMEM((tt, tc), F32),
                        pltpu.VMEM((tt, tc), F32), pltpu.VMEM((tt, tc), F32), pltpu.VMEM((1, tc), F32)],
        compiler_params=_params(("parallel", "arbitrary"), 32),
        name="rglru",
    )(xr, xr, gate, conv_w, row(conv_b), w_a.astype(BF16), row(b_a), w_i.astype(BF16), row(b_i), row(lam))


def _moe_and_ple(mixed, i, p, ln_ffn_g, ln_ffn_b, moe_w1, moe_w3, moe_w2, ple_w_proj, ple_w_gate, ple_b_gate):
    x1, x1_pairs, route = mixed
    dest, block_expert, n_used = _moe_plan(route)
    xs = _dispatch_rows(x1_pairs, dest, block_expert.shape[0])
    ys = _grouped_ffn(xs, moe_w1, moe_w3, moe_w2, i, block_expert, n_used)
    return _combine_ple(ys, dest, x1, route, ln_ffn_g[i], ln_ffn_b[i], ple_w_gate[i].astype(BF16),
                        ple_b_gate[i], p[i, 0], ple_w_proj[i].astype(BF16))


def kernel(x, p, positions, ev_w_in, ev_conv_w, ev_conv_b, ev_cnorm_g, ev_cnorm_b, ev_qnorm_g, ev_w_uq, ev_kvnorm_g, ev_w_ukv, ev_w_out, od_w_in, od_conv_w, od_conv_b, od_w_a, od_b_a, od_w_i, od_b_i, od_lam, od_w_out, ln_mix_g, ln_mix_b, ln_ffn_g, ln_ffn_b, moe_w_grp, moe_b_grp, moe_w_exp, moe_b_exp, moe_w1, moe_w3, moe_w2, ple_w_proj, ple_w_gate, ple_b_gate):
    x0 = x[0]
    x0b = x0.astype(BF16)
    moe = functools.partial(_moe_and_ple, p=p, ln_ffn_g=ln_ffn_g, ln_ffn_b=ln_ffn_b, moe_w1=moe_w1,
                            moe_w3=moe_w3, moe_w2=moe_w2, ple_w_proj=ple_w_proj, ple_w_gate=ple_w_gate,
                            ple_b_gate=ple_b_gate)

    w_in = ev_w_in[0]
    u = _glu_proj(x0b, w_in[:, :2 * CONV_CH].astype(BF16))
    ub = _conv_gln(u, ev_conv_w[0], ev_conv_b[0], ev_cnorm_g[0], ev_cnorm_b[0], groups=CONV_GROUPS)
    cosm, sinm = _rope_tables(positions[0])
    qn, qr, kn, kr, v = _mla_proj(x0b, w_in[:, 2 * CONV_CH:], ev_qnorm_g[0], ev_w_uq[0], ev_kvnorm_g[0],
                                  ev_w_ukv[0], cosm, sinm)
    att = _attention(qn, qr, kn, kr, v)
    mixed0 = _out_ln_route(ub, att, ev_w_out[0].astype(BF16), x0, ln_mix_g[0], ln_mix_b[0],
                           moe_w_grp[0], moe_b_grp[0], moe_w_exp[0], moe_b_exp[0])
    x3, x3b = moe(mixed0, 0)

    w_in1 = od_w_in[0]
    gate = _proj(x3b, w_in1[:, :RNN_WIDTH].astype(BF16), act="gelu", out_dtype=BF16)
    xr = _proj(x3b, w_in1[:, RNN_WIDTH:].astype(BF16), act=None, out_dtype=F32)
    y = _rglru(xr, gate, od_conv_w[0], od_conv_b[0], od_w_a[0], od_b_a[0], od_w_i[0], od_b_i[0], od_lam[0])
    mixed1 = _out_ln_route(y, y, od_w_out[0].astype(BF16), x3, ln_mix_g[1], ln_mix_b[1],
                           moe_w_grp[1], moe_b_grp[1], moe_w_exp[1], moe_b_exp[1], col1=0, col2=1)
    x6, _ = moe(mixed1, 1)
    return x6[None]
```

```python
import functools
import math

import jax
import jax.numpy as jnp
from jax import lax
from jax.experimental import pallas as pl
from jax.experimental.pallas import tpu as pltpu

F32 = jnp.float32
BF16 = jnp.bfloat16

D_MODEL = 2048
SEQ = 8192
DEPTH = 2
CONV_CH = 1024
CONV_GROUPS = 16
CONV_WIDTH = 31
MLA_HEADS = 8
QK_NOPE = 128
QK_ROPE = 64
V_HEAD = 128
Q_RANK = 512
KV_RANK = 256
ROPE_THETA = 10000.0
RNN_WIDTH = 2048
RNN_HEADS = 16
RNN_HEAD_DIM = RNN_WIDTH // RNN_HEADS
RNN_CONV_WIDTH = 4
RG_C = 8.0
N_GROUPS = 8
EXPERTS_PER_GROUP = 8
N_EXPERTS = N_GROUPS * EXPERTS_PER_GROUP
TOP_K = 2
D_EXPERT = 512
PLE_DIM = 256
ALPHA = (2 * DEPTH) ** 0.25
LN_EPS = 1e-5
RMS_EPS = 1e-6

LANES = 128
SUBLANES = 8
CONV_HALO = 32
RNN_HALO = 8
MOE_ROWS = 256
ROUTE_LANES = 128
MIB = 2 ** 20


def _params(semantics, vmem_mib):
    return pltpu.CompilerParams(dimension_semantics=semantics, vmem_limit_bytes=vmem_mib * MIB)


def _layer_norm_rows(z, g, b):
    mu = jnp.mean(z, axis=-1, keepdims=True)
    d = z - mu
    var = jnp.mean(d * d, axis=-1, keepdims=True)
    return d * lax.rsqrt(var + LN_EPS) * g + b


def _pack_bf16_pair(a, b):
    def rounded_bits(v):
        u = lax.bitcast_convert_type(v, jnp.uint32)
        return u + jnp.uint32(0x7FFF) + ((u >> 16) & jnp.uint32(1))

    return (rounded_bits(a) & jnp.uint32(0xFFFF0000)) | (rounded_bits(b) >> 16)


def _unpack_bf16_pair(u):
    a = lax.bitcast_convert_type(u & jnp.uint32(0xFFFF0000), F32)
    b = lax.bitcast_convert_type(u << 16, F32)
    return a, b


def _split_bf16(v):
    hi = v.astype(BF16)
    lo = (v - hi.astype(F32)).astype(BF16)
    return hi, lo


def _gelu_tanh(x):
    c = math.sqrt(2.0 / math.pi)
    return 0.5 * x * (1.0 + jnp.tanh(c * (x + 0.044715 * (x * x * x))))


def _proj_kernel(x_ref, w_ref, o_ref, *, act):
    y = jnp.dot(x_ref[...], w_ref[...], preferred_element_type=F32)
    if act == "gelu":
        y = _gelu_tanh(y)
    o_ref[...] = y.astype(o_ref.dtype)


def _proj(x, w, *, act, out_dtype, tm=512, tn=512):
    m, k = x.shape
    n = w.shape[1]
    return pl.pallas_call(
        functools.partial(_proj_kernel, act=act),
        grid=(n // tn, m // tm),
        in_specs=[pl.BlockSpec((tm, k), lambda j, i: (i, 0)),
                  pl.BlockSpec((k, tn), lambda j, i: (0, j))],
        out_specs=pl.BlockSpec((tm, tn), lambda j, i: (i, j)),
        out_shape=jax.ShapeDtypeStruct((m, n), out_dtype),
        compiler_params=_params(("parallel", "parallel"), 40),
        name="proj_" + str(act),
    )(x, w)


def _glu_kernel(x_ref, wv_ref, wg_ref, o_ref):
    x = x_ref[...]
    v = jnp.dot(x, wv_ref[...], preferred_element_type=F32)
    g = jnp.dot(x, wg_ref[...], preferred_element_type=F32)
    o_ref[...] = v * jax.nn.sigmoid(g)


def _glu_proj(x, w, *, tm=512, tn=512):
    m, k = x.shape
    n = w.shape[1] // 2
    nb = n // tn
    return pl.pallas_call(
        _glu_kernel,
        grid=(nb, m // tm),
        in_specs=[pl.BlockSpec((tm, k), lambda j, i: (i, 0)),
                  pl.BlockSpec((k, tn), lambda j, i: (0, j)),
                  pl.BlockSpec((k, tn), lambda j, i: (0, j + nb))],
        out_specs=pl.BlockSpec((tm, tn), lambda j, i: (i, j)),
        out_shape=jax.ShapeDtypeStruct((m, n), F32),
        compiler_params=_params(("parallel", "parallel"), 40),
        name="glu_proj",
    )(x, w, w)


def _conv_gln_kernel(cur_ref, prev_ref, w_ref, b_ref, g_ref, beta_ref, gm_ref, o_ref,
                     buf_ref, sh_ref, y_ref, *, tt, ch, width, rc, cc):
    i = pl.program_id(0)
    buf_ref[0:CONV_HALO, :] = jnp.where(i > 0, prev_ref[...], 0.0)
    buf_ref[CONV_HALO:, :] = cur_ref[...]
    off = CONV_HALO - (width - 1)
    sh_rows = sh_ref.shape[1]
    step = 7 * SUBLANES
    assert sh_rows % step == 0
    for b in range(1, SUBLANES):
        for c0 in range(0, ch, cc):
            for r0 in range(0, sh_rows, step):
                sh_ref[b - 1, r0:r0 + step, c0:c0 + cc] = buf_ref[r0 + b:r0 + b + step, c0:c0 + cc]
    for c0 in range(0, ch, cc):
        for r0 in range(0, tt, rc):
            acc = jnp.broadcast_to(b_ref[:, c0:c0 + cc], (rc, cc))
            for k in range(width):
                q, b = divmod(off + k, SUBLANES)
                lo = r0 + q * SUBLANES
                src = buf_ref[lo:lo + rc, c0:c0 + cc] if b == 0 else sh_ref[b - 1, lo:lo + rc, c0:c0 + cc]
                acc = acc + src * w_ref[k:k + 1, c0:c0 + cc]
            y_ref[r0:r0 + rc, c0:c0 + cc] = acc
    gm = gm_ref[...]

    def seg_mean(v):
        hi, lo = _split_bf16(v)
        return (jnp.dot(hi, gm, preferred_element_type=F32)
                + jnp.dot(lo, gm, preferred_element_type=F32))

    for c0 in range(0, ch, LANES):
        y = y_ref[:, c0:c0 + LANES]
        d = y - seg_mean(y)
        var = seg_mean(d * d)
        z = d * lax.rsqrt(var + LN_EPS) * g_ref[:, c0:c0 + LANES] + beta_ref[:, c0:c0 + LANES]
        o_ref[:, c0:c0 + LANES] = (z * jax.nn.sigmoid(z)).astype(o_ref.dtype)


def _conv_gln(u, w, b, g, beta, *, groups, tt=256):
    s, ch = u.shape
    width = w.shape[0]
    gsz = ch // groups
    assert LANES % gsz == 0 and width - 1 <= CONV_HALO and tt % CONV_HALO == 0
    wpad = jnp.zeros((CONV_HALO, ch), F32).at[:width].set(w)
    lane = jnp.arange(LANES)
    gm = jnp.where((lane[:, None] // gsz) == (lane[None, :] // gsz), 1.0 / gsz, 0.0).astype(BF16)
    hb = tt // CONV_HALO
    row = lambda v: v.reshape(1, ch)
    return pl.pallas_call(
        functools.partial(_conv_gln_kernel, tt=tt, ch=ch, width=width, rc=32, cc=256),
        grid=(s // tt,),
        in_specs=[pl.BlockSpec((tt, ch), lambda i: (i, 0)),
                  pl.BlockSpec((CONV_HALO, ch), lambda i: (jnp.maximum(i * hb - 1, 0), 0)),
                  pl.BlockSpec((CONV_HALO, ch), lambda i: (0, 0)),
                  pl.BlockSpec((1, ch), lambda i: (0, 0)),
                  pl.BlockSpec((1, ch), lambda i: (0, 0)),
                  pl.BlockSpec((1, ch), lambda i: (0, 0)),
                  pl.BlockSpec((LANES, LANES), lambda i: (0, 0))],
        out_specs=pl.BlockSpec((tt, ch), lambda i: (i, 0)),
        out_shape=jax.ShapeDtypeStruct((s, ch), BF16),
        scratch_shapes=[pltpu.VMEM((tt + CONV_HALO, ch), F32),
                        pltpu.VMEM((SUBLANES - 1, tt + CONV_HALO - SUBLANES, ch), F32),
                        pltpu.VMEM((tt, ch), F32)],
        compiler_params=_params(("parallel",), 40),
        name="conv_gln",
    )(u, u, wpad, row(b), row(g), row(beta), gm)


def _rope_table_kernel(pos_ref, invf_ref, cos_ref, sin_ref):
    ang = pos_ref[...].astype(F32) * invf_ref[...]
    cos_ref[...] = jnp.cos(ang)
    sin_ref[...] = jnp.sin(ang)


def _rope_tables(positions):
    s = positions.shape[0]
    half = QK_ROPE // 2
    per_row = LANES // half
    inv_freq = 1.0 / (ROPE_THETA ** (jnp.arange(0, QK_ROPE, 2, dtype=F32) / QK_ROPE))
    pos_rep = jnp.repeat(positions.reshape(s // per_row, per_row), half, axis=1)
    invf = jnp.tile(inv_freq, per_row).reshape(1, LANES)
    cos, sin = pl.pallas_call(
        _rope_table_kernel,
        out_shape=[jax.ShapeDtypeStruct((s // per_row, LANES), F32)] * 2,
        name="rope_table",
    )(pos_rep, invf)
    cos = cos.reshape(s, half)
    sin = sin.reshape(s, half)
    zero = jnp.zeros((s, LANES - QK_ROPE), F32)
    return jnp.concatenate([cos, cos, zero], axis=1), jnp.concatenate([-sin, sin, zero], axis=1)


def _mla_proj_kernel(x_ref, wm_ref, qg_ref, kvg_ref, wq_ref, wkv_ref, c_ref, s_ref,
                     qn_ref, qr_ref, kn_ref, kr_ref, v_ref, *, scale):
    nq = MLA_HEADS * QK_NOPE
    c = jnp.dot(x_ref[...], wm_ref[...], preferred_element_type=F32)
    cq = c[:, :Q_RANK]
    ckv = c[:, Q_RANK:Q_RANK + KV_RANK]
    kr_pad = c[:, Q_RANK + KV_RANK:Q_RANK + KV_RANK + LANES]
    kr_rot = c[:, Q_RANK + KV_RANK + LANES:]
    cqn = cq * lax.rsqrt(jnp.mean(cq * cq, axis=-1, keepdims=True) + RMS_EPS) * qg_ref[...]
    ckvn = ckv * lax.rsqrt(jnp.mean(ckv * ckv, axis=-1, keepdims=True) + RMS_EPS) * kvg_ref[...]
    q = jnp.dot(cqn.astype(BF16), wq_ref[...], preferred_element_type=F32)
    kv = jnp.dot(ckvn.astype(BF16), wkv_ref[...], preferred_element_type=F32)
    cosm = c_ref[...]
    sinm = s_ref[...]
    qn_ref[...] = (q[:, :nq] * scale).astype(BF16)
    for h in range(MLA_HEADS):
        lo = nq + h * LANES
        rot = q[:, lo:lo + LANES] * cosm + q[:, lo + nq:lo + nq + LANES] * sinm
        qr_ref[:, h * LANES:(h + 1) * LANES] = (rot * scale).astype(BF16)
    kn_ref[...] = kv[:, :nq].astype(BF16)
    v_ref[...] = kv[:, nq:].astype(BF16)
    kr_ref[...] = (kr_pad * cosm + kr_rot * sinm).astype(BF16)


def _rope_swap(w):
    half = QK_ROPE // 2
    return jnp.concatenate([w[..., half:], w[..., :half]], axis=-1)


def _mla_proj(xb, w_in_mla, qnorm_g, w_uq, kvnorm_g, w_ukv, cosm, sinm, *, tm=256):
    s, d = xb.shape
    nq = MLA_HEADS * QK_NOPE
    pad = lambda w: jnp.concatenate([w, jnp.zeros(w.shape[:-1] + (LANES - QK_ROPE,), w.dtype)], axis=-1)
    w_kr = w_in_mla[:, Q_RANK + KV_RANK:]
    wm = jnp.concatenate([w_in_mla[:, :Q_RANK + KV_RANK], pad(w_kr), pad(_rope_swap(w_kr))], axis=1).astype(BF16)
    wq3 = w_uq.reshape(Q_RANK, MLA_HEADS, QK_NOPE + QK_ROPE)
    wq_rope = wq3[:, :, QK_NOPE:]
    wq = jnp.concatenate([wq3[:, :, :QK_NOPE].reshape(Q_RANK, nq),
                          pad(wq_rope).reshape(Q_RANK, MLA_HEADS * LANES),
                          pad(_rope_swap(wq_rope)).reshape(Q_RANK, MLA_HEADS * LANES)], axis=1).astype(BF16)
    wkv3 = w_ukv.reshape(KV_RANK, MLA_HEADS, QK_NOPE + V_HEAD)
    wkv = jnp.concatenate([wkv3[:, :, :QK_NOPE].reshape(KV_RANK, nq),
                           wkv3[:, :, QK_NOPE:].reshape(KV_RANK, MLA_HEADS * V_HEAD)], axis=1).astype(BF16)
    scale = (QK_NOPE + QK_ROPE) ** -0.5 * math.log2(math.e)
    full = lambda a: pl.BlockSpec(a.shape, lambda i: (0,) * a.ndim)
    rows = lambda n: pl.BlockSpec((tm, n), lambda i: (i, 0))
    qg = qnorm_g.reshape(1, Q_RANK)
    kvg = kvnorm_g.reshape(1, KV_RANK)
    return pl.pallas_call(
        functools.partial(_mla_proj_kernel, scale=scale),
        grid=(s // tm,),
        in_specs=[rows(d), full(wm), full(qg), full(kvg), full(wq), full(wkv), rows(LANES), rows(LANES)],
        out_specs=[rows(nq), rows(MLA_HEADS * LANES), rows(nq), rows(LANES), rows(MLA_HEADS * V_HEAD)],
        out_shape=[jax.ShapeDtypeStruct((s, nq), BF16),
                   jax.ShapeDtypeStruct((s, MLA_HEADS * LANES), BF16),
                   jax.ShapeDtypeStruct((s, nq), BF16),
                   jax.ShapeDtypeStruct((s, LANES), BF16),
                   jax.ShapeDtypeStruct((s, MLA_HEADS * V_HEAD), BF16)],
        compiler_params=_params(("parallel",), 48),
        name="mla_proj",
    )(xb, wm, qg, kvg, wq, wkv, cosm, sinm)


def _attn_kernel(qn_ref, qr_ref, kn_ref, kr_ref, v_ref, o_ref, m_ref, acc_ref, *, tq, tk, hb):
    qi = pl.program_id(1)
    per_q = tq // tk
    m_ref[...] = jnp.full(m_ref.shape, -jnp.inf, F32)
    acc_ref[...] = jnp.zeros(acc_ref.shape, F32)
    ones = jnp.ones((tk, LANES), BF16)
    head_cols = [slice(h * LANES, (h + 1) * LANES) for h in range(hb)]
    qs = [jnp.concatenate([qn_ref[:, c], qr_ref[:, c]], axis=1) for c in head_cols]

    def chunk(j, diag):
        rows = pl.ds(pl.multiple_of(j * tk, tk), tk)
        k_rope = kr_ref[rows, :]
        for h, c in enumerate(head_cols):
            k = jnp.concatenate([kn_ref[rows, c], k_rope], axis=1)
            s = lax.dot_general(qs[h], k, (((1,), (1,)), ((), ())), preferred_element_type=F32)
            if diag is not None:
                keep = (lax.broadcasted_iota(jnp.int32, s.shape, 0)
                        >= lax.broadcasted_iota(jnp.int32, s.shape, 1) + diag * tk)
                s = jnp.where(keep, s, -jnp.inf)
            m_old = m_ref[h]
            m_new = jnp.maximum(m_old, jnp.max(s, axis=-1, keepdims=True))
            alpha = jnp.exp2(m_old - m_new)
            p = jnp.exp2(s - jnp.concatenate([m_new] * (tk // LANES), axis=1))
            v_ext = jnp.concatenate([v_ref[rows, c], ones], axis=1)
            acc_ref[h] = (jnp.concatenate([alpha, alpha], axis=1) * acc_ref[h]
                          + jnp.dot(p.astype(BF16), v_ext, preferred_element_type=F32))
            m_ref[h] = m_new

    def body(j, carry):
        chunk(j, None)
        return carry

    lax.fori_loop(0, qi * per_q, body, 0)
    for d in range(per_q):
        chunk(qi * per_q + d, d)
    for h, c in enumerate(head_cols):
        acc = acc_ref[h]
        o_ref[:, c] = (acc[:, :V_HEAD] / acc[:, V_HEAD:]).astype(o_ref.dtype)


def _attention(qn, qr, kn, kr, v, *, tq=1024, tk=512, hb=2):
    s = qn.shape[0]
    wide = hb * LANES
    tile = lambda: pl.BlockSpec((tq, wide), lambda h, i: (i, h))
    keys = lambda: pl.BlockSpec((s, wide), lambda h, i: (0, h))
    return pl.pallas_call(
        functools.partial(_attn_kernel, tq=tq, tk=tk, hb=hb),
        grid=(MLA_HEADS // hb, s // tq),
        in_specs=[tile(), tile(), keys(), pl.BlockSpec((s, LANES), lambda h, i: (0, 0)), keys()],
        out_specs=tile(),
        out_shape=jax.ShapeDtypeStruct((s, MLA_HEADS * V_HEAD), BF16),
        scratch_shapes=[pltpu.VMEM((hb, tq, LANES), F32), pltpu.VMEM((hb, tq, 2 * V_HEAD), F32)],
        compiler_params=_params(("parallel", "parallel"), 48),
        name="mla_attention",
    )(qn, qr, kn, kr, v)


def _route(logits):
    lane = lax.broadcasted_iota(jnp.int32, logits.shape, 1).astype(F32)
    big = float(2 * ROUTE_LANES)
    neg = -jnp.inf
    gl = jnp.where(lane < N_GROUPS, logits, neg)
    gmax = jnp.max(gl, axis=-1, keepdims=True)
    gsum = jnp.sum(jnp.exp(gl - gmax), axis=-1, keepdims=True)
    g_prob = 1.0 / gsum
    g_idx = jnp.min(jnp.where(gl == gmax, lane, big), axis=-1, keepdims=True)
    lo = N_GROUPS + g_idx * EXPERTS_PER_GROUP
    el = jnp.where((lane >= lo) & (lane < lo + EXPERTS_PER_GROUP), logits, neg)
    emax = jnp.max(el, axis=-1, keepdims=True)
    esum = jnp.sum(jnp.exp(el - emax), axis=-1, keepdims=True)
    i0 = jnp.min(jnp.where(el == emax, lane, big), axis=-1, keepdims=True)
    el2 = jnp.where(lane == i0, neg, el)
    emax2 = jnp.max(el2, axis=-1, keepdims=True)
    i1 = jnp.min(jnp.where(el2 == emax2, lane, big), axis=-1, keepdims=True)
    p0 = 1.0 / esum
    p1 = jnp.exp(emax2 - emax) / esum
    g0 = g_prob * p0 / (p0 + p1)
    g1 = g_prob * p1 / (p0 + p1)
    e0 = i0 - N_GROUPS
    e1 = i1 - N_GROUPS
    return jnp.where(lane == 0, e0, jnp.where(lane == 1, e1, jnp.where(lane == 2, g0, jnp.where(lane == 3, g1, 0.0))))


def _out_ln_route_kernel(a1_ref, a2_ref, w_ref, x_ref, g_ref, b_ref, wrh_ref, wrl_ref, br_ref,
                         x1_ref, x1p_ref, route_ref, *, sub):
    half = a1_ref.shape[1]
    for r0 in range(0, x_ref.shape[0], sub):
        rows = slice(r0, r0 + sub)
        m = (jnp.dot(a1_ref[rows, :], w_ref[:half, :], preferred_element_type=F32)
             + jnp.dot(a2_ref[rows, :], w_ref[half:, :], preferred_element_type=F32))
        x1 = _layer_norm_rows(ALPHA * x_ref[rows, :] + m, g_ref[...], b_ref[...])
        x1_ref[rows, :] = x1
        hd = x1.shape[1] // 2
        x1p_ref[rows, :] = _pack_bf16_pair(x1[:, :hd], x1[:, hd:])
        xh, xl = _split_bf16(x1)
        wrh = wrh_ref[...]
        logits = (jnp.dot(xh, wrh, preferred_element_type=F32) + jnp.dot(xl, wrh, preferred_element_type=F32)
                  + jnp.dot(xh, wrl_ref[...], preferred_element_type=F32) + br_ref[...])
        route_ref[rows, :] = _route(logits)


def _out_ln_route(a1, a2, w, x, ln_g, ln_b, w_grp, b_grp, w_exp, b_exp, *, col1=0, col2=0, tm=512, sub=256):
    s, d = x.shape
    half = w.shape[0] // 2
    pad = ROUTE_LANES - N_GROUPS - N_EXPERTS
    wr = jnp.concatenate([w_grp, w_exp, jnp.zeros((d, pad), F32)], axis=1)
    br = jnp.concatenate([b_grp, b_exp, jnp.zeros((pad,), F32)]).reshape(1, ROUTE_LANES)
    wrh = wr.astype(BF16)
    wrl = (wr - wrh.astype(F32)).astype(BF16)
    full = lambda a: pl.BlockSpec(a.shape, lambda i: (0,) * a.ndim)
    rows = lambda n: pl.BlockSpec((tm, n), lambda i: (i, 0))
    g2 = ln_g.reshape(1, d)
    b2 = ln_b.reshape(1, d)
    return pl.pallas_call(
        functools.partial(_out_ln_route_kernel, sub=sub),
        grid=(s // tm,),
        in_specs=[pl.BlockSpec((tm, half), lambda i: (i, col1)), pl.BlockSpec((tm, half), lambda i: (i, col2)),
                  pl.BlockSpec(w.shape, lambda i: (0, 0), pipeline_mode=pl.Buffered(1)),
                  rows(d), full(g2), full(b2), full(wrh), full(wrl), full(br)],
        out_specs=[rows(d), rows(d // 2), rows(ROUTE_LANES)],
        out_shape=[jax.ShapeDtypeStruct((s, d), F32), jax.ShapeDtypeStruct((s, d // 2), jnp.uint32),
                   jax.ShapeDtypeStruct((s, ROUTE_LANES), F32)],
        compiler_params=_params(("parallel",), 48),
        name="out_ln_route",
    )(a1, a2, w, x, g2, b2, wrh, wrl, br)


def _rank_kernel(route_ref, tri_ref, rank_ref, cnt_ref, carry_ref):
    @pl.when(pl.program_id(0) == 0)
    def _():
        carry_ref[...] = jnp.zeros(carry_ref.shape, F32)

    route = route_ref[...]
    lane = lax.broadcasted_iota(jnp.int32, route.shape, 1).astype(F32)
    oh0 = jnp.where(lane == route[:, 0:1], 1.0, 0.0)
    oh1 = jnp.where(lane == route[:, 1:2], 1.0, 0.0)
    both = oh0 + oh1
    before = jnp.dot(tri_ref[...], both.astype(BF16), preferred_element_type=F32) + carry_ref[...]
    r0 = jnp.sum(before * oh0, axis=-1, keepdims=True)
    r1 = jnp.sum(before * oh1, axis=-1, keepdims=True)
    rank_ref[...] = jnp.where(lane == 0.0, r0, jnp.where(lane == 1.0, r1, 0.0))
    carry_ref[...] = carry_ref[...] + jnp.sum(both, axis=0, keepdims=True)
    cnt_ref[...] = carry_ref[...]


def _dest_kernel(route_ref, rank_ref, pstart_ref, dest_ref):
    route = route_ref[...]
    rank = rank_ref[...]
    lane = lax.broadcasted_iota(jnp.int32, route.shape, 1).astype(F32)
    ps = pstart_ref[...]
    d0 = jnp.sum(jnp.where(lane == route[:, 0:1], ps, 0.0), axis=-1, keepdims=True) + rank[:, 0:1]
    d1 = jnp.sum(jnp.where(lane == route[:, 1:2], ps, 0.0), axis=-1, keepdims=True) + rank[:, 1:2]
    dest_ref[...] = jnp.where(lane == 0.0, d0, jnp.where(lane == 1.0, d1, 0.0)).astype(jnp.int32)


def _moe_plan(route, *, tb=512):
    s = route.shape[0]
    tri = (jnp.arange(tb)[:, None] > jnp.arange(tb)[None, :]).astype(BF16)
    rows = pl.BlockSpec((tb, ROUTE_LANES), lambda i: (i, 0))
    one = lambda: pl.BlockSpec((1, ROUTE_LANES), lambda i: (0, 0))
    rank, cnt = pl.pallas_call(
        _rank_kernel,
        grid=(s // tb,),
        in_specs=[rows, pl.BlockSpec((tb, tb), lambda i: (0, 0))],
        out_specs=[rows, one()],
        out_shape=[jax.ShapeDtypeStruct((s, ROUTE_LANES), F32), jax.ShapeDtypeStruct((1, ROUTE_LANES), F32)],
        scratch_shapes=[pltpu.VMEM((1, ROUTE_LANES), F32)],
        compiler_params=_params(("arbitrary",), 32),
        name="moe_rank",
    )(route, tri)
    counts = cnt[0, :N_EXPERTS].astype(jnp.int32)
    pcounts = (counts + MOE_ROWS - 1) // MOE_ROWS * MOE_ROWS
    pends = jnp.cumsum(pcounts)
    pstarts = pends - pcounts
    n_blocks = s * TOP_K // MOE_ROWS + N_EXPERTS
    n_used = pends[-1] // MOE_ROWS
    blk = jnp.minimum(jnp.arange(n_blocks, dtype=jnp.int32), n_used - 1) * MOE_ROWS
    block_expert = jnp.minimum(jnp.sum(pends[None, :] <= blk[:, None], axis=1), N_EXPERTS - 1).astype(jnp.int32)
    ps = jnp.zeros((1, ROUTE_LANES), F32).at[0, :N_EXPERTS].set(pstarts.astype(F32))
    dest = pl.pallas_call(
        _dest_kernel,
        grid=(s // tb,),
        in_specs=[rows, rows, one()],
        out_specs=rows,
        out_shape=jax.ShapeDtypeStruct((s, ROUTE_LANES), jnp.int32),
        compiler_params=_params(("parallel",), 32),
        name="moe_dest",
    )(route, rank, ps)
    dest = dest[:, :TOP_K].T.reshape(TOP_K * s)
    return dest, block_expert, n_used.astype(jnp.int32).reshape(1)


ROW_UNROLL = 8


def _dispatch_kernel(dest_ref, x_ref, xs_hbm, sems, *, tm, s):
    t0 = pl.program_id(0) * tm

    def issue(g, carry):
        for u in range(ROW_UNROLL):
            r = g * ROW_UNROLL + u
            src = x_ref.at[pl.ds(r, 1), :]
            for k in range(TOP_K):
                pltpu.make_async_copy(src, xs_hbm.at[pl.ds(dest_ref[k * s + t0 + r], 1), :],
                                      sems.at[k]).start(priority=k)
        return carry

    lax.fori_loop(0, tm // ROW_UNROLL, issue, 0)
    tile = xs_hbm.at[pl.ds(0, tm), :]
    pltpu.make_async_copy(x_ref, tile, sems.at[0]).wait()
    pltpu.make_async_copy(x_ref, tile, sems.at[1]).wait()


def _dispatch_rows(x, dest, n_blocks, *, tm=256):
    s, d = x.shape
    return pl.pallas_call(
        functools.partial(_dispatch_kernel, tm=tm, s=s),
        grid_spec=pltpu.PrefetchScalarGridSpec(
            num_scalar_prefetch=1,
            grid=(s // tm,),
            in_specs=[pl.BlockSpec((tm, d), lambda i, dest: (i, 0))],
            out_specs=pl.BlockSpec(memory_space=pl.ANY),
            scratch_shapes=[pltpu.SemaphoreType.DMA((2,))]),
        out_shape=jax.ShapeDtypeStruct((n_blocks * MOE_ROWS, d), x.dtype),
        compiler_params=_params(("arbitrary",), 32),
        name="moe_dispatch",
    )(dest, x)


FFN_K_CHUNK = 512


def _ffn_kernel(nused_ref, run_ref, first_ref, rexp_ref, nruns_ref, xs_ref, w1_hbm, w3_hbm, w2_hbm, ys_ref,
                w1_buf, w3_buf, w2_buf, sems, *, layer):
    b = pl.program_id(0)

    def weight_copies(run, slot):
        e = rexp_ref[run]
        return [pltpu.make_async_copy(w_hbm.at[layer, e], buf.at[slot], sems.at[slot, j])
                for j, (w_hbm, buf) in enumerate(((w1_hbm, w1_buf), (w3_hbm, w3_buf), (w2_hbm, w2_buf)))]

    @pl.when(b == 0)
    def _():
        for c in weight_copies(0, 0):
            c.start()

    @pl.when(b < nused_ref[0])
    def _():
        run = run_ref[b]
        slot = run % 2

        @pl.when(first_ref[b] == 1)
        def _():
            @pl.when(run + 1 < nruns_ref[0])
            def _():
                for c in weight_copies(run + 1, 1 - slot):
                    c.start()

            for c in weight_copies(run, slot):
                c.wait()

        x_lo, x_hi = _unpack_bf16_pair(xs_ref[...])
        hd = x_lo.shape[1]
        x = jnp.concatenate([x_lo.astype(BF16), x_hi.astype(BF16)], axis=1)

        def up(buf):
            acc = None
            for k0 in range(0, 2 * hd, FFN_K_CHUNK):
                part = jnp.dot(x[:, k0:k0 + FFN_K_CHUNK], buf[slot, k0:k0 + FFN_K_CHUNK, :].astype(BF16),
                               preferred_element_type=F32)
                acc = part if acc is None else acc + part
            return acc

        h1 = up(w1_buf)
        h3 = up(w3_buf)
        hmid = (h1 * jax.nn.sigmoid(h1) * h3).astype(BF16)
        y = jnp.dot(hmid, w2_buf[slot].astype(BF16), preferred_element_type=F32)
        ys_ref[...] = _pack_bf16_pair(y[:, :hd], y[:, hd:])


def _grouped_ffn(xs, w1, w3, w2, layer, block_expert, n_used):
    d = xs.shape[1]
    dm = w1.shape[2]
    de = w1.shape[3]
    nb = block_expert.shape[0]
    first = jnp.concatenate([jnp.ones((1,), jnp.int32), (block_expert[1:] != block_expert[:-1]).astype(jnp.int32)])
    run = jnp.cumsum(first) - 1
    n_runs = run[-1:] + 1
    hit = (run[None, :] == jnp.arange(nb, dtype=jnp.int32)[:, None]) & (first[None, :] == 1)
    run_expert = jnp.sum(jnp.where(hit, block_expert[None, :], 0), axis=1).astype(jnp.int32)
    blk = lambda b, nu, *_: (jnp.minimum(b, nu[0] - 1), 0)
    return pl.pallas_call(
        functools.partial(_ffn_kernel, layer=layer),
        grid_spec=pltpu.PrefetchScalarGridSpec(
            num_scalar_prefetch=5,
            grid=(nb,),
            in_specs=[pl.BlockSpec((MOE_ROWS, d), blk),
                      pl.BlockSpec(memory_space=pl.ANY),
                      pl.BlockSpec(memory_space=pl.ANY),
                      pl.BlockSpec(memory_space=pl.ANY)],
            out_specs=pl.BlockSpec((MOE_ROWS, d), blk),
            scratch_shapes=[pltpu.VMEM((2, dm, de), F32), pltpu.VMEM((2, dm, de), F32),
                            pltpu.VMEM((2, de, dm), F32), pltpu.SemaphoreType.DMA((2, 3))]),
        out_shape=jax.ShapeDtypeStruct(xs.shape, jnp.uint32),
        compiler_params=_params(("arbitrary",), 56),
        name="moe_ffn",
    )(n_used, run.astype(jnp.int32), first, run_expert, n_runs.astype(jnp.int32), xs, w1, w3, w2)


def _combine_ple_kernel(pos_ref, ys_hbm, x1_ref, route_ref, g_ref, b_ref, wg_ref, bg_ref, p_ref, wp_ref,
                        o_ref, ob_ref, ybuf_ref, sems, *, tm, s):
    i = pl.program_id(0)
    n = pl.num_programs(0)
    slot = i % 2

    def start_row(t0, buf, r):
        for k in range(TOP_K):
            pltpu.make_async_copy(ys_hbm.at[pl.ds(pos_ref[k * s + t0 + r], 1), :],
                                  ybuf_ref.at[buf, k, pl.ds(r, 1), :], sems.at[buf, k]).start(priority=k)

    def wait_buffer(buf):
        tile = ys_hbm.at[pl.ds(0, tm), :]
        for k in range(TOP_K):
            pltpu.make_async_copy(tile, ybuf_ref.at[buf, k], sems.at[buf, k]).wait()

    @pl.when(i == 0)
    def _():
        def issue(g, carry):
            for u in range(ROW_UNROLL):
                start_row(0, 0, g * ROW_UNROLL + u)
            return carry

        lax.fori_loop(0, tm // ROW_UNROLL, issue, 0)

    wait_buffer(slot)
    nxt = jnp.minimum(i + 1, n - 1) * tm
    for r in range(tm):
        start_row(nxt, 1 - slot, r)

    route = route_ref[...]
    f = None
    for k in range(TOP_K):
        lo, hi = _unpack_bf16_pair(ybuf_ref[slot, k])
        fk = route[:, TOP_K + k:TOP_K + k + 1] * jnp.concatenate([lo, hi], axis=1)
        f = fk if f is None else f + fk
    x2 = _layer_norm_rows(ALPHA * x1_ref[...] + f, g_ref[...], b_ref[...])
    gate = jax.nn.sigmoid(jnp.dot(x2.astype(BF16), wg_ref[...], preferred_element_type=F32) + bg_ref[...])
    emb = jnp.dot(p_ref[...].astype(BF16), wp_ref[...], preferred_element_type=F32)
    out = x2 + gate * emb
    o_ref[...] = out
    ob_ref[...] = out.astype(BF16)

    @pl.when(i == n - 1)
    def _():
        wait_buffer(1 - slot)


def _combine_ple(ys, pos, x1, route, ln_g, ln_b, w_gate, b_gate, p, w_proj, *, tm=256):
    s, d = x1.shape
    pd = p.shape[1]
    rows = lambda n: pl.BlockSpec((tm, n), lambda i, pos: (i, 0))
    full = lambda a: pl.BlockSpec(a.shape, lambda i, pos: (0,) * a.ndim)
    g2, b2, bg = ln_g.reshape(1, d), ln_b.reshape(1, d), b_gate.reshape(1, d)
    return pl.pallas_call(
        functools.partial(_combine_ple_kernel, tm=tm, s=s),
        grid_spec=pltpu.PrefetchScalarGridSpec(
            num_scalar_prefetch=1,
            grid=(s // tm,),
            in_specs=[pl.BlockSpec(memory_space=pl.ANY), rows(d), rows(ROUTE_LANES), full(g2), full(b2),
                      full(w_gate), full(bg), rows(pd), full(w_proj)],
            out_specs=[rows(d), rows(d)],
            scratch_shapes=[pltpu.VMEM((2, TOP_K, tm, d // 2), jnp.uint32),
                            pltpu.SemaphoreType.DMA((2, TOP_K))]),
        out_shape=[jax.ShapeDtypeStruct((s, d), F32), jax.ShapeDtypeStruct((s, d), BF16)],
        compiler_params=_params(("arbitrary",), 56),
        name="moe_combine_ple",
    )(pos, ys, x1, route, g2, b2, w_gate, bg, p, w_proj)


def _rglru_kernel(cur_ref, prev_ref, gate_ref, cw_ref, cb_ref, wa_ref, ba_ref, wi_ref, bi_ref, lam_ref,
                  o_ref, buf_ref, a_ref, b_ref, h_ref, carry_ref, *, tt, tc, width):
    i = pl.program_id(1)

    @pl.when(i == 0)
    def _():
        carry_ref[...] = jnp.zeros(carry_ref.shape, F32)

    buf_ref[0:RNN_HALO, :] = jnp.where(i > 0, prev_ref[...], 0.0)
    buf_ref[RNN_HALO:, :] = cur_ref[...]
    off = RNN_HALO - (width - 1)
    z = -lam_ref[...]
    softplus = jnp.log1p(jnp.exp(-jnp.abs(z))) + jnp.maximum(z, 0.0)

    def sigmoid(v):
        return 0.5 * jnp.tanh(0.5 * v) + 0.5

    for hh in range(tc // RNN_HEAD_DIM):
        cols = slice(hh * RNN_HEAD_DIM, (hh + 1) * RNN_HEAD_DIM)
        xh = jnp.broadcast_to(cb_ref[:, cols], (tt, RNN_HEAD_DIM))
        for k in range(width):
            xh = xh + buf_ref[off + k:off + k + tt, cols] * cw_ref[k:k + 1, cols]
        xhb = xh.astype(BF16)
        r = sigmoid(jnp.dot(xhb, wa_ref[hh], preferred_element_type=F32) + ba_ref[:, cols])
        ig = sigmoid(jnp.dot(xhb, wi_ref[hh], preferred_element_type=F32) + bi_ref[:, cols])
        log_a = (-RG_C) * r * softplus[:, cols]
        a = jnp.exp(log_a)
        bval = jnp.sqrt(-jnp.tanh(log_a) * (a * a + 1.0)) * (ig * xh)
        a_ref[:, cols] = a
        b_ref[:, cols] = bval

    def group(gi, h):
        base = pl.multiple_of(gi * 8, 8)
        for r in range(8):
            row = pl.ds(base + r, 1)
            h = a_ref[row, :] * h + b_ref[row, :]
            h_ref[row, :] = h
        return h

    carry_ref[...] = lax.fori_loop(0, tt // 8, group, carry_ref[...])
    o_ref[...] = (gate_ref[...].astype(F32) * h_ref[...]).astype(o_ref.dtype)


def _rglru(xr, gate, conv_w, conv_b, w_a, b_a, w_i, b_i, lam, *, tt=256, tc=2048):
    s, wd = xr.shape
    width = conv_w.shape[0]
    hpc = tc // RNN_HEAD_DIM
    hb = tt // RNN_HALO
    row = lambda v: v.reshape(1, wd)
    cvec = lambda: pl.BlockSpec((1, tc), lambda c, i: (0, c))
    heads = lambda: pl.BlockSpec((hpc, RNN_HEAD_DIM, RNN_HEAD_DIM), lambda c, i: (c, 0, 0))
    tile = lambda: pl.BlockSpec((tt, tc), lambda c, i: (i, c))
    return pl.pallas_call(
        functools.partial(_rglru_kernel, tt=tt, tc=tc, width=width),
        grid=(wd // tc, s // tt),
        in_specs=[tile(),
                  pl.BlockSpec((RNN_HALO, tc), lambda c, i: (jnp.maximum(i * hb - 1, 0), c)),
                  tile(),
                  pl.BlockSpec((width, tc), lambda c, i: (0, c)),
                  cvec(), heads(), cvec(), heads(), cvec(), cvec()],
        out_specs=tile(),
        out_shape=jax.ShapeDtypeStruct((s, wd), BF16),
        scratch_shapes=[pltpu.VMEM((tt + RNN_HALO, tc), F32), pltpu.VMEM((tt, tc), F32),
                        pltpu.VMEM((tt, tc), F32), pltpu.VMEM((tt, tc), F32), pltpu.VMEM((1, tc), F32)],
        compiler_params=_params(("parallel", "arbitrary"), 32),
        name="rglru",
    )(xr, xr, gate, conv_w, row(conv_b), w_a.astype(BF16), row(b_a), w_i.astype(BF16), row(b_i), row(lam))


def _moe_and_ple(mixed, i, p, ln_ffn_g, ln_ffn_b, moe_w1, moe_w3, moe_w2, ple_w_proj, ple_w_gate, ple_b_gate):
    x1, x1_pairs, route = mixed
    dest, block_expert, n_used = _moe_plan(route)
    xs = _dispatch_rows(x1_pairs, dest, block_expert.shape[0])
    ys = _grouped_ffn(xs, moe_w1, moe_w3, moe_w2, i, block_expert, n_used)
    return _combine_ple(ys, dest, x1, route, ln_ffn_g[i], ln_ffn_b[i], ple_w_gate[i].astype(BF16),
                        ple_b_gate[i], p[i, 0], ple_w_proj[i].astype(BF16))


def kernel(x, p, positions, ev_w_in, ev_conv_w, ev_conv_b, ev_cnorm_g, ev_cnorm_b, ev_qnorm_g, ev_w_uq, ev_kvnorm_g, ev_w_ukv, ev_w_out, od_w_in, od_conv_w, od_conv_b, od_w_a, od_b_a, od_w_i, od_b_i, od_lam, od_w_out, ln_mix_g, ln_mix_b, ln_ffn_g, ln_ffn_b, moe_w_grp, moe_b_grp, moe_w_exp, moe_b_exp, moe_w1, moe_w3, moe_w2, ple_w_proj, ple_w_gate, ple_b_gate):
    x0 = x[0]
    x0b = x0.astype(BF16)
    moe = functools.partial(_moe_and_ple, p=p, ln_ffn_g=ln_ffn_g, ln_ffn_b=ln_ffn_b, moe_w1=moe_w1,
                            moe_w3=moe_w3, moe_w2=moe_w2, ple_w_proj=ple_w_proj, ple_w_gate=ple_w_gate,
                            ple_b_gate=ple_b_gate)

    w_in = ev_w_in[0]
    u = _glu_proj(x0b, w_in[:, :2 * CONV_CH].astype(BF16))
    ub = _conv_gln(u, ev_conv_w[0], ev_conv_b[0], ev_cnorm_g[0], ev_cnorm_b[0], groups=CONV_GROUPS)
    cosm, sinm = _rope_tables(positions[0])
    qn, qr, kn, kr, v = _mla_proj(x0b, w_in[:, 2 * CONV_CH:], ev_qnorm_g[0], ev_w_uq[0], ev_kvnorm_g[0],
                                  ev_w_ukv[0], cosm, sinm)
    att = _attention(qn, qr, kn, kr, v)
    mixed0 = _out_ln_route(ub, att, ev_w_out[0].astype(BF16), x0, ln_mix_g[0], ln_mix_b[0],
                           moe_w_grp[0], moe_b_grp[0], moe_w_exp[0], moe_b_exp[0])
    x3, x3b = moe(mixed0, 0)

    w_in1 = od_w_in[0]
    gate = _proj(x3b, w_in1[:, :RNN_WIDTH].astype(BF16), act="gelu", out_dtype=BF16)
    xr = _proj(x3b, w_in1[:, RNN_WIDTH:].astype(BF16), act=None, out_dtype=F32)
    y = _rglru(xr, gate, od_conv_w[0], od_conv_b[0], od_w_a[0], od_b_a[0], od_w_i[0], od_b_i[0], od_lam[0])
    mixed1 = _out_ln_route(y, y, od_w_out[0].astype(BF16), x3, ln_mix_g[1], ln_mix_b[1],
                           moe_w_grp[1], moe_b_grp[1], moe_w_exp[1], moe_b_exp[1], col1=0, col2=1)
    x6, _ = moe(mixed1, 1)
    return x6[None]
```

```python
import functools
import math

import jax
import jax.numpy as jnp
from jax import lax
from jax.experimental import pallas as pl
from jax.experimental.pallas import tpu as pltpu

F32 = jnp.float32
BF16 = jnp.bfloat16

D_MODEL = 2048
SEQ = 8192
DEPTH = 2
CONV_CH = 1024
CONV_GROUPS = 16
CONV_WIDTH = 31
MLA_HEADS = 8
QK_NOPE = 128
QK_ROPE = 64
V_HEAD = 128
Q_RANK = 512
KV_RANK = 256
ROPE_THETA = 10000.0
RNN_WIDTH = 2048
RNN_HEADS = 16
RNN_HEAD_DIM = RNN_WIDTH // RNN_HEADS
RNN_CONV_WIDTH = 4
RG_C = 8.0
N_GROUPS = 8
EXPERTS_PER_GROUP = 8
N_EXPERTS = N_GROUPS * EXPERTS_PER_GROUP
TOP_K = 2
D_EXPERT = 512
PLE_DIM = 256
ALPHA = (2 * DEPTH) ** 0.25
LN_EPS = 1e-5
RMS_EPS = 1e-6

LANES = 128
SUBLANES = 8
CONV_HALO = 32
RNN_HALO = 8
MOE_ROWS = 256
ROUTE_LANES = 128
MIB = 2 ** 20


def _params(semantics, vmem_mib):
    return pltpu.CompilerParams(dimension_semantics=semantics, vmem_limit_bytes=vmem_mib * MIB)


def _layer_norm_rows(z, g, b):
    mu = jnp.mean(z, axis=-1, keepdims=True)
    d = z - mu
    var = jnp.mean(d * d, axis=-1, keepdims=True)
    return d * lax.rsqrt(var + LN_EPS) * g + b


def _pack_bf16_pair(a, b):
    def rounded_bits(v):
        u = lax.bitcast_convert_type(v, jnp.uint32)
        return u + jnp.uint32(0x7FFF) + ((u >> 16) & jnp.uint32(1))

    return (rounded_bits(a) & jnp.uint32(0xFFFF0000)) | (rounded_bits(b) >> 16)


def _unpack_bf16_pair(u):
    a = lax.bitcast_convert_type(u & jnp.uint32(0xFFFF0000), F32)
    b = lax.bitcast_convert_type(u << 16, F32)
    return a, b


def _split_bf16(v):
    hi = v.astype(BF16)
    lo = (v - hi.astype(F32)).astype(BF16)
    return hi, lo


def _gelu_tanh(x):
    c = math.sqrt(2.0 / math.pi)
    return 0.5 * x * (1.0 + jnp.tanh(c * (x + 0.044715 * (x * x * x))))


def _rnn_in_kernel(x_ref, wg_ref, wx_ref, gate_ref, xr_ref):
    x = x_ref[...]
    gate_ref[...] = _gelu_tanh(jnp.dot(x, wg_ref[...], preferred_element_type=F32)).astype(gate_ref.dtype)
    xr_ref[...] = jnp.dot(x, wx_ref[...], preferred_element_type=F32)


def _rnn_in_proj(x, w, *, tm=512, tn=512):
    m, k = x.shape
    n = w.shape[1] // 2
    nb = n // tn
    tile = pl.BlockSpec((tm, tn), lambda j, i: (i, j))
    return pl.pallas_call(
        _rnn_in_kernel,
        grid=(nb, m // tm),
        in_specs=[pl.BlockSpec((tm, k), lambda j, i: (i, 0)),
                  pl.BlockSpec((k, tn), lambda j, i: (0, j)),
                  pl.BlockSpec((k, tn), lambda j, i: (0, j + nb))],
        out_specs=[tile, tile],
        out_shape=[jax.ShapeDtypeStruct((m, n), BF16), jax.ShapeDtypeStruct((m, n), F32)],
        compiler_params=_params(("parallel", "parallel"), 40),
        name="rnn_in_proj",
    )(x, w, w)


def _glu_kernel(x_ref, wv_ref, wg_ref, o_ref):
    x = x_ref[...]
    v = jnp.dot(x, wv_ref[...], preferred_element_type=F32)
    g = jnp.dot(x, wg_ref[...], preferred_element_type=F32)
    o_ref[...] = v * jax.nn.sigmoid(g)


def _glu_proj(x, w, *, tm=512, tn=512):
    m, k = x.shape
    n = w.shape[1] // 2
    nb = n // tn
    return pl.pallas_call(
        _glu_kernel,
        grid=(nb, m // tm),
        in_specs=[pl.BlockSpec((tm, k), lambda j, i: (i, 0)),
                  pl.BlockSpec((k, tn), lambda j, i: (0, j)),
                  pl.BlockSpec((k, tn), lambda j, i: (0, j + nb))],
        out_specs=pl.BlockSpec((tm, tn), lambda j, i: (i, j)),
        out_shape=jax.ShapeDtypeStruct((m, n), F32),
        compiler_params=_params(("parallel", "parallel"), 40),
        name="glu_proj",
    )(x, w, w)


def _conv_gln_kernel(cur_ref, prev_ref, w_ref, b_ref, g_ref, beta_ref, gm_ref, o_ref,
                     buf_ref, sh_ref, y_ref, *, tt, ch, width, rc, cc):
    i = pl.program_id(0)
    buf_ref[0:CONV_HALO, :] = jnp.where(i > 0, prev_ref[...], 0.0)
    buf_ref[CONV_HALO:, :] = cur_ref[...]
    off = CONV_HALO - (width - 1)
    sh_rows = sh_ref.shape[1]
    step = 7 * SUBLANES
    assert sh_rows % step == 0
    for b in range(1, SUBLANES):
        for c0 in range(0, ch, cc):
            for r0 in range(0, sh_rows, step):
                sh_ref[b - 1, r0:r0 + step, c0:c0 + cc] = buf_ref[r0 + b:r0 + b + step, c0:c0 + cc]
    for c0 in range(0, ch, cc):
        for r0 in range(0, tt, rc):
            acc = jnp.broadcast_to(b_ref[:, c0:c0 + cc], (rc, cc))
            for k in range(width):
                q, b = divmod(off + k, SUBLANES)
                lo = r0 + q * SUBLANES
                src = buf_ref[lo:lo + rc, c0:c0 + cc] if b == 0 else sh_ref[b - 1, lo:lo + rc, c0:c0 + cc]
                acc = acc + src * w_ref[k:k + 1, c0:c0 + cc]
            y_ref[r0:r0 + rc, c0:c0 + cc] = acc
    gm = gm_ref[...]

    def seg_mean(v):
        hi, lo = _split_bf16(v)
        return (jnp.dot(hi, gm, preferred_element_type=F32)
                + jnp.dot(lo, gm, preferred_element_type=F32))

    for c0 in range(0, ch, LANES):
        y = y_ref[:, c0:c0 + LANES]
        d = y - seg_mean(y)
        var = seg_mean(d * d)
        z = d * lax.rsqrt(var + LN_EPS) * g_ref[:, c0:c0 + LANES] + beta_ref[:, c0:c0 + LANES]
        o_ref[:, c0:c0 + LANES] = (z * jax.nn.sigmoid(z)).astype(o_ref.dtype)


def _conv_gln(u, w, b, g, beta, *, groups, tt=256):
    s, ch = u.shape
    width = w.shape[0]
    gsz = ch // groups
    assert LANES % gsz == 0 and width - 1 <= CONV_HALO and tt % CONV_HALO == 0
    wpad = jnp.zeros((CONV_HALO, ch), F32).at[:width].set(w)
    lane = jnp.arange(LANES)
    gm = jnp.where((lane[:, None] // gsz) == (lane[None, :] // gsz), 1.0 / gsz, 0.0).astype(BF16)
    hb = tt // CONV_HALO
    row = lambda v: v.reshape(1, ch)
    return pl.pallas_call(
        functools.partial(_conv_gln_kernel, tt=tt, ch=ch, width=width, rc=32, cc=256),
        grid=(s // tt,),
        in_specs=[pl.BlockSpec((tt, ch), lambda i: (i, 0)),
                  pl.BlockSpec((CONV_HALO, ch), lambda i: (jnp.maximum(i * hb - 1, 0), 0)),
                  pl.BlockSpec((CONV_HALO, ch), lambda i: (0, 0)),
                  pl.BlockSpec((1, ch), lambda i: (0, 0)),
                  pl.BlockSpec((1, ch), lambda i: (0, 0)),
                  pl.BlockSpec((1, ch), lambda i: (0, 0)),
                  pl.BlockSpec((LANES, LANES), lambda i: (0, 0))],
        out_specs=pl.BlockSpec((tt, ch), lambda i: (i, 0)),
        out_shape=jax.ShapeDtypeStruct((s, ch), BF16),
        scratch_shapes=[pltpu.VMEM((tt + CONV_HALO, ch), F32),
                        pltpu.VMEM((SUBLANES - 1, tt + CONV_HALO - SUBLANES, ch), F32),
                        pltpu.VMEM((tt, ch), F32)],
        compiler_params=_params(("parallel",), 40),
        name="conv_gln",
    )(u, u, wpad, row(b), row(g), row(beta), gm)


def _rope_table_kernel(pos_ref, invf_ref, cos_ref, sin_ref):
    ang = pos_ref[...].astype(F32) * invf_ref[...]
    cos_ref[...] = jnp.cos(ang)
    sin_ref[...] = jnp.sin(ang)


def _rope_tables(positions):
    s = positions.shape[0]
    half = QK_ROPE // 2
    per_row = LANES // half
    inv_freq = 1.0 / (ROPE_THETA ** (jnp.arange(0, QK_ROPE, 2, dtype=F32) / QK_ROPE))
    pos_rep = jnp.repeat(positions.reshape(s // per_row, per_row), half, axis=1)
    invf = jnp.tile(inv_freq, per_row).reshape(1, LANES)
    cos, sin = pl.pallas_call(
        _rope_table_kernel,
        out_shape=[jax.ShapeDtypeStruct((s // per_row, LANES), F32)] * 2,
        name="rope_table",
    )(pos_rep, invf)
    cos = cos.reshape(s, half)
    sin = sin.reshape(s, half)
    zero = jnp.zeros((s, LANES - QK_ROPE), F32)
    return jnp.concatenate([cos, cos, zero], axis=1), jnp.concatenate([-sin, sin, zero], axis=1)


def _mla_proj_kernel(x_ref, wm_ref, qg_ref, kvg_ref, wq_ref, wkv_ref, c_ref, s_ref,
                     qn_ref, qr_ref, kn_ref, kr_ref, v_ref, *, scale):
    nq = MLA_HEADS * QK_NOPE
    c = jnp.dot(x_ref[...], wm_ref[...], preferred_element_type=F32)
    cq = c[:, :Q_RANK]
    ckv = c[:, Q_RANK:Q_RANK + KV_RANK]
    kr_pad = c[:, Q_RANK + KV_RANK:Q_RANK + KV_RANK + LANES]
    kr_rot = c[:, Q_RANK + KV_RANK + LANES:]
    cqn = cq * lax.rsqrt(jnp.mean(cq * cq, axis=-1, keepdims=True) + RMS_EPS) * qg_ref[...]
    ckvn = ckv * lax.rsqrt(jnp.mean(ckv * ckv, axis=-1, keepdims=True) + RMS_EPS) * kvg_ref[...]
    q = jnp.dot(cqn.astype(BF16), wq_ref[...], preferred_element_type=F32)
    kv = jnp.dot(ckvn.astype(BF16), wkv_ref[...], preferred_element_type=F32)
    cosm = c_ref[...]
    sinm = s_ref[...]
    qn_ref[...] = (q[:, :nq] * scale).astype(BF16)
    for h in range(MLA_HEADS):
        lo = nq + h * LANES
        rot = q[:, lo:lo + LANES] * cosm + q[:, lo + nq:lo + nq + LANES] * sinm
        qr_ref[:, h * LANES:(h + 1) * LANES] = (rot * scale).astype(BF16)
    kn_ref[...] = kv[:, :nq].astype(BF16)
    v_ref[...] = kv[:, nq:].astype(BF16)
    kr_ref[...] = (kr_pad * cosm + kr_rot * sinm).astype(BF16)


def _rope_swap(w):
    half = QK_ROPE // 2
    return jnp.concatenate([w[..., half:], w[..., :half]], axis=-1)


def _mla_proj(xb, w_in_mla, qnorm_g, w_uq, kvnorm_g, w_ukv, cosm, sinm, *, tm=256):
    s, d = xb.shape
    nq = MLA_HEADS * QK_NOPE
    pad = lambda w: jnp.concatenate([w, jnp.zeros(w.shape[:-1] + (LANES - QK_ROPE,), w.dtype)], axis=-1)
    w_kr = w_in_mla[:, Q_RANK + KV_RANK:]
    wm = jnp.concatenate([w_in_mla[:, :Q_RANK + KV_RANK], pad(w_kr), pad(_rope_swap(w_kr))], axis=1).astype(BF16)
    wq3 = w_uq.reshape(Q_RANK, MLA_HEADS, QK_NOPE + QK_ROPE)
    wq_rope = wq3[:, :, QK_NOPE:]
    wq = jnp.concatenate([wq3[:, :, :QK_NOPE].reshape(Q_RANK, nq),
                          pad(wq_rope).reshape(Q_RANK, MLA_HEADS * LANES),
                          pad(_rope_swap(wq_rope)).reshape(Q_RANK, MLA_HEADS * LANES)], axis=1).astype(BF16)
    wkv3 = w_ukv.reshape(KV_RANK, MLA_HEADS, QK_NOPE + V_HEAD)
    wkv = jnp.concatenate([wkv3[:, :, :QK_NOPE].reshape(KV_RANK, nq),
                           wkv3[:, :, QK_NOPE:].reshape(KV_RANK, MLA_HEADS * V_HEAD)], axis=1).astype(BF16)
    scale = (QK_NOPE + QK_ROPE) ** -0.5 * math.log2(math.e)
    full = lambda a: pl.BlockSpec(a.shape, lambda i: (0,) * a.ndim)
    rows = lambda n: pl.BlockSpec((tm, n), lambda i: (i, 0))
    qg = qnorm_g.reshape(1, Q_RANK)
    kvg = kvnorm_g.reshape(1, KV_RANK)
    return pl.pallas_call(
        functools.partial(_mla_proj_kernel, scale=scale),
        grid=(s // tm,),
        in_specs=[rows(d), full(wm), full(qg), full(kvg), full(wq), full(wkv), rows(LANES), rows(LANES)],
        out_specs=[rows(nq), rows(MLA_HEADS * LANES), rows(nq), rows(LANES), rows(MLA_HEADS * V_HEAD)],
        out_shape=[jax.ShapeDtypeStruct((s, nq), BF16),
                   jax.ShapeDtypeStruct((s, MLA_HEADS * LANES), BF16),
                   jax.ShapeDtypeStruct((s, nq), BF16),
                   jax.ShapeDtypeStruct((s, LANES), BF16),
                   jax.ShapeDtypeStruct((s, MLA_HEADS * V_HEAD), BF16)],
        compiler_params=_params(("parallel",), 48),
        name="mla_proj",
    )(xb, wm, qg, kvg, wq, wkv, cosm, sinm)


def _attn_kernel(qn_ref, qr_ref, kn_ref, kr_ref, v_ref, o_ref, m_ref, acc_ref, *, tq, tk, hb):
    qi = pl.program_id(1)
    per_q = tq // tk
    m_ref[...] = jnp.full(m_ref.shape, -jnp.inf, F32)
    acc_ref[...] = jnp.zeros(acc_ref.shape, F32)
    ones = jnp.ones((tk, LANES), BF16)
    head_cols = [slice(h * LANES, (h + 1) * LANES) for h in range(hb)]
    qs = [jnp.concatenate([qn_ref[:, c], qr_ref[:, c]], axis=1) for c in head_cols]

    def chunk(j, diag):
        rows = pl.ds(pl.multiple_of(j * tk, tk), tk)
        k_rope = kr_ref[rows, :]
        q0 = 0 if diag is None else diag * tk
        for h, c in enumerate(head_cols):
            k = jnp.concatenate([kn_ref[rows, c], k_rope], axis=1)
            s = lax.dot_general(qs[h][q0:], k, (((1,), (1,)), ((), ())), preferred_element_type=F32)
            if diag is not None:
                keep = lax.broadcasted_iota(jnp.int32, s.shape, 0) >= lax.broadcasted_iota(jnp.int32, s.shape, 1)
                s = jnp.where(keep, s, -jnp.inf)
            m_old = m_ref[h, q0:, :]
            m_new = jnp.maximum(m_old, jnp.max(s, axis=-1, keepdims=True))
            alpha = jnp.exp2(m_old - m_new)
            p = jnp.exp2(s - jnp.concatenate([m_new] * (tk // LANES), axis=1))
            v_ext = jnp.concatenate([v_ref[rows, c], ones], axis=1)
            acc_ref[h, q0:, :] = (jnp.concatenate([alpha, alpha], axis=1) * acc_ref[h, q0:, :]
                                  + jnp.dot(p.astype(BF16), v_ext, preferred_element_type=F32))
            m_ref[h, q0:, :] = m_new

    def body(j, carry):
        chunk(j, None)
        return carry

    lax.fori_loop(0, qi * per_q, body, 0)
    for d in range(per_q):
        chunk(qi * per_q + d, d)
    for h, c in enumerate(head_cols):
        acc = acc_ref[h]
        o_ref[:, c] = (acc[:, :V_HEAD] / acc[:, V_HEAD:]).astype(o_ref.dtype)


def _attention(qn, qr, kn, kr, v, *, tq=1024, tk=512, hb=4):
    s = qn.shape[0]
    wide = hb * LANES
    tile = lambda: pl.BlockSpec((tq, wide), lambda h, i: (i, h))
    keys = lambda: pl.BlockSpec((s, wide), lambda h, i: (0, h), pipeline_mode=pl.Buffered(1))
    return pl.pallas_call(
        functools.partial(_attn_kernel, tq=tq, tk=tk, hb=hb),
        grid=(MLA_HEADS // hb, s // tq),
        in_specs=[tile(), tile(), keys(),
                  pl.BlockSpec((s, LANES), lambda h, i: (0, 0), pipeline_mode=pl.Buffered(1)), keys()],
        out_specs=tile(),
        out_shape=jax.ShapeDtypeStruct((s, MLA_HEADS * V_HEAD), BF16),
        scratch_shapes=[pltpu.VMEM((hb, tq, LANES), F32), pltpu.VMEM((hb, tq, 2 * V_HEAD), F32)],
        compiler_params=_params(("parallel", "parallel"), 48),
        name="mla_attention",
    )(qn, qr, kn, kr, v)


def _route(logits):
    lane = lax.broadcasted_iota(jnp.int32, logits.shape, 1).astype(F32)
    big = float(2 * ROUTE_LANES)
    neg = -jnp.inf
    gl = jnp.where(lane < N_GROUPS, logits, neg)
    gmax = jnp.max(gl, axis=-1, keepdims=True)
    gsum = jnp.sum(jnp.exp(gl - gmax), axis=-1, keepdims=True)
    g_prob = 1.0 / gsum
    g_idx = jnp.min(jnp.where(gl == gmax, lane, big), axis=-1, keepdims=True)
    lo = N_GROUPS + g_idx * EXPERTS_PER_GROUP
    el = jnp.where((lane >= lo) & (lane < lo + EXPERTS_PER_GROUP), logits, neg)
    emax = jnp.max(el, axis=-1, keepdims=True)
    esum = jnp.sum(jnp.exp(el - emax), axis=-1, keepdims=True)
    i0 = jnp.min(jnp.where(el == emax, lane, big), axis=-1, keepdims=True)
    el2 = jnp.where(lane == i0, neg, el)
    emax2 = jnp.max(el2, axis=-1, keepdims=True)
    i1 = jnp.min(jnp.where(el2 == emax2, lane, big), axis=-1, keepdims=True)
    p0 = 1.0 / esum
    p1 = jnp.exp(emax2 - emax) / esum
    g0 = g_prob * p0 / (p0 + p1)
    g1 = g_prob * p1 / (p0 + p1)
    e0 = i0 - N_GROUPS
    e1 = i1 - N_GROUPS
    return jnp.where(lane == 0, e0, jnp.where(lane == 1, e1, jnp.where(lane == 2, g0, jnp.where(lane == 3, g1, 0.0))))


def _out_ln_route_kernel(a1_ref, a2_ref, w_ref, x_ref, g_ref, b_ref, wrh_ref, wrl_ref, br_ref,
                         x1_ref, x1p_ref, route_ref, *, sub):
    half = a1_ref.shape[1]
    for r0 in range(0, x_ref.shape[0], sub):
        rows = slice(r0, r0 + sub)
        m = (jnp.dot(a1_ref[rows, :], w_ref[:half, :], preferred_element_type=F32)
             + jnp.dot(a2_ref[rows, :], w_ref[half:, :], preferred_element_type=F32))
        x1 = _layer_norm_rows(ALPHA * x_ref[rows, :] + m, g_ref[...], b_ref[...])
        x1_ref[rows, :] = x1
        hd = x1.shape[1] // 2
        x1p_ref[rows, :] = _pack_bf16_pair(x1[:, :hd], x1[:, hd:])
        xh, xl = _split_bf16(x1)
        wrh = wrh_ref[...]
        logits = (jnp.dot(xh, wrh, preferred_element_type=F32) + jnp.dot(xl, wrh, preferred_element_type=F32)
                  + jnp.dot(xh, wrl_ref[...], preferred_element_type=F32) + br_ref[...])
        route_ref[rows, :] = _route(logits)


def _out_ln_route(a1, a2, w, x, ln_g, ln_b, w_grp, b_grp, w_exp, b_exp, *, col1=0, col2=0, tm=512, sub=256):
    s, d = x.shape
    half = w.shape[0] // 2
    pad = ROUTE_LANES - N_GROUPS - N_EXPERTS
    wr = jnp.concatenate([w_grp, w_exp, jnp.zeros((d, pad), F32)], axis=1)
    br = jnp.concatenate([b_grp, b_exp, jnp.zeros((pad,), F32)]).reshape(1, ROUTE_LANES)
    wrh = wr.astype(BF16)
    wrl = (wr - wrh.astype(F32)).astype(BF16)
    full = lambda a: pl.BlockSpec(a.shape, lambda i: (0,) * a.ndim)
    rows = lambda n: pl.BlockSpec((tm, n), lambda i: (i, 0))
    g2 = ln_g.reshape(1, d)
    b2 = ln_b.reshape(1, d)
    return pl.pallas_call(
        functools.partial(_out_ln_route_kernel, sub=sub),
        grid=(s // tm,),
        in_specs=[pl.BlockSpec((tm, half), lambda i: (i, col1)), pl.BlockSpec((tm, half), lambda i: (i, col2)),
                  pl.BlockSpec(w.shape, lambda i: (0, 0), pipeline_mode=pl.Buffered(1)),
                  rows(d), full(g2), full(b2), full(wrh), full(wrl), full(br)],
        out_specs=[rows(d), rows(d // 2), rows(ROUTE_LANES)],
        out_shape=[jax.ShapeDtypeStruct((s, d), F32), jax.ShapeDtypeStruct((s, d // 2), jnp.uint32),
                   jax.ShapeDtypeStruct((s, ROUTE_LANES), F32)],
        compiler_params=_params(("parallel",), 48),
        name="out_ln_route",
    )(a1, a2, w, x, g2, b2, wrh, wrl, br)


def _rank_kernel(route_ref, tri_ref, rank_ref, cnt_ref, carry_ref):
    @pl.when(pl.program_id(0) == 0)
    def _():
        carry_ref[...] = jnp.zeros(carry_ref.shape, F32)

    route = route_ref[...]
    lane = lax.broadcasted_iota(jnp.int32, route.shape, 1).astype(F32)
    oh0 = jnp.where(lane == route[:, 0:1], 1.0, 0.0)
    oh1 = jnp.where(lane == route[:, 1:2], 1.0, 0.0)
    both = oh0 + oh1
    before = jnp.dot(tri_ref[...], both.astype(BF16), preferred_element_type=F32) + carry_ref[...]
    r0 = jnp.sum(before * oh0, axis=-1, keepdims=True)
    r1 = jnp.sum(before * oh1, axis=-1, keepdims=True)
    rank_ref[...] = jnp.where(lane == 0.0, r0, jnp.where(lane == 1.0, r1, 0.0))
    carry_ref[...] = carry_ref[...] + jnp.sum(both, axis=0, keepdims=True)
    cnt_ref[...] = carry_ref[...]


def _dest_kernel(route_ref, rank_ref, pstart_ref, dest_ref):
    route = route_ref[...]
    rank = rank_ref[...]
    lane = lax.broadcasted_iota(jnp.int32, route.shape, 1).astype(F32)
    ps = pstart_ref[...]
    d0 = jnp.sum(jnp.where(lane == route[:, 0:1], ps, 0.0), axis=-1, keepdims=True) + rank[:, 0:1]
    d1 = jnp.sum(jnp.where(lane == route[:, 1:2], ps, 0.0), axis=-1, keepdims=True) + rank[:, 1:2]
    dest_ref[...] = jnp.where(lane == 0.0, d0, jnp.where(lane == 1.0, d1, 0.0)).astype(jnp.int32)


def _moe_plan(route, *, tb=512):
    s = route.shape[0]
    tri = (jnp.arange(tb)[:, None] > jnp.arange(tb)[None, :]).astype(BF16)
    rows = pl.BlockSpec((tb, ROUTE_LANES), lambda i: (i, 0))
    one = lambda: pl.BlockSpec((1, ROUTE_LANES), lambda i: (0, 0))
    rank, cnt = pl.pallas_call(
        _rank_kernel,
        grid=(s // tb,),
        in_specs=[rows, pl.BlockSpec((tb, tb), lambda i: (0, 0))],
        out_specs=[rows, one()],
        out_shape=[jax.ShapeDtypeStruct((s, ROUTE_LANES), F32), jax.ShapeDtypeStruct((1, ROUTE_LANES), F32)],
        scratch_shapes=[pltpu.VMEM((1, ROUTE_LANES), F32)],
        compiler_params=_params(("arbitrary",), 32),
        name="moe_rank",
    )(route, tri)
    counts = cnt[0, :N_EXPERTS].astype(jnp.int32)
    pcounts = (counts + MOE_ROWS - 1) // MOE_ROWS * MOE_ROWS
    pends = jnp.cumsum(pcounts)
    pstarts = pends - pcounts
    n_blocks = s * TOP_K // MOE_ROWS + N_EXPERTS
    n_used = pends[-1] // MOE_ROWS
    blk = jnp.minimum(jnp.arange(n_blocks, dtype=jnp.int32), n_used - 1) * MOE_ROWS
    block_expert = jnp.minimum(jnp.sum(pends[None, :] <= blk[:, None], axis=1), N_EXPERTS - 1).astype(jnp.int32)
    ps = jnp.zeros((1, ROUTE_LANES), F32).at[0, :N_EXPERTS].set(pstarts.astype(F32))
    dest = pl.pallas_call(
        _dest_kernel,
        grid=(s // tb,),
        in_specs=[rows, rows, one()],
        out_specs=rows,
        out_shape=jax.ShapeDtypeStruct((s, ROUTE_LANES), jnp.int32),
        compiler_params=_params(("parallel",), 32),
        name="moe_dest",
    )(route, rank, ps)
    dest = dest[:, :TOP_K].T.reshape(TOP_K * s)
    return dest, block_expert, n_used.astype(jnp.int32).reshape(1)


ROW_UNROLL = 8


def _dispatch_kernel(dest_ref, x_ref, xs_hbm, sems, *, tm, s):
    t0 = pl.program_id(0) * tm

    def issue(g, carry):
        for u in range(ROW_UNROLL):
            r = g * ROW_UNROLL + u
            src = x_ref.at[pl.ds(r, 1), :]
            for k in range(TOP_K):
                pltpu.make_async_copy(src, xs_hbm.at[pl.ds(dest_ref[k * s + t0 + r], 1), :],
                                      sems.at[k]).start(priority=k)
        return carry

    lax.fori_loop(0, tm // ROW_UNROLL, issue, 0)
    tile = xs_hbm.at[pl.ds(0, tm), :]
    pltpu.make_async_copy(x_ref, tile, sems.at[0]).wait()
    pltpu.make_async_copy(x_ref, tile, sems.at[1]).wait()


def _dispatch_rows(x, dest, n_blocks, *, tm=256):
    s, d = x.shape
    return pl.pallas_call(
        functools.partial(_dispatch_kernel, tm=tm, s=s),
        grid_spec=pltpu.PrefetchScalarGridSpec(
            num_scalar_prefetch=1,
            grid=(s // tm,),
            in_specs=[pl.BlockSpec((tm, d), lambda i, dest: (i, 0))],
            out_specs=pl.BlockSpec(memory_space=pl.ANY),
            scratch_shapes=[pltpu.SemaphoreType.DMA((2,))]),
        out_shape=jax.ShapeDtypeStruct((n_blocks * MOE_ROWS, d), x.dtype),
        compiler_params=_params(("arbitrary",), 32),
        name="moe_dispatch",
    )(dest, x)


FFN_K_CHUNK = 512


def _ffn_kernel(nused_ref, run_ref, first_ref, rexp_ref, nruns_ref, xs_ref, w1_hbm, w3_hbm, w2_hbm, ys_ref,
                w1_buf, w3_buf, w2_buf, sems, *, layer):
    b = pl.program_id(0)

    def weight_copies(run, slot):
        e = rexp_ref[run]
        return [pltpu.make_async_copy(w_hbm.at[layer, e], buf.at[slot], sems.at[slot, j])
                for j, (w_hbm, buf) in enumerate(((w1_hbm, w1_buf), (w3_hbm, w3_buf), (w2_hbm, w2_buf)))]

    @pl.when(b == 0)
    def _():
        for c in weight_copies(0, 0):
            c.start()

    @pl.when(b < nused_ref[0])
    def _():
        run = run_ref[b]
        slot = run % 2

        @pl.when(first_ref[b] == 1)
        def _():
            @pl.when(run + 1 < nruns_ref[0])
            def _():
                for c in weight_copies(run + 1, 1 - slot):
                    c.start()

            for c in weight_copies(run, slot):
                c.wait()

        x_lo, x_hi = _unpack_bf16_pair(xs_ref[...])
        hd = x_lo.shape[1]
        x = jnp.concatenate([x_lo.astype(BF16), x_hi.astype(BF16)], axis=1)

        def up(buf):
            acc = None
            for k0 in range(0, 2 * hd, FFN_K_CHUNK):
                part = jnp.dot(x[:, k0:k0 + FFN_K_CHUNK], buf[slot, k0:k0 + FFN_K_CHUNK, :].astype(BF16),
                               preferred_element_type=F32)
                acc = part if acc is None else acc + part
            return acc

        h1 = up(w1_buf)
        h3 = up(w3_buf)
        hmid = (h1 * jax.nn.sigmoid(h1) * h3).astype(BF16)
        y = jnp.dot(hmid, w2_buf[slot].astype(BF16), preferred_element_type=F32)
        ys_ref[...] = _pack_bf16_pair(y[:, :hd], y[:, hd:])


def _grouped_ffn(xs, w1, w3, w2, layer, block_expert, n_used):
    d = xs.shape[1]
    dm = w1.shape[2]
    de = w1.shape[3]
    nb = block_expert.shape[0]
    first = jnp.concatenate([jnp.ones((1,), jnp.int32), (block_expert[1:] != block_expert[:-1]).astype(jnp.int32)])
    run = jnp.cumsum(first) - 1
    n_runs = run[-1:] + 1
    hit = (run[None, :] == jnp.arange(nb, dtype=jnp.int32)[:, None]) & (first[None, :] == 1)
    run_expert = jnp.sum(jnp.where(hit, block_expert[None, :], 0), axis=1).astype(jnp.int32)
    blk = lambda b, nu, *_: (jnp.minimum(b, nu[0] - 1), 0)
    return pl.pallas_call(
        functools.partial(_ffn_kernel, layer=layer),
        grid_spec=pltpu.PrefetchScalarGridSpec(
            num_scalar_prefetch=5,
            grid=(nb,),
            in_specs=[pl.BlockSpec((MOE_ROWS, d), blk),
                      pl.BlockSpec(memory_space=pl.ANY),
                      pl.BlockSpec(memory_space=pl.ANY),
                      pl.BlockSpec(memory_space=pl.ANY)],
            out_specs=pl.BlockSpec((MOE_ROWS, d), blk),
            scratch_shapes=[pltpu.VMEM((2, dm, de), F32), pltpu.VMEM((2, dm, de), F32),
                            pltpu.VMEM((2, de, dm), F32), pltpu.SemaphoreType.DMA((2, 3))]),
        out_shape=jax.ShapeDtypeStruct(xs.shape, jnp.uint32),
        compiler_params=_params(("arbitrary",), 56),
        name="moe_ffn",
    )(n_used, run.astype(jnp.int32), first, run_expert, n_runs.astype(jnp.int32), xs, w1, w3, w2)


def _combine_ple_kernel(pos_ref, ys_hbm, x1_ref, route_ref, g_ref, b_ref, wg_ref, bg_ref, p_ref, wp_ref,
                        o_ref, ob_ref, ybuf_a, ybuf_b, sems, *, tm, s):
    i = pl.program_id(0)
    n = pl.num_programs(0)

    def start_row(t0, buf, buf_id, r):
        for k in range(TOP_K):
            pltpu.make_async_copy(ys_hbm.at[pl.ds(pos_ref[k * s + t0 + r], 1), :],
                                  buf.at[k, pl.ds(r, 1), :], sems.at[buf_id, k]).start(priority=k)

    def wait_buffer(buf, buf_id):
        tile = ys_hbm.at[pl.ds(0, tm), :]
        for k in range(TOP_K):
            pltpu.make_async_copy(tile, buf.at[k], sems.at[buf_id, k]).wait()

    @pl.when(i == 0)
    def _():
        def issue(g, carry):
            for u in range(ROW_UNROLL):
                start_row(0, ybuf_a, 0, g * ROW_UNROLL + u)
            return carry

        lax.fori_loop(0, tm // ROW_UNROLL, issue, 0)

    def step(cur, cur_id, nxt_buf, nxt_id):
        wait_buffer(cur, cur_id)
        nxt = jnp.minimum(i + 1, n - 1) * tm
        for r in range(tm):
            start_row(nxt, nxt_buf, nxt_id, r)

        route = route_ref[...]
        f = None
        for k in range(TOP_K):
            lo, hi = _unpack_bf16_pair(cur[k])
            fk = route[:, TOP_K + k:TOP_K + k + 1] * jnp.concatenate([lo, hi], axis=1)
            f = fk if f is None else f + fk
        x2 = _layer_norm_rows(ALPHA * x1_ref[...] + f, g_ref[...], b_ref[...])
        gate = jax.nn.sigmoid(jnp.dot(x2.astype(BF16), wg_ref[...], preferred_element_type=F32) + bg_ref[...])
        emb = jnp.dot(p_ref[...].astype(BF16), wp_ref[...], preferred_element_type=F32)
        out = x2 + gate * emb
        o_ref[...] = out
        ob_ref[...] = out.astype(BF16)

        @pl.when(i == n - 1)
        def _():
            wait_buffer(nxt_buf, nxt_id)

    @pl.when(i % 2 == 0)
    def _():
        step(ybuf_a, 0, ybuf_b, 1)

    @pl.when(i % 2 == 1)
    def _():
        step(ybuf_b, 1, ybuf_a, 0)


def _combine_ple(ys, pos, x1, route, ln_g, ln_b, w_gate, b_gate, p, w_proj, *, tm=256):
    s, d = x1.shape
    pd = p.shape[1]
    rows = lambda n: pl.BlockSpec((tm, n), lambda i, pos: (i, 0))
    full = lambda a: pl.BlockSpec(a.shape, lambda i, pos: (0,) * a.ndim)
    g2, b2, bg = ln_g.reshape(1, d), ln_b.reshape(1, d), b_gate.reshape(1, d)
    return pl.pallas_call(
        functools.partial(_combine_ple_kernel, tm=tm, s=s),
        grid_spec=pltpu.PrefetchScalarGridSpec(
            num_scalar_prefetch=1,
            grid=(s // tm,),
            in_specs=[pl.BlockSpec(memory_space=pl.ANY), rows(d), rows(ROUTE_LANES), full(g2), full(b2),
                      full(w_gate), full(bg), rows(pd), full(w_proj)],
            out_specs=[rows(d), rows(d)],
            scratch_shapes=[pltpu.VMEM((TOP_K, tm, d // 2), jnp.uint32),
                            pltpu.VMEM((TOP_K, tm, d // 2), jnp.uint32),
                            pltpu.SemaphoreType.DMA((2, TOP_K))]),
        out_shape=[jax.ShapeDtypeStruct((s, d), F32), jax.ShapeDtypeStruct((s, d), BF16)],
        compiler_params=_params(("arbitrary",), 56),
        name="moe_combine_ple",
    )(pos, ys, x1, route, g2, b2, w_gate, bg, p, w_proj)


def _rglru_kernel(cur_ref, prev_ref, gate_ref, cw_ref, cb_ref, wa_ref, ba_ref, wi_ref, bi_ref, lam_ref,
                  o_ref, buf_ref, a_ref, b_ref, h_ref, carry_ref, *, tt, tc, width):
    i = pl.program_id(1)

    @pl.when(i == 0)
    def _():
        carry_ref[...] = jnp.zeros(carry_ref.shape, F32)

    buf_ref[0:RNN_HALO, :] = jnp.where(i > 0, prev_ref[...], 0.0)
    buf_ref[RNN_HALO:, :] = cur_ref[...]
    off = RNN_HALO - (width - 1)
    z = -lam_ref[...]
    softplus = jnp.log1p(jnp.exp(-jnp.abs(z))) + jnp.maximum(z, 0.0)

    def sigmoid(v):
        return 0.5 * jnp.tanh(0.5 * v) + 0.5

    for hh in range(tc // RNN_HEAD_DIM):
        cols = slice(hh * RNN_HEAD_DIM, (hh + 1) * RNN_HEAD_DIM)
        xh = jnp.broadcast_to(cb_ref[:, cols], (tt, RNN_HEAD_DIM))
        for k in range(width):
            xh = xh + buf_ref[off + k:off + k + tt, cols] * cw_ref[k:k + 1, cols]
        xhb = xh.astype(BF16)
        r = sigmoid(jnp.dot(xhb, wa_ref[hh], preferred_element_type=F32) + ba_ref[:, cols])
        ig = sigmoid(jnp.dot(xhb, wi_ref[hh], preferred_element_type=F32) + bi_ref[:, cols])
        log_a = (-RG_C) * r * softplus[:, cols]
        a = jnp.exp(log_a)
        bval = jnp.sqrt(-jnp.tanh(log_a) * (a * a + 1.0)) * (ig * xh)
        a_ref[:, cols] = a
        b_ref[:, cols] = bval

    def group(gi, h):
        base = pl.multiple_of(gi * 8, 8)
        for r in range(8):
            row = pl.ds(base + r, 1)
            h = a_ref[row, :] * h + b_ref[row, :]
            h_ref[row, :] = h
        return h

    carry_ref[...] = lax.fori_loop(0, tt // 8, group, carry_ref[...])
    o_ref[...] = (gate_ref[...].astype(F32) * h_ref[...]).astype(o_ref.dtype)


def _rglru(xr, gate, conv_w, conv_b, w_a, b_a, w_i, b_i, lam, *, tt=256, tc=2048):
    s, wd = xr.shape
    width = conv_w.shape[0]
    hpc = tc // RNN_HEAD_DIM
    hb = tt // RNN_HALO
    row = lambda v: v.reshape(1, wd)
    cvec = lambda: pl.BlockSpec((1, tc), lambda c, i: (0, c))
    heads = lambda: pl.BlockSpec((hpc, RNN_HEAD_DIM, RNN_HEAD_DIM), lambda c, i: (c, 0, 0))
    tile = lambda: pl.BlockSpec((tt, tc), lambda c, i: (i, c))
    return pl.pallas_call(
        functools.partial(_rglru_kernel, tt=tt, tc=tc, width=width),
        grid=(wd // tc, s // tt),
        in_specs=[tile(),
                  pl.BlockSpec((RNN_HALO, tc), lambda c, i: (jnp.maximum(i * hb - 1, 0), c)),
                  tile(),
                  pl.BlockSpec((width, tc), lambda c, i: (0, c)),
                  cvec(), heads(), cvec(), heads(), cvec(), cvec()],
        out_specs=tile(),
        out_shape=jax.ShapeDtypeStruct((s, wd), BF16),
        scratch_shapes=[pltpu.VMEM((tt + RNN_HALO, tc), F32), pltpu.VMEM((tt, tc), F32),
                        pltpu.VMEM((tt, tc), F32), pltpu.VMEM((tt, tc), F32), pltpu.VMEM((1, tc), F32)],
        compiler_params=_params(("parallel", "arbitrary"), 32),
        name="rglru",
    )(xr, xr, gate, conv_w, row(conv_b), w_a.astype(BF16), row(b_a), w_i.astype(BF16), row(b_i), row(lam))


def _moe_and_ple(mixed, i, p, ln_ffn_g, ln_ffn_b, moe_w1, moe_w3, moe_w2, ple_w_proj, ple_w_gate, ple_b_gate):
    x1, x1_pairs, route = mixed
    dest, block_expert, n_used = _moe_plan(route)
    xs = _dispatch_rows(x1_pairs, dest, block_expert.shape[0])
    ys = _grouped_ffn(xs, moe_w1, moe_w3, moe_w2, i, block_expert, n_used)
    return _combine_ple(ys, dest, x1, route, ln_ffn_g[i], ln_ffn_b[i], ple_w_gate[i].astype(BF16),
                        ple_b_gate[i], p[i, 0], ple_w_proj[i].astype(BF16))


def kernel(x, p, positions, ev_w_in, ev_conv_w, ev_conv_b, ev_cnorm_g, ev_cnorm_b, ev_qnorm_g, ev_w_uq, ev_kvnorm_g, ev_w_ukv, ev_w_out, od_w_in, od_conv_w, od_conv_b, od_w_a, od_b_a, od_w_i, od_b_i, od_lam, od_w_out, ln_mix_g, ln_mix_b, ln_ffn_g, ln_ffn_b, moe_w_grp, moe_b_grp, moe_w_exp, moe_b_exp, moe_w1, moe_w3, moe_w2, ple_w_proj, ple_w_gate, ple_b_gate):
    x0 = x[0]
    x0b = x0.astype(BF16)
    moe = functools.partial(_moe_and_ple, p=p, ln_ffn_g=ln_ffn_g, ln_ffn_b=ln_ffn_b, moe_w1=moe_w1,
                            moe_w3=moe_w3, moe_w2=moe_w2, ple_w_proj=ple_w_proj, ple_w_gate=ple_w_gate,
                            ple_b_gate=ple_b_gate)

    w_in = ev_w_in[0]
    u = _glu_proj(x0b, w_in[:, :2 * CONV_CH].astype(BF16))
    ub = _conv_gln(u, ev_conv_w[0], ev_conv_b[0], ev_cnorm_g[0], ev_cnorm_b[0], groups=CONV_GROUPS)
    cosm, sinm = _rope_tables(positions[0])
    qn, qr, kn, kr, v = _mla_proj(x0b, w_in[:, 2 * CONV_CH:], ev_qnorm_g[0], ev_w_uq[0], ev_kvnorm_g[0],
                                  ev_w_ukv[0], cosm, sinm)
    att = _attention(qn, qr, kn, kr, v)
    mixed0 = _out_ln_route(ub, att, ev_w_out[0].astype(BF16), x0, ln_mix_g[0], ln_mix_b[0],
                           moe_w_grp[0], moe_b_grp[0], moe_w_exp[0], moe_b_exp[0])
    x3, x3b = moe(mixed0, 0)

    w_in1 = od_w_in[0]
    gate, xr = _rnn_in_proj(x3b, w_in1.astype(BF16))
    y = _rglru(xr, gate, od_conv_w[0], od_conv_b[0], od_w_a[0], od_b_a[0], od_w_i[0], od_b_i[0], od_lam[0])
    mixed1 = _out_ln_route(y, y, od_w_out[0].astype(BF16), x3, ln_mix_g[1], ln_mix_b[1],
                           moe_w_grp[1], moe_b_grp[1], moe_w_exp[1], moe_b_exp[1], col1=0, col2=1)
    x6, _ = moe(mixed1, 1)
    return x6[None]
```

```python
import functools
import math

import jax
import jax.numpy as jnp
from jax import lax
from jax.experimental import pallas as pl
from jax.experimental.pallas import tpu as pltpu

F32 = jnp.float32
BF16 = jnp.bfloat16

D_MODEL = 2048
SEQ = 8192
DEPTH = 2
CONV_CH = 1024
CONV_GROUPS = 16
CONV_WIDTH = 31
MLA_HEADS = 8
QK_NOPE = 128
QK_ROPE = 64
V_HEAD = 128
Q_RANK = 512
KV_RANK = 256
ROPE_THETA = 10000.0
RNN_WIDTH = 2048
RNN_HEADS = 16
RNN_HEAD_DIM = RNN_WIDTH // RNN_HEADS
RNN_CONV_WIDTH = 4
RG_C = 8.0
N_GROUPS = 8
EXPERTS_PER_GROUP = 8
N_EXPERTS = N_GROUPS * EXPERTS_PER_GROUP
TOP_K = 2
D_EXPERT = 512
PLE_DIM = 256
ALPHA = (2 * DEPTH) ** 0.25
LN_EPS = 1e-5
RMS_EPS = 1e-6

LANES = 128
SUBLANES = 8
CONV_HALO = 32
RNN_HALO = 8
MOE_ROWS = 256
ROUTE_LANES = 128
MIB = 2 ** 20


def _params(semantics, vmem_mib):
    return pltpu.CompilerParams(dimension_semantics=semantics, vmem_limit_bytes=vmem_mib * MIB)


def _layer_norm_rows(z, g, b):
    mu = jnp.mean(z, axis=-1, keepdims=True)
    d = z - mu
    var = jnp.mean(d * d, axis=-1, keepdims=True)
    return d * lax.rsqrt(var + LN_EPS) * g + b


def _pack_bf16_pair(a, b):
    def rounded_bits(v):
        u = lax.bitcast_convert_type(v, jnp.uint32)
        return u + jnp.uint32(0x7FFF) + ((u >> 16) & jnp.uint32(1))

    return (rounded_bits(a) & jnp.uint32(0xFFFF0000)) | (rounded_bits(b) >> 16)


def _unpack_bf16_pair(u):
    a = lax.bitcast_convert_type(u & jnp.uint32(0xFFFF0000), F32)
    b = lax.bitcast_convert_type(u << 16, F32)
    return a, b


def _store_row_tiles(ref, packed):
    n, w = packed.shape
    assert w == SUBLANES * LANES
    for j in range(SUBLANES):
        ref[pl.ds(j, n, stride=SUBLANES), :] = packed[:, j * LANES:(j + 1) * LANES]


def _load_row_tiles(ref):
    n = ref.shape[0] // SUBLANES
    return jnp.concatenate([ref[pl.ds(j, n, stride=SUBLANES), :] for j in range(SUBLANES)], axis=1)


def _split_bf16(v):
    hi = v.astype(BF16)
    lo = (v - hi.astype(F32)).astype(BF16)
    return hi, lo


def _gelu_tanh(x):
    c = math.sqrt(2.0 / math.pi)
    return 0.5 * x * (1.0 + jnp.tanh(c * (x + 0.044715 * (x * x * x))))


def _rnn_in_kernel(x_ref, wg_ref, wx_ref, gate_ref, xr_ref):
    x = x_ref[...]
    gate_ref[...] = _gelu_tanh(jnp.dot(x, wg_ref[...], preferred_element_type=F32)).astype(gate_ref.dtype)
    xr_ref[...] = jnp.dot(x, wx_ref[...], preferred_element_type=F32)


def _rnn_in_proj(x, w, *, tm=512, tn=512):
    m, k = x.shape
    n = w.shape[1] // 2
    nb = n // tn
    tile = pl.BlockSpec((tm, tn), lambda j, i: (i, j))
    return pl.pallas_call(
        _rnn_in_kernel,
        grid=(nb, m // tm),
        in_specs=[pl.BlockSpec((tm, k), lambda j, i: (i, 0)),
                  pl.BlockSpec((k, tn), lambda j, i: (0, j)),
                  pl.BlockSpec((k, tn), lambda j, i: (0, j + nb))],
        out_specs=[tile, tile],
        out_shape=[jax.ShapeDtypeStruct((m, n), BF16), jax.ShapeDtypeStruct((m, n), F32)],
        compiler_params=_params(("parallel", "parallel"), 40),
        name="rnn_in_proj",
    )(x, w, w)


def _glu_kernel(x_ref, wv_ref, wg_ref, o_ref):
    x = x_ref[...]
    v = jnp.dot(x, wv_ref[...], preferred_element_type=F32)
    g = jnp.dot(x, wg_ref[...], preferred_element_type=F32)
    o_ref[...] = v * jax.nn.sigmoid(g)


def _glu_proj(x, w, *, tm=512, tn=512):
    m, k = x.shape
    n = w.shape[1] // 2
    nb = n // tn
    return pl.pallas_call(
        _glu_kernel,
        grid=(nb, m // tm),
        in_specs=[pl.BlockSpec((tm, k), lambda j, i: (i, 0)),
                  pl.BlockSpec((k, tn), lambda j, i: (0, j)),
                  pl.BlockSpec((k, tn), lambda j, i: (0, j + nb))],
        out_specs=pl.BlockSpec((tm, tn), lambda j, i: (i, j)),
        out_shape=jax.ShapeDtypeStruct((m, n), F32),
        compiler_params=_params(("parallel", "parallel"), 40),
        name="glu_proj",
    )(x, w, w)


def _conv_gln_kernel(cur_ref, prev_ref, w_ref, b_ref, g_ref, beta_ref, gm_ref, o_ref,
                     buf_ref, sh_ref, y_ref, *, tt, ch, width, rc, cc):
    i = pl.program_id(0)
    buf_ref[0:CONV_HALO, :] = jnp.where(i > 0, prev_ref[...], 0.0)
    buf_ref[CONV_HALO:, :] = cur_ref[...]
    off = CONV_HALO - (width - 1)
    sh_rows = sh_ref.shape[1]
    step = 7 * SUBLANES
    assert sh_rows % step == 0
    for b in range(1, SUBLANES):
        for c0 in range(0, ch, cc):
            for r0 in range(0, sh_rows, step):
                sh_ref[b - 1, r0:r0 + step, c0:c0 + cc] = buf_ref[r0 + b:r0 + b + step, c0:c0 + cc]
    for c0 in range(0, ch, cc):
        for r0 in range(0, tt, rc):
            acc = jnp.broadcast_to(b_ref[:, c0:c0 + cc], (rc, cc))
            for k in range(width):
                q, b = divmod(off + k, SUBLANES)
                lo = r0 + q * SUBLANES
                src = buf_ref[lo:lo + rc, c0:c0 + cc] if b == 0 else sh_ref[b - 1, lo:lo + rc, c0:c0 + cc]
                acc = acc + src * w_ref[k:k + 1, c0:c0 + cc]
            y_ref[r0:r0 + rc, c0:c0 + cc] = acc
    gm = gm_ref[...]

    def seg_mean(v):
        hi, lo = _split_bf16(v)
        return (jnp.dot(hi, gm, preferred_element_type=F32)
                + jnp.dot(lo, gm, preferred_element_type=F32))

    for c0 in range(0, ch, LANES):
        y = y_ref[:, c0:c0 + LANES]
        d = y - seg_mean(y)
        var = seg_mean(d * d)
        z = d * lax.rsqrt(var + LN_EPS) * g_ref[:, c0:c0 + LANES] + beta_ref[:, c0:c0 + LANES]
        o_ref[:, c0:c0 + LANES] = (z * jax.nn.sigmoid(z)).astype(o_ref.dtype)


def _conv_gln(u, w, b, g, beta, *, groups, tt=256):
    s, ch = u.shape
    width = w.shape[0]
    gsz = ch // groups
    assert LANES % gsz == 0 and width - 1 <= CONV_HALO and tt % CONV_HALO == 0
    wpad = jnp.zeros((CONV_HALO, ch), F32).at[:width].set(w)
    lane = jnp.arange(LANES)
    gm = jnp.where((lane[:, None] // gsz) == (lane[None, :] // gsz), 1.0 / gsz, 0.0).astype(BF16)
    hb = tt // CONV_HALO
    row = lambda v: v.reshape(1, ch)
    return pl.pallas_call(
        functools.partial(_conv_gln_kernel, tt=tt, ch=ch, width=width, rc=32, cc=256),
        grid=(s // tt,),
        in_specs=[pl.BlockSpec((tt, ch), lambda i: (i, 0)),
                  pl.BlockSpec((CONV_HALO, ch), lambda i: (jnp.maximum(i * hb - 1, 0), 0)),
                  pl.BlockSpec((CONV_HALO, ch), lambda i: (0, 0)),
                  pl.BlockSpec((1, ch), lambda i: (0, 0)),
                  pl.BlockSpec((1, ch), lambda i: (0, 0)),
                  pl.BlockSpec((1, ch), lambda i: (0, 0)),
                  pl.BlockSpec((LANES, LANES), lambda i: (0, 0))],
        out_specs=pl.BlockSpec((tt, ch), lambda i: (i, 0)),
        out_shape=jax.ShapeDtypeStruct((s, ch), BF16),
        scratch_shapes=[pltpu.VMEM((tt + CONV_HALO, ch), F32),
                        pltpu.VMEM((SUBLANES - 1, tt + CONV_HALO - SUBLANES, ch), F32),
                        pltpu.VMEM((tt, ch), F32)],
        compiler_params=_params(("parallel",), 40),
        name="conv_gln",
    )(u, u, wpad, row(b), row(g), row(beta), gm)


def _rope_table_kernel(pos_ref, invf_ref, cos_ref, sin_ref):
    ang = pos_ref[...].astype(F32) * invf_ref[...]
    cos_ref[...] = jnp.cos(ang)
    sin_ref[...] = jnp.sin(ang)


def _rope_tables(positions):
    s = positions.shape[0]
    half = QK_ROPE // 2
    per_row = LANES // half
    inv_freq = 1.0 / (ROPE_THETA ** (jnp.arange(0, QK_ROPE, 2, dtype=F32) / QK_ROPE))
    pos_rep = jnp.repeat(positions.reshape(s // per_row, per_row), half, axis=1)
    invf = jnp.tile(inv_freq, per_row).reshape(1, LANES)
    cos, sin = pl.pallas_call(
        _rope_table_kernel,
        out_shape=[jax.ShapeDtypeStruct((s // per_row, LANES), F32)] * 2,
        name="rope_table",
    )(pos_rep, invf)
    cos = cos.reshape(s, half)
    sin = sin.reshape(s, half)
    zero = jnp.zeros((s, LANES - QK_ROPE), F32)
    return jnp.concatenate([cos, cos, zero], axis=1), jnp.concatenate([-sin, sin, zero], axis=1)


def _mla_proj_kernel(x_ref, wm_ref, qg_ref, kvg_ref, wq_ref, wkv_ref, c_ref, s_ref,
                     qn_ref, qr_ref, kn_ref, kr_ref, v_ref, *, scale):
    nq = MLA_HEADS * QK_NOPE
    c = jnp.dot(x_ref[...], wm_ref[...], preferred_element_type=F32)
    cq = c[:, :Q_RANK]
    ckv = c[:, Q_RANK:Q_RANK + KV_RANK]
    kr_pad = c[:, Q_RANK + KV_RANK:Q_RANK + KV_RANK + LANES]
    kr_rot = c[:, Q_RANK + KV_RANK + LANES:]
    cqn = cq * lax.rsqrt(jnp.mean(cq * cq, axis=-1, keepdims=True) + RMS_EPS) * qg_ref[...]
    ckvn = ckv * lax.rsqrt(jnp.mean(ckv * ckv, axis=-1, keepdims=True) + RMS_EPS) * kvg_ref[...]
    q = jnp.dot(cqn.astype(BF16), wq_ref[...], preferred_element_type=F32)
    kv = jnp.dot(ckvn.astype(BF16), wkv_ref[...], preferred_element_type=F32)
    cosm = c_ref[...]
    sinm = s_ref[...]
    qn_ref[...] = (q[:, :nq] * scale).astype(BF16)
    for h in range(MLA_HEADS):
        lo = nq + h * LANES
        rot = q[:, lo:lo + LANES] * cosm + q[:, lo + nq:lo + nq + LANES] * sinm
        qr_ref[:, h * LANES:(h + 1) * LANES] = (rot * scale).astype(BF16)
    kn_ref[...] = kv[:, :nq].astype(BF16)
    v_ref[...] = kv[:, nq:].astype(BF16)
    kr_ref[...] = (kr_pad * cosm + kr_rot * sinm).astype(BF16)


def _rope_swap(w):
    half = QK_ROPE // 2
    return jnp.concatenate([w[..., half:], w[..., :half]], axis=-1)


def _mla_proj(xb, w_in_mla, qnorm_g, w_uq, kvnorm_g, w_ukv, cosm, sinm, *, tm=256):
    s, d = xb.shape
    nq = MLA_HEADS * QK_NOPE
    pad = lambda w: jnp.concatenate([w, jnp.zeros(w.shape[:-1] + (LANES - QK_ROPE,), w.dtype)], axis=-1)
    w_kr = w_in_mla[:, Q_RANK + KV_RANK:]
    wm = jnp.concatenate([w_in_mla[:, :Q_RANK + KV_RANK], pad(w_kr), pad(_rope_swap(w_kr))], axis=1).astype(BF16)
    wq3 = w_uq.reshape(Q_RANK, MLA_HEADS, QK_NOPE + QK_ROPE)
    wq_rope = wq3[:, :, QK_NOPE:]
    wq = jnp.concatenate([wq3[:, :, :QK_NOPE].reshape(Q_RANK, nq),
                          pad(wq_rope).reshape(Q_RANK, MLA_HEADS * LANES),
                          pad(_rope_swap(wq_rope)).reshape(Q_RANK, MLA_HEADS * LANES)], axis=1).astype(BF16)
    wkv3 = w_ukv.reshape(KV_RANK, MLA_HEADS, QK_NOPE + V_HEAD)
    wkv = jnp.concatenate([wkv3[:, :, :QK_NOPE].reshape(KV_RANK, nq),
                           wkv3[:, :, QK_NOPE:].reshape(KV_RANK, MLA_HEADS * V_HEAD)], axis=1).astype(BF16)
    scale = (QK_NOPE + QK_ROPE) ** -0.5 * math.log2(math.e)
    full = lambda a: pl.BlockSpec(a.shape, lambda i: (0,) * a.ndim)
    rows = lambda n: pl.BlockSpec((tm, n), lambda i: (i, 0))
    qg = qnorm_g.reshape(1, Q_RANK)
    kvg = kvnorm_g.reshape(1, KV_RANK)
    return pl.pallas_call(
        functools.partial(_mla_proj_kernel, scale=scale),
        grid=(s // tm,),
        in_specs=[rows(d), full(wm), full(qg), full(kvg), full(wq), full(wkv), rows(LANES), rows(LANES)],
        out_specs=[rows(nq), rows(MLA_HEADS * LANES), rows(nq), rows(LANES), rows(MLA_HEADS * V_HEAD)],
        out_shape=[jax.ShapeDtypeStruct((s, nq), BF16),
                   jax.ShapeDtypeStruct((s, MLA_HEADS * LANES), BF16),
                   jax.ShapeDtypeStruct((s, nq), BF16),
                   jax.ShapeDtypeStruct((s, LANES), BF16),
                   jax.ShapeDtypeStruct((s, MLA_HEADS * V_HEAD), BF16)],
        compiler_params=_params(("parallel",), 48),
        name="mla_proj",
    )(xb, wm, qg, kvg, wq, wkv, cosm, sinm)


def _attn_kernel(qn_ref, qr_ref, kn_ref, kr_ref, v_ref, o_ref, m_ref, acc_ref, *, tq, tk, hb):
    qi = pl.program_id(1)
    per_q = tq // tk
    m_ref[...] = jnp.full(m_ref.shape, -jnp.inf, F32)
    acc_ref[...] = jnp.zeros(acc_ref.shape, F32)
    ones = jnp.ones((tk, LANES), BF16)
    head_cols = [slice(h * LANES, (h + 1) * LANES) for h in range(hb)]
    qs = [jnp.concatenate([qn_ref[:, c], qr_ref[:, c]], axis=1) for c in head_cols]

    def chunk(j, diag):
        rows = pl.ds(pl.multiple_of(j * tk, tk), tk)
        k_rope = kr_ref[rows, :]
        q0 = 0 if diag is None else diag * tk
        for h, c in enumerate(head_cols):
            k = jnp.concatenate([kn_ref[rows, c], k_rope], axis=1)
            s = lax.dot_general(qs[h][q0:], k, (((1,), (1,)), ((), ())), preferred_element_type=F32)
            if diag is not None:
                keep = lax.broadcasted_iota(jnp.int32, s.shape, 0) >= lax.broadcasted_iota(jnp.int32, s.shape, 1)
                s = jnp.where(keep, s, -jnp.inf)
            m_old = m_ref[h, q0:, :]
            m_new = jnp.maximum(m_old, jnp.max(s, axis=-1, keepdims=True))
            alpha = jnp.exp2(m_old - m_new)
            p = jnp.exp2(s - jnp.concatenate([m_new] * (tk // LANES), axis=1))
            v_ext = jnp.concatenate([v_ref[rows, c], ones], axis=1)
            acc_ref[h, q0:, :] = (jnp.concatenate([alpha, alpha], axis=1) * acc_ref[h, q0:, :]
                                  + jnp.dot(p.astype(BF16), v_ext, preferred_element_type=F32))
            m_ref[h, q0:, :] = m_new

    def body(j, carry):
        chunk(j, None)
        return carry

    lax.fori_loop(0, qi * per_q, body, 0)
    for d in range(per_q):
        chunk(qi * per_q + d, d)
    for h, c in enumerate(head_cols):
        acc = acc_ref[h]
        o_ref[:, c] = (acc[:, :V_HEAD] / acc[:, V_HEAD:]).astype(o_ref.dtype)


def _attention(qn, qr, kn, kr, v, *, tq=1024, tk=512, hb=4):
    s = qn.shape[0]
    wide = hb * LANES
    tile = lambda: pl.BlockSpec((tq, wide), lambda h, i: (i, h))
    keys = lambda: pl.BlockSpec((s, wide), lambda h, i: (0, h), pipeline_mode=pl.Buffered(1))
    return pl.pallas_call(
        functools.partial(_attn_kernel, tq=tq, tk=tk, hb=hb),
        grid=(MLA_HEADS // hb, s // tq),
        in_specs=[tile(), tile(), keys(),
                  pl.BlockSpec((s, LANES), lambda h, i: (0, 0), pipeline_mode=pl.Buffered(1)), keys()],
        out_specs=tile(),
        out_shape=jax.ShapeDtypeStruct((s, MLA_HEADS * V_HEAD), BF16),
        scratch_shapes=[pltpu.VMEM((hb, tq, LANES), F32), pltpu.VMEM((hb, tq, 2 * V_HEAD), F32)],
        compiler_params=_params(("parallel", "parallel"), 48),
        name="mla_attention",
    )(qn, qr, kn, kr, v)


def _route(logits):
    lane = lax.broadcasted_iota(jnp.int32, logits.shape, 1).astype(F32)
    big = float(2 * ROUTE_LANES)
    neg = -jnp.inf
    gl = jnp.where(lane < N_GROUPS, logits, neg)
    gmax = jnp.max(gl, axis=-1, keepdims=True)
    gsum = jnp.sum(jnp.exp(gl - gmax), axis=-1, keepdims=True)
    g_prob = 1.0 / gsum
    g_idx = jnp.min(jnp.where(gl == gmax, lane, big), axis=-1, keepdims=True)
    lo = N_GROUPS + g_idx * EXPERTS_PER_GROUP
    el = jnp.where((lane >= lo) & (lane < lo + EXPERTS_PER_GROUP), logits, neg)
    emax = jnp.max(el, axis=-1, keepdims=True)
    esum = jnp.sum(jnp.exp(el - emax), axis=-1, keepdims=True)
    i0 = jnp.min(jnp.where(el == emax, lane, big), axis=-1, keepdims=True)
    el2 = jnp.where(lane == i0, neg, el)
    emax2 = jnp.max(el2, axis=-1, keepdims=True)
    i1 = jnp.min(jnp.where(el2 == emax2, lane, big), axis=-1, keepdims=True)
    p0 = 1.0 / esum
    p1 = jnp.exp(emax2 - emax) / esum
    g0 = g_prob * p0 / (p0 + p1)
    g1 = g_prob * p1 / (p0 + p1)
    e0 = i0 - N_GROUPS
    e1 = i1 - N_GROUPS
    return jnp.where(lane == 0, e0, jnp.where(lane == 1, e1, jnp.where(lane == 2, g0, jnp.where(lane == 3, g1, 0.0))))


def _out_ln_route_kernel(a1_ref, a2_ref, w_ref, x_ref, g_ref, b_ref, wrh_ref, wrl_ref, br_ref,
                         x1_ref, x1p_ref, route_ref, *, sub):
    half = a1_ref.shape[1]
    for r0 in range(0, x_ref.shape[0], sub):
        rows = slice(r0, r0 + sub)
        m = (jnp.dot(a1_ref[rows, :], w_ref[:half, :], preferred_element_type=F32)
             + jnp.dot(a2_ref[rows, :], w_ref[half:, :], preferred_element_type=F32))
        x1 = _layer_norm_rows(ALPHA * x_ref[rows, :] + m, g_ref[...], b_ref[...])
        x1_ref[rows, :] = x1
        hd = x1.shape[1] // 2
        _store_row_tiles(x1p_ref.at[pl.ds(r0 * SUBLANES, sub * SUBLANES), :], _pack_bf16_pair(x1[:, :hd], x1[:, hd:]))
        xh, xl = _split_bf16(x1)
        wrh = wrh_ref[...]
        logits = (jnp.dot(xh, wrh, preferred_element_type=F32) + jnp.dot(xl, wrh, preferred_element_type=F32)
                  + jnp.dot(xh, wrl_ref[...], preferred_element_type=F32) + br_ref[...])
        route_ref[rows, :] = _route(logits)


def _out_ln_route(a1, a2, w, x, ln_g, ln_b, w_grp, b_grp, w_exp, b_exp, *, col1=0, col2=0, tm=512, sub=256):
    s, d = x.shape
    half = w.shape[0] // 2
    pad = ROUTE_LANES - N_GROUPS - N_EXPERTS
    wr = jnp.concatenate([w_grp, w_exp, jnp.zeros((d, pad), F32)], axis=1)
    br = jnp.concatenate([b_grp, b_exp, jnp.zeros((pad,), F32)]).reshape(1, ROUTE_LANES)
    wrh = wr.astype(BF16)
    wrl = (wr - wrh.astype(F32)).astype(BF16)
    full = lambda a: pl.BlockSpec(a.shape, lambda i: (0,) * a.ndim)
    rows = lambda n: pl.BlockSpec((tm, n), lambda i: (i, 0))
    g2 = ln_g.reshape(1, d)
    b2 = ln_b.reshape(1, d)
    return pl.pallas_call(
        functools.partial(_out_ln_route_kernel, sub=sub),
        grid=(s // tm,),
        in_specs=[pl.BlockSpec((tm, half), lambda i: (i, col1)), pl.BlockSpec((tm, half), lambda i: (i, col2)),
                  pl.BlockSpec(w.shape, lambda i: (0, 0), pipeline_mode=pl.Buffered(1)),
                  rows(d), full(g2), full(b2), full(wrh), full(wrl), full(br)],
        out_specs=[rows(d), pl.BlockSpec((tm * SUBLANES, LANES), lambda i: (i, 0)), rows(ROUTE_LANES)],
        out_shape=[jax.ShapeDtypeStruct((s, d), F32), jax.ShapeDtypeStruct((s * SUBLANES, LANES), jnp.uint32),
                   jax.ShapeDtypeStruct((s, ROUTE_LANES), F32)],
        compiler_params=_params(("parallel",), 48),
        name="out_ln_route",
    )(a1, a2, w, x, g2, b2, wrh, wrl, br)


def _rank_kernel(route_ref, tri_ref, rank_ref, cnt_ref, carry_ref):
    @pl.when(pl.program_id(0) == 0)
    def _():
        carry_ref[...] = jnp.zeros(carry_ref.shape, F32)

    route = route_ref[...]
    lane = lax.broadcasted_iota(jnp.int32, route.shape, 1).astype(F32)
    oh0 = jnp.where(lane == route[:, 0:1], 1.0, 0.0)
    oh1 = jnp.where(lane == route[:, 1:2], 1.0, 0.0)
    both = oh0 + oh1
    before = jnp.dot(tri_ref[...], both.astype(BF16), preferred_element_type=F32) + carry_ref[...]
    r0 = jnp.sum(before * oh0, axis=-1, keepdims=True)
    r1 = jnp.sum(before * oh1, axis=-1, keepdims=True)
    rank_ref[...] = jnp.where(lane == 0.0, r0, jnp.where(lane == 1.0, r1, 0.0))
    carry_ref[...] = carry_ref[...] + jnp.sum(both, axis=0, keepdims=True)
    cnt_ref[...] = carry_ref[...]


def _dest_kernel(route_ref, rank_ref, pstart_ref, dest_ref):
    route = route_ref[...]
    rank = rank_ref[...]
    lane = lax.broadcasted_iota(jnp.int32, route.shape, 1).astype(F32)
    ps = pstart_ref[...]
    d0 = jnp.sum(jnp.where(lane == route[:, 0:1], ps, 0.0), axis=-1, keepdims=True) + rank[:, 0:1]
    d1 = jnp.sum(jnp.where(lane == route[:, 1:2], ps, 0.0), axis=-1, keepdims=True) + rank[:, 1:2]
    dest = jnp.where(lane == 0.0, d0, jnp.where(lane == 1.0, d1, 0.0)) * float(SUBLANES)
    dest_ref[...] = dest.astype(jnp.int32)


def _moe_plan(route, *, tb=512):
    s = route.shape[0]
    tri = (jnp.arange(tb)[:, None] > jnp.arange(tb)[None, :]).astype(BF16)
    rows = pl.BlockSpec((tb, ROUTE_LANES), lambda i: (i, 0))
    one = lambda: pl.BlockSpec((1, ROUTE_LANES), lambda i: (0, 0))
    rank, cnt = pl.pallas_call(
        _rank_kernel,
        grid=(s // tb,),
        in_specs=[rows, pl.BlockSpec((tb, tb), lambda i: (0, 0))],
        out_specs=[rows, one()],
        out_shape=[jax.ShapeDtypeStruct((s, ROUTE_LANES), F32), jax.ShapeDtypeStruct((1, ROUTE_LANES), F32)],
        scratch_shapes=[pltpu.VMEM((1, ROUTE_LANES), F32)],
        compiler_params=_params(("arbitrary",), 32),
        name="moe_rank",
    )(route, tri)
    counts = cnt[0, :N_EXPERTS].astype(jnp.int32)
    pcounts = (counts + MOE_ROWS - 1) // MOE_ROWS * MOE_ROWS
    pends = jnp.cumsum(pcounts)
    pstarts = pends - pcounts
    n_blocks = s * TOP_K // MOE_ROWS + N_EXPERTS
    n_used = pends[-1] // MOE_ROWS
    blk = jnp.minimum(jnp.arange(n_blocks, dtype=jnp.int32), n_used - 1) * MOE_ROWS
    block_expert = jnp.minimum(jnp.sum(pends[None, :] <= blk[:, None], axis=1), N_EXPERTS - 1).astype(jnp.int32)
    ps = jnp.zeros((1, ROUTE_LANES), F32).at[0, :N_EXPERTS].set(pstarts.astype(F32))
    dest = pl.pallas_call(
        _dest_kernel,
        grid=(s // tb,),
        in_specs=[rows, rows, one()],
        out_specs=rows,
        out_shape=jax.ShapeDtypeStruct((s, ROUTE_LANES), jnp.int32),
        compiler_params=_params(("parallel",), 32),
        name="moe_dest",
    )(route, rank, ps)
    dest = dest[:, :TOP_K].T.reshape(TOP_K * s)
    return dest, block_expert, n_used.astype(jnp.int32).reshape(1)


ROW_UNROLL = 8


def _row_tile(ref, first_row):
    return ref.at[pl.ds(pl.multiple_of(first_row, SUBLANES), SUBLANES), :]


def _dispatch_kernel(dest_ref, x_hbm, xs_hbm, sems, *, tm, s):
    t0 = pl.program_id(0) * tm

    def issue(g, carry):
        for u in range(ROW_UNROLL):
            r = g * ROW_UNROLL + u
            src = _row_tile(x_hbm, (t0 + r) * SUBLANES)
            for k in range(TOP_K):
                pltpu.make_async_copy(src, _row_tile(xs_hbm, dest_ref[k * s + t0 + r]), sems.at[k]).start(priority=k)
        return carry

    lax.fori_loop(0, tm // ROW_UNROLL, issue, 0)
    span = pl.ds(0, tm * SUBLANES)
    for k in range(TOP_K):
        pltpu.make_async_copy(x_hbm.at[span, :], xs_hbm.at[span, :], sems.at[k]).wait()


def _dispatch_rows(x, dest, n_blocks, *, tm=256):
    s = x.shape[0] // SUBLANES
    return pl.pallas_call(
        functools.partial(_dispatch_kernel, tm=tm, s=s),
        grid_spec=pltpu.PrefetchScalarGridSpec(
            num_scalar_prefetch=1,
            grid=(s // tm,),
            in_specs=[pl.BlockSpec(memory_space=pl.ANY)],
            out_specs=pl.BlockSpec(memory_space=pl.ANY),
            scratch_shapes=[pltpu.SemaphoreType.DMA((TOP_K,))]),
        out_shape=jax.ShapeDtypeStruct((n_blocks * MOE_ROWS * SUBLANES, LANES), x.dtype),
        compiler_params=_params(("arbitrary",), 32),
        name="moe_dispatch",
    )(dest, x)


FFN_K_CHUNK = 512


def _ffn_kernel(nused_ref, run_ref, first_ref, rexp_ref, nruns_ref, xs_ref, w1_hbm, w3_hbm, w2_hbm, ys_ref,
                w1_buf, w3_buf, w2_buf, sems, *, layer):
    b = pl.program_id(0)

    def weight_copies(run, slot):
        e = rexp_ref[run]
        return [pltpu.make_async_copy(w_hbm.at[layer, e], buf.at[slot], sems.at[slot, j])
                for j, (w_hbm, buf) in enumerate(((w1_hbm, w1_buf), (w3_hbm, w3_buf), (w2_hbm, w2_buf)))]

    @pl.when(b == 0)
    def _():
        for c in weight_copies(0, 0):
            c.start()

    @pl.when(b < nused_ref[0])
    def _():
        run = run_ref[b]
        slot = run % 2

        @pl.when(first_ref[b] == 1)
        def _():
            @pl.when(run + 1 < nruns_ref[0])
            def _():
                for c in weight_copies(run + 1, 1 - slot):
                    c.start()

            for c in weight_copies(run, slot):
                c.wait()

        x_lo, x_hi = _unpack_bf16_pair(_load_row_tiles(xs_ref))
        hd = x_lo.shape[1]
        x = jnp.concatenate([x_lo.astype(BF16), x_hi.astype(BF16)], axis=1)

        def up(buf):
            acc = None
            for k0 in range(0, 2 * hd, FFN_K_CHUNK):
                part = jnp.dot(x[:, k0:k0 + FFN_K_CHUNK], buf[slot, k0:k0 + FFN_K_CHUNK, :].astype(BF16),
                               preferred_element_type=F32)
                acc = part if acc is None else acc + part
            return acc

        h1 = up(w1_buf)
        h3 = up(w3_buf)
        hmid = (h1 * jax.nn.sigmoid(h1) * h3).astype(BF16)
        y = jnp.dot(hmid, w2_buf[slot].astype(BF16), preferred_element_type=F32)
        _store_row_tiles(ys_ref, _pack_bf16_pair(y[:, :hd], y[:, hd:]))


def _grouped_ffn(xs, w1, w3, w2, layer, block_expert, n_used):
    d = xs.shape[1]
    dm = w1.shape[2]
    de = w1.shape[3]
    nb = block_expert.shape[0]
    first = jnp.concatenate([jnp.ones((1,), jnp.int32), (block_expert[1:] != block_expert[:-1]).astype(jnp.int32)])
    run = jnp.cumsum(first) - 1
    n_runs = run[-1:] + 1
    hit = (run[None, :] == jnp.arange(nb, dtype=jnp.int32)[:, None]) & (first[None, :] == 1)
    run_expert = jnp.sum(jnp.where(hit, block_expert[None, :], 0), axis=1).astype(jnp.int32)
    blk = lambda b, nu, *_: (jnp.minimum(b, nu[0] - 1), 0)
    return pl.pallas_call(
        functools.partial(_ffn_kernel, layer=layer),
        grid_spec=pltpu.PrefetchScalarGridSpec(
            num_scalar_prefetch=5,
            grid=(nb,),
            in_specs=[pl.BlockSpec((MOE_ROWS * SUBLANES, d), blk),
                      pl.BlockSpec(memory_space=pl.ANY),
                      pl.BlockSpec(memory_space=pl.ANY),
                      pl.BlockSpec(memory_space=pl.ANY)],
            out_specs=pl.BlockSpec((MOE_ROWS * SUBLANES, d), blk),
            scratch_shapes=[pltpu.VMEM((2, dm, de), F32), pltpu.VMEM((2, dm, de), F32),
                            pltpu.VMEM((2, de, dm), F32), pltpu.SemaphoreType.DMA((2, 3))]),
        out_shape=jax.ShapeDtypeStruct(xs.shape, jnp.uint32),
        compiler_params=_params(("arbitrary",), 56),
        name="moe_ffn",
    )(n_used, run.astype(jnp.int32), first, run_expert, n_runs.astype(jnp.int32), xs, w1, w3, w2)


def _combine_ple_kernel(pos_ref, ys_hbm, x1_ref, route_ref, g_ref, b_ref, wg_ref, bg_ref, p_ref, wp_ref,
                        o_ref, ob_ref, ybuf_a, ybuf_b, sems, *, tm, s):
    i = pl.program_id(0)
    n = pl.num_programs(0)

    def start_row(t0, buf, buf_id, r):
        first = r * SUBLANES if isinstance(r, int) else pl.multiple_of(r * SUBLANES, SUBLANES)
        for k in range(TOP_K):
            pltpu.make_async_copy(_row_tile(ys_hbm, pos_ref[k * s + t0 + r]),
                                  buf.at[k, pl.ds(first, SUBLANES), :], sems.at[buf_id, k]).start(priority=k)

    def wait_buffer(buf, buf_id):
        span = ys_hbm.at[pl.ds(0, tm * SUBLANES), :]
        for k in range(TOP_K):
            pltpu.make_async_copy(span, buf.at[k], sems.at[buf_id, k]).wait()

    @pl.when(i == 0)
    def _():
        def issue(g, carry):
            for u in range(ROW_UNROLL):
                start_row(0, ybuf_a, 0, g * ROW_UNROLL + u)
            return carry

        lax.fori_loop(0, tm // ROW_UNROLL, issue, 0)

    def step(cur, cur_id, nxt_buf, nxt_id):
        wait_buffer(cur, cur_id)
        nxt = jnp.minimum(i + 1, n - 1) * tm
        for r in range(tm):
            start_row(nxt, nxt_buf, nxt_id, r)

        route = route_ref[...]
        f = None
        for k in range(TOP_K):
            lo, hi = _unpack_bf16_pair(_load_row_tiles(cur.at[k]))
            fk = route[:, TOP_K + k:TOP_K + k + 1] * jnp.concatenate([lo, hi], axis=1)
            f = fk if f is None else f + fk
        x2 = _layer_norm_rows(ALPHA * x1_ref[...] + f, g_ref[...], b_ref[...])
        gate = jax.nn.sigmoid(jnp.dot(x2.astype(BF16), wg_ref[...], preferred_element_type=F32) + bg_ref[...])
        emb = jnp.dot(p_ref[...].astype(BF16), wp_ref[...], preferred_element_type=F32)
        out = x2 + gate * emb
        o_ref[...] = out
        ob_ref[...] = out.astype(BF16)

        @pl.when(i == n - 1)
        def _():
            wait_buffer(nxt_buf, nxt_id)

    @pl.when(i % 2 == 0)
    def _():
        step(ybuf_a, 0, ybuf_b, 1)

    @pl.when(i % 2 == 1)
    def _():
        step(ybuf_b, 1, ybuf_a, 0)


def _combine_ple(ys, pos, x1, route, ln_g, ln_b, w_gate, b_gate, p, w_proj, *, tm=256):
    s, d = x1.shape
    pd = p.shape[1]
    rows = lambda n: pl.BlockSpec((tm, n), lambda i, pos: (i, 0))
    full = lambda a: pl.BlockSpec(a.shape, lambda i, pos: (0,) * a.ndim)
    g2, b2, bg = ln_g.reshape(1, d), ln_b.reshape(1, d), b_gate.reshape(1, d)
    return pl.pallas_call(
        functools.partial(_combine_ple_kernel, tm=tm, s=s),
        grid_spec=pltpu.PrefetchScalarGridSpec(
            num_scalar_prefetch=1,
            grid=(s // tm,),
            in_specs=[pl.BlockSpec(memory_space=pl.ANY), rows(d), rows(ROUTE_LANES), full(g2), full(b2),
                      full(w_gate), full(bg), rows(pd), full(w_proj)],
            out_specs=[rows(d), rows(d)],
            scratch_shapes=[pltpu.VMEM((TOP_K, tm * SUBLANES, LANES), jnp.uint32),
                            pltpu.VMEM((TOP_K, tm * SUBLANES, LANES), jnp.uint32),
                            pltpu.SemaphoreType.DMA((2, TOP_K))]),
        out_shape=[jax.ShapeDtypeStruct((s, d), F32), jax.ShapeDtypeStruct((s, d), BF16)],
        compiler_params=_params(("arbitrary",), 56),
        name="moe_combine_ple",
    )(pos, ys, x1, route, g2, b2, w_gate, bg, p, w_proj)


def _rglru_kernel(cur_ref, prev_ref, gate_ref, cw_ref, cb_ref, wa_ref, ba_ref, wi_ref, bi_ref, lam_ref,
                  o_ref, buf_ref, a_ref, b_ref, h_ref, carry_ref, *, tt, tc, width):
    i = pl.program_id(1)

    @pl.when(i == 0)
    def _():
        carry_ref[...] = jnp.zeros(carry_ref.shape, F32)

    buf_ref[0:RNN_HALO, :] = jnp.where(i > 0, prev_ref[...], 0.0)
    buf_ref[RNN_HALO:, :] = cur_ref[...]
    off = RNN_HALO - (width - 1)
    z = -lam_ref[...]
    softplus = jnp.log1p(jnp.exp(-jnp.abs(z))) + jnp.maximum(z, 0.0)

    def sigmoid(v):
        return 0.5 * jnp.tanh(0.5 * v) + 0.5

    for hh in range(tc // RNN_HEAD_DIM):
        cols = slice(hh * RNN_HEAD_DIM, (hh + 1) * RNN_HEAD_DIM)
        xh = jnp.broadcast_to(cb_ref[:, cols], (tt, RNN_HEAD_DIM))
        for k in range(width):
            xh = xh + buf_ref[off + k:off + k + tt, cols] * cw_ref[k:k + 1, cols]
        xhb = xh.astype(BF16)
        r = sigmoid(jnp.dot(xhb, wa_ref[hh], preferred_element_type=F32) + ba_ref[:, cols])
        ig = sigmoid(jnp.dot(xhb, wi_ref[hh], preferred_element_type=F32) + bi_ref[:, cols])
        log_a = (-RG_C) * r * softplus[:, cols]
        a = jnp.exp(log_a)
        bval = jnp.sqrt(-jnp.tanh(log_a) * (a * a + 1.0)) * (ig * xh)
        a_ref[:, cols] = a
        b_ref[:, cols] = bval

    def group(gi, h):
        base = pl.multiple_of(gi * 8, 8)
        for r in range(8):
            row = pl.ds(base + r, 1)
            h = a_ref[row, :] * h + b_ref[row, :]
            h_ref[row, :] = h
        return h

    carry_ref[...] = lax.fori_loop(0, tt // 8, group, carry_ref[...])
    o_ref[...] = (gate_ref[...].astype(F32) * h_ref[...]).astype(o_ref.dtype)


def _rglru(xr, gate, conv_w, conv_b, w_a, b_a, w_i, b_i, lam, *, tt=256, tc=2048):
    s, wd = xr.shape
    width = conv_w.shape[0]
    hpc = tc // RNN_HEAD_DIM
    hb = tt // RNN_HALO
    row = lambda v: v.reshape(1, wd)
    cvec = lambda: pl.BlockSpec((1, tc), lambda c, i: (0, c))
    heads = lambda: pl.BlockSpec((hpc, RNN_HEAD_DIM, RNN_HEAD_DIM), lambda c, i: (c, 0, 0))
    tile = lambda: pl.BlockSpec((tt, tc), lambda c, i: (i, c))
    return pl.pallas_call(
        functools.partial(_rglru_kernel, tt=tt, tc=tc, width=width),
        grid=(wd // tc, s // tt),
        in_specs=[tile(),
                  pl.BlockSpec((RNN_HALO, tc), lambda c, i: (jnp.maximum(i * hb - 1, 0), c)),
                  tile(),
                  pl.BlockSpec((width, tc), lambda c, i: (0, c)),
                  cvec(), heads(), cvec(), heads(), cvec(), cvec()],
        out_specs=tile(),
        out_shape=jax.ShapeDtypeStruct((s, wd), BF16),
        scratch_shapes=[pltpu.VMEM((tt + RNN_HALO, tc), F32), pltpu.VMEM((tt, tc), F32),
                        pltpu.VMEM((tt, tc), F32), pltpu.VMEM((tt, tc), F32), pltpu.VMEM((1, tc), F32)],
        compiler_params=_params(("parallel", "arbitrary"), 32),
        name="rglru",
    )(xr, xr, gate, conv_w, row(conv_b), w_a.astype(BF16), row(b_a), w_i.astype(BF16), row(b_i), row(lam))


def _moe_and_ple(mixed, i, p, ln_ffn_g, ln_ffn_b, moe_w1, moe_w3, moe_w2, ple_w_proj, ple_w_gate, ple_b_gate):
    x1, x1_pairs, route = mixed
    dest, block_expert, n_used = _moe_plan(route)
    xs = _dispatch_rows(x1_pairs, dest, block_expert.shape[0])
    ys = _grouped_ffn(xs, moe_w1, moe_w3, moe_w2, i, block_expert, n_used)
    return _combine_ple(ys, dest, x1, route, ln_ffn_g[i], ln_ffn_b[i], ple_w_gate[i].astype(BF16),
                        ple_b_gate[i], p[i, 0], ple_w_proj[i].astype(BF16))


def kernel(x, p, positions, ev_w_in, ev_conv_w, ev_conv_b, ev_cnorm_g, ev_cnorm_b, ev_qnorm_g, ev_w_uq, ev_kvnorm_g, ev_w_ukv, ev_w_out, od_w_in, od_conv_w, od_conv_b, od_w_a, od_b_a, od_w_i, od_b_i, od_lam, od_w_out, ln_mix_g, ln_mix_b, ln_ffn_g, ln_ffn_b, moe_w_grp, moe_b_grp, moe_w_exp, moe_b_exp, moe_w1, moe_w3, moe_w2, ple_w_proj, ple_w_gate, ple_b_gate):
    x0 = x[0]
    x0b = x0.astype(BF16)
    moe = functools.partial(_moe_and_ple, p=p, ln_ffn_g=ln_ffn_g, ln_ffn_b=ln_ffn_b, moe_w1=moe_w1,
                            moe_w3=moe_w3, moe_w2=moe_w2, ple_w_proj=ple_w_proj, ple_w_gate=ple_w_gate,
                            ple_b_gate=ple_b_gate)

    w_in = ev_w_in[0]
    u = _glu_proj(x0b, w_in[:, :2 * CONV_CH].astype(BF16))
    ub = _conv_gln(u, ev_conv_w[0], ev_conv_b[0], ev_cnorm_g[0], ev_cnorm_b[0], groups=CONV_GROUPS)
    cosm, sinm = _rope_tables(positions[0])
    qn, qr, kn, kr, v = _mla_proj(x0b, w_in[:, 2 * CONV_CH:], ev_qnorm_g[0], ev_w_uq[0], ev_kvnorm_g[0],
                                  ev_w_ukv[0], cosm, sinm)
    att = _attention(qn, qr, kn, kr, v)
    mixed0 = _out_ln_route(ub, att, ev_w_out[0].astype(BF16), x0, ln_mix_g[0], ln_mix_b[0],
                           moe_w_grp[0], moe_b_grp[0], moe_w_exp[0], moe_b_exp[0])
    x3, x3b = moe(mixed0, 0)

    w_in1 = od_w_in[0]
    gate, xr = _rnn_in_proj(x3b, w_in1.astype(BF16))
    y = _rglru(xr, gate, od_conv_w[0], od_conv_b[0], od_w_a[0], od_b_a[0], od_w_i[0], od_b_i[0], od_lam[0])
    mixed1 = _out_ln_route(y, y, od_w_out[0].astype(BF16), x3, ln_mix_g[1], ln_mix_b[1],
                           moe_w_grp[1], moe_b_grp[1], moe_w_exp[1], moe_b_exp[1], col1=0, col2=1)
    x6, _ = moe(mixed1, 1)
    return x6[None]
```

```python
import functools
import math

import jax
import jax.numpy as jnp
from jax import lax
from jax.experimental import pallas as pl
from jax.experimental.pallas import tpu as pltpu

F32 = jnp.float32
BF16 = jnp.bfloat16

D_MODEL = 2048
SEQ = 8192
DEPTH = 2
CONV_CH = 1024
CONV_GROUPS = 16
CONV_WIDTH = 31
MLA_HEADS = 8
QK_NOPE = 128
QK_ROPE = 64
V_HEAD = 128
Q_RANK = 512
KV_RANK = 256
ROPE_THETA = 10000.0
RNN_WIDTH = 2048
RNN_HEADS = 16
RNN_HEAD_DIM = RNN_WIDTH // RNN_HEADS
RNN_CONV_WIDTH = 4
RG_C = 8.0
N_GROUPS = 8
EXPERTS_PER_GROUP = 8
N_EXPERTS = N_GROUPS * EXPERTS_PER_GROUP
TOP_K = 2
D_EXPERT = 512
PLE_DIM = 256
ALPHA = (2 * DEPTH) ** 0.25
LN_EPS = 1e-5
RMS_EPS = 1e-6

LANES = 128
SUBLANES = 8
CONV_HALO = 32
RNN_HALO = 8
MOE_ROWS = 256
ROUTE_LANES = 128
MIB = 2 ** 20


def _params(semantics, vmem_mib):
    return pltpu.CompilerParams(dimension_semantics=semantics, vmem_limit_bytes=vmem_mib * MIB)


def _layer_norm_rows(z, g, b):
    mu = jnp.mean(z, axis=-1, keepdims=True)
    d = z - mu
    var = jnp.mean(d * d, axis=-1, keepdims=True)
    return d * lax.rsqrt(var + LN_EPS) * g + b


def _pack_bf16_pair(a, b):
    def rounded_bits(v):
        u = lax.bitcast_convert_type(v, jnp.uint32)
        return u + jnp.uint32(0x7FFF) + ((u >> 16) & jnp.uint32(1))

    return (rounded_bits(a) & jnp.uint32(0xFFFF0000)) | (rounded_bits(b) >> 16)


def _unpack_bf16_pair(u):
    a = lax.bitcast_convert_type(u & jnp.uint32(0xFFFF0000), F32)
    b = lax.bitcast_convert_type(u << 16, F32)
    return a, b


def _store_row_tiles(ref, packed):
    n, w = packed.shape
    assert w == SUBLANES * LANES
    for j in range(SUBLANES):
        ref[pl.ds(j, n, stride=SUBLANES), :] = packed[:, j * LANES:(j + 1) * LANES]


def _load_row_tiles(ref):
    n = ref.shape[0] // SUBLANES
    return jnp.concatenate([ref[pl.ds(j, n, stride=SUBLANES), :] for j in range(SUBLANES)], axis=1)


def _split_bf16(v):
    hi = v.astype(BF16)
    lo = (v - hi.astype(F32)).astype(BF16)
    return hi, lo


def _gelu_tanh(x):
    c = math.sqrt(2.0 / math.pi)
    return 0.5 * x * (1.0 + jnp.tanh(c * (x + 0.044715 * (x * x * x))))


def _rnn_in_kernel(x_ref, wg_ref, wx_ref, gate_ref, xr_ref):
    x = x_ref[...]
    gate_ref[...] = _gelu_tanh(jnp.dot(x, wg_ref[...], preferred_element_type=F32)).astype(gate_ref.dtype)
    xr_ref[...] = jnp.dot(x, wx_ref[...], preferred_element_type=F32)


def _rnn_in_proj(x, w, *, tm=512, tn=512):
    m, k = x.shape
    n = w.shape[1] // 2
    nb = n // tn
    tile = pl.BlockSpec((tm, tn), lambda j, i: (i, j))
    return pl.pallas_call(
        _rnn_in_kernel,
        grid=(nb, m // tm),
        in_specs=[pl.BlockSpec((tm, k), lambda j, i: (i, 0)),
                  pl.BlockSpec((k, tn), lambda j, i: (0, j)),
                  pl.BlockSpec((k, tn), lambda j, i: (0, j + nb))],
        out_specs=[tile, tile],
        out_shape=[jax.ShapeDtypeStruct((m, n), BF16), jax.ShapeDtypeStruct((m, n), F32)],
        compiler_params=_params(("parallel", "parallel"), 40),
        name="rnn_in_proj",
    )(x, w, w)


def _glu_kernel(x_ref, wv_ref, wg_ref, o_ref):
    x = x_ref[...]
    v = jnp.dot(x, wv_ref[...], preferred_element_type=F32)
    g = jnp.dot(x, wg_ref[...], preferred_element_type=F32)
    o_ref[...] = v * jax.nn.sigmoid(g)


def _glu_proj(x, w, *, tm=512, tn=512):
    m, k = x.shape
    n = w.shape[1] // 2
    nb = n // tn
    return pl.pallas_call(
        _glu_kernel,
        grid=(nb, m // tm),
        in_specs=[pl.BlockSpec((tm, k), lambda j, i: (i, 0)),
                  pl.BlockSpec((k, tn), lambda j, i: (0, j)),
                  pl.BlockSpec((k, tn), lambda j, i: (0, j + nb))],
        out_specs=pl.BlockSpec((tm, tn), lambda j, i: (i, j)),
        out_shape=jax.ShapeDtypeStruct((m, n), F32),
        compiler_params=_params(("parallel", "parallel"), 40),
        name="glu_proj",
    )(x, w, w)


def _conv_gln_kernel(cur_ref, prev_ref, w_ref, b_ref, g_ref, beta_ref, gm_ref, o_ref,
                     buf_ref, sh_ref, y_ref, *, tt, ch, width, rc, cc):
    i = pl.program_id(0)
    buf_ref[0:CONV_HALO, :] = jnp.where(i > 0, prev_ref[...], 0.0)
    buf_ref[CONV_HALO:, :] = cur_ref[...]
    off = CONV_HALO - (width - 1)
    sh_rows = sh_ref.shape[1]
    step = 7 * SUBLANES
    assert sh_rows % step == 0
    for b in range(1, SUBLANES):
        for c0 in range(0, ch, cc):
            for r0 in range(0, sh_rows, step):
                sh_ref[b - 1, r0:r0 + step, c0:c0 + cc] = buf_ref[r0 + b:r0 + b + step, c0:c0 + cc]
    for c0 in range(0, ch, cc):
        for r0 in range(0, tt, rc):
            acc = jnp.broadcast_to(b_ref[:, c0:c0 + cc], (rc, cc))
            for k in range(width):
                q, b = divmod(off + k, SUBLANES)
                lo = r0 + q * SUBLANES
                src = buf_ref[lo:lo + rc, c0:c0 + cc] if b == 0 else sh_ref[b - 1, lo:lo + rc, c0:c0 + cc]
                acc = acc + src * w_ref[k:k + 1, c0:c0 + cc]
            y_ref[r0:r0 + rc, c0:c0 + cc] = acc
    gm = gm_ref[...]

    def seg_mean(v):
        hi, lo = _split_bf16(v)
        return (jnp.dot(hi, gm, preferred_element_type=F32)
                + jnp.dot(lo, gm, preferred_element_type=F32))

    for c0 in range(0, ch, LANES):
        y = y_ref[:, c0:c0 + LANES]
        d = y - seg_mean(y)
        var = seg_mean(d * d)
        z = d * lax.rsqrt(var + LN_EPS) * g_ref[:, c0:c0 + LANES] + beta_ref[:, c0:c0 + LANES]
        o_ref[:, c0:c0 + LANES] = (z * jax.nn.sigmoid(z)).astype(o_ref.dtype)


def _conv_gln(u, w, b, g, beta, *, groups, tt=256):
    s, ch = u.shape
    width = w.shape[0]
    gsz = ch // groups
    assert LANES % gsz == 0 and width - 1 <= CONV_HALO and tt % CONV_HALO == 0
    wpad = jnp.zeros((CONV_HALO, ch), F32).at[:width].set(w)
    lane = jnp.arange(LANES)
    gm = jnp.where((lane[:, None] // gsz) == (lane[None, :] // gsz), 1.0 / gsz, 0.0).astype(BF16)
    hb = tt // CONV_HALO
    row = lambda v: v.reshape(1, ch)
    return pl.pallas_call(
        functools.partial(_conv_gln_kernel, tt=tt, ch=ch, width=width, rc=32, cc=256),
        grid=(s // tt,),
        in_specs=[pl.BlockSpec((tt, ch), lambda i: (i, 0)),
                  pl.BlockSpec((CONV_HALO, ch), lambda i: (jnp.maximum(i * hb - 1, 0), 0)),
                  pl.BlockSpec((CONV_HALO, ch), lambda i: (0, 0)),
                  pl.BlockSpec((1, ch), lambda i: (0, 0)),
                  pl.BlockSpec((1, ch), lambda i: (0, 0)),
                  pl.BlockSpec((1, ch), lambda i: (0, 0)),
                  pl.BlockSpec((LANES, LANES), lambda i: (0, 0))],
        out_specs=pl.BlockSpec((tt, ch), lambda i: (i, 0)),
        out_shape=jax.ShapeDtypeStruct((s, ch), BF16),
        scratch_shapes=[pltpu.VMEM((tt + CONV_HALO, ch), F32),
                        pltpu.VMEM((SUBLANES - 1, tt + CONV_HALO - SUBLANES, ch), F32),
                        pltpu.VMEM((tt, ch), F32)],
        compiler_params=_params(("parallel",), 40),
        name="conv_gln",
    )(u, u, wpad, row(b), row(g), row(beta), gm)


def _rope_table_kernel(pos_ref, invf_ref, cos_ref, sin_ref):
    ang = pos_ref[...].astype(F32) * invf_ref[...]
    cos_ref[...] = jnp.cos(ang)
    sin_ref[...] = jnp.sin(ang)


def _rope_tables(positions):
    s = positions.shape[0]
    half = QK_ROPE // 2
    per_row = LANES // half
    inv_freq = 1.0 / (ROPE_THETA ** (jnp.arange(0, QK_ROPE, 2, dtype=F32) / QK_ROPE))
    pos_rep = jnp.repeat(positions.reshape(s // per_row, per_row), half, axis=1)
    invf = jnp.tile(inv_freq, per_row).reshape(1, LANES)
    cos, sin = pl.pallas_call(
        _rope_table_kernel,
        out_shape=[jax.ShapeDtypeStruct((s // per_row, LANES), F32)] * 2,
        name="rope_table",
    )(pos_rep, invf)
    cos = cos.reshape(s, half)
    sin = sin.reshape(s, half)
    zero = jnp.zeros((s, LANES - QK_ROPE), F32)
    return jnp.concatenate([cos, cos, zero], axis=1), jnp.concatenate([-sin, sin, zero], axis=1)


def _mla_proj_kernel(x_ref, wm_ref, qg_ref, kvg_ref, wq_ref, wkv_ref, c_ref, s_ref,
                     qn_ref, qr_ref, kn_ref, kr_ref, v_ref, *, scale):
    nq = MLA_HEADS * QK_NOPE
    c = jnp.dot(x_ref[...], wm_ref[...], preferred_element_type=F32)
    cq = c[:, :Q_RANK]
    ckv = c[:, Q_RANK:Q_RANK + KV_RANK]
    kr_pad = c[:, Q_RANK + KV_RANK:Q_RANK + KV_RANK + LANES]
    kr_rot = c[:, Q_RANK + KV_RANK + LANES:]
    cqn = cq * lax.rsqrt(jnp.mean(cq * cq, axis=-1, keepdims=True) + RMS_EPS) * qg_ref[...]
    ckvn = ckv * lax.rsqrt(jnp.mean(ckv * ckv, axis=-1, keepdims=True) + RMS_EPS) * kvg_ref[...]
    q = jnp.dot(cqn.astype(BF16), wq_ref[...], preferred_element_type=F32)
    kv = jnp.dot(ckvn.astype(BF16), wkv_ref[...], preferred_element_type=F32)
    cosm = c_ref[...]
    sinm = s_ref[...]
    qn_ref[...] = (q[:, :nq] * scale).astype(BF16)
    for h in range(MLA_HEADS):
        lo = nq + h * LANES
        rot = q[:, lo:lo + LANES] * cosm + q[:, lo + nq:lo + nq + LANES] * sinm
        qr_ref[:, h * LANES:(h + 1) * LANES] = (rot * scale).astype(BF16)
    kn_ref[...] = kv[:, :nq].astype(BF16)
    v_ref[...] = kv[:, nq:].astype(BF16)
    kr_ref[...] = (kr_pad * cosm + kr_rot * sinm).astype(BF16)


def _rope_swap(w):
    half = QK_ROPE // 2
    return jnp.concatenate([w[..., half:], w[..., :half]], axis=-1)


def _mla_proj(xb, w_in_mla, qnorm_g, w_uq, kvnorm_g, w_ukv, cosm, sinm, *, tm=256):
    s, d = xb.shape
    nq = MLA_HEADS * QK_NOPE
    pad = lambda w: jnp.concatenate([w, jnp.zeros(w.shape[:-1] + (LANES - QK_ROPE,), w.dtype)], axis=-1)
    w_kr = w_in_mla[:, Q_RANK + KV_RANK:]
    wm = jnp.concatenate([w_in_mla[:, :Q_RANK + KV_RANK], pad(w_kr), pad(_rope_swap(w_kr))], axis=1).astype(BF16)
    wq3 = w_uq.reshape(Q_RANK, MLA_HEADS, QK_NOPE + QK_ROPE)
    wq_rope = wq3[:, :, QK_NOPE:]
    wq = jnp.concatenate([wq3[:, :, :QK_NOPE].reshape(Q_RANK, nq),
                          pad(wq_rope).reshape(Q_RANK, MLA_HEADS * LANES),
                          pad(_rope_swap(wq_rope)).reshape(Q_RANK, MLA_HEADS * LANES)], axis=1).astype(BF16)
    wkv3 = w_ukv.reshape(KV_RANK, MLA_HEADS, QK_NOPE + V_HEAD)
    wkv = jnp.concatenate([wkv3[:, :, :QK_NOPE].reshape(KV_RANK, nq),
                           wkv3[:, :, QK_NOPE:].reshape(KV_RANK, MLA_HEADS * V_HEAD)], axis=1).astype(BF16)
    scale = (QK_NOPE + QK_ROPE) ** -0.5 * math.log2(math.e)
    full = lambda a: pl.BlockSpec(a.shape, lambda i: (0,) * a.ndim)
    rows = lambda n: pl.BlockSpec((tm, n), lambda i: (i, 0))
    qg = qnorm_g.reshape(1, Q_RANK)
    kvg = kvnorm_g.reshape(1, KV_RANK)
    return pl.pallas_call(
        functools.partial(_mla_proj_kernel, scale=scale),
        grid=(s // tm,),
        in_specs=[rows(d), full(wm), full(qg), full(kvg), full(wq), full(wkv), rows(LANES), rows(LANES)],
        out_specs=[rows(nq), rows(MLA_HEADS * LANES), rows(nq), rows(LANES), rows(MLA_HEADS * V_HEAD)],
        out_shape=[jax.ShapeDtypeStruct((s, nq), BF16),
                   jax.ShapeDtypeStruct((s, MLA_HEADS * LANES), BF16),
                   jax.ShapeDtypeStruct((s, nq), BF16),
                   jax.ShapeDtypeStruct((s, LANES), BF16),
                   jax.ShapeDtypeStruct((s, MLA_HEADS * V_HEAD), BF16)],
        compiler_params=_params(("parallel",), 48),
        name="mla_proj",
    )(xb, wm, qg, kvg, wq, wkv, cosm, sinm)


def _attn_kernel(qn_ref, qr_ref, kn_ref, kr_ref, v_ref, o_ref, m_ref, acc_ref, *, tq, tk, hb):
    qi = pl.program_id(1)
    per_q = tq // tk
    m_ref[...] = jnp.full(m_ref.shape, -jnp.inf, F32)
    acc_ref[...] = jnp.zeros(acc_ref.shape, F32)
    ones = jnp.ones((tk, LANES), BF16)
    head_cols = [slice(h * LANES, (h + 1) * LANES) for h in range(hb)]
    qs = [jnp.concatenate([qn_ref[:, c], qr_ref[:, c]], axis=1) for c in head_cols]

    def chunk(j, diag):
        rows = pl.ds(pl.multiple_of(j * tk, tk), tk)
        k_rope = kr_ref[rows, :]
        q0 = 0 if diag is None else diag * tk
        for h, c in enumerate(head_cols):
            k = jnp.concatenate([kn_ref[rows, c], k_rope], axis=1)
            s = lax.dot_general(qs[h][q0:], k, (((1,), (1,)), ((), ())), preferred_element_type=F32)
            if diag is not None:
                keep = lax.broadcasted_iota(jnp.int32, s.shape, 0) >= lax.broadcasted_iota(jnp.int32, s.shape, 1)
                s = jnp.where(keep, s, -jnp.inf)
            m_old = m_ref[h, q0:, :]
            m_new = jnp.maximum(m_old, jnp.max(s, axis=-1, keepdims=True))
            alpha = jnp.exp2(m_old - m_new)
            p = jnp.exp2(s - jnp.concatenate([m_new] * (tk // LANES), axis=1))
            v_ext = jnp.concatenate([v_ref[rows, c], ones], axis=1)
            acc_ref[h, q0:, :] = (jnp.concatenate([alpha, alpha], axis=1) * acc_ref[h, q0:, :]
                                  + jnp.dot(p.astype(BF16), v_ext, preferred_element_type=F32))
            m_ref[h, q0:, :] = m_new

    def body(j, carry):
        chunk(j, None)
        return carry

    lax.fori_loop(0, qi * per_q, body, 0)
    for d in range(per_q):
        chunk(qi * per_q + d, d)
    for h, c in enumerate(head_cols):
        acc = acc_ref[h]
        o_ref[:, c] = (acc[:, :V_HEAD] / acc[:, V_HEAD:]).astype(o_ref.dtype)


def _attention(qn, qr, kn, kr, v, *, tq=1024, tk=512, hb=4):
    s = qn.shape[0]
    wide = hb * LANES
    tile = lambda: pl.BlockSpec((tq, wide), lambda h, i: (i, h))
    keys = lambda: pl.BlockSpec((s, wide), lambda h, i: (0, h), pipeline_mode=pl.Buffered(1))
    return pl.pallas_call(
        functools.partial(_attn_kernel, tq=tq, tk=tk, hb=hb),
        grid=(MLA_HEADS // hb, s // tq),
        in_specs=[tile(), tile(), keys(),
                  pl.BlockSpec((s, LANES), lambda h, i: (0, 0), pipeline_mode=pl.Buffered(1)), keys()],
        out_specs=tile(),
        out_shape=jax.ShapeDtypeStruct((s, MLA_HEADS * V_HEAD), BF16),
        scratch_shapes=[pltpu.VMEM((hb, tq, LANES), F32), pltpu.VMEM((hb, tq, 2 * V_HEAD), F32)],
        compiler_params=_params(("parallel", "parallel"), 48),
        name="mla_attention",
    )(qn, qr, kn, kr, v)


def _route(logits):
    lane = lax.broadcasted_iota(jnp.int32, logits.shape, 1).astype(F32)
    big = float(2 * ROUTE_LANES)
    neg = -jnp.inf
    gl = jnp.where(lane < N_GROUPS, logits, neg)
    gmax = jnp.max(gl, axis=-1, keepdims=True)
    gsum = jnp.sum(jnp.exp(gl - gmax), axis=-1, keepdims=True)
    g_prob = 1.0 / gsum
    g_idx = jnp.min(jnp.where(gl == gmax, lane, big), axis=-1, keepdims=True)
    lo = N_GROUPS + g_idx * EXPERTS_PER_GROUP
    el = jnp.where((lane >= lo) & (lane < lo + EXPERTS_PER_GROUP), logits, neg)
    emax = jnp.max(el, axis=-1, keepdims=True)
    esum = jnp.sum(jnp.exp(el - emax), axis=-1, keepdims=True)
    i0 = jnp.min(jnp.where(el == emax, lane, big), axis=-1, keepdims=True)
    el2 = jnp.where(lane == i0, neg, el)
    emax2 = jnp.max(el2, axis=-1, keepdims=True)
    i1 = jnp.min(jnp.where(el2 == emax2, lane, big), axis=-1, keepdims=True)
    p0 = 1.0 / esum
    p1 = jnp.exp(emax2 - emax) / esum
    g0 = g_prob * p0 / (p0 + p1)
    g1 = g_prob * p1 / (p0 + p1)
    e0 = i0 - N_GROUPS
    e1 = i1 - N_GROUPS
    return jnp.where(lane == 0, e0, jnp.where(lane == 1, e1, jnp.where(lane == 2, g0, jnp.where(lane == 3, g1, 0.0))))


def _out_ln_route_kernel(a1_ref, a2_ref, w_ref, x_ref, g_ref, b_ref, wrh_ref, br_ref,
                         x1_ref, x1p_ref, route_ref, *, sub):
    half = a1_ref.shape[1]
    for r0 in range(0, x_ref.shape[0], sub):
        rows = slice(r0, r0 + sub)
        m = (jnp.dot(a1_ref[rows, :], w_ref[:half, :], preferred_element_type=F32)
             + jnp.dot(a2_ref[rows, :], w_ref[half:, :], preferred_element_type=F32))
        x1 = _layer_norm_rows(ALPHA * x_ref[rows, :] + m, g_ref[...], b_ref[...])
        x1_ref[rows, :] = x1
        hd = x1.shape[1] // 2
        _store_row_tiles(x1p_ref.at[pl.ds(r0 * SUBLANES, sub * SUBLANES), :], _pack_bf16_pair(x1[:, :hd], x1[:, hd:]))
        xh, xl = _split_bf16(x1)
        both = jnp.dot(xh, wrh_ref[...], preferred_element_type=F32)
        logits = (both[:, :ROUTE_LANES] + both[:, ROUTE_LANES:]
                  + jnp.dot(xl, wrh_ref[:, :ROUTE_LANES], preferred_element_type=F32) + br_ref[...])
        route_ref[rows, :] = _route(logits)


def _out_ln_route(a1, a2, w, x, ln_g, ln_b, w_grp, b_grp, w_exp, b_exp, *, col1=0, col2=0, tm=512, sub=256):
    s, d = x.shape
    half = w.shape[0] // 2
    pad = ROUTE_LANES - N_GROUPS - N_EXPERTS
    wr = jnp.concatenate([w_grp, w_exp, jnp.zeros((d, pad), F32)], axis=1)
    br = jnp.concatenate([b_grp, b_exp, jnp.zeros((pad,), F32)]).reshape(1, ROUTE_LANES)
    wr_hi = wr.astype(BF16)
    wrh = jnp.concatenate([wr_hi, (wr - wr_hi.astype(F32)).astype(BF16)], axis=1)
    full = lambda a: pl.BlockSpec(a.shape, lambda i: (0,) * a.ndim)
    rows = lambda n: pl.BlockSpec((tm, n), lambda i: (i, 0))
    g2 = ln_g.reshape(1, d)
    b2 = ln_b.reshape(1, d)
    return pl.pallas_call(
        functools.partial(_out_ln_route_kernel, sub=sub),
        grid=(s // tm,),
        in_specs=[pl.BlockSpec((tm, half), lambda i: (i, col1)), pl.BlockSpec((tm, half), lambda i: (i, col2)),
                  pl.BlockSpec(w.shape, lambda i: (0, 0), pipeline_mode=pl.Buffered(1)),
                  rows(d), full(g2), full(b2), full(wrh), full(br)],
        out_specs=[rows(d), pl.BlockSpec((tm * SUBLANES, LANES), lambda i: (i, 0)), rows(ROUTE_LANES)],
        out_shape=[jax.ShapeDtypeStruct((s, d), F32), jax.ShapeDtypeStruct((s * SUBLANES, LANES), jnp.uint32),
                   jax.ShapeDtypeStruct((s, ROUTE_LANES), F32)],
        compiler_params=_params(("parallel",), 48),
        name="out_ln_route",
    )(a1, a2, w, x, g2, b2, wrh, br)


def _rank_kernel(route_ref, tri_ref, rank_ref, cnt_ref, carry_ref):
    @pl.when(pl.program_id(0) == 0)
    def _():
        carry_ref[...] = jnp.zeros(carry_ref.shape, F32)

    route = route_ref[...]
    lane = lax.broadcasted_iota(jnp.int32, route.shape, 1).astype(F32)
    oh0 = jnp.where(lane == route[:, 0:1], 1.0, 0.0)
    oh1 = jnp.where(lane == route[:, 1:2], 1.0, 0.0)
    both = oh0 + oh1
    before = jnp.dot(tri_ref[...], both.astype(BF16), preferred_element_type=F32) + carry_ref[...]
    r0 = jnp.sum(before * oh0, axis=-1, keepdims=True)
    r1 = jnp.sum(before * oh1, axis=-1, keepdims=True)
    rank_ref[...] = jnp.where(lane == 0.0, r0, jnp.where(lane == 1.0, r1, 0.0))
    carry_ref[...] = carry_ref[...] + jnp.sum(both, axis=0, keepdims=True)
    cnt_ref[...] = carry_ref[...]


def _dest_kernel(route_ref, rank_ref, pstart_ref, dest_ref):
    route = route_ref[...]
    rank = rank_ref[...]
    lane = lax.broadcasted_iota(jnp.int32, route.shape, 1).astype(F32)
    ps = pstart_ref[...]
    d0 = jnp.sum(jnp.where(lane == route[:, 0:1], ps, 0.0), axis=-1, keepdims=True) + rank[:, 0:1]
    d1 = jnp.sum(jnp.where(lane == route[:, 1:2], ps, 0.0), axis=-1, keepdims=True) + rank[:, 1:2]
    dest = jnp.where(lane == 0.0, d0, jnp.where(lane == 1.0, d1, 0.0)) * float(SUBLANES)
    dest_ref[...] = dest.astype(jnp.int32)


def _moe_plan(route, *, tb=512):
    s = route.shape[0]
    tri = (jnp.arange(tb)[:, None] > jnp.arange(tb)[None, :]).astype(BF16)
    rows = pl.BlockSpec((tb, ROUTE_LANES), lambda i: (i, 0))
    one = lambda: pl.BlockSpec((1, ROUTE_LANES), lambda i: (0, 0))
    rank, cnt = pl.pallas_call(
        _rank_kernel,
        grid=(s // tb,),
        in_specs=[rows, pl.BlockSpec((tb, tb), lambda i: (0, 0))],
        out_specs=[rows, one()],
        out_shape=[jax.ShapeDtypeStruct((s, ROUTE_LANES), F32), jax.ShapeDtypeStruct((1, ROUTE_LANES), F32)],
        scratch_shapes=[pltpu.VMEM((1, ROUTE_LANES), F32)],
        compiler_params=_params(("arbitrary",), 32),
        name="moe_rank",
    )(route, tri)
    counts = cnt[0, :N_EXPERTS].astype(jnp.int32)
    pcounts = (counts + MOE_ROWS - 1) // MOE_ROWS * MOE_ROWS
    pends = jnp.cumsum(pcounts)
    pstarts = pends - pcounts
    n_blocks = s * TOP_K // MOE_ROWS + N_EXPERTS
    n_used = pends[-1] // MOE_ROWS
    blk = jnp.minimum(jnp.arange(n_blocks, dtype=jnp.int32), n_used - 1) * MOE_ROWS
    block_expert = jnp.minimum(jnp.sum(pends[None, :] <= blk[:, None], axis=1), N_EXPERTS - 1).astype(jnp.int32)
    ps = jnp.zeros((1, ROUTE_LANES), F32).at[0, :N_EXPERTS].set(pstarts.astype(F32))
    dest = pl.pallas_call(
        _dest_kernel,
        grid=(s // tb,),
        in_specs=[rows, rows, one()],
        out_specs=rows,
        out_shape=jax.ShapeDtypeStruct((s, ROUTE_LANES), jnp.int32),
        compiler_params=_params(("parallel",), 32),
        name="moe_dest",
    )(route, rank, ps)
    dest = dest[:, :TOP_K].T.reshape(TOP_K * s)
    return dest, block_expert, n_used.astype(jnp.int32).reshape(1)


ROW_UNROLL = 8


def _row_tile(ref, first_row):
    return ref.at[pl.ds(pl.multiple_of(first_row, SUBLANES), SUBLANES), :]


def _dispatch_kernel(dest_ref, x_ref, xs_hbm, sems, *, tm, s):
    t0 = pl.program_id(0) * tm

    def issue(g, carry):
        for u in range(ROW_UNROLL):
            r = g * ROW_UNROLL + u
            src = _row_tile(x_ref, r * SUBLANES)
            for k in range(TOP_K):
                pltpu.make_async_copy(src, _row_tile(xs_hbm, dest_ref[k * s + t0 + r]), sems.at[k]).start(priority=k)
        return carry

    lax.fori_loop(0, tm // ROW_UNROLL, issue, 0)
    for k in range(TOP_K):
        pltpu.make_async_copy(x_ref, xs_hbm.at[pl.ds(0, tm * SUBLANES), :], sems.at[k]).wait()


def _dispatch_rows(x, dest, n_blocks, *, tm=256):
    s = x.shape[0] // SUBLANES
    return pl.pallas_call(
        functools.partial(_dispatch_kernel, tm=tm, s=s),
        grid_spec=pltpu.PrefetchScalarGridSpec(
            num_scalar_prefetch=1,
            grid=(s // tm,),
            in_specs=[pl.BlockSpec((tm * SUBLANES, LANES), lambda i, dest: (i, 0))],
            out_specs=pl.BlockSpec(memory_space=pl.ANY),
            scratch_shapes=[pltpu.SemaphoreType.DMA((TOP_K,))]),
        out_shape=jax.ShapeDtypeStruct((n_blocks * MOE_ROWS * SUBLANES, LANES), x.dtype),
        compiler_params=_params(("arbitrary",), 32),
        name="moe_dispatch",
    )(dest, x)


FFN_K_CHUNK = 512


def _ffn_kernel(nused_ref, run_ref, first_ref, rexp_ref, nruns_ref, xs_ref, w1_hbm, w3_hbm, w2_hbm, ys_ref,
                w1_buf, w3_buf, w2_buf, sems, *, layer):
    b = pl.program_id(0)

    def weight_copies(run, slot):
        e = rexp_ref[run]
        return [pltpu.make_async_copy(w_hbm.at[layer, e], buf.at[slot], sems.at[slot, j])
                for j, (w_hbm, buf) in enumerate(((w1_hbm, w1_buf), (w3_hbm, w3_buf), (w2_hbm, w2_buf)))]

    @pl.when(b == 0)
    def _():
        for c in weight_copies(0, 0):
            c.start()

    @pl.when(b < nused_ref[0])
    def _():
        run = run_ref[b]
        slot = run % 2

        @pl.when(first_ref[b] == 1)
        def _():
            @pl.when(run + 1 < nruns_ref[0])
            def _():
                for c in weight_copies(run + 1, 1 - slot):
                    c.start()

            for c in weight_copies(run, slot):
                c.wait()

        x_lo, x_hi = _unpack_bf16_pair(_load_row_tiles(xs_ref))
        hd = x_lo.shape[1]
        x = jnp.concatenate([x_lo.astype(BF16), x_hi.astype(BF16)], axis=1)

        def up(buf):
            acc = None
            for k0 in range(0, 2 * hd, FFN_K_CHUNK):
                part = jnp.dot(x[:, k0:k0 + FFN_K_CHUNK], buf[slot, k0:k0 + FFN_K_CHUNK, :].astype(BF16),
                               preferred_element_type=F32)
                acc = part if acc is None else acc + part
            return acc

        h1 = up(w1_buf)
        h3 = up(w3_buf)
        hmid = (h1 * jax.nn.sigmoid(h1) * h3).astype(BF16)
        y = jnp.dot(hmid, w2_buf[slot].astype(BF16), preferred_element_type=F32)
        _store_row_tiles(ys_ref, _pack_bf16_pair(y[:, :hd], y[:, hd:]))


def _grouped_ffn(xs, w1, w3, w2, layer, block_expert, n_used):
    d = xs.shape[1]
    dm = w1.shape[2]
    de = w1.shape[3]
    nb = block_expert.shape[0]
    first = jnp.concatenate([jnp.ones((1,), jnp.int32), (block_expert[1:] != block_expert[:-1]).astype(jnp.int32)])
    run = jnp.cumsum(first) - 1
    n_runs = run[-1:] + 1
    hit = (run[None, :] == jnp.arange(nb, dtype=jnp.int32)[:, None]) & (first[None, :] == 1)
    run_expert = jnp.sum(jnp.where(hit, block_expert[None, :], 0), axis=1).astype(jnp.int32)
    blk = lambda b, nu, *_: (jnp.minimum(b, nu[0] - 1), 0)
    return pl.pallas_call(
        functools.partial(_ffn_kernel, layer=layer),
        grid_spec=pltpu.PrefetchScalarGridSpec(
            num_scalar_prefetch=5,
            grid=(nb,),
            in_specs=[pl.BlockSpec((MOE_ROWS * SUBLANES, d), blk),
                      pl.BlockSpec(memory_space=pl.ANY),
                      pl.BlockSpec(memory_space=pl.ANY),
                      pl.BlockSpec(memory_space=pl.ANY)],
            out_specs=pl.BlockSpec((MOE_ROWS * SUBLANES, d), blk),
            scratch_shapes=[pltpu.VMEM((2, dm, de), F32), pltpu.VMEM((2, dm, de), F32),
                            pltpu.VMEM((2, de, dm), F32), pltpu.SemaphoreType.DMA((2, 3))]),
        out_shape=jax.ShapeDtypeStruct(xs.shape, jnp.uint32),
        compiler_params=_params(("arbitrary",), 56),
        name="moe_ffn",
    )(n_used, run.astype(jnp.int32), first, run_expert, n_runs.astype(jnp.int32), xs, w1, w3, w2)


def _combine_ple_kernel(pos_ref, ys_hbm, x1_ref, route_ref, g_ref, b_ref, wg_ref, bg_ref, p_ref, wp_ref,
                        o_ref, ob_ref, ybuf_a, ybuf_b, sems, *, tm, s):
    i = pl.program_id(0)
    n = pl.num_programs(0)

    def start_row(t0, buf, buf_id, r):
        first = r * SUBLANES if isinstance(r, int) else pl.multiple_of(r * SUBLANES, SUBLANES)
        for k in range(TOP_K):
            pltpu.make_async_copy(_row_tile(ys_hbm, pos_ref[k * s + t0 + r]),
                                  buf.at[k, pl.ds(first, SUBLANES), :], sems.at[buf_id, k]).start(priority=k)

    def wait_buffer(buf, buf_id):
        span = ys_hbm.at[pl.ds(0, tm * SUBLANES), :]
        for k in range(TOP_K):
            pltpu.make_async_copy(span, buf.at[k], sems.at[buf_id, k]).wait()

    @pl.when(i == 0)
    def _():
        def issue(g, carry):
            for u in range(ROW_UNROLL):
                start_row(0, ybuf_a, 0, g * ROW_UNROLL + u)
            return carry

        lax.fori_loop(0, tm // ROW_UNROLL, issue, 0)

    def step(cur, cur_id, nxt_buf, nxt_id):
        wait_buffer(cur, cur_id)
        nxt = jnp.minimum(i + 1, n - 1) * tm
        for r in range(tm):
            start_row(nxt, nxt_buf, nxt_id, r)

        route = route_ref[...]
        f = None
        for k in range(TOP_K):
            lo, hi = _unpack_bf16_pair(_load_row_tiles(cur.at[k]))
            fk = route[:, TOP_K + k:TOP_K + k + 1] * jnp.concatenate([lo, hi], axis=1)
            f = fk if f is None else f + fk
        x2 = _layer_norm_rows(ALPHA * x1_ref[...] + f, g_ref[...], b_ref[...])
        gate = jax.nn.sigmoid(jnp.dot(x2.astype(BF16), wg_ref[...], preferred_element_type=F32) + bg_ref[...])
        emb = jnp.dot(p_ref[...].astype(BF16), wp_ref[...], preferred_element_type=F32)
        out = x2 + gate * emb
        o_ref[...] = out
        ob_ref[...] = out.astype(BF16)

        @pl.when(i == n - 1)
        def _():
            wait_buffer(nxt_buf, nxt_id)

    @pl.when(i % 2 == 0)
    def _():
        step(ybuf_a, 0, ybuf_b, 1)

    @pl.when(i % 2 == 1)
    def _():
        step(ybuf_b, 1, ybuf_a, 0)


def _combine_ple(ys, pos, x1, route, ln_g, ln_b, w_gate, b_gate, p, w_proj, *, tm=256):
    s, d = x1.shape
    pd = p.shape[1]
    rows = lambda n: pl.BlockSpec((tm, n), lambda i, pos: (i, 0))
    full = lambda a: pl.BlockSpec(a.shape, lambda i, pos: (0,) * a.ndim)
    g2, b2, bg = ln_g.reshape(1, d), ln_b.reshape(1, d), b_gate.reshape(1, d)
    return pl.pallas_call(
        functools.partial(_combine_ple_kernel, tm=tm, s=s),
        grid_spec=pltpu.PrefetchScalarGridSpec(
            num_scalar_prefetch=1,
            grid=(s // tm,),
            in_specs=[pl.BlockSpec(memory_space=pl.ANY), rows(d), rows(ROUTE_LANES), full(g2), full(b2),
                      full(w_gate), full(bg), rows(pd), full(w_proj)],
            out_specs=[rows(d), rows(d)],
            scratch_shapes=[pltpu.VMEM((TOP_K, tm * SUBLANES, LANES), jnp.uint32),
                            pltpu.VMEM((TOP_K, tm * SUBLANES, LANES), jnp.uint32),
                            pltpu.SemaphoreType.DMA((2, TOP_K))]),
        out_shape=[jax.ShapeDtypeStruct((s, d), F32), jax.ShapeDtypeStruct((s, d), BF16)],
        compiler_params=_params(("arbitrary",), 56),
        name="moe_combine_ple",
    )(pos, ys, x1, route, g2, b2, w_gate, bg, p, w_proj)


def _rglru_kernel(cur_ref, prev_ref, gate_ref, cw_ref, cb_ref, wa_ref, ba_ref, wi_ref, bi_ref, lam_ref,
                  o_ref, buf_ref, a_ref, b_ref, h_ref, carry_ref, *, tt, tc, width):
    i = pl.program_id(1)

    @pl.when(i == 0)
    def _():
        carry_ref[...] = jnp.zeros(carry_ref.shape, F32)

    buf_ref[0:RNN_HALO, :] = jnp.where(i > 0, prev_ref[...], 0.0)
    buf_ref[RNN_HALO:, :] = cur_ref[...]
    off = RNN_HALO - (width - 1)
    z = -lam_ref[...]
    softplus = jnp.log1p(jnp.exp(-jnp.abs(z))) + jnp.maximum(z, 0.0)

    def sigmoid(v):
        return 0.5 * jnp.tanh(0.5 * v) + 0.5

    for hh in range(tc // RNN_HEAD_DIM):
        cols = slice(hh * RNN_HEAD_DIM, (hh + 1) * RNN_HEAD_DIM)
        xh = jnp.broadcast_to(cb_ref[:, cols], (tt, RNN_HEAD_DIM))
        for k in range(width):
            xh = xh + buf_ref[off + k:off + k + tt, cols] * cw_ref[k:k + 1, cols]
        xhb = xh.astype(BF16)
        r = sigmoid(jnp.dot(xhb, wa_ref[hh], preferred_element_type=F32) + ba_ref[:, cols])
        ig = sigmoid(jnp.dot(xhb, wi_ref[hh], preferred_element_type=F32) + bi_ref[:, cols])
        log_a = (-RG_C) * r * softplus[:, cols]
        a = jnp.exp(log_a)
        bval = jnp.sqrt(-jnp.tanh(log_a) * (a * a + 1.0)) * (ig * xh)
        a_ref[:, cols] = a
        b_ref[:, cols] = bval

    def group(gi, h):
        base = pl.multiple_of(gi * 8, 8)
        for r in range(8):
            row = pl.ds(base + r, 1)
            h = a_ref[row, :] * h + b_ref[row, :]
            h_ref[row, :] = h
        return h

    carry_ref[...] = lax.fori_loop(0, tt // 8, group, carry_ref[...])
    o_ref[...] = (gate_ref[...].astype(F32) * h_ref[...]).astype(o_ref.dtype)


def _rglru(xr, gate, conv_w, conv_b, w_a, b_a, w_i, b_i, lam, *, tt=256, tc=2048):
    s, wd = xr.shape
    width = conv_w.shape[0]
    hpc = tc // RNN_HEAD_DIM
    hb = tt // RNN_HALO
    row = lambda v: v.reshape(1, wd)
    cvec = lambda: pl.BlockSpec((1, tc), lambda c, i: (0, c))
    heads = lambda: pl.BlockSpec((hpc, RNN_HEAD_DIM, RNN_HEAD_DIM), lambda c, i: (c, 0, 0))
    tile = lambda: pl.BlockSpec((tt, tc), lambda c, i: (i, c))
    return pl.pallas_call(
        functools.partial(_rglru_kernel, tt=tt, tc=tc, width=width),
        grid=(wd // tc, s // tt),
        in_specs=[tile(),
                  pl.BlockSpec((RNN_HALO, tc), lambda c, i: (jnp.maximum(i * hb - 1, 0), c)),
                  tile(),
                  pl.BlockSpec((width, tc), lambda c, i: (0, c)),
                  cvec(), heads(), cvec(), heads(), cvec(), cvec()],
        out_specs=tile(),
        out_shape=jax.ShapeDtypeStruct((s, wd), BF16),
        scratch_shapes=[pltpu.VMEM((tt + RNN_HALO, tc), F32), pltpu.VMEM((tt, tc), F32),
                        pltpu.VMEM((tt, tc), F32), pltpu.VMEM((tt, tc), F32), pltpu.VMEM((1, tc), F32)],
        compiler_params=_params(("parallel", "arbitrary"), 32),
        name="rglru",
    )(xr, xr, gate, conv_w, row(conv_b), w_a.astype(BF16), row(b_a), w_i.astype(BF16), row(b_i), row(lam))


def _moe_and_ple(mixed, i, p, ln_ffn_g, ln_ffn_b, moe_w1, moe_w3, moe_w2, ple_w_proj, ple_w_gate, ple_b_gate):
    x1, x1_pairs, route = mixed
    dest, block_expert, n_used = _moe_plan(route)
    xs = _dispatch_rows(x1_pairs, dest, block_expert.shape[0])
    ys = _grouped_ffn(xs, moe_w1, moe_w3, moe_w2, i, block_expert, n_used)
    return _combine_ple(ys, dest, x1, route, ln_ffn_g[i], ln_ffn_b[i], ple_w_gate[i].astype(BF16),
                        ple_b_gate[i], p[i, 0], ple_w_proj[i].astype(BF16))


def kernel(x, p, positions, ev_w_in, ev_conv_w, ev_conv_b, ev_cnorm_g, ev_cnorm_b, ev_qnorm_g, ev_w_uq, ev_kvnorm_g, ev_w_ukv, ev_w_out, od_w_in, od_conv_w, od_conv_b, od_w_a, od_b_a, od_w_i, od_b_i, od_lam, od_w_out, ln_mix_g, ln_mix_b, ln_ffn_g, ln_ffn_b, moe_w_grp, moe_b_grp, moe_w_exp, moe_b_exp, moe_w1, moe_w3, moe_w2, ple_w_proj, ple_w_gate, ple_b_gate):
    x0 = x[0]
    x0b = x0.astype(BF16)
    moe = functools.partial(_moe_and_ple, p=p, ln_ffn_g=ln_ffn_g, ln_ffn_b=ln_ffn_b, moe_w1=moe_w1,
                            moe_w3=moe_w3, moe_w2=moe_w2, ple_w_proj=ple_w_proj, ple_w_gate=ple_w_gate,
                            ple_b_gate=ple_b_gate)

    w_in = ev_w_in[0]
    u = _glu_proj(x0b, w_in[:, :2 * CONV_CH].astype(BF16))
    ub = _conv_gln(u, ev_conv_w[0], ev_conv_b[0], ev_cnorm_g[0], ev_cnorm_b[0], groups=CONV_GROUPS)
    cosm, sinm = _rope_tables(positions[0])
    qn, qr, kn, kr, v = _mla_proj(x0b, w_in[:, 2 * CONV_CH:], ev_qnorm_g[0], ev_w_uq[0], ev_kvnorm_g[0],
                                  ev_w_ukv[0], cosm, sinm)
    att = _attention(qn, qr, kn, kr, v)
    mixed0 = _out_ln_route(ub, att, ev_w_out[0].astype(BF16), x0, ln_mix_g[0], ln_mix_b[0],
                           moe_w_grp[0], moe_b_grp[0], moe_w_exp[0], moe_b_exp[0])
    x3, x3b = moe(mixed0, 0)

    w_in1 = od_w_in[0]
    gate, xr = _rnn_in_proj(x3b, w_in1.astype(BF16))
    y = _rglru(xr, gate, od_conv_w[0], od_conv_b[0], od_w_a[0], od_b_a[0], od_w_i[0], od_b_i[0], od_lam[0])
    mixed1 = _out_ln_route(y, y, od_w_out[0].astype(BF16), x3, ln_mix_g[1], ln_mix_b[1],
                           moe_w_grp[1], moe_b_grp[1], moe_w_exp[1], moe_b_exp[1], col1=0, col2=1)
    x6, _ = moe(mixed1, 1)
    return x6[None]
```

```python
import functools
import math

import jax
import jax.numpy as jnp
from jax import lax
from jax.experimental import pallas as pl
from jax.experimental.pallas import tpu as pltpu

F32 = jnp.float32
BF16 = jnp.bfloat16

D_MODEL = 2048
SEQ = 8192
DEPTH = 2
CONV_CH = 1024
CONV_GROUPS = 16
CONV_WIDTH = 31
MLA_HEADS = 8
QK_NOPE = 128
QK_ROPE = 64
V_HEAD = 128
Q_RANK = 512
KV_RANK = 256
ROPE_THETA = 10000.0
RNN_WIDTH = 2048
RNN_HEADS = 16
RNN_HEAD_DIM = RNN_WIDTH // RNN_HEADS
RNN_CONV_WIDTH = 4
RG_C = 8.0
N_GROUPS = 8
EXPERTS_PER_GROUP = 8
N_EXPERTS = N_GROUPS * EXPERTS_PER_GROUP
TOP_K = 2
D_EXPERT = 512
PLE_DIM = 256
ALPHA = (2 * DEPTH) ** 0.25
LN_EPS = 1e-5
RMS_EPS = 1e-6

LANES = 128
SUBLANES = 8
CONV_HALO = 32
RNN_HALO = 8
MOE_ROWS = 256
ROUTE_LANES = 128
MIB = 2 ** 20


def _params(semantics, vmem_mib):
    return pltpu.CompilerParams(dimension_semantics=semantics, vmem_limit_bytes=vmem_mib * MIB)


def _layer_norm_rows(z, g, b):
    mu = jnp.mean(z, axis=-1, keepdims=True)
    d = z - mu
    var = jnp.mean(d * d, axis=-1, keepdims=True)
    return d * lax.rsqrt(var + LN_EPS) * g + b


def _pack_bf16_pair(a, b):
    def rounded_bits(v):
        u = lax.bitcast_convert_type(v, jnp.uint32)
        return u + jnp.uint32(0x7FFF) + ((u >> 16) & jnp.uint32(1))

    return (rounded_bits(a) & jnp.uint32(0xFFFF0000)) | (rounded_bits(b) >> 16)


def _unpack_bf16_pair(u):
    a = lax.bitcast_convert_type(u & jnp.uint32(0xFFFF0000), F32)
    b = lax.bitcast_convert_type(u << 16, F32)
    return a, b


def _store_row_tiles(ref, packed):
    n, w = packed.shape
    assert w == SUBLANES * LANES
    for j in range(SUBLANES):
        ref[pl.ds(j, n, stride=SUBLANES), :] = packed[:, j * LANES:(j + 1) * LANES]


def _load_row_tiles(ref):
    n = ref.shape[0] // SUBLANES
    return jnp.concatenate([ref[pl.ds(j, n, stride=SUBLANES), :] for j in range(SUBLANES)], axis=1)


def _split_bf16(v):
    hi = v.astype(BF16)
    lo = (v - hi.astype(F32)).astype(BF16)
    return hi, lo


def _gelu_tanh(x):
    c = math.sqrt(2.0 / math.pi)
    return 0.5 * x * (1.0 + jnp.tanh(c * (x + 0.044715 * (x * x * x))))


def _pair_proj_kernel(x_ref, wa_ref, wb_ref, *refs, mode):
    out_refs, (wa_bf, wb_bf) = refs[:-2], refs[-2:]

    @pl.when(pl.program_id(1) == 0)
    def _():
        wa_bf[...] = wa_ref[...].astype(BF16)
        wb_bf[...] = wb_ref[...].astype(BF16)

    x = x_ref[...]
    a = jnp.dot(x, wa_bf[...], preferred_element_type=F32)
    b = jnp.dot(x, wb_bf[...], preferred_element_type=F32)
    if mode == "glu":
        out_refs[0][...] = a * jax.nn.sigmoid(b)
    else:
        out_refs[0][...] = _gelu_tanh(a).astype(out_refs[0].dtype)
        out_refs[1][...] = b


def _pair_proj(x, w, layer, n, *, mode, tm=512, tn=512):
    m, k = x.shape
    nb = n // tn
    tile = pl.BlockSpec((tm, tn), lambda j, i: (i, j))
    if mode == "glu":
        out_specs, out_shape = tile, jax.ShapeDtypeStruct((m, n), F32)
    else:
        out_specs = [tile, tile]
        out_shape = [jax.ShapeDtypeStruct((m, n), BF16), jax.ShapeDtypeStruct((m, n), F32)]
    return pl.pallas_call(
        functools.partial(_pair_proj_kernel, mode=mode),
        grid=(nb, m // tm),
        in_specs=[pl.BlockSpec((tm, k), lambda j, i: (i, 0)),
                  pl.BlockSpec((None, k, tn), lambda j, i: (layer, 0, j)),
                  pl.BlockSpec((None, k, tn), lambda j, i: (layer, 0, j + nb))],
        out_specs=out_specs,
        out_shape=out_shape,
        scratch_shapes=[pltpu.VMEM((k, tn), BF16), pltpu.VMEM((k, tn), BF16)],
        compiler_params=_params(("parallel", "arbitrary"), 48),
        name=mode + "_proj",
    )(x, w, w)


def _conv_gln_kernel(cur_ref, prev_ref, w_ref, b_ref, g_ref, beta_ref, gm_ref, o_ref,
                     buf_ref, sh_ref, y_ref, *, tt, ch, width, rc, cc):
    i = pl.program_id(0)
    buf_ref[0:CONV_HALO, :] = jnp.where(i > 0, prev_ref[...], 0.0)
    buf_ref[CONV_HALO:, :] = cur_ref[...]
    off = CONV_HALO - (width - 1)
    sh_rows = sh_ref.shape[1]
    step = 7 * SUBLANES
    assert sh_rows % step == 0
    for b in range(1, SUBLANES):
        for c0 in range(0, ch, cc):
            for r0 in range(0, sh_rows, step):
                sh_ref[b - 1, r0:r0 + step, c0:c0 + cc] = buf_ref[r0 + b:r0 + b + step, c0:c0 + cc]
    for c0 in range(0, ch, cc):
        for r0 in range(0, tt, rc):
            acc = jnp.broadcast_to(b_ref[:, c0:c0 + cc], (rc, cc))
            for k in range(width):
                q, b = divmod(off + k, SUBLANES)
                lo = r0 + q * SUBLANES
                src = buf_ref[lo:lo + rc, c0:c0 + cc] if b == 0 else sh_ref[b - 1, lo:lo + rc, c0:c0 + cc]
                acc = acc + src * w_ref[k:k + 1, c0:c0 + cc]
            y_ref[r0:r0 + rc, c0:c0 + cc] = acc
    gm = gm_ref[...]

    def seg_mean(v):
        hi, lo = _split_bf16(v)
        return (jnp.dot(hi, gm, preferred_element_type=F32)
                + jnp.dot(lo, gm, preferred_element_type=F32))

    for c0 in range(0, ch, LANES):
        y = y_ref[:, c0:c0 + LANES]
        d = y - seg_mean(y)
        var = seg_mean(d * d)
        z = d * lax.rsqrt(var + LN_EPS) * g_ref[:, c0:c0 + LANES] + beta_ref[:, c0:c0 + LANES]
        o_ref[:, c0:c0 + LANES] = (z * jax.nn.sigmoid(z)).astype(o_ref.dtype)


def _conv_gln(u, w, b, g, beta, *, groups, tt=256):
    s, ch = u.shape
    width = w.shape[0]
    gsz = ch // groups
    assert LANES % gsz == 0 and width - 1 <= CONV_HALO and tt % CONV_HALO == 0
    wpad = jnp.zeros((CONV_HALO, ch), F32).at[:width].set(w)
    lane = jnp.arange(LANES)
    gm = jnp.where((lane[:, None] // gsz) == (lane[None, :] // gsz), 1.0 / gsz, 0.0).astype(BF16)
    hb = tt // CONV_HALO
    row = lambda v: v.reshape(1, ch)
    return pl.pallas_call(
        functools.partial(_conv_gln_kernel, tt=tt, ch=ch, width=width, rc=32, cc=256),
        grid=(s // tt,),
        in_specs=[pl.BlockSpec((tt, ch), lambda i: (i, 0)),
                  pl.BlockSpec((CONV_HALO, ch), lambda i: (jnp.maximum(i * hb - 1, 0), 0)),
                  pl.BlockSpec((CONV_HALO, ch), lambda i: (0, 0)),
                  pl.BlockSpec((1, ch), lambda i: (0, 0)),
                  pl.BlockSpec((1, ch), lambda i: (0, 0)),
                  pl.BlockSpec((1, ch), lambda i: (0, 0)),
                  pl.BlockSpec((LANES, LANES), lambda i: (0, 0))],
        out_specs=pl.BlockSpec((tt, ch), lambda i: (i, 0)),
        out_shape=jax.ShapeDtypeStruct((s, ch), BF16),
        scratch_shapes=[pltpu.VMEM((tt + CONV_HALO, ch), F32),
                        pltpu.VMEM((SUBLANES - 1, tt + CONV_HALO - SUBLANES, ch), F32),
                        pltpu.VMEM((tt, ch), F32)],
        compiler_params=_params(("parallel",), 40),
        name="conv_gln",
    )(u, u, wpad, row(b), row(g), row(beta), gm)


def _rope_table_kernel(pos_ref, invf_ref, cos_ref, sin_ref):
    ang = pos_ref[...].astype(F32) * invf_ref[...]
    cos_ref[...] = jnp.cos(ang)
    sin_ref[...] = jnp.sin(ang)


def _rope_tables(positions):
    s = positions.shape[0]
    half = QK_ROPE // 2
    per_row = LANES // half
    inv_freq = 1.0 / (ROPE_THETA ** (jnp.arange(0, QK_ROPE, 2, dtype=F32) / QK_ROPE))
    pos_rep = jnp.repeat(positions.reshape(s // per_row, per_row), half, axis=1)
    invf = jnp.tile(inv_freq, per_row).reshape(1, LANES)
    cos, sin = pl.pallas_call(
        _rope_table_kernel,
        out_shape=[jax.ShapeDtypeStruct((s // per_row, LANES), F32)] * 2,
        name="rope_table",
    )(pos_rep, invf)
    cos = cos.reshape(s, half)
    sin = sin.reshape(s, half)
    zero = jnp.zeros((s, LANES - QK_ROPE), F32)
    return jnp.concatenate([cos, cos, zero], axis=1), jnp.concatenate([-sin, sin, zero], axis=1)


def _mla_proj_kernel(x_ref, wm_ref, qg_ref, kvg_ref, wq_ref, wkv_ref, c_ref, s_ref,
                     xb_ref, qn_ref, qr_ref, kn_ref, kr_ref, v_ref, *, scale):
    nq = MLA_HEADS * QK_NOPE
    xb = x_ref[...].astype(BF16)
    xb_ref[...] = xb
    c = jnp.dot(xb, wm_ref[...], preferred_element_type=F32)
    cq = c[:, :Q_RANK]
    ckv = c[:, Q_RANK:Q_RANK + KV_RANK]
    kr_pad = c[:, Q_RANK + KV_RANK:Q_RANK + KV_RANK + LANES]
    kr_rot = c[:, Q_RANK + KV_RANK + LANES:]
    cqn = cq * lax.rsqrt(jnp.mean(cq * cq, axis=-1, keepdims=True) + RMS_EPS) * qg_ref[...]
    ckvn = ckv * lax.rsqrt(jnp.mean(ckv * ckv, axis=-1, keepdims=True) + RMS_EPS) * kvg_ref[...]
    q = jnp.dot(cqn.astype(BF16), wq_ref[...], preferred_element_type=F32)
    kv = jnp.dot(ckvn.astype(BF16), wkv_ref[...], preferred_element_type=F32)
    cosm = c_ref[...]
    sinm = s_ref[...]
    qn_ref[...] = (q[:, :nq] * scale).astype(BF16)
    for h in range(MLA_HEADS):
        lo = nq + h * LANES
        rot = q[:, lo:lo + LANES] * cosm + q[:, lo + nq:lo + nq + LANES] * sinm
        qr_ref[:, h * LANES:(h + 1) * LANES] = (rot * scale).astype(BF16)
    kn_ref[...] = kv[:, :nq].astype(BF16)
    v_ref[...] = kv[:, nq:].astype(BF16)
    kr_ref[...] = (kr_pad * cosm + kr_rot * sinm).astype(BF16)


def _rope_swap(w):
    half = QK_ROPE // 2
    return jnp.concatenate([w[..., half:], w[..., :half]], axis=-1)


def _mla_proj(x, w_in_mla, qnorm_g, w_uq, kvnorm_g, w_ukv, cosm, sinm, *, tm=256):
    s, d = x.shape
    nq = MLA_HEADS * QK_NOPE
    pad = lambda w: jnp.concatenate([w, jnp.zeros(w.shape[:-1] + (LANES - QK_ROPE,), w.dtype)], axis=-1)
    w_kr = w_in_mla[:, Q_RANK + KV_RANK:]
    wm = jnp.concatenate([w_in_mla[:, :Q_RANK + KV_RANK], pad(w_kr), pad(_rope_swap(w_kr))], axis=1).astype(BF16)
    wq3 = w_uq.reshape(Q_RANK, MLA_HEADS, QK_NOPE + QK_ROPE)
    wq_rope = wq3[:, :, QK_NOPE:]
    wq = jnp.concatenate([wq3[:, :, :QK_NOPE].reshape(Q_RANK, nq),
                          pad(wq_rope).reshape(Q_RANK, MLA_HEADS * LANES),
                          pad(_rope_swap(wq_rope)).reshape(Q_RANK, MLA_HEADS * LANES)], axis=1).astype(BF16)
    wkv3 = w_ukv.reshape(KV_RANK, MLA_HEADS, QK_NOPE + V_HEAD)
    wkv = jnp.concatenate([wkv3[:, :, :QK_NOPE].reshape(KV_RANK, nq),
                           wkv3[:, :, QK_NOPE:].reshape(KV_RANK, MLA_HEADS * V_HEAD)], axis=1).astype(BF16)
    scale = (QK_NOPE + QK_ROPE) ** -0.5 * math.log2(math.e)
    full = lambda a: pl.BlockSpec(a.shape, lambda i: (0,) * a.ndim)
    rows = lambda n: pl.BlockSpec((tm, n), lambda i: (i, 0))
    qg = qnorm_g.reshape(1, Q_RANK)
    kvg = kvnorm_g.reshape(1, KV_RANK)
    return pl.pallas_call(
        functools.partial(_mla_proj_kernel, scale=scale),
        grid=(s // tm,),
        in_specs=[rows(d), full(wm), full(qg), full(kvg), full(wq), full(wkv), rows(LANES), rows(LANES)],
        out_specs=[rows(d), rows(nq), rows(MLA_HEADS * LANES), rows(nq), rows(LANES), rows(MLA_HEADS * V_HEAD)],
        out_shape=[jax.ShapeDtypeStruct((s, d), BF16),
                   jax.ShapeDtypeStruct((s, nq), BF16),
                   jax.ShapeDtypeStruct((s, MLA_HEADS * LANES), BF16),
                   jax.ShapeDtypeStruct((s, nq), BF16),
                   jax.ShapeDtypeStruct((s, LANES), BF16),
                   jax.ShapeDtypeStruct((s, MLA_HEADS * V_HEAD), BF16)],
        compiler_params=_params(("parallel",), 48),
        name="mla_proj",
    )(x, wm, qg, kvg, wq, wkv, cosm, sinm)


def _attn_kernel(qn_ref, qr_ref, kn_ref, kr_ref, v_ref, o_ref, m_ref, acc_ref, *, tq, tk, hb):
    qi = pl.program_id(1)
    per_q = tq // tk
    m_ref[...] = jnp.full(m_ref.shape, -jnp.inf, F32)
    acc_ref[...] = jnp.zeros(acc_ref.shape, F32)
    ones = jnp.ones((tk, LANES), BF16)
    head_cols = [slice(h * LANES, (h + 1) * LANES) for h in range(hb)]
    qs = [jnp.concatenate([qn_ref[:, c], qr_ref[:, c]], axis=1) for c in head_cols]

    def chunk(j, diag):
        rows = pl.ds(pl.multiple_of(j * tk, tk), tk)
        k_rope = kr_ref[rows, :]
        q0 = 0 if diag is None else diag * tk
        for h, c in enumerate(head_cols):
            k = jnp.concatenate([kn_ref[rows, c], k_rope], axis=1)
            s = lax.dot_general(qs[h][q0:], k, (((1,), (1,)), ((), ())), preferred_element_type=F32)
            if diag is not None:
                keep = lax.broadcasted_iota(jnp.int32, s.shape, 0) >= lax.broadcasted_iota(jnp.int32, s.shape, 1)
                s = jnp.where(keep, s, -jnp.inf)
            m_old = m_ref[h, q0:, :]
            m_new = jnp.maximum(m_old, jnp.max(s, axis=-1, keepdims=True))
            alpha = jnp.exp2(m_old - m_new)
            p = jnp.exp2(s - jnp.concatenate([m_new] * (tk // LANES), axis=1))
            v_ext = jnp.concatenate([v_ref[rows, c], ones], axis=1)
            acc_ref[h, q0:, :] = (jnp.concatenate([alpha, alpha], axis=1) * acc_ref[h, q0:, :]
                                  + jnp.dot(p.astype(BF16), v_ext, preferred_element_type=F32))
            m_ref[h, q0:, :] = m_new

    def body(j, carry):
        chunk(j, None)
        return carry

    lax.fori_loop(0, qi * per_q, body, 0)
    for d in range(per_q):
        chunk(qi * per_q + d, d)
    for h, c in enumerate(head_cols):
        acc = acc_ref[h]
        o_ref[:, c] = (acc[:, :V_HEAD] / acc[:, V_HEAD:]).astype(o_ref.dtype)


def _attention(qn, qr, kn, kr, v, *, tq=1024, tk=512, hb=4):
    s = qn.shape[0]
    wide = hb * LANES
    tile = lambda: pl.BlockSpec((tq, wide), lambda h, i: (i, h))
    keys = lambda: pl.BlockSpec((s, wide), lambda h, i: (0, h), pipeline_mode=pl.Buffered(1))
    return pl.pallas_call(
        functools.partial(_attn_kernel, tq=tq, tk=tk, hb=hb),
        grid=(MLA_HEADS // hb, s // tq),
        in_specs=[tile(), tile(), keys(),
                  pl.BlockSpec((s, LANES), lambda h, i: (0, 0), pipeline_mode=pl.Buffered(1)), keys()],
        out_specs=tile(),
        out_shape=jax.ShapeDtypeStruct((s, MLA_HEADS * V_HEAD), BF16),
        scratch_shapes=[pltpu.VMEM((hb, tq, LANES), F32), pltpu.VMEM((hb, tq, 2 * V_HEAD), F32)],
        compiler_params=_params(("parallel", "parallel"), 48),
        name="mla_attention",
    )(qn, qr, kn, kr, v)


def _route(logits):
    lane = lax.broadcasted_iota(jnp.int32, logits.shape, 1).astype(F32)
    big = float(2 * ROUTE_LANES)
    neg = -jnp.inf
    gl = jnp.where(lane < N_GROUPS, logits, neg)
    gmax = jnp.max(gl, axis=-1, keepdims=True)
    gsum = jnp.sum(jnp.exp(gl - gmax), axis=-1, keepdims=True)
    g_prob = 1.0 / gsum
    g_idx = jnp.min(jnp.where(gl == gmax, lane, big), axis=-1, keepdims=True)
    lo = N_GROUPS + g_idx * EXPERTS_PER_GROUP
    el = jnp.where((lane >= lo) & (lane < lo + EXPERTS_PER_GROUP), logits, neg)
    emax = jnp.max(el, axis=-1, keepdims=True)
    esum = jnp.sum(jnp.exp(el - emax), axis=-1, keepdims=True)
    i0 = jnp.min(jnp.where(el == emax, lane, big), axis=-1, keepdims=True)
    el2 = jnp.where(lane == i0, neg, el)
    emax2 = jnp.max(el2, axis=-1, keepdims=True)
    i1 = jnp.min(jnp.where(el2 == emax2, lane, big), axis=-1, keepdims=True)
    p0 = 1.0 / esum
    p1 = jnp.exp(emax2 - emax) / esum
    g0 = g_prob * p0 / (p0 + p1)
    g1 = g_prob * p1 / (p0 + p1)
    e0 = i0 - N_GROUPS
    e1 = i1 - N_GROUPS
    return jnp.where(lane == 0, e0, jnp.where(lane == 1, e1, jnp.where(lane == 2, g0, jnp.where(lane == 3, g1, 0.0))))


def _out_ln_route_kernel(a1_ref, a2_ref, w_ref, x_ref, g_ref, b_ref, wrh_ref, br_ref,
                         x1_ref, x1p_ref, route_ref, *, sub):
    half = a1_ref.shape[1]
    for r0 in range(0, x_ref.shape[0], sub):
        rows = slice(r0, r0 + sub)
        m = (jnp.dot(a1_ref[rows, :], w_ref[:half, :], preferred_element_type=F32)
             + jnp.dot(a2_ref[rows, :], w_ref[half:, :], preferred_element_type=F32))
        x1 = _layer_norm_rows(ALPHA * x_ref[rows, :] + m, g_ref[...], b_ref[...])
        x1_ref[rows, :] = x1
        hd = x1.shape[1] // 2
        _store_row_tiles(x1p_ref.at[pl.ds(r0 * SUBLANES, sub * SUBLANES), :], _pack_bf16_pair(x1[:, :hd], x1[:, hd:]))
        xh, xl = _split_bf16(x1)
        both = jnp.dot(xh, wrh_ref[...], preferred_element_type=F32)
        logits = (both[:, :ROUTE_LANES] + both[:, ROUTE_LANES:]
                  + jnp.dot(xl, wrh_ref[:, :ROUTE_LANES], preferred_element_type=F32) + br_ref[...])
        route_ref[rows, :] = _route(logits)


def _out_ln_route(a1, a2, w, x, ln_g, ln_b, w_grp, b_grp, w_exp, b_exp, *, col1=0, col2=0, tm=512, sub=256):
    s, d = x.shape
    half = w.shape[0] // 2
    pad = ROUTE_LANES - N_GROUPS - N_EXPERTS
    wr = jnp.concatenate([w_grp, w_exp, jnp.zeros((d, pad), F32)], axis=1)
    br = jnp.concatenate([b_grp, b_exp, jnp.zeros((pad,), F32)]).reshape(1, ROUTE_LANES)
    wr_hi = wr.astype(BF16)
    wrh = jnp.concatenate([wr_hi, (wr - wr_hi.astype(F32)).astype(BF16)], axis=1)
    full = lambda a: pl.BlockSpec(a.shape, lambda i: (0,) * a.ndim)
    rows = lambda n: pl.BlockSpec((tm, n), lambda i: (i, 0))
    g2 = ln_g.reshape(1, d)
    b2 = ln_b.reshape(1, d)
    return pl.pallas_call(
        functools.partial(_out_ln_route_kernel, sub=sub),
        grid=(s // tm,),
        in_specs=[pl.BlockSpec((tm, half), lambda i: (i, col1)), pl.BlockSpec((tm, half), lambda i: (i, col2)),
                  pl.BlockSpec(w.shape, lambda i: (0, 0), pipeline_mode=pl.Buffered(1)),
                  rows(d), full(g2), full(b2), full(wrh), full(br)],
        out_specs=[rows(d), pl.BlockSpec((tm * SUBLANES, LANES), lambda i: (i, 0)), rows(ROUTE_LANES)],
        out_shape=[jax.ShapeDtypeStruct((s, d), F32), jax.ShapeDtypeStruct((s * SUBLANES, LANES), jnp.uint32),
                   jax.ShapeDtypeStruct((s, ROUTE_LANES), F32)],
        compiler_params=_params(("parallel",), 48),
        name="out_ln_route",
    )(a1, a2, w, x, g2, b2, wrh, br)


def _rank_kernel(route_ref, tri_ref, rank_ref, cnt_ref, carry_ref):
    @pl.when(pl.program_id(0) == 0)
    def _():
        carry_ref[...] = jnp.zeros(carry_ref.shape, F32)

    route = route_ref[...]
    lane = lax.broadcasted_iota(jnp.int32, route.shape, 1).astype(F32)
    oh0 = jnp.where(lane == route[:, 0:1], 1.0, 0.0)
    oh1 = jnp.where(lane == route[:, 1:2], 1.0, 0.0)
    both = oh0 + oh1
    before = jnp.dot(tri_ref[...], both.astype(BF16), preferred_element_type=F32) + carry_ref[...]
    r0 = jnp.sum(before * oh0, axis=-1, keepdims=True)
    r1 = jnp.sum(before * oh1, axis=-1, keepdims=True)
    rank_ref[...] = jnp.where(lane == 0.0, r0, jnp.where(lane == 1.0, r1, 0.0))
    carry_ref[...] = carry_ref[...] + jnp.sum(both, axis=0, keepdims=True)
    cnt_ref[...] = carry_ref[...]


def _dest_kernel(route_ref, rank_ref, pstart_ref, dest_ref):
    route = route_ref[...]
    rank = rank_ref[...]
    lane = lax.broadcasted_iota(jnp.int32, route.shape, 1).astype(F32)
    ps = pstart_ref[...]
    d0 = jnp.sum(jnp.where(lane == route[:, 0:1], ps, 0.0), axis=-1, keepdims=True) + rank[:, 0:1]
    d1 = jnp.sum(jnp.where(lane == route[:, 1:2], ps, 0.0), axis=-1, keepdims=True) + rank[:, 1:2]
    dest = jnp.where(lane == 0.0, d0, jnp.where(lane == 1.0, d1, 0.0)) * float(SUBLANES)
    dest_ref[...] = dest.astype(jnp.int32)


def _moe_plan(route, *, tb=512):
    s = route.shape[0]
    tri = (jnp.arange(tb)[:, None] > jnp.arange(tb)[None, :]).astype(BF16)
    rows = pl.BlockSpec((tb, ROUTE_LANES), lambda i: (i, 0))
    one = lambda: pl.BlockSpec((1, ROUTE_LANES), lambda i: (0, 0))
    rank, cnt = pl.pallas_call(
        _rank_kernel,
        grid=(s // tb,),
        in_specs=[rows, pl.BlockSpec((tb, tb), lambda i: (0, 0))],
        out_specs=[rows, one()],
        out_shape=[jax.ShapeDtypeStruct((s, ROUTE_LANES), F32), jax.ShapeDtypeStruct((1, ROUTE_LANES), F32)],
        scratch_shapes=[pltpu.VMEM((1, ROUTE_LANES), F32)],
        compiler_params=_params(("arbitrary",), 32),
        name="moe_rank",
    )(route, tri)
    counts = cnt[0, :N_EXPERTS].astype(jnp.int32)
    pcounts = (counts + MOE_ROWS - 1) // MOE_ROWS * MOE_ROWS
    pends = jnp.cumsum(pcounts)
    pstarts = pends - pcounts
    n_blocks = s * TOP_K // MOE_ROWS + N_EXPERTS
    n_used = pends[-1] // MOE_ROWS
    blk = jnp.minimum(jnp.arange(n_blocks, dtype=jnp.int32), n_used - 1) * MOE_ROWS
    block_expert = jnp.minimum(jnp.sum(pends[None, :] <= blk[:, None], axis=1), N_EXPERTS - 1).astype(jnp.int32)
    ps = jnp.zeros((1, ROUTE_LANES), F32).at[0, :N_EXPERTS].set(pstarts.astype(F32))
    dest = pl.pallas_call(
        _dest_kernel,
        grid=(s // tb,),
        in_specs=[rows, rows, one()],
        out_specs=rows,
        out_shape=jax.ShapeDtypeStruct((s, ROUTE_LANES), jnp.int32),
        compiler_params=_params(("parallel",), 32),
        name="moe_dest",
    )(route, rank, ps)
    dest = dest[:, :TOP_K].T.reshape(TOP_K * s)
    return dest, block_expert, n_used.astype(jnp.int32).reshape(1)


ROW_UNROLL = 8


def _row_tile(ref, first_row):
    return ref.at[pl.ds(pl.multiple_of(first_row, SUBLANES), SUBLANES), :]


def _dispatch_kernel(dest_ref, x_ref, xs_hbm, sems, *, tm, s):
    t0 = pl.program_id(0) * tm

    def issue(g, carry):
        for u in range(ROW_UNROLL):
            r = g * ROW_UNROLL + u
            src = _row_tile(x_ref, r * SUBLANES)
            for k in range(TOP_K):
                pltpu.make_async_copy(src, _row_tile(xs_hbm, dest_ref[k * s + t0 + r]), sems.at[k]).start(priority=k)
        return carry

    lax.fori_loop(0, tm // ROW_UNROLL, issue, 0)
    for k in range(TOP_K):
        pltpu.make_async_copy(x_ref, xs_hbm.at[pl.ds(0, tm * SUBLANES), :], sems.at[k]).wait()


def _dispatch_rows(x, dest, n_blocks, *, tm=256):
    s = x.shape[0] // SUBLANES
    return pl.pallas_call(
        functools.partial(_dispatch_kernel, tm=tm, s=s),
        grid_spec=pltpu.PrefetchScalarGridSpec(
            num_scalar_prefetch=1,
            grid=(s // tm,),
            in_specs=[pl.BlockSpec((tm * SUBLANES, LANES), lambda i, dest: (i, 0))],
            out_specs=pl.BlockSpec(memory_space=pl.ANY),
            scratch_shapes=[pltpu.SemaphoreType.DMA((TOP_K,))]),
        out_shape=jax.ShapeDtypeStruct((n_blocks * MOE_ROWS * SUBLANES, LANES), x.dtype),
        compiler_params=_params(("arbitrary",), 32),
        name="moe_dispatch",
    )(dest, x)


FFN_K_CHUNK = 512


def _ffn_kernel(nused_ref, run_ref, first_ref, rexp_ref, nruns_ref, xs_ref, w1_hbm, w3_hbm, w2_hbm, ys_ref,
                w1_buf, w3_buf, w2_buf, sems, *, layer):
    b = pl.program_id(0)

    def weight_copies(run, slot):
        e = rexp_ref[run]
        return [pltpu.make_async_copy(w_hbm.at[layer, e], buf.at[slot], sems.at[slot, j])
                for j, (w_hbm, buf) in enumerate(((w1_hbm, w1_buf), (w3_hbm, w3_buf), (w2_hbm, w2_buf)))]

    @pl.when(b == 0)
    def _():
        for c in weight_copies(0, 0):
            c.start()

    @pl.when(b < nused_ref[0])
    def _():
        run = run_ref[b]
        slot = run % 2

        @pl.when(first_ref[b] == 1)
        def _():
            @pl.when(run + 1 < nruns_ref[0])
            def _():
                for c in weight_copies(run + 1, 1 - slot):
                    c.start()

            for c in weight_copies(run, slot):
                c.wait()

        x_lo, x_hi = _unpack_bf16_pair(_load_row_tiles(xs_ref))
        hd = x_lo.shape[1]
        x = jnp.concatenate([x_lo.astype(BF16), x_hi.astype(BF16)], axis=1)

        def up(buf):
            acc = None
            for k0 in range(0, 2 * hd, FFN_K_CHUNK):
                part = jnp.dot(x[:, k0:k0 + FFN_K_CHUNK], buf[slot, k0:k0 + FFN_K_CHUNK, :].astype(BF16),
                               preferred_element_type=F32)
                acc = part if acc is None else acc + part
            return acc

        h1 = up(w1_buf)
        h3 = up(w3_buf)
        hmid = (h1 * jax.nn.sigmoid(h1) * h3).astype(BF16)
        y = jnp.dot(hmid, w2_buf[slot].astype(BF16), preferred_element_type=F32)
        _store_row_tiles(ys_ref, _pack_bf16_pair(y[:, :hd], y[:, hd:]))


def _grouped_ffn(xs, w1, w3, w2, layer, block_expert, n_used):
    d = xs.shape[1]
    dm = w1.shape[2]
    de = w1.shape[3]
    nb = block_expert.shape[0]
    first = jnp.concatenate([jnp.ones((1,), jnp.int32), (block_expert[1:] != block_expert[:-1]).astype(jnp.int32)])
    run = jnp.cumsum(first) - 1
    n_runs = run[-1:] + 1
    hit = (run[None, :] == jnp.arange(nb, dtype=jnp.int32)[:, None]) & (first[None, :] == 1)
    run_expert = jnp.sum(jnp.where(hit, block_expert[None, :], 0), axis=1).astype(jnp.int32)
    blk = lambda b, nu, *_: (jnp.minimum(b, nu[0] - 1), 0)
    return pl.pallas_call(
        functools.partial(_ffn_kernel, layer=layer),
        grid_spec=pltpu.PrefetchScalarGridSpec(
            num_scalar_prefetch=5,
            grid=(nb,),
            in_specs=[pl.BlockSpec((MOE_ROWS * SUBLANES, d), blk),
                      pl.BlockSpec(memory_space=pl.ANY),
                      pl.BlockSpec(memory_space=pl.ANY),
                      pl.BlockSpec(memory_space=pl.ANY)],
            out_specs=pl.BlockSpec((MOE_ROWS * SUBLANES, d), blk),
            scratch_shapes=[pltpu.VMEM((2, dm, de), F32), pltpu.VMEM((2, dm, de), F32),
                            pltpu.VMEM((2, de, dm), F32), pltpu.SemaphoreType.DMA((2, 3))]),
        out_shape=jax.ShapeDtypeStruct(xs.shape, jnp.uint32),
        compiler_params=_params(("arbitrary",), 56),
        name="moe_ffn",
    )(n_used, run.astype(jnp.int32), first, run_expert, n_runs.astype(jnp.int32), xs, w1, w3, w2)


def _combine_ple_kernel(pos_ref, ys_hbm, x1_ref, route_ref, g_ref, b_ref, wg_ref, bg_ref, p_ref, wp_ref,
                        o_ref, ob_ref, ybuf_a, ybuf_b, sems, *, tm, s):
    i = pl.program_id(0)
    n = pl.num_programs(0)

    def start_row(t0, buf, buf_id, r):
        first = r * SUBLANES if isinstance(r, int) else pl.multiple_of(r * SUBLANES, SUBLANES)
        for k in range(TOP_K):
            pltpu.make_async_copy(_row_tile(ys_hbm, pos_ref[k * s + t0 + r]),
                                  buf.at[k, pl.ds(first, SUBLANES), :], sems.at[buf_id, k]).start(priority=k)

    def wait_buffer(buf, buf_id):
        span = ys_hbm.at[pl.ds(0, tm * SUBLANES), :]
        for k in range(TOP_K):
            pltpu.make_async_copy(span, buf.at[k], sems.at[buf_id, k]).wait()

    @pl.when(i == 0)
    def _():
        def issue(g, carry):
            for u in range(ROW_UNROLL):
                start_row(0, ybuf_a, 0, g * ROW_UNROLL + u)
            return carry

        lax.fori_loop(0, tm // ROW_UNROLL, issue, 0)

    def step(cur, cur_id, nxt_buf, nxt_id):
        wait_buffer(cur, cur_id)
        nxt = jnp.minimum(i + 1, n - 1) * tm
        for r in range(tm):
            start_row(nxt, nxt_buf, nxt_id, r)

        route = route_ref[...]
        f = None
        for k in range(TOP_K):
            lo, hi = _unpack_bf16_pair(_load_row_tiles(cur.at[k]))
            fk = route[:, TOP_K + k:TOP_K + k + 1] * jnp.concatenate([lo, hi], axis=1)
            f = fk if f is None else f + fk
        x2 = _layer_norm_rows(ALPHA * x1_ref[...] + f, g_ref[...], b_ref[...])
        gate = jax.nn.sigmoid(jnp.dot(x2.astype(BF16), wg_ref[...], preferred_element_type=F32) + bg_ref[...])
        emb = jnp.dot(p_ref[...].astype(BF16), wp_ref[...], preferred_element_type=F32)
        out = x2 + gate * emb
        o_ref[...] = out
        ob_ref[...] = out.astype(BF16)

        @pl.when(i == n - 1)
        def _():
            wait_buffer(nxt_buf, nxt_id)

    @pl.when(i % 2 == 0)
    def _():
        step(ybuf_a, 0, ybuf_b, 1)

    @pl.when(i % 2 == 1)
    def _():
        step(ybuf_b, 1, ybuf_a, 0)


def _combine_ple(ys, pos, x1, route, ln_g, ln_b, w_gate, b_gate, p, w_proj, *, tm=256):
    s, d = x1.shape
    pd = p.shape[1]
    rows = lambda n: pl.BlockSpec((tm, n), lambda i, pos: (i, 0))
    full = lambda a: pl.BlockSpec(a.shape, lambda i, pos: (0,) * a.ndim)
    g2, b2, bg = ln_g.reshape(1, d), ln_b.reshape(1, d), b_gate.reshape(1, d)
    return pl.pallas_call(
        functools.partial(_combine_ple_kernel, tm=tm, s=s),
        grid_spec=pltpu.PrefetchScalarGridSpec(
            num_scalar_prefetch=1,
            grid=(s // tm,),
            in_specs=[pl.BlockSpec(memory_space=pl.ANY), rows(d), rows(ROUTE_LANES), full(g2), full(b2),
                      full(w_gate), full(bg), rows(pd), full(w_proj)],
            out_specs=[rows(d), rows(d)],
            scratch_shapes=[pltpu.VMEM((TOP_K, tm * SUBLANES, LANES), jnp.uint32),
                            pltpu.VMEM((TOP_K, tm * SUBLANES, LANES), jnp.uint32),
                            pltpu.SemaphoreType.DMA((2, TOP_K))]),
        out_shape=[jax.ShapeDtypeStruct((s, d), F32), jax.ShapeDtypeStruct((s, d), BF16)],
        compiler_params=_params(("arbitrary",), 56),
        name="moe_combine_ple",
    )(pos, ys, x1, route, g2, b2, w_gate, bg, p, w_proj)


def _rglru_kernel(cur_ref, prev_ref, gate_ref, cw_ref, cb_ref, wa_ref, ba_ref, wi_ref, bi_ref, lam_ref,
                  o_ref, buf_ref, a_ref, b_ref, h_ref, carry_ref, *, tt, tc, width):
    i = pl.program_id(1)

    @pl.when(i == 0)
    def _():
        carry_ref[...] = jnp.zeros(carry_ref.shape, F32)

    buf_ref[0:RNN_HALO, :] = jnp.where(i > 0, prev_ref[...], 0.0)
    buf_ref[RNN_HALO:, :] = cur_ref[...]
    off = RNN_HALO - (width - 1)
    z = -lam_ref[...]
    softplus = jnp.log1p(jnp.exp(-jnp.abs(z))) + jnp.maximum(z, 0.0)

    def sigmoid(v):
        return 0.5 * jnp.tanh(0.5 * v) + 0.5

    for hh in range(tc // RNN_HEAD_DIM):
        cols = slice(hh * RNN_HEAD_DIM, (hh + 1) * RNN_HEAD_DIM)
        xh = jnp.broadcast_to(cb_ref[:, cols], (tt, RNN_HEAD_DIM))
        for k in range(width):
            xh = xh + buf_ref[off + k:off + k + tt, cols] * cw_ref[k:k + 1, cols]
        xhb = xh.astype(BF16)
        r = sigmoid(jnp.dot(xhb, wa_ref[hh], preferred_element_type=F32) + ba_ref[:, cols])
        ig = sigmoid(jnp.dot(xhb, wi_ref[hh], preferred_element_type=F32) + bi_ref[:, cols])
        log_a = (-RG_C) * r * softplus[:, cols]
        a = jnp.exp(log_a)
        bval = jnp.sqrt(-jnp.tanh(log_a) * (a * a + 1.0)) * (ig * xh)
        a_ref[:, cols] = a
        b_ref[:, cols] = bval

    def group(gi, h):
        base = pl.multiple_of(gi * 8, 8)
        for r in range(8):
            row = pl.ds(base + r, 1)
            h = a_ref[row, :] * h + b_ref[row, :]
            h_ref[row, :] = h
        return h

    carry_ref[...] = lax.fori_loop(0, tt // 8, group, carry_ref[...])
    o_ref[...] = (gate_ref[...].astype(F32) * h_ref[...]).astype(o_ref.dtype)


def _rglru(xr, gate, conv_w, conv_b, w_a, b_a, w_i, b_i, lam, *, tt=256, tc=2048):
    s, wd = xr.shape
    width = conv_w.shape[0]
    hpc = tc // RNN_HEAD_DIM
    hb = tt // RNN_HALO
    row = lambda v: v.reshape(1, wd)
    cvec = lambda: pl.BlockSpec((1, tc), lambda c, i: (0, c))
    heads = lambda: pl.BlockSpec((hpc, RNN_HEAD_DIM, RNN_HEAD_DIM), lambda c, i: (c, 0, 0))
    tile = lambda: pl.BlockSpec((tt, tc), lambda c, i: (i, c))
    return pl.pallas_call(
        functools.partial(_rglru_kernel, tt=tt, tc=tc, width=width),
        grid=(wd // tc, s // tt),
        in_specs=[tile(),
                  pl.BlockSpec((RNN_HALO, tc), lambda c, i: (jnp.maximum(i * hb - 1, 0), c)),
                  tile(),
                  pl.BlockSpec((width, tc), lambda c, i: (0, c)),
                  cvec(), heads(), cvec(), heads(), cvec(), cvec()],
        out_specs=tile(),
        out_shape=jax.ShapeDtypeStruct((s, wd), BF16),
        scratch_shapes=[pltpu.VMEM((tt + RNN_HALO, tc), F32), pltpu.VMEM((tt, tc), F32),
                        pltpu.VMEM((tt, tc), F32), pltpu.VMEM((tt, tc), F32), pltpu.VMEM((1, tc), F32)],
        compiler_params=_params(("parallel", "arbitrary"), 32),
        name="rglru",
    )(xr, xr, gate, conv_w, row(conv_b), w_a.astype(BF16), row(b_a), w_i.astype(BF16), row(b_i), row(lam))


def _moe_and_ple(mixed, i, p, ln_ffn_g, ln_ffn_b, moe_w1, moe_w3, moe_w2, ple_w_proj, ple_w_gate, ple_b_gate):
    x1, x1_pairs, route = mixed
    dest, block_expert, n_used = _moe_plan(route)
    xs = _dispatch_rows(x1_pairs, dest, block_expert.shape[0])
    ys = _grouped_ffn(xs, moe_w1, moe_w3, moe_w2, i, block_expert, n_used)
    return _combine_ple(ys, dest, x1, route, ln_ffn_g[i], ln_ffn_b[i], ple_w_gate[i].astype(BF16),
                        ple_b_gate[i], p[i, 0], ple_w_proj[i].astype(BF16))


def kernel(x, p, positions, ev_w_in, ev_conv_w, ev_conv_b, ev_cnorm_g, ev_cnorm_b, ev_qnorm_g, ev_w_uq, ev_kvnorm_g, ev_w_ukv, ev_w_out, od_w_in, od_conv_w, od_conv_b, od_w_a, od_b_a, od_w_i, od_b_i, od_lam, od_w_out, ln_mix_g, ln_mix_b, ln_ffn_g, ln_ffn_b, moe_w_grp, moe_b_grp, moe_w_exp, moe_b_exp, moe_w1, moe_w3, moe_w2, ple_w_proj, ple_w_gate, ple_b_gate):
    x0 = x[0]
    moe =functools.partial(_moe_and_ple, p=p, ln_ffn_g=ln_ffn_g, ln_ffn_b=ln_ffn_b, moe_w1=moe_w1,
                            moe_w3=moe_w3, moe_w2=moe_w2, ple_w_proj=ple_w_proj, ple_w_gate=ple_w_gate,
                            ple_b_gate=ple_b_gate)

    cosm, sinm = _rope_tables(positions[0])
    x0b, qn, qr, kn, kr, v = _mla_proj(x0, ev_w_in[0, :, 2 * CONV_CH:], ev_qnorm_g[0], ev_w_uq[0], ev_kvnorm_g[0],
                                       ev_w_ukv[0], cosm, sinm)
    u = _pair_proj(x0b, ev_w_in, 0, CONV_CH, mode="glu")
    ub = _conv_gln(u, ev_conv_w[0], ev_conv_b[0], ev_cnorm_g[0], ev_cnorm_b[0], groups=CONV_GROUPS)
    att = _attention(qn, qr, kn, kr, v)
    mixed0 = _out_ln_route(ub, att, ev_w_out[0].astype(BF16), x0, ln_mix_g[0], ln_mix_b[0],
                           moe_w_grp[0], moe_b_grp[0], moe_w_exp[0], moe_b_exp[0])
    x3, x3b = moe(mixed0, 0)

    gate, xr = _pair_proj(x3b, od_w_in, 0, RNN_WIDTH, mode="rnn")
    y = _rglru(xr, gate, od_conv_w[0], od_conv_b[0], od_w_a[0], od_b_a[0], od_w_i[0], od_b_i[0], od_lam[0])
    mixed1 = _out_ln_route(y, y, od_w_out[0].astype(BF16), x3, ln_mix_g[1], ln_mix_b[1],
                           moe_w_grp[1], moe_b_grp[1], moe_w_exp[1], moe_b_exp[1], col1=0, col2=1)
    x6, _ = moe(mixed1, 1)
    return x6[None]
```

```python
import functools
import math

import jax
import jax.numpy as jnp
from jax import lax
from jax.experimental import pallas as pl
from jax.experimental.pallas import tpu as pltpu

F32 = jnp.float32
BF16 = jnp.bfloat16

D_MODEL = 2048
SEQ = 8192
DEPTH = 2
CONV_CH = 1024
CONV_GROUPS = 16
CONV_WIDTH = 31
MLA_HEADS = 8
QK_NOPE = 128
QK_ROPE = 64
V_HEAD = 128
Q_RANK = 512
KV_RANK = 256
ROPE_THETA = 10000.0
RNN_WIDTH = 2048
RNN_HEADS = 16
RNN_HEAD_DIM = RNN_WIDTH // RNN_HEADS
RNN_CONV_WIDTH = 4
RG_C = 8.0
N_GROUPS = 8
EXPERTS_PER_GROUP = 8
N_EXPERTS = N_GROUPS * EXPERTS_PER_GROUP
TOP_K = 2
D_EXPERT = 512
PLE_DIM = 256
ALPHA = (2 * DEPTH) ** 0.25
LN_EPS = 1e-5
RMS_EPS = 1e-6

LANES = 128
SUBLANES = 8
CONV_HALO = 32
RNN_HALO = 8
MOE_ROWS = 256
ROUTE_LANES = 128
MIB = 2 ** 20


def _params(semantics, vmem_mib):
    return pltpu.CompilerParams(dimension_semantics=semantics, vmem_limit_bytes=vmem_mib * MIB)


def _layer_norm_rows(z, g, b):
    mu = jnp.mean(z, axis=-1, keepdims=True)
    d = z - mu
    var = jnp.mean(d * d, axis=-1, keepdims=True)
    return d * lax.rsqrt(var + LN_EPS) * g + b


def _pack_bf16_pair(a, b):
    def rounded_bits(v):
        u = lax.bitcast_convert_type(v, jnp.uint32)
        return u + jnp.uint32(0x7FFF) + ((u >> 16) & jnp.uint32(1))

    return (rounded_bits(a) & jnp.uint32(0xFFFF0000)) | (rounded_bits(b) >> 16)


def _unpack_bf16_pair(u):
    a = lax.bitcast_convert_type(u & jnp.uint32(0xFFFF0000), F32)
    b = lax.bitcast_convert_type(u << 16, F32)
    return a, b


def _store_row_tiles(ref, packed):
    n, w = packed.shape
    assert w == SUBLANES * LANES
    for j in range(SUBLANES):
        ref[pl.ds(j, n, stride=SUBLANES), :] = packed[:, j * LANES:(j + 1) * LANES]


def _load_row_tiles(ref):
    n = ref.shape[0] // SUBLANES
    return jnp.concatenate([ref[pl.ds(j, n, stride=SUBLANES), :] for j in range(SUBLANES)], axis=1)


def _split_bf16(v):
    hi = v.astype(BF16)
    lo = (v - hi.astype(F32)).astype(BF16)
    return hi, lo


def _gelu_tanh(x):
    c = math.sqrt(2.0 / math.pi)
    return 0.5 * x * (1.0 + jnp.tanh(c * (x + 0.044715 * (x * x * x))))


def _pair_proj_kernel(x_ref, wa_ref, wb_ref, *refs, mode):
    out_refs, (wa_bf, wb_bf) = refs[:-2], refs[-2:]

    @pl.when(pl.program_id(1) == 0)
    def _():
        wa_bf[...] = wa_ref[...].astype(BF16)
        wb_bf[...] = wb_ref[...].astype(BF16)

    x = x_ref[...]
    a = jnp.dot(x, wa_bf[...], preferred_element_type=F32)
    b = jnp.dot(x, wb_bf[...], preferred_element_type=F32)
    if mode == "glu":
        out_refs[0][...] = a * jax.nn.sigmoid(b)
    else:
        out_refs[0][...] = _gelu_tanh(a).astype(out_refs[0].dtype)
        out_refs[1][...] = b


def _pair_proj(x, w, n, *, mode, tm=512, tn=512):
    m, k = x.shape
    nb = n // tn
    tile = pl.BlockSpec((tm, tn), lambda j, i: (i, j))
    if mode == "glu":
        out_specs, out_shape = tile, jax.ShapeDtypeStruct((m, n), F32)
    else:
        out_specs = [tile, tile]
        out_shape = [jax.ShapeDtypeStruct((m, n), BF16), jax.ShapeDtypeStruct((m, n), F32)]
    return pl.pallas_call(
        functools.partial(_pair_proj_kernel, mode=mode),
        grid=(nb, m // tm),
        in_specs=[pl.BlockSpec((tm, k), lambda j, i: (i, 0)),
                  pl.BlockSpec((k, tn), lambda j, i: (0, j)),
                  pl.BlockSpec((k, tn), lambda j, i: (0, j + nb))],
        out_specs=out_specs,
        out_shape=out_shape,
        scratch_shapes=[pltpu.VMEM((k, tn), BF16), pltpu.VMEM((k, tn), BF16)],
        compiler_params=_params(("parallel", "arbitrary"), 48),
        name=mode + "_proj",
    )(x, w, w)


def _conv_gln_kernel(cur_ref, prev_ref, w_ref, b_ref, g_ref, beta_ref, gm_ref, o_ref,
                     buf_ref, sh_ref, y_ref, *, tt, ch, width, rc, cc):
    i = pl.program_id(0)
    buf_ref[0:CONV_HALO, :] = jnp.where(i > 0, prev_ref[...], 0.0)
    buf_ref[CONV_HALO:, :] = cur_ref[...]
    off = CONV_HALO - (width - 1)
    sh_rows = sh_ref.shape[1]
    step = 7 * SUBLANES
    assert sh_rows % step == 0
    for b in range(1, SUBLANES):
        for c0 in range(0, ch, cc):
            for r0 in range(0, sh_rows, step):
                sh_ref[b - 1, r0:r0 + step, c0:c0 + cc] = buf_ref[r0 + b:r0 + b + step, c0:c0 + cc]
    for c0 in range(0, ch, cc):
        for r0 in range(0, tt, rc):
            acc = jnp.broadcast_to(b_ref[:, c0:c0 + cc], (rc, cc))
            for k in range(width):
                q, b = divmod(off + k, SUBLANES)
                lo = r0 + q * SUBLANES
                src = buf_ref[lo:lo + rc, c0:c0 + cc] if b == 0 else sh_ref[b - 1, lo:lo + rc, c0:c0 + cc]
                acc = acc + src * w_ref[k:k + 1, c0:c0 + cc]
            y_ref[r0:r0 + rc, c0:c0 + cc] = acc
    gm = gm_ref[...]

    def seg_mean(v):
        hi, lo = _split_bf16(v)
        return (jnp.dot(hi, gm, preferred_element_type=F32)
                + jnp.dot(lo, gm, preferred_element_type=F32))

    for c0 in range(0, ch, LANES):
        y = y_ref[:, c0:c0 + LANES]
        d = y - seg_mean(y)
        var = seg_mean(d * d)
        z = d * lax.rsqrt(var + LN_EPS) * g_ref[:, c0:c0 + LANES] + beta_ref[:, c0:c0 + LANES]
        o_ref[:, c0:c0 + LANES] = (z * jax.nn.sigmoid(z)).astype(o_ref.dtype)


def _conv_gln(u, w, b, g, beta, *, groups, tt=256):
    s, ch = u.shape
    width = w.shape[0]
    gsz = ch // groups
    assert LANES % gsz == 0 and width - 1 <= CONV_HALO and tt % CONV_HALO == 0
    wpad = jnp.zeros((CONV_HALO, ch), F32).at[:width].set(w)
    lane = jnp.arange(LANES)
    gm = jnp.where((lane[:, None] // gsz) == (lane[None, :] // gsz), 1.0 / gsz, 0.0).astype(BF16)
    hb = tt // CONV_HALO
    row = lambda v: v.reshape(1, ch)
    return pl.pallas_call(
        functools.partial(_conv_gln_kernel, tt=tt, ch=ch, width=width, rc=32, cc=256),
        grid=(s // tt,),
        in_specs=[pl.BlockSpec((tt, ch), lambda i: (i, 0)),
                  pl.BlockSpec((CONV_HALO, ch), lambda i: (jnp.maximum(i * hb - 1, 0), 0)),
                  pl.BlockSpec((CONV_HALO, ch), lambda i: (0, 0)),
                  pl.BlockSpec((1, ch), lambda i: (0, 0)),
                  pl.BlockSpec((1, ch), lambda i: (0, 0)),
                  pl.BlockSpec((1, ch), lambda i: (0, 0)),
                  pl.BlockSpec((LANES, LANES), lambda i: (0, 0))],
        out_specs=pl.BlockSpec((tt, ch), lambda i: (i, 0)),
        out_shape=jax.ShapeDtypeStruct((s, ch), BF16),
        scratch_shapes=[pltpu.VMEM((tt + CONV_HALO, ch), F32),
                        pltpu.VMEM((SUBLANES - 1, tt + CONV_HALO - SUBLANES, ch), F32),
                        pltpu.VMEM((tt, ch), F32)],
        compiler_params=_params(("parallel",), 40),
        name="conv_gln",
    )(u, u, wpad, row(b), row(g), row(beta), gm)


def _rope_table_kernel(pos_ref, invf_ref, cos_ref, sin_ref):
    ang = pos_ref[...].astype(F32) * invf_ref[...]
    cos_ref[...] = jnp.cos(ang)
    sin_ref[...] = jnp.sin(ang)


def _rope_tables(positions):
    s = positions.shape[0]
    half = QK_ROPE // 2
    per_row = LANES // half
    inv_freq = 1.0 / (ROPE_THETA ** (jnp.arange(0, QK_ROPE, 2, dtype=F32) / QK_ROPE))
    pos_rep = jnp.repeat(positions.reshape(s // per_row, per_row), half, axis=1)
    invf = jnp.tile(inv_freq, per_row).reshape(1, LANES)
    cos, sin = pl.pallas_call(
        _rope_table_kernel,
        out_shape=[jax.ShapeDtypeStruct((s // per_row, LANES), F32)] * 2,
        name="rope_table",
    )(pos_rep, invf)
    cos = cos.reshape(s, half)
    sin = sin.reshape(s, half)
    zero = jnp.zeros((s, LANES - QK_ROPE), F32)
    return jnp.concatenate([cos, cos, zero], axis=1), jnp.concatenate([-sin, sin, zero], axis=1)


def _mla_proj_kernel(x_ref, wm_ref, qg_ref, kvg_ref, wq_ref, wkv_ref, c_ref, s_ref,
                     xb_ref, qn_ref, qr_ref, kn_ref, kr_ref, v_ref, *, scale):
    nq = MLA_HEADS * QK_NOPE
    xb = x_ref[...].astype(BF16)
    xb_ref[...] = xb
    c = jnp.dot(xb, wm_ref[...], preferred_element_type=F32)
    cq = c[:, :Q_RANK]
    ckv = c[:, Q_RANK:Q_RANK + KV_RANK]
    kr_pad = c[:, Q_RANK + KV_RANK:Q_RANK + KV_RANK + LANES]
    kr_rot = c[:, Q_RANK + KV_RANK + LANES:]
    cqn = cq * lax.rsqrt(jnp.mean(cq * cq, axis=-1, keepdims=True) + RMS_EPS) * qg_ref[...]
    ckvn = ckv * lax.rsqrt(jnp.mean(ckv * ckv, axis=-1, keepdims=True) + RMS_EPS) * kvg_ref[...]
    q = jnp.dot(cqn.astype(BF16), wq_ref[...], preferred_element_type=F32)
    kv = jnp.dot(ckvn.astype(BF16), wkv_ref[...], preferred_element_type=F32)
    cosm = c_ref[...]
    sinm = s_ref[...]
    qn_ref[...] = (q[:, :nq] * scale).astype(BF16)
    for h in range(MLA_HEADS):
        lo = nq + h * LANES
        rot = q[:, lo:lo + LANES] * cosm + q[:, lo + nq:lo + nq + LANES] * sinm
        qr_ref[:, h * LANES:(h + 1) * LANES] = (rot * scale).astype(BF16)
    kn_ref[...] = kv[:, :nq].astype(BF16)
    v_ref[...] = kv[:, nq:].astype(BF16)
    kr_ref[...] = (kr_pad * cosm + kr_rot * sinm).astype(BF16)


def _rope_swap(w):
    half = QK_ROPE // 2
    return jnp.concatenate([w[..., half:], w[..., :half]], axis=-1)


def _mla_proj(x, w_in_mla, qnorm_g, w_uq, kvnorm_g, w_ukv, cosm, sinm, *, tm=256):
    s, d = x.shape
    nq = MLA_HEADS * QK_NOPE
    pad = lambda w: jnp.concatenate([w, jnp.zeros(w.shape[:-1] + (LANES - QK_ROPE,), w.dtype)], axis=-1)
    w_kr = w_in_mla[:, Q_RANK + KV_RANK:]
    wm = jnp.concatenate([w_in_mla[:, :Q_RANK + KV_RANK], pad(w_kr), pad(_rope_swap(w_kr))], axis=1).astype(BF16)
    wq3 = w_uq.reshape(Q_RANK, MLA_HEADS, QK_NOPE + QK_ROPE)
    wq_rope = wq3[:, :, QK_NOPE:]
    wq = jnp.concatenate([wq3[:, :, :QK_NOPE].reshape(Q_RANK, nq),
                          pad(wq_rope).reshape(Q_RANK, MLA_HEADS * LANES),
                          pad(_rope_swap(wq_rope)).reshape(Q_RANK, MLA_HEADS * LANES)], axis=1).astype(BF16)
    wkv3 = w_ukv.reshape(KV_RANK, MLA_HEADS, QK_NOPE + V_HEAD)
    wkv = jnp.concatenate([wkv3[:, :, :QK_NOPE].reshape(KV_RANK, nq),
                           wkv3[:, :, QK_NOPE:].reshape(KV_RANK, MLA_HEADS * V_HEAD)], axis=1).astype(BF16)
    scale = (QK_NOPE + QK_ROPE) ** -0.5 * math.log2(math.e)
    full = lambda a: pl.BlockSpec(a.shape, lambda i: (0,) * a.ndim)
    rows = lambda n: pl.BlockSpec((tm, n), lambda i: (i, 0))
    qg = qnorm_g.reshape(1, Q_RANK)
    kvg = kvnorm_g.reshape(1, KV_RANK)
    return pl.pallas_call(
        functools.partial(_mla_proj_kernel, scale=scale),
        grid=(s // tm,),
        in_specs=[rows(d), full(wm), full(qg), full(kvg), full(wq), full(wkv), rows(LANES), rows(LANES)],
        out_specs=[rows(d), rows(nq), rows(MLA_HEADS * LANES), rows(nq), rows(LANES), rows(MLA_HEADS * V_HEAD)],
        out_shape=[jax.ShapeDtypeStruct((s, d), BF16),
                   jax.ShapeDtypeStruct((s, nq), BF16),
                   jax.ShapeDtypeStruct((s, MLA_HEADS * LANES), BF16),
                   jax.ShapeDtypeStruct((s, nq), BF16),
                   jax.ShapeDtypeStruct((s, LANES), BF16),
                   jax.ShapeDtypeStruct((s, MLA_HEADS * V_HEAD), BF16)],
        compiler_params=_params(("parallel",), 48),
        name="mla_proj",
    )(x, wm, qg, kvg, wq, wkv, cosm, sinm)


def _attn_kernel(qn_ref, qr_ref, kn_ref, kr_ref, v_ref, o_ref, m_ref, acc_ref, *, tq, tk, hb):
    qi = pl.program_id(1)
    per_q = tq // tk
    m_ref[...] = jnp.full(m_ref.shape, -jnp.inf, F32)
    acc_ref[...] = jnp.zeros(acc_ref.shape, F32)
    ones = jnp.ones((tk, LANES), BF16)
    head_cols = [slice(h * LANES, (h + 1) * LANES) for h in range(hb)]
    qs = [jnp.concatenate([qn_ref[:, c], qr_ref[:, c]], axis=1) for c in head_cols]

    def chunk(j, diag):
        rows = pl.ds(pl.multiple_of(j * tk, tk), tk)
        k_rope = kr_ref[rows, :]
        q0 = 0 if diag is None else diag * tk
        for h, c in enumerate(head_cols):
            k = jnp.concatenate([kn_ref[rows, c], k_rope], axis=1)
            s = lax.dot_general(qs[h][q0:], k, (((1,), (1,)), ((), ())), preferred_element_type=F32)
            if diag is not None:
                keep = lax.broadcasted_iota(jnp.int32, s.shape, 0) >= lax.broadcasted_iota(jnp.int32, s.shape, 1)
                s = jnp.where(keep, s, -jnp.inf)
            m_old = m_ref[h, q0:, :]
            m_new = jnp.maximum(m_old, jnp.max(s, axis=-1, keepdims=True))
            alpha = jnp.exp2(m_old - m_new)
            p = jnp.exp2(s - jnp.concatenate([m_new] * (tk // LANES), axis=1))
            v_ext = jnp.concatenate([v_ref[rows, c], ones], axis=1)
            acc_ref[h, q0:, :] = (jnp.concatenate([alpha, alpha], axis=1) * acc_ref[h, q0:, :]
                                  + jnp.dot(p.astype(BF16), v_ext, preferred_element_type=F32))
            m_ref[h, q0:, :] = m_new

    def body(j, carry):
        chunk(j, None)
        return carry

    lax.fori_loop(0, qi * per_q, body, 0)
    for d in range(per_q):
        chunk(qi * per_q + d, d)
    for h, c in enumerate(head_cols):
        acc = acc_ref[h]
        o_ref[:, c] = (acc[:, :V_HEAD] / acc[:, V_HEAD:]).astype(o_ref.dtype)


def _attention(qn, qr, kn, kr, v, *, tq=1024, tk=512, hb=4):
    s = qn.shape[0]
    wide = hb * LANES
    tile = lambda: pl.BlockSpec((tq, wide), lambda h, i: (i, h))
    keys = lambda: pl.BlockSpec((s, wide), lambda h, i: (0, h), pipeline_mode=pl.Buffered(1))
    return pl.pallas_call(
        functools.partial(_attn_kernel, tq=tq, tk=tk, hb=hb),
        grid=(MLA_HEADS // hb, s // tq),
        in_specs=[tile(), tile(), keys(),
                  pl.BlockSpec((s, LANES), lambda h, i: (0, 0), pipeline_mode=pl.Buffered(1)), keys()],
        out_specs=tile(),
        out_shape=jax.ShapeDtypeStruct((s, MLA_HEADS * V_HEAD), BF16),
        scratch_shapes=[pltpu.VMEM((hb, tq, LANES), F32), pltpu.VMEM((hb, tq, 2 * V_HEAD), F32)],
        compiler_params=_params(("parallel", "parallel"), 48),
        name="mla_attention",
    )(qn, qr, kn, kr, v)


def _route(logits):
    lane = lax.broadcasted_iota(jnp.int32, logits.shape, 1).astype(F32)
    big = float(2 * ROUTE_LANES)
    neg = -jnp.inf
    gl = jnp.where(lane < N_GROUPS, logits, neg)
    gmax = jnp.max(gl, axis=-1, keepdims=True)
    gsum = jnp.sum(jnp.exp(gl - gmax), axis=-1, keepdims=True)
    g_prob = 1.0 / gsum
    g_idx = jnp.min(jnp.where(gl == gmax, lane, big), axis=-1, keepdims=True)
    lo = N_GROUPS + g_idx * EXPERTS_PER_GROUP
    el = jnp.where((lane >= lo) & (lane < lo + EXPERTS_PER_GROUP), logits, neg)
    emax = jnp.max(el, axis=-1, keepdims=True)
    esum = jnp.sum(jnp.exp(el - emax), axis=-1, keepdims=True)
    i0 = jnp.min(jnp.where(el == emax, lane, big), axis=-1, keepdims=True)
    el2 = jnp.where(lane == i0, neg, el)
    emax2 = jnp.max(el2, axis=-1, keepdims=True)
    i1 = jnp.min(jnp.where(el2 == emax2, lane, big), axis=-1, keepdims=True)
    p0 = 1.0 / esum
    p1 = jnp.exp(emax2 - emax) / esum
    g0 = g_prob * p0 / (p0 + p1)
    g1 = g_prob * p1 / (p0 + p1)
    e0 = i0 - N_GROUPS
    e1 = i1 - N_GROUPS
    return jnp.where(lane == 0, e0, jnp.where(lane == 1, e1, jnp.where(lane == 2, g0, jnp.where(lane == 3, g1, 0.0))))


def _out_ln_route_kernel(a1_ref, a2_ref, w_ref, x_ref, g_ref, b_ref, wrh_ref, br_ref,
                         x1_ref, x1p_ref, route_ref, *, sub):
    half = a1_ref.shape[1]
    for r0 in range(0, x_ref.shape[0], sub):
        rows = slice(r0, r0 + sub)
        m = (jnp.dot(a1_ref[rows, :], w_ref[:half, :], preferred_element_type=F32)
             + jnp.dot(a2_ref[rows, :], w_ref[half:, :], preferred_element_type=F32))
        x1 = _layer_norm_rows(ALPHA * x_ref[rows, :] + m, g_ref[...], b_ref[...])
        x1_ref[rows, :] = x1
        hd = x1.shape[1] // 2
        _store_row_tiles(x1p_ref.at[pl.ds(r0 * SUBLANES, sub * SUBLANES), :], _pack_bf16_pair(x1[:, :hd], x1[:, hd:]))
        xh, xl = _split_bf16(x1)
        both = jnp.dot(xh, wrh_ref[...], preferred_element_type=F32)
        logits = (both[:, :ROUTE_LANES] + both[:, ROUTE_LANES:]
                  + jnp.dot(xl, wrh_ref[:, :ROUTE_LANES], preferred_element_type=F32) + br_ref[...])
        route_ref[rows, :] = _route(logits)


def _out_ln_route(a1, a2, w, x, ln_g, ln_b, w_grp, b_grp, w_exp, b_exp, *, col1=0, col2=0, tm=512, sub=512):
    s, d = x.shape
    half = w.shape[0] // 2
    pad = ROUTE_LANES - N_GROUPS - N_EXPERTS
    wr = jnp.concatenate([w_grp, w_exp, jnp.zeros((d, pad), F32)], axis=1)
    br = jnp.concatenate([b_grp, b_exp, jnp.zeros((pad,), F32)]).reshape(1, ROUTE_LANES)
    wr_hi = wr.astype(BF16)
    wrh = jnp.concatenate([wr_hi, (wr - wr_hi.astype(F32)).astype(BF16)], axis=1)
    full = lambda a: pl.BlockSpec(a.shape, lambda i: (0,) * a.ndim)
    rows = lambda n: pl.BlockSpec((tm, n), lambda i: (i, 0))
    g2 = ln_g.reshape(1, d)
    b2 = ln_b.reshape(1, d)
    return pl.pallas_call(
        functools.partial(_out_ln_route_kernel, sub=sub),
        grid=(s // tm,),
        in_specs=[pl.BlockSpec((tm, half), lambda i: (i, col1)), pl.BlockSpec((tm, half), lambda i: (i, col2)),
                  pl.BlockSpec(w.shape, lambda i: (0, 0), pipeline_mode=pl.Buffered(1)),
                  rows(d), full(g2), full(b2), full(wrh), full(br)],
        out_specs=[rows(d), pl.BlockSpec((tm * SUBLANES, LANES), lambda i: (i, 0)), rows(ROUTE_LANES)],
        out_shape=[jax.ShapeDtypeStruct((s, d), F32), jax.ShapeDtypeStruct((s * SUBLANES, LANES), jnp.uint32),
                   jax.ShapeDtypeStruct((s, ROUTE_LANES), F32)],
        compiler_params=_params(("parallel",), 48),
        name="out_ln_route",
    )(a1, a2, w, x, g2, b2, wrh, br)


def _rank_kernel(route_ref, tri_ref, rank_ref, cnt_ref, carry_ref):
    @pl.when(pl.program_id(0) == 0)
    def _():
        carry_ref[...] = jnp.zeros(carry_ref.shape, F32)

    route = route_ref[...]
    lane = lax.broadcasted_iota(jnp.int32, route.shape, 1).astype(F32)
    oh0 = jnp.where(lane == route[:, 0:1], 1.0, 0.0)
    oh1 = jnp.where(lane == route[:, 1:2], 1.0, 0.0)
    both = oh0 + oh1
    before = jnp.dot(tri_ref[...], both.astype(BF16), preferred_element_type=F32) + carry_ref[...]
    r0 = jnp.sum(before * oh0, axis=-1, keepdims=True)
    r1 = jnp.sum(before * oh1, axis=-1, keepdims=True)
    rank_ref[...] = jnp.where(lane == 0.0, r0, jnp.where(lane == 1.0, r1, 0.0))
    carry_ref[...] = carry_ref[...] + jnp.sum(both, axis=0, keepdims=True)
    cnt_ref[...] = carry_ref[...]


def _dest_kernel(route_ref, rank_ref, pstart_ref, dest_ref):
    route = route_ref[...]
    rank = rank_ref[...]
    lane = lax.broadcasted_iota(jnp.int32, route.shape, 1).astype(F32)
    ps = pstart_ref[...]
    d0 = jnp.sum(jnp.where(lane == route[:, 0:1], ps, 0.0), axis=-1, keepdims=True) + rank[:, 0:1]
    d1 = jnp.sum(jnp.where(lane == route[:, 1:2], ps, 0.0), axis=-1, keepdims=True) + rank[:, 1:2]
    dest = jnp.where(lane == 0.0, d0, jnp.where(lane == 1.0, d1, 0.0)) * float(SUBLANES)
    dest_ref[...] = dest.astype(jnp.int32)


def _moe_plan(route, *, tb=512):
    s = route.shape[0]
    tri = (jnp.arange(tb)[:, None] > jnp.arange(tb)[None, :]).astype(BF16)
    rows = pl.BlockSpec((tb, ROUTE_LANES), lambda i: (i, 0))
    one = lambda: pl.BlockSpec((1, ROUTE_LANES), lambda i: (0, 0))
    rank, cnt = pl.pallas_call(
        _rank_kernel,
        grid=(s // tb,),
        in_specs=[rows, pl.BlockSpec((tb, tb), lambda i: (0, 0))],
        out_specs=[rows, one()],
        out_shape=[jax.ShapeDtypeStruct((s, ROUTE_LANES), F32), jax.ShapeDtypeStruct((1, ROUTE_LANES), F32)],
        scratch_shapes=[pltpu.VMEM((1, ROUTE_LANES), F32)],
        compiler_params=_params(("arbitrary",), 32),
        name="moe_rank",
    )(route, tri)
    counts = cnt[0, :N_EXPERTS].astype(jnp.int32)
    pcounts = (counts + MOE_ROWS - 1) // MOE_ROWS * MOE_ROWS
    pends = jnp.cumsum(pcounts)
    pstarts = pends - pcounts
    n_blocks = s * TOP_K // MOE_ROWS + N_EXPERTS
    n_used = pends[-1] // MOE_ROWS
    blk = jnp.minimum(jnp.arange(n_blocks, dtype=jnp.int32), n_used - 1) * MOE_ROWS
    block_expert = jnp.minimum(jnp.sum(pends[None, :] <= blk[:, None], axis=1), N_EXPERTS - 1).astype(jnp.int32)
    ps = jnp.zeros((1, ROUTE_LANES), F32).at[0, :N_EXPERTS].set(pstarts.astype(F32))
    dest = pl.pallas_call(
        _dest_kernel,
        grid=(s // tb,),
        in_specs=[rows, rows, one()],
        out_specs=rows,
        out_shape=jax.ShapeDtypeStruct((s, ROUTE_LANES), jnp.int32),
        compiler_params=_params(("parallel",), 32),
        name="moe_dest",
    )(route, rank, ps)
    dest = dest[:, :TOP_K].T.reshape(TOP_K * s)
    return dest, block_expert, n_used.astype(jnp.int32).reshape(1)


ROW_UNROLL = 8


def _row_tile(ref, first_row):
    return ref.at[pl.ds(pl.multiple_of(first_row, SUBLANES), SUBLANES), :]


def _dispatch_kernel(dest_ref, x_ref, xs_hbm, sems, *, tm, s):
    t0 = pl.program_id(0) * tm

    def issue(g, carry):
        for u in range(ROW_UNROLL):
            r = g * ROW_UNROLL + u
            src = _row_tile(x_ref, r * SUBLANES)
            for k in range(TOP_K):
                pltpu.make_async_copy(src, _row_tile(xs_hbm, dest_ref[k * s + t0 + r]), sems.at[k]).start(priority=k)
        return carry

    lax.fori_loop(0, tm // ROW_UNROLL, issue, 0)
    for k in range(TOP_K):
        pltpu.make_async_copy(x_ref, xs_hbm.at[pl.ds(0, tm * SUBLANES), :], sems.at[k]).wait()


def _dispatch_rows(x, dest, n_blocks, *, tm=256):
    s = x.shape[0] // SUBLANES
    return pl.pallas_call(
        functools.partial(_dispatch_kernel, tm=tm, s=s),
        grid_spec=pltpu.PrefetchScalarGridSpec(
            num_scalar_prefetch=1,
            grid=(s // tm,),
            in_specs=[pl.BlockSpec((tm * SUBLANES, LANES), lambda i, dest: (i, 0))],
            out_specs=pl.BlockSpec(memory_space=pl.ANY),
            scratch_shapes=[pltpu.SemaphoreType.DMA((TOP_K,))]),
        out_shape=jax.ShapeDtypeStruct((n_blocks * MOE_ROWS * SUBLANES, LANES), x.dtype),
        compiler_params=_params(("arbitrary",), 32),
        name="moe_dispatch",
    )(dest, x)


FFN_K_CHUNK = 512


def _ffn_kernel(nused_ref, run_ref, first_ref, rexp_ref, nruns_ref, xs_ref, w1_hbm, w3_hbm, w2_hbm, ys_ref,
                w1_buf, w3_buf, w2_buf, sems, *, layer):
    b = pl.program_id(0)

    def weight_copies(run, slot):
        e = rexp_ref[run]
        return [pltpu.make_async_copy(w_hbm.at[layer, e], buf.at[slot], sems.at[slot, j])
                for j, (w_hbm, buf) in enumerate(((w1_hbm, w1_buf), (w3_hbm, w3_buf), (w2_hbm, w2_buf)))]

    @pl.when(b == 0)
    def _():
        for c in weight_copies(0, 0):
            c.start()

    @pl.when(b < nused_ref[0])
    def _():
        run = run_ref[b]
        slot = run % 2

        @pl.when(first_ref[b] == 1)
        def _():
            @pl.when(run + 1 < nruns_ref[0])
            def _():
                for c in weight_copies(run + 1, 1 - slot):
                    c.start()

            for c in weight_copies(run, slot):
                c.wait()

        x_lo, x_hi = _unpack_bf16_pair(_load_row_tiles(xs_ref))
        hd = x_lo.shape[1]
        x = jnp.concatenate([x_lo.astype(BF16), x_hi.astype(BF16)], axis=1)

        def up(buf):
            acc = None
            for k0 in range(0, 2 * hd, FFN_K_CHUNK):
                part = jnp.dot(x[:, k0:k0 + FFN_K_CHUNK], buf[slot, k0:k0 + FFN_K_CHUNK, :].astype(BF16),
                               preferred_element_type=F32)
                acc = part if acc is None else acc + part
            return acc

        h1 = up(w1_buf)
        h3 = up(w3_buf)
        hmid = (h1 * jax.nn.sigmoid(h1) * h3).astype(BF16)
        y = jnp.dot(hmid, w2_buf[slot].astype(BF16), preferred_element_type=F32)
        _store_row_tiles(ys_ref, _pack_bf16_pair(y[:, :hd], y[:, hd:]))


def _grouped_ffn(xs, w1, w3, w2, layer, block_expert, n_used):
    d = xs.shape[1]
    dm = w1.shape[2]
    de = w1.shape[3]
    nb = block_expert.shape[0]
    first = jnp.concatenate([jnp.ones((1,), jnp.int32), (block_expert[1:] != block_expert[:-1]).astype(jnp.int32)])
    run = jnp.cumsum(first) - 1
    n_runs = run[-1:] + 1
    hit = (run[None, :] == jnp.arange(nb, dtype=jnp.int32)[:, None]) & (first[None, :] == 1)
    run_expert = jnp.sum(jnp.where(hit, block_expert[None, :], 0), axis=1).astype(jnp.int32)
    blk = lambda b, nu, *_: (jnp.minimum(b, nu[0] - 1), 0)
    return pl.pallas_call(
        functools.partial(_ffn_kernel, layer=layer),
        grid_spec=pltpu.PrefetchScalarGridSpec(
            num_scalar_prefetch=5,
            grid=(nb,),
            in_specs=[pl.BlockSpec((MOE_ROWS * SUBLANES, d), blk),
                      pl.BlockSpec(memory_space=pl.ANY),
                      pl.BlockSpec(memory_space=pl.ANY),
                      pl.BlockSpec(memory_space=pl.ANY)],
            out_specs=pl.BlockSpec((MOE_ROWS * SUBLANES, d), blk),
            scratch_shapes=[pltpu.VMEM((2, dm, de), F32), pltpu.VMEM((2, dm, de), F32),
                            pltpu.VMEM((2, de, dm), F32), pltpu.SemaphoreType.DMA((2, 3))]),
        out_shape=jax.ShapeDtypeStruct(xs.shape, jnp.uint32),
        compiler_params=_params(("arbitrary",), 56),
        name="moe_ffn",
    )(n_used, run.astype(jnp.int32), first, run_expert, n_runs.astype(jnp.int32), xs, w1, w3, w2)


def _combine_ple_kernel(pos_ref, ys_hbm, x1_ref, route_ref, g_ref, b_ref, wg_ref, bg_ref, p_ref, wp_ref,
                        o_ref, ob_ref, ybuf_a, ybuf_b, wg_bf, wp_bf, sems, *, tm, s):
    i = pl.program_id(0)
    n = pl.num_programs(0)

    def start_row(t0, buf, buf_id, r):
        first = r * SUBLANES if isinstance(r, int) else pl.multiple_of(r * SUBLANES, SUBLANES)
        for k in range(TOP_K):
            pltpu.make_async_copy(_row_tile(ys_hbm, pos_ref[k * s + t0 + r]),
                                  buf.at[k, pl.ds(first, SUBLANES), :], sems.at[buf_id, k]).start(priority=k)

    def wait_buffer(buf, buf_id):
        span = ys_hbm.at[pl.ds(0, tm * SUBLANES), :]
        for k in range(TOP_K):
            pltpu.make_async_copy(span, buf.at[k], sems.at[buf_id, k]).wait()

    @pl.when(i == 0)
    def _():
        def issue(g, carry):
            for u in range(ROW_UNROLL):
                start_row(0, ybuf_a, 0, g * ROW_UNROLL + u)
            return carry

        lax.fori_loop(0, tm // ROW_UNROLL, issue, 0)
        chunk = wg_ref.shape[0] // 8
        for r0 in range(0, wg_ref.shape[0], chunk):
            wg_bf[r0:r0 + chunk, :] = wg_ref[r0:r0 + chunk, :].astype(BF16)
        wp_bf[...] = wp_ref[...].astype(BF16)

    def step(cur, cur_id, nxt_buf, nxt_id):
        wait_buffer(cur, cur_id)
        nxt = jnp.minimum(i + 1, n - 1) * tm
        for r in range(tm):
            start_row(nxt, nxt_buf, nxt_id, r)

        route = route_ref[...]
        f = None
        for k in range(TOP_K):
            lo, hi = _unpack_bf16_pair(_load_row_tiles(cur.at[k]))
            fk = route[:, TOP_K + k:TOP_K + k + 1] * jnp.concatenate([lo, hi], axis=1)
            f = fk if f is None else f + fk
        x2 = _layer_norm_rows(ALPHA * x1_ref[...] + f, g_ref[...], b_ref[...])
        gate = jax.nn.sigmoid(jnp.dot(x2.astype(BF16), wg_bf[...], preferred_element_type=F32) + bg_ref[...])
        emb = jnp.dot(p_ref[...].astype(BF16), wp_bf[...], preferred_element_type=F32)
        out = x2 + gate * emb
        o_ref[...] = out
        ob_ref[...] = out.astype(BF16)

        @pl.when(i == n - 1)
        def _():
            wait_buffer(nxt_buf, nxt_id)

    @pl.when(i % 2 == 0)
    def _():
        step(ybuf_a, 0, ybuf_b, 1)

    @pl.when(i % 2 == 1)
    def _():
        step(ybuf_b, 1, ybuf_a, 0)


def _combine_ple(ys, pos, x1, route, ln_g, ln_b, w_gate, b_gate, p, w_proj, layer, *, tm=256):
    s, d = x1.shape
    pd = p.shape[1]
    rows = lambda n: pl.BlockSpec((tm, n), lambda i, pos: (i, 0))
    full = lambda a: pl.BlockSpec(a.shape, lambda i, pos: (0,) * a.ndim, pipeline_mode=pl.Buffered(1))
    of_layer = lambda a: pl.BlockSpec((None,) + a.shape[1:], lambda i, pos: (layer, 0, 0),
                                      pipeline_mode=pl.Buffered(1))
    g2, b2, bg = ln_g.reshape(1, d), ln_b.reshape(1, d), b_gate.reshape(1, d)
    return pl.pallas_call(
        functools.partial(_combine_ple_kernel, tm=tm, s=s),
        grid_spec=pltpu.PrefetchScalarGridSpec(
            num_scalar_prefetch=1,
            grid=(s // tm,),
            in_specs=[pl.BlockSpec(memory_space=pl.ANY), rows(d), rows(ROUTE_LANES), full(g2), full(b2),
                      of_layer(w_gate), full(bg), rows(pd), of_layer(w_proj)],
            out_specs=[rows(d), rows(d)],
            scratch_shapes=[pltpu.VMEM((TOP_K, tm * SUBLANES, LANES), jnp.uint32),
                            pltpu.VMEM((TOP_K, tm * SUBLANES, LANES), jnp.uint32),
                            pltpu.VMEM(w_gate.shape[1:], BF16), pltpu.VMEM(w_proj.shape[1:], BF16),
                            pltpu.SemaphoreType.DMA((2, TOP_K))]),
        out_shape=[jax.ShapeDtypeStruct((s, d), F32), jax.ShapeDtypeStruct((s, d), BF16)],
        compiler_params=_params(("arbitrary",), 56),
        name="moe_combine_ple",
    )(pos, ys, x1, route, g2, b2, w_gate, bg, p, w_proj)


def _rglru_kernel(cur_ref, prev_ref, gate_ref, cw_ref, cb_ref, wa_ref, ba_ref, wi_ref, bi_ref, lam_ref,
                  o_ref, buf_ref, a_ref, b_ref, h_ref, carry_ref, *, tt, tc, width):
    i = pl.program_id(1)

    @pl.when(i == 0)
    def _():
        carry_ref[...] = jnp.zeros(carry_ref.shape, F32)

    buf_ref[0:RNN_HALO, :] = jnp.where(i > 0, prev_ref[...], 0.0)
    buf_ref[RNN_HALO:, :] = cur_ref[...]
    off = RNN_HALO - (width - 1)
    z = -lam_ref[...]
    softplus = jnp.log1p(jnp.exp(-jnp.abs(z))) + jnp.maximum(z, 0.0)

    def sigmoid(v):
        return 0.5 * jnp.tanh(0.5 * v) + 0.5

    for hh in range(tc // RNN_HEAD_DIM):
        cols = slice(hh * RNN_HEAD_DIM, (hh + 1) * RNN_HEAD_DIM)
        xh = jnp.broadcast_to(cb_ref[:, cols], (tt, RNN_HEAD_DIM))
        for k in range(width):
            xh = xh + buf_ref[off + k:off + k + tt, cols] * cw_ref[k:k + 1, cols]
        xhb = xh.astype(BF16)
        r = sigmoid(jnp.dot(xhb, wa_ref[hh], preferred_element_type=F32) + ba_ref[:, cols])
        ig = sigmoid(jnp.dot(xhb, wi_ref[hh], preferred_element_type=F32) + bi_ref[:, cols])
        log_a = (-RG_C) * r * softplus[:, cols]
        a = jnp.exp(log_a)
        bval = jnp.sqrt(-jnp.tanh(log_a) * (a * a + 1.0)) * (ig * xh)
        a_ref[:, cols] = a
        b_ref[:, cols] = bval

    def group(gi, h):
        base = pl.multiple_of(gi * 8, 8)
        for r in range(8):
            row = pl.ds(base + r, 1)
            h = a_ref[row, :] * h + b_ref[row, :]
            h_ref[row, :] = h
        return h

    carry_ref[...] = lax.fori_loop(0, tt // 8, group, carry_ref[...])
    o_ref[...] = (gate_ref[...].astype(F32) * h_ref[...]).astype(o_ref.dtype)


def _rglru(xr, gate, conv_w, conv_b, w_a, b_a, w_i, b_i, lam, *, tt=256, tc=2048):
    s, wd = xr.shape
    width = conv_w.shape[0]
    hpc = tc // RNN_HEAD_DIM
    hb = tt // RNN_HALO
    row = lambda v: v.reshape(1, wd)
    cvec = lambda: pl.BlockSpec((1, tc), lambda c, i: (0, c))
    heads = lambda: pl.BlockSpec((hpc, RNN_HEAD_DIM, RNN_HEAD_DIM), lambda c, i: (c, 0, 0))
    tile = lambda: pl.BlockSpec((tt, tc), lambda c, i: (i, c))
    return pl.pallas_call(
        functools.partial(_rglru_kernel, tt=tt, tc=tc, width=width),
        grid=(wd // tc, s // tt),
        in_specs=[tile(),
                  pl.BlockSpec((RNN_HALO, tc), lambda c, i: (jnp.maximum(i * hb - 1, 0), c)),
                  tile(),
                  pl.BlockSpec((width, tc), lambda c, i: (0, c)),
                  cvec(), heads(), cvec(), heads(), cvec(), cvec()],
        out_specs=tile(),
        out_shape=jax.ShapeDtypeStruct((s, wd), BF16),
        scratch_shapes=[pltpu.VMEM((tt + RNN_HALO, tc), F32), pltpu.VMEM((tt, tc), F32),
                        pltpu.VMEM((tt, tc), F32), pltpu.VMEM((tt, tc), F32), pltpu.VMEM((1, tc), F32)],
        compiler_params=_params(("parallel", "arbitrary"), 32),
        name="rglru",
    )(xr, xr, gate, conv_w, row(conv_b), w_a.astype(BF16), row(b_a), w_i.astype(BF16), row(b_i), row(lam))


def _moe_and_ple(mixed, i, p, ln_ffn_g, ln_ffn_b, moe_w1, moe_w3, moe_w2, ple_w_proj, ple_w_gate, ple_b_gate):
    x1, x1_pairs, route = mixed
    dest, block_expert, n_used = _moe_plan(route)
    xs = _dispatch_rows(x1_pairs, dest, block_expert.shape[0])
    ys = _grouped_ffn(xs, moe_w1, moe_w3, moe_w2, i, block_expert, n_used)
    return _combine_ple(ys, dest, x1, route, ln_ffn_g[i], ln_ffn_b[i], ple_w_gate, ple_b_gate[i], p[i, 0],
                        ple_w_proj, i)


def kernel(x, p, positions, ev_w_in, ev_conv_w, ev_conv_b, ev_cnorm_g, ev_cnorm_b, ev_qnorm_g, ev_w_uq, ev_kvnorm_g, ev_w_ukv, ev_w_out, od_w_in, od_conv_w, od_conv_b, od_w_a, od_b_a, od_w_i, od_b_i, od_lam, od_w_out, ln_mix_g, ln_mix_b, ln_ffn_g, ln_ffn_b, moe_w_grp, moe_b_grp, moe_w_exp, moe_b_exp, moe_w1, moe_w3, moe_w2, ple_w_proj, ple_w_gate, ple_b_gate):
    x0 = x[0]
    moe =functools.partial(_moe_and_ple, p=p, ln_ffn_g=ln_ffn_g, ln_ffn_b=ln_ffn_b, moe_w1=moe_w1,
                            moe_w3=moe_w3, moe_w2=moe_w2, ple_w_proj=ple_w_proj, ple_w_gate=ple_w_gate,
                            ple_b_gate=ple_b_gate)

    cosm, sinm = _rope_tables(positions[0])
    x0b, qn, qr, kn, kr, v = _mla_proj(x0, ev_w_in[0, :, 2 * CONV_CH:], ev_qnorm_g[0], ev_w_uq[0], ev_kvnorm_g[0],
                                       ev_w_ukv[0], cosm, sinm)
    u = _pair_proj(x0b, ev_w_in.reshape(D_MODEL, -1), CONV_CH, mode="glu")
    ub = _conv_gln(u, ev_conv_w[0], ev_conv_b[0], ev_cnorm_g[0], ev_cnorm_b[0], groups=CONV_GROUPS)
    att = _attention(qn, qr, kn, kr, v)
    mixed0 = _out_ln_route(ub, att, ev_w_out[0].astype(BF16), x0, ln_mix_g[0], ln_mix_b[0],
                           moe_w_grp[0], moe_b_grp[0], moe_w_exp[0], moe_b_exp[0])
    x3, x3b = moe(mixed0, 0)

    gate, xr = _pair_proj(x3b, od_w_in.reshape(D_MODEL, -1), RNN_WIDTH, mode="rnn")
    y = _rglru(xr, gate, od_conv_w[0], od_conv_b[0], od_w_a[0], od_b_a[0], od_w_i[0], od_b_i[0], od_lam[0])
    mixed1 = _out_ln_route(y, y, od_w_out[0].astype(BF16), x3, ln_mix_g[1], ln_mix_b[1],
                           moe_w_grp[1], moe_b_grp[1], moe_w_exp[1], moe_b_exp[1], col1=0, col2=1)
    x6, _ = moe(mixed1, 1)
    return x6[None]
```

```python
import functools
import math

import jax
import jax.numpy as jnp
from jax import lax
from jax.experimental import pallas as pl
from jax.experimental.pallas import tpu as pltpu

F32 = jnp.float32
BF16 = jnp.bfloat16

D_MODEL = 2048
SEQ = 8192
DEPTH = 2
CONV_CH = 1024
CONV_GROUPS = 16
CONV_WIDTH = 31
MLA_HEADS = 8
QK_NOPE = 128
QK_ROPE = 64
V_HEAD = 128
Q_RANK = 512
KV_RANK = 256
ROPE_THETA = 10000.0
RNN_WIDTH = 2048
RNN_HEADS = 16
RNN_HEAD_DIM = RNN_WIDTH // RNN_HEADS
RNN_CONV_WIDTH = 4
RG_C = 8.0
N_GROUPS = 8
EXPERTS_PER_GROUP = 8
N_EXPERTS = N_GROUPS * EXPERTS_PER_GROUP
TOP_K = 2
D_EXPERT = 512
PLE_DIM = 256
ALPHA = (2 * DEPTH) ** 0.25
LN_EPS = 1e-5
RMS_EPS = 1e-6

LANES = 128
SUBLANES = 8
CONV_HALO = 32
RNN_HALO = 8
MOE_ROWS = 256
ROUTE_LANES = 128
MIB = 2 ** 20


def _params(semantics, vmem_mib):
    return pltpu.CompilerParams(dimension_semantics=semantics, vmem_limit_bytes=vmem_mib * MIB)


def _layer_norm_rows(z, g, b):
    mu = jnp.mean(z, axis=-1, keepdims=True)
    d = z - mu
    var = jnp.mean(d * d, axis=-1, keepdims=True)
    return d * lax.rsqrt(var + LN_EPS) * g + b


def _pack_bf16_pair(a, b):
    def rounded_bits(v):
        u = lax.bitcast_convert_type(v, jnp.uint32)
        return u + jnp.uint32(0x7FFF) + ((u >> 16) & jnp.uint32(1))

    return (rounded_bits(a) & jnp.uint32(0xFFFF0000)) | (rounded_bits(b) >> 16)


def _unpack_bf16_pair(u):
    a = lax.bitcast_convert_type(u & jnp.uint32(0xFFFF0000), F32)
    b = lax.bitcast_convert_type(u << 16, F32)
    return a, b


def _store_row_tiles(ref, packed):
    n, w = packed.shape
    assert w == SUBLANES * LANES
    for j in range(SUBLANES):
        ref[pl.ds(j, n, stride=SUBLANES), :] = packed[:, j * LANES:(j + 1) * LANES]


def _load_row_tiles(ref):
    n = ref.shape[0] // SUBLANES
    return jnp.concatenate([ref[pl.ds(j, n, stride=SUBLANES), :] for j in range(SUBLANES)], axis=1)


def _split_bf16(v):
    hi = v.astype(BF16)
    lo = (v - hi.astype(F32)).astype(BF16)
    return hi, lo


def _gelu_tanh(x):
    c = math.sqrt(2.0 / math.pi)
    return 0.5 * x * (1.0 + jnp.tanh(c * (x + 0.044715 * (x * x * x))))


def _pair_proj_kernel(x_ref, wa_ref, wb_ref, *refs, mode, transposed):
    out_refs, (wa_bf, wb_bf) = refs[:-2], refs[-2:]

    @pl.when(pl.program_id(1) == 0)
    def _():
        for src, dst in ((wa_ref, wa_bf), (wb_ref, wb_bf)):
            w = src[...]
            dst[...] = (w.T if transposed else w).astype(BF16)

    x = x_ref[...]
    a = jnp.dot(x, wa_bf[...], preferred_element_type=F32)
    b = jnp.dot(x, wb_bf[...], preferred_element_type=F32)
    if mode == "glu":
        out_refs[0][...] = a * jax.nn.sigmoid(b)
    else:
        out_refs[0][...] = _gelu_tanh(a).astype(out_refs[0].dtype)
        out_refs[1][...] = b


def _pair_proj(x, w, n, *, mode, transposed=False, tm=512, tn=512):
    m, k = x.shape
    nb = n // tn
    tile = pl.BlockSpec((tm, tn), lambda j, i: (i, j))
    if transposed:
        w_specs = [pl.BlockSpec((tn, k), lambda j, i: (j, 0)), pl.BlockSpec((tn, k), lambda j, i: (j + nb, 0))]
    else:
        w_specs = [pl.BlockSpec((k, tn), lambda j, i: (0, j)), pl.BlockSpec((k, tn), lambda j, i: (0, j + nb))]
    if mode == "glu":
        out_specs, out_shape = tile, jax.ShapeDtypeStruct((m, n), F32)
    else:
        out_specs = [tile, tile]
        out_shape = [jax.ShapeDtypeStruct((m, n), BF16), jax.ShapeDtypeStruct((m, n), F32)]
    return pl.pallas_call(
        functools.partial(_pair_proj_kernel, mode=mode, transposed=transposed),
        grid=(nb, m // tm),
        in_specs=[pl.BlockSpec((tm, k), lambda j, i: (i, 0))] + w_specs,
        out_specs=out_specs,
        out_shape=out_shape,
        scratch_shapes=[pltpu.VMEM((k, tn), BF16), pltpu.VMEM((k, tn), BF16)],
        compiler_params=_params(("parallel", "arbitrary"), 48),
        name=mode + "_proj",
    )(x, w, w)


def _conv_gln_kernel(cur_ref, prev_ref, w_ref, b_ref, g_ref, beta_ref, gm_ref, o_ref,
                     buf_ref, sh_ref, y_ref, *, tt, ch, width, rc, cc):
    i = pl.program_id(0)
    buf_ref[0:CONV_HALO, :] = jnp.where(i > 0, prev_ref[...], 0.0)
    buf_ref[CONV_HALO:, :] = cur_ref[...]
    off = CONV_HALO - (width - 1)
    sh_rows = sh_ref.shape[1]
    step = 7 * SUBLANES
    assert sh_rows % step == 0
    for b in range(1, SUBLANES):
        for c0 in range(0, ch, cc):
            for r0 in range(0, sh_rows, step):
                sh_ref[b - 1, r0:r0 + step, c0:c0 + cc] = buf_ref[r0 + b:r0 + b + step, c0:c0 + cc]
    for c0 in range(0, ch, cc):
        for r0 in range(0, tt, rc):
            acc = jnp.broadcast_to(b_ref[:, c0:c0 + cc], (rc, cc))
            for k in range(width):
                q, b = divmod(off + k, SUBLANES)
                lo = r0 + q * SUBLANES
                src = buf_ref[lo:lo + rc, c0:c0 + cc] if b == 0 else sh_ref[b - 1, lo:lo + rc, c0:c0 + cc]
                acc = acc + src * w_ref[k:k + 1, c0:c0 + cc]
            y_ref[r0:r0 + rc, c0:c0 + cc] = acc
    gm = gm_ref[...]

    def seg_mean(v):
        hi, lo = _split_bf16(v)
        return (jnp.dot(hi, gm, preferred_element_type=F32)
                + jnp.dot(lo, gm, preferred_element_type=F32))

    for c0 in range(0, ch, LANES):
        y = y_ref[:, c0:c0 + LANES]
        d = y - seg_mean(y)
        var = seg_mean(d * d)
        z = d * lax.rsqrt(var + LN_EPS) * g_ref[:, c0:c0 + LANES] + beta_ref[:, c0:c0 + LANES]
        o_ref[:, c0:c0 + LANES] = (z * jax.nn.sigmoid(z)).astype(o_ref.dtype)


def _conv_gln(u, w, b, g, beta, *, groups, tt=256):
    s, ch = u.shape
    width = w.shape[0]
    gsz = ch // groups
    assert LANES % gsz == 0 and width - 1 <= CONV_HALO and tt % CONV_HALO == 0
    wpad = jnp.zeros((CONV_HALO, ch), F32).at[:width].set(w)
    lane = jnp.arange(LANES)
    gm = jnp.where((lane[:, None] // gsz) == (lane[None, :] // gsz), 1.0 / gsz, 0.0).astype(BF16)
    hb = tt // CONV_HALO
    row = lambda v: v.reshape(1, ch)
    return pl.pallas_call(
        functools.partial(_conv_gln_kernel, tt=tt, ch=ch, width=width, rc=32, cc=256),
        grid=(s // tt,),
        in_specs=[pl.BlockSpec((tt, ch), lambda i: (i, 0)),
                  pl.BlockSpec((CONV_HALO, ch), lambda i: (jnp.maximum(i * hb - 1, 0), 0)),
                  pl.BlockSpec((CONV_HALO, ch), lambda i: (0, 0)),
                  pl.BlockSpec((1, ch), lambda i: (0, 0)),
                  pl.BlockSpec((1, ch), lambda i: (0, 0)),
                  pl.BlockSpec((1, ch), lambda i: (0, 0)),
                  pl.BlockSpec((LANES, LANES), lambda i: (0, 0))],
        out_specs=pl.BlockSpec((tt, ch), lambda i: (i, 0)),
        out_shape=jax.ShapeDtypeStruct((s, ch), BF16),
        scratch_shapes=[pltpu.VMEM((tt + CONV_HALO, ch), F32),
                        pltpu.VMEM((SUBLANES - 1, tt + CONV_HALO - SUBLANES, ch), F32),
                        pltpu.VMEM((tt, ch), F32)],
        compiler_params=_params(("parallel",), 40),
        name="conv_gln",
    )(u, u, wpad, row(b), row(g), row(beta), gm)


def _rope_table_kernel(pos_ref, invf_ref, cos_ref, sin_ref):
    ang = pos_ref[...].astype(F32) * invf_ref[...]
    cos_ref[...] = jnp.cos(ang)
    sin_ref[...] = jnp.sin(ang)


def _rope_tables(positions):
    s = positions.shape[0]
    half = QK_ROPE // 2
    per_row = LANES // half
    inv_freq = 1.0 / (ROPE_THETA ** (jnp.arange(0, QK_ROPE, 2, dtype=F32) / QK_ROPE))
    pos_rep = jnp.repeat(positions.reshape(s // per_row, per_row), half, axis=1)
    invf = jnp.tile(inv_freq, per_row).reshape(1, LANES)
    cos, sin = pl.pallas_call(
        _rope_table_kernel,
        out_shape=[jax.ShapeDtypeStruct((s // per_row, LANES), F32)] * 2,
        name="rope_table",
    )(pos_rep, invf)
    cos = cos.reshape(s, half)
    sin = sin.reshape(s, half)
    zero = jnp.zeros((s, LANES - QK_ROPE), F32)
    return jnp.concatenate([cos, cos, zero], axis=1), jnp.concatenate([-sin, sin, zero], axis=1)


def _mla_proj_kernel(x_ref, wm_ref, qg_ref, kvg_ref, wq_ref, wkv_ref, c_ref, s_ref,
                     xb_ref, qn_ref, qr_ref, kn_ref, kr_ref, v_ref, *, scale):
    nq = MLA_HEADS * QK_NOPE
    xb = x_ref[...].astype(BF16)
    xb_ref[...] = xb
    c = jnp.dot(xb, wm_ref[...], preferred_element_type=F32)
    cq = c[:, :Q_RANK]
    ckv = c[:, Q_RANK:Q_RANK + KV_RANK]
    kr_pad = c[:, Q_RANK + KV_RANK:Q_RANK + KV_RANK + LANES]
    kr_rot = c[:, Q_RANK + KV_RANK + LANES:]
    cqn = cq * lax.rsqrt(jnp.mean(cq * cq, axis=-1, keepdims=True) + RMS_EPS) * qg_ref[...]
    ckvn = ckv * lax.rsqrt(jnp.mean(ckv * ckv, axis=-1, keepdims=True) + RMS_EPS) * kvg_ref[...]
    q = jnp.dot(cqn.astype(BF16), wq_ref[...], preferred_element_type=F32)
    kv = jnp.dot(ckvn.astype(BF16), wkv_ref[...], preferred_element_type=F32)
    cosm = c_ref[...]
    sinm = s_ref[...]
    qn_ref[...] = (q[:, :nq] * scale).astype(BF16)
    for h in range(MLA_HEADS):
        lo = nq + h * LANES
        rot = q[:, lo:lo + LANES] * cosm + q[:, lo + nq:lo + nq + LANES] * sinm
        qr_ref[:, h * LANES:(h + 1) * LANES] = (rot * scale).astype(BF16)
    kn_ref[...] = kv[:, :nq].astype(BF16)
    v_ref[...] = kv[:, nq:].astype(BF16)
    kr_ref[...] = (kr_pad * cosm + kr_rot * sinm).astype(BF16)


def _rope_swap(w):
    half = QK_ROPE // 2
    return jnp.concatenate([w[..., half:], w[..., :half]], axis=-1)


def _mla_proj(x, w_in_mla, qnorm_g, w_uq, kvnorm_g, w_ukv, cosm, sinm, *, tm=256):
    s, d = x.shape
    nq = MLA_HEADS * QK_NOPE
    pad = lambda w: jnp.concatenate([w, jnp.zeros(w.shape[:-1] + (LANES - QK_ROPE,), w.dtype)], axis=-1)
    w_kr = w_in_mla[:, Q_RANK + KV_RANK:]
    wm = jnp.concatenate([w_in_mla[:, :Q_RANK + KV_RANK], pad(w_kr), pad(_rope_swap(w_kr))], axis=1).astype(BF16)
    wq3 = w_uq.reshape(Q_RANK, MLA_HEADS, QK_NOPE + QK_ROPE)
    wq_rope = wq3[:, :, QK_NOPE:]
    wq = jnp.concatenate([wq3[:, :, :QK_NOPE].reshape(Q_RANK, nq),
                          pad(wq_rope).reshape(Q_RANK, MLA_HEADS * LANES),
                          pad(_rope_swap(wq_rope)).reshape(Q_RANK, MLA_HEADS * LANES)], axis=1).astype(BF16)
    wkv3 = w_ukv.reshape(KV_RANK, MLA_HEADS, QK_NOPE + V_HEAD)
    wkv = jnp.concatenate([wkv3[:, :, :QK_NOPE].reshape(KV_RANK, nq),
                           wkv3[:, :, QK_NOPE:].reshape(KV_RANK, MLA_HEADS * V_HEAD)], axis=1).astype(BF16)
    scale = (QK_NOPE + QK_ROPE) ** -0.5 * math.log2(math.e)
    full = lambda a: pl.BlockSpec(a.shape, lambda i: (0,) * a.ndim)
    rows = lambda n: pl.BlockSpec((tm, n), lambda i: (i, 0))
    qg = qnorm_g.reshape(1, Q_RANK)
    kvg = kvnorm_g.reshape(1, KV_RANK)
    return pl.pallas_call(
        functools.partial(_mla_proj_kernel, scale=scale),
        grid=(s // tm,),
        in_specs=[rows(d), full(wm), full(qg), full(kvg), full(wq), full(wkv), rows(LANES), rows(LANES)],
        out_specs=[rows(d), rows(nq), rows(MLA_HEADS * LANES), rows(nq), rows(LANES), rows(MLA_HEADS * V_HEAD)],
        out_shape=[jax.ShapeDtypeStruct((s, d), BF16),
                   jax.ShapeDtypeStruct((s, nq), BF16),
                   jax.ShapeDtypeStruct((s, MLA_HEADS * LANES), BF16),
                   jax.ShapeDtypeStruct((s, nq), BF16),
                   jax.ShapeDtypeStruct((s, LANES), BF16),
                   jax.ShapeDtypeStruct((s, MLA_HEADS * V_HEAD), BF16)],
        compiler_params=_params(("parallel",), 48),
        name="mla_proj",
    )(x, wm, qg, kvg, wq, wkv, cosm, sinm)


def _attn_kernel(qn_ref, qr_ref, kn_ref, kr_ref, v_ref, o_ref, m_ref, acc_ref, *, tq, tk, hb):
    qi = pl.program_id(1)
    per_q = tq // tk
    m_ref[...] = jnp.full(m_ref.shape, -jnp.inf, F32)
    acc_ref[...] = jnp.zeros(acc_ref.shape, F32)
    ones = jnp.ones((tk, LANES), BF16)
    head_cols = [slice(h * LANES, (h + 1) * LANES) for h in range(hb)]
    qs = [jnp.concatenate([qn_ref[:, c], qr_ref[:, c]], axis=1) for c in head_cols]

    def chunk(j, diag):
        rows = pl.ds(pl.multiple_of(j * tk, tk), tk)
        k_rope = kr_ref[rows, :]
        q0 = 0 if diag is None else diag * tk
        for h, c in enumerate(head_cols):
            k = jnp.concatenate([kn_ref[rows, c], k_rope], axis=1)
            s = lax.dot_general(qs[h][q0:], k, (((1,), (1,)), ((), ())), preferred_element_type=F32)
            if diag is not None:
                keep = lax.broadcasted_iota(jnp.int32, s.shape, 0) >= lax.broadcasted_iota(jnp.int32, s.shape, 1)
                s = jnp.where(keep, s, -jnp.inf)
            m_old = m_ref[h, q0:, :]
            m_new = jnp.maximum(m_old, jnp.max(s, axis=-1, keepdims=True))
            alpha = jnp.exp2(m_old - m_new)
            p = jnp.exp2(s - jnp.concatenate([m_new] * (tk // LANES), axis=1))
            v_ext = jnp.concatenate([v_ref[rows, c], ones], axis=1)
            acc_ref[h, q0:, :] = (jnp.concatenate([alpha, alpha], axis=1) * acc_ref[h, q0:, :]
                                  + jnp.dot(p.astype(BF16), v_ext, preferred_element_type=F32))
            m_ref[h, q0:, :] = m_new

    def body(j, carry):
        chunk(j, None)
        return carry

    lax.fori_loop(0, qi * per_q, body, 0)
    for d in range(per_q):
        chunk(qi * per_q + d, d)
    for h, c in enumerate(head_cols):
        acc = acc_ref[h]
        o_ref[:, c] = (acc[:, :V_HEAD] / acc[:, V_HEAD:]).astype(o_ref.dtype)


def _attention(qn, qr, kn, kr, v, *, tq=1024, tk=512, hb=4):
    s = qn.shape[0]
    wide = hb * LANES
    tile = lambda: pl.BlockSpec((tq, wide), lambda h, i: (i, h))
    keys = lambda: pl.BlockSpec((s, wide), lambda h, i: (0, h), pipeline_mode=pl.Buffered(1))
    return pl.pallas_call(
        functools.partial(_attn_kernel, tq=tq, tk=tk, hb=hb),
        grid=(MLA_HEADS // hb, s // tq),
        in_specs=[tile(), tile(), keys(),
                  pl.BlockSpec((s, LANES), lambda h, i: (0, 0), pipeline_mode=pl.Buffered(1)), keys()],
        out_specs=tile(),
        out_shape=jax.ShapeDtypeStruct((s, MLA_HEADS * V_HEAD), BF16),
        scratch_shapes=[pltpu.VMEM((hb, tq, LANES), F32), pltpu.VMEM((hb, tq, 2 * V_HEAD), F32)],
        compiler_params=_params(("parallel", "parallel"), 48),
        name="mla_attention",
    )(qn, qr, kn, kr, v)


def _route(logits):
    lane = lax.broadcasted_iota(jnp.int32, logits.shape, 1).astype(F32)
    big = float(2 * ROUTE_LANES)
    neg = -jnp.inf
    gl = jnp.where(lane < N_GROUPS, logits, neg)
    gmax = jnp.max(gl, axis=-1, keepdims=True)
    gsum = jnp.sum(jnp.exp(gl - gmax), axis=-1, keepdims=True)
    g_prob = 1.0 / gsum
    g_idx = jnp.min(jnp.where(gl == gmax, lane, big), axis=-1, keepdims=True)
    lo = N_GROUPS + g_idx * EXPERTS_PER_GROUP
    el = jnp.where((lane >= lo) & (lane < lo + EXPERTS_PER_GROUP), logits, neg)
    emax = jnp.max(el, axis=-1, keepdims=True)
    esum = jnp.sum(jnp.exp(el - emax), axis=-1, keepdims=True)
    i0 = jnp.min(jnp.where(el == emax, lane, big), axis=-1, keepdims=True)
    el2 = jnp.where(lane == i0, neg, el)
    emax2 = jnp.max(el2, axis=-1, keepdims=True)
    i1 = jnp.min(jnp.where(el2 == emax2, lane, big), axis=-1, keepdims=True)
    p0 = 1.0 / esum
    p1 = jnp.exp(emax2 - emax) / esum
    g0 = g_prob * p0 / (p0 + p1)
    g1 = g_prob * p1 / (p0 + p1)
    e0 = i0 - N_GROUPS
    e1 = i1 - N_GROUPS
    return jnp.where(lane == 0, e0, jnp.where(lane == 1, e1, jnp.where(lane == 2, g0, jnp.where(lane == 3, g1, 0.0))))


def _out_ln_route_kernel(a1_ref, a2_ref, w_ref, x_ref, g_ref, b_ref, wrh_ref, br_ref,
                         x1_ref, x1p_ref, route_ref, *, sub):
    half = a1_ref.shape[1]
    for r0 in range(0, x_ref.shape[0], sub):
        rows = slice(r0, r0 + sub)
        m = (jnp.dot(a1_ref[rows, :], w_ref[:half, :], preferred_element_type=F32)
             + jnp.dot(a2_ref[rows, :], w_ref[half:, :], preferred_element_type=F32))
        x1 = _layer_norm_rows(ALPHA * x_ref[rows, :] + m, g_ref[...], b_ref[...])
        x1_ref[rows, :] = x1
        hd = x1.shape[1] // 2
        _store_row_tiles(x1p_ref.at[pl.ds(r0 * SUBLANES, sub * SUBLANES), :], _pack_bf16_pair(x1[:, :hd], x1[:, hd:]))
        xh, xl = _split_bf16(x1)
        both = jnp.dot(xh, wrh_ref[...], preferred_element_type=F32)
        logits = (both[:, :ROUTE_LANES] + both[:, ROUTE_LANES:]
                  + jnp.dot(xl, wrh_ref[:, :ROUTE_LANES], preferred_element_type=F32) + br_ref[...])
        route_ref[rows, :] = _route(logits)


def _out_ln_route(a1, a2, w, x, ln_g, ln_b, w_grp, b_grp, w_exp, b_exp, *, col1=0, col2=0, tm=512, sub=512):
    s, d = x.shape
    half = w.shape[0] // 2
    pad = ROUTE_LANES - N_GROUPS - N_EXPERTS
    wr = jnp.concatenate([w_grp, w_exp, jnp.zeros((d, pad), F32)], axis=1)
    br = jnp.concatenate([b_grp, b_exp, jnp.zeros((pad,), F32)]).reshape(1, ROUTE_LANES)
    wr_hi = wr.astype(BF16)
    wrh = jnp.concatenate([wr_hi, (wr - wr_hi.astype(F32)).astype(BF16)], axis=1)
    full = lambda a: pl.BlockSpec(a.shape, lambda i: (0,) * a.ndim)
    rows = lambda n: pl.BlockSpec((tm, n), lambda i: (i, 0))
    g2 = ln_g.reshape(1, d)
    b2 = ln_b.reshape(1, d)
    return pl.pallas_call(
        functools.partial(_out_ln_route_kernel, sub=sub),
        grid=(s // tm,),
        in_specs=[pl.BlockSpec((tm, half), lambda i: (i, col1)), pl.BlockSpec((tm, half), lambda i: (i, col2)),
                  pl.BlockSpec(w.shape, lambda i: (0, 0), pipeline_mode=pl.Buffered(1)),
                  rows(d), full(g2), full(b2), full(wrh), full(br)],
        out_specs=[rows(d), pl.BlockSpec((tm * SUBLANES, LANES), lambda i: (i, 0)), rows(ROUTE_LANES)],
        out_shape=[jax.ShapeDtypeStruct((s, d), F32), jax.ShapeDtypeStruct((s * SUBLANES, LANES), jnp.uint32),
                   jax.ShapeDtypeStruct((s, ROUTE_LANES), F32)],
        compiler_params=_params(("parallel",), 48),
        name="out_ln_route",
    )(a1, a2, w, x, g2, b2, wrh, br)


def _rank_kernel(route_ref, tri_ref, rank_ref, cnt_ref, carry_ref):
    @pl.when(pl.program_id(0) == 0)
    def _():
        carry_ref[...] = jnp.zeros(carry_ref.shape, F32)

    route = route_ref[...]
    lane = lax.broadcasted_iota(jnp.int32, route.shape, 1).astype(F32)
    oh0 = jnp.where(lane == route[:, 0:1], 1.0, 0.0)
    oh1 = jnp.where(lane == route[:, 1:2], 1.0, 0.0)
    both = oh0 + oh1
    before = jnp.dot(tri_ref[...], both.astype(BF16), preferred_element_type=F32) + carry_ref[...]
    r0 = jnp.sum(before * oh0, axis=-1, keepdims=True)
    r1 = jnp.sum(before * oh1, axis=-1, keepdims=True)
    rank_ref[...] = jnp.where(lane == 0.0, r0, jnp.where(lane == 1.0, r1, 0.0))
    carry_ref[...] = carry_ref[...] + jnp.sum(both, axis=0, keepdims=True)
    cnt_ref[...] = carry_ref[...]


def _dest_kernel(route_ref, rank_ref, pstart_ref, dest_ref):
    route = route_ref[...]
    rank = rank_ref[...]
    lane = lax.broadcasted_iota(jnp.int32, route.shape, 1).astype(F32)
    ps = pstart_ref[...]
    d0 = jnp.sum(jnp.where(lane == route[:, 0:1], ps, 0.0), axis=-1, keepdims=True) + rank[:, 0:1]
    d1 = jnp.sum(jnp.where(lane == route[:, 1:2], ps, 0.0), axis=-1, keepdims=True) + rank[:, 1:2]
    dest = jnp.where(lane == 0.0, d0, jnp.where(lane == 1.0, d1, 0.0)) * float(SUBLANES)
    dest_ref[...] = dest.astype(jnp.int32)


def _moe_plan(route, *, tb=512):
    s = route.shape[0]
    tri = (jnp.arange(tb)[:, None] > jnp.arange(tb)[None, :]).astype(BF16)
    rows = pl.BlockSpec((tb, ROUTE_LANES), lambda i: (i, 0))
    one = lambda: pl.BlockSpec((1, ROUTE_LANES), lambda i: (0, 0))
    rank, cnt = pl.pallas_call(
        _rank_kernel,
        grid=(s // tb,),
        in_specs=[rows, pl.BlockSpec((tb, tb), lambda i: (0, 0))],
        out_specs=[rows, one()],
        out_shape=[jax.ShapeDtypeStruct((s, ROUTE_LANES), F32), jax.ShapeDtypeStruct((1, ROUTE_LANES), F32)],
        scratch_shapes=[pltpu.VMEM((1, ROUTE_LANES), F32)],
        compiler_params=_params(("arbitrary",), 32),
        name="moe_rank",
    )(route, tri)
    counts = cnt[0, :N_EXPERTS].astype(jnp.int32)
    pcounts = (counts + MOE_ROWS - 1) // MOE_ROWS * MOE_ROWS
    pends = jnp.cumsum(pcounts)
    pstarts = pends - pcounts
    n_blocks = s * TOP_K // MOE_ROWS + N_EXPERTS
    n_used = pends[-1] // MOE_ROWS
    blk = jnp.minimum(jnp.arange(n_blocks, dtype=jnp.int32), n_used - 1) * MOE_ROWS
    block_expert = jnp.minimum(jnp.sum(pends[None, :] <= blk[:, None], axis=1), N_EXPERTS - 1).astype(jnp.int32)
    ps = jnp.zeros((1, ROUTE_LANES), F32).at[0, :N_EXPERTS].set(pstarts.astype(F32))
    dest = pl.pallas_call(
        _dest_kernel,
        grid=(s // tb,),
        in_specs=[rows, rows, one()],
        out_specs=rows,
        out_shape=jax.ShapeDtypeStruct((s, ROUTE_LANES), jnp.int32),
        compiler_params=_params(("parallel",), 32),
        name="moe_dest",
    )(route, rank, ps)
    dest = dest[:, :TOP_K].T.reshape(TOP_K * s)
    return dest, block_expert, n_used.astype(jnp.int32).reshape(1)


ROW_UNROLL = 8


def _row_tile(ref, first_row):
    return ref.at[pl.ds(pl.multiple_of(first_row, SUBLANES), SUBLANES), :]


def _dispatch_kernel(dest_ref, x_ref, xs_hbm, sems, *, tm, s):
    t0 = pl.program_id(0) * tm

    def issue(g, carry):
        for u in range(ROW_UNROLL):
            r = g * ROW_UNROLL + u
            src = _row_tile(x_ref, r * SUBLANES)
            for k in range(TOP_K):
                pltpu.make_async_copy(src, _row_tile(xs_hbm, dest_ref[k * s + t0 + r]), sems.at[k]).start(priority=k)
        return carry

    lax.fori_loop(0, tm // ROW_UNROLL, issue, 0)
    for k in range(TOP_K):
        pltpu.make_async_copy(x_ref, xs_hbm.at[pl.ds(0, tm * SUBLANES), :], sems.at[k]).wait()


def _dispatch_rows(x, dest, n_blocks, *, tm=256):
    s = x.shape[0] // SUBLANES
    return pl.pallas_call(
        functools.partial(_dispatch_kernel, tm=tm, s=s),
        grid_spec=pltpu.PrefetchScalarGridSpec(
            num_scalar_prefetch=1,
            grid=(s // tm,),
            in_specs=[pl.BlockSpec((tm * SUBLANES, LANES), lambda i, dest: (i, 0))],
            out_specs=pl.BlockSpec(memory_space=pl.ANY),
            scratch_shapes=[pltpu.SemaphoreType.DMA((TOP_K,))]),
        out_shape=jax.ShapeDtypeStruct((n_blocks * MOE_ROWS * SUBLANES, LANES), x.dtype),
        compiler_params=_params(("arbitrary",), 32),
        name="moe_dispatch",
    )(dest, x)


FFN_K_CHUNK = 512


def _ffn_kernel(nused_ref, run_ref, first_ref, rexp_ref, nruns_ref, xs_ref, w1_hbm, w3_hbm, w2_hbm, ys_ref,
                w1_buf, w3_buf, w2_buf, sems, *, layer):
    b = pl.program_id(0)

    def weight_copies(run, slot):
        e = rexp_ref[run]
        return [pltpu.make_async_copy(w_hbm.at[layer, e], buf.at[slot], sems.at[slot, j])
                for j, (w_hbm, buf) in enumerate(((w1_hbm, w1_buf), (w3_hbm, w3_buf), (w2_hbm, w2_buf)))]

    @pl.when(b == 0)
    def _():
        for c in weight_copies(0, 0):
            c.start()

    @pl.when(b < nused_ref[0])
    def _():
        run = run_ref[b]
        slot = run % 2

        @pl.when(first_ref[b] == 1)
        def _():
            @pl.when(run + 1 < nruns_ref[0])
            def _():
                for c in weight_copies(run + 1, 1 - slot):
                    c.start()

            for c in weight_copies(run, slot):
                c.wait()

        x_lo, x_hi = _unpack_bf16_pair(_load_row_tiles(xs_ref))
        hd = x_lo.shape[1]
        x = jnp.concatenate([x_lo.astype(BF16), x_hi.astype(BF16)], axis=1)

        def up(buf):
            acc = None
            for k0 in range(0, 2 * hd, FFN_K_CHUNK):
                part = jnp.dot(x[:, k0:k0 + FFN_K_CHUNK], buf[slot, k0:k0 + FFN_K_CHUNK, :].astype(BF16),
                               preferred_element_type=F32)
                acc = part if acc is None else acc + part
            return acc

        h1 = up(w1_buf)
        h3 = up(w3_buf)
        hmid = (h1 * jax.nn.sigmoid(h1) * h3).astype(BF16)
        y = jnp.dot(hmid, w2_buf[slot].astype(BF16), preferred_element_type=F32)
        _store_row_tiles(ys_ref, _pack_bf16_pair(y[:, :hd], y[:, hd:]))


def _grouped_ffn(xs, w1, w3, w2, layer, block_expert, n_used):
    d = xs.shape[1]
    dm = w1.shape[2]
    de = w1.shape[3]
    nb = block_expert.shape[0]
    first = jnp.concatenate([jnp.ones((1,), jnp.int32), (block_expert[1:] != block_expert[:-1]).astype(jnp.int32)])
    run = jnp.cumsum(first) - 1
    n_runs = run[-1:] + 1
    hit = (run[None, :] == jnp.arange(nb, dtype=jnp.int32)[:, None]) & (first[None, :] == 1)
    run_expert = jnp.sum(jnp.where(hit, block_expert[None, :], 0), axis=1).astype(jnp.int32)
    blk = lambda b, nu, *_: (jnp.minimum(b, nu[0] - 1), 0)
    return pl.pallas_call(
        functools.partial(_ffn_kernel, layer=layer),
        grid_spec=pltpu.PrefetchScalarGridSpec(
            num_scalar_prefetch=5,
            grid=(nb,),
            in_specs=[pl.BlockSpec((MOE_ROWS * SUBLANES, d), blk),
                      pl.BlockSpec(memory_space=pl.ANY),
                      pl.BlockSpec(memory_space=pl.ANY),
                      pl.BlockSpec(memory_space=pl.ANY)],
            out_specs=pl.BlockSpec((MOE_ROWS * SUBLANES, d), blk),
            scratch_shapes=[pltpu.VMEM((2, dm, de), F32), pltpu.VMEM((2, dm, de), F32),
                            pltpu.VMEM((2, de, dm), F32), pltpu.SemaphoreType.DMA((2, 3))]),
        out_shape=jax.ShapeDtypeStruct(xs.shape, jnp.uint32),
        compiler_params=_params(("arbitrary",), 56),
        name="moe_ffn",
    )(n_used, run.astype(jnp.int32), first, run_expert, n_runs.astype(jnp.int32), xs, w1, w3, w2)


def _combine_ple_kernel(pos_ref, ys_hbm, x1_ref, route_ref, g_ref, b_ref, wg_ref, bg_ref, p_ref, wp_ref,
                        o_ref, ob_ref, ybuf_a, ybuf_b, wg_bf, wp_bf, sems, *, tm, s):
    i = pl.program_id(0)
    n = pl.num_programs(0)

    def start_row(t0, buf, buf_id, r):
        first = r * SUBLANES if isinstance(r, int) else pl.multiple_of(r * SUBLANES, SUBLANES)
        for k in range(TOP_K):
            pltpu.make_async_copy(_row_tile(ys_hbm, pos_ref[k * s + t0 + r]),
                                  buf.at[k, pl.ds(first, SUBLANES), :], sems.at[buf_id, k]).start(priority=k)

    def wait_buffer(buf, buf_id):
        span = ys_hbm.at[pl.ds(0, tm * SUBLANES), :]
        for k in range(TOP_K):
            pltpu.make_async_copy(span, buf.at[k], sems.at[buf_id, k]).wait()

    @pl.when(i == 0)
    def _():
        def issue(g, carry):
            for u in range(ROW_UNROLL):
                start_row(0, ybuf_a, 0, g * ROW_UNROLL + u)
            return carry

        lax.fori_loop(0, tm // ROW_UNROLL, issue, 0)
        chunk = wg_ref.shape[0] // 8
        for r0 in range(0, wg_ref.shape[0], chunk):
            wg_bf[r0:r0 + chunk, :] = wg_ref[r0:r0 + chunk, :].astype(BF16)
        wp_bf[...] = wp_ref[...].astype(BF16)

    def step(cur, cur_id, nxt_buf, nxt_id):
        wait_buffer(cur, cur_id)
        nxt = jnp.minimum(i + 1, n - 1) * tm
        for r in range(tm):
            start_row(nxt, nxt_buf, nxt_id, r)

        route = route_ref[...]
        f = None
        for k in range(TOP_K):
            lo, hi = _unpack_bf16_pair(_load_row_tiles(cur.at[k]))
            fk = route[:, TOP_K + k:TOP_K + k + 1] * jnp.concatenate([lo, hi], axis=1)
            f = fk if f is None else f + fk
        x2 = _layer_norm_rows(ALPHA * x1_ref[...] + f, g_ref[...], b_ref[...])
        gate = jax.nn.sigmoid(jnp.dot(x2.astype(BF16), wg_bf[...], preferred_element_type=F32) + bg_ref[...])
        emb = jnp.dot(p_ref[...].astype(BF16), wp_bf[...], preferred_element_type=F32)
        out = x2 + gate * emb
        o_ref[...] = out
        ob_ref[...] = out.astype(BF16)

        @pl.when(i == n - 1)
        def _():
            wait_buffer(nxt_buf, nxt_id)

    @pl.when(i % 2 == 0)
    def _():
        step(ybuf_a, 0, ybuf_b, 1)

    @pl.when(i % 2 == 1)
    def _():
        step(ybuf_b, 1, ybuf_a, 0)


def _combine_ple(ys, pos, x1, route, ln_g, ln_b, w_gate, b_gate, p, w_proj, layer, *, tm=256):
    s, d = x1.shape
    pd = p.shape[-1]
    rows = lambda n: pl.BlockSpec((tm, n), lambda i, pos: (i, 0))
    p_rows = pl.BlockSpec((None, None, tm, pd), lambda i, pos: (layer, 0, i, 0))
    full = lambda a: pl.BlockSpec(a.shape, lambda i, pos: (0,) * a.ndim, pipeline_mode=pl.Buffered(1))
    of_layer = lambda a: pl.BlockSpec((None,) + a.shape[1:], lambda i, pos: (layer, 0, 0),
                                      pipeline_mode=pl.Buffered(1))
    g2, b2, bg = ln_g.reshape(1, d), ln_b.reshape(1, d), b_gate.reshape(1, d)
    return pl.pallas_call(
        functools.partial(_combine_ple_kernel, tm=tm, s=s),
        grid_spec=pltpu.PrefetchScalarGridSpec(
            num_scalar_prefetch=1,
            grid=(s // tm,),
            in_specs=[pl.BlockSpec(memory_space=pl.ANY), rows(d), rows(ROUTE_LANES), full(g2), full(b2),
                      of_layer(w_gate), full(bg), p_rows, of_layer(w_proj)],
            out_specs=[rows(d), rows(d)],
            scratch_shapes=[pltpu.VMEM((TOP_K, tm * SUBLANES, LANES), jnp.uint32),
                            pltpu.VMEM((TOP_K, tm * SUBLANES, LANES), jnp.uint32),
                            pltpu.VMEM(w_gate.shape[1:], BF16), pltpu.VMEM(w_proj.shape[1:], BF16),
                            pltpu.SemaphoreType.DMA((2, TOP_K))]),
        out_shape=[jax.ShapeDtypeStruct((s, d), F32), jax.ShapeDtypeStruct((s, d), BF16)],
        compiler_params=_params(("arbitrary",), 56),
        name="moe_combine_ple",
    )(pos, ys, x1, route, g2, b2, w_gate, bg, p, w_proj)


def _rglru_kernel(cur_ref, prev_ref, gate_ref, cw_ref, cb_ref, wa_ref, ba_ref, wi_ref, bi_ref, lam_ref,
                  o_ref, buf_ref, a_ref, b_ref, h_ref, carry_ref, *, tt, tc, width):
    i = pl.program_id(1)

    @pl.when(i == 0)
    def _():
        carry_ref[...] = jnp.zeros(carry_ref.shape, F32)

    buf_ref[0:RNN_HALO, :] = jnp.where(i > 0, prev_ref[...], 0.0)
    buf_ref[RNN_HALO:, :] = cur_ref[...]
    off = RNN_HALO - (width - 1)
    z = -lam_ref[...]
    softplus = jnp.log1p(jnp.exp(-jnp.abs(z))) + jnp.maximum(z, 0.0)

    def sigmoid(v):
        return 0.5 * jnp.tanh(0.5 * v) + 0.5

    for hh in range(tc // RNN_HEAD_DIM):
        cols = slice(hh * RNN_HEAD_DIM, (hh + 1) * RNN_HEAD_DIM)
        xh = jnp.broadcast_to(cb_ref[:, cols], (tt, RNN_HEAD_DIM))
        for k in range(width):
            xh = xh + buf_ref[off + k:off + k + tt, cols] * cw_ref[k:k + 1, cols]
        xhb = xh.astype(BF16)
        r = sigmoid(jnp.dot(xhb, wa_ref[hh], preferred_element_type=F32) + ba_ref[:, cols])
        ig = sigmoid(jnp.dot(xhb, wi_ref[hh], preferred_element_type=F32) + bi_ref[:, cols])
        log_a = (-RG_C) * r * softplus[:, cols]
        a = jnp.exp(log_a)
        bval = jnp.sqrt(-jnp.tanh(log_a) * (a * a + 1.0)) * (ig * xh)
        a_ref[:, cols] = a
        b_ref[:, cols] = bval

    def group(gi, h):
        base = pl.multiple_of(gi * 8, 8)
        for r in range(8):
            row = pl.ds(base + r, 1)
            h = a_ref[row, :] * h + b_ref[row, :]
            h_ref[row, :] = h
        return h

    carry_ref[...] = lax.fori_loop(0, tt // 8, group, carry_ref[...])
    o_ref[...] = (gate_ref[...].astype(F32) * h_ref[...]).astype(o_ref.dtype)


def _rglru(xr, gate, conv_w, conv_b, w_a, b_a, w_i, b_i, lam, *, tt=256, tc=2048):
    s, wd = xr.shape
    width = conv_w.shape[0]
    hpc = tc // RNN_HEAD_DIM
    hb = tt // RNN_HALO
    row = lambda v: v.reshape(1, wd)
    cvec = lambda: pl.BlockSpec((1, tc), lambda c, i: (0, c))
    heads = lambda: pl.BlockSpec((hpc, RNN_HEAD_DIM, RNN_HEAD_DIM), lambda c, i: (c, 0, 0))
    tile = lambda: pl.BlockSpec((tt, tc), lambda c, i: (i, c))
    return pl.pallas_call(
        functools.partial(_rglru_kernel, tt=tt, tc=tc, width=width),
        grid=(wd // tc, s // tt),
        in_specs=[tile(),
                  pl.BlockSpec((RNN_HALO, tc), lambda c, i: (jnp.maximum(i * hb - 1, 0), c)),
                  tile(),
                  pl.BlockSpec((width, tc), lambda c, i: (0, c)),
                  cvec(), heads(), cvec(), heads(), cvec(), cvec()],
        out_specs=tile(),
        out_shape=jax.ShapeDtypeStruct((s, wd), BF16),
        scratch_shapes=[pltpu.VMEM((tt + RNN_HALO, tc), F32), pltpu.VMEM((tt, tc), F32),
                        pltpu.VMEM((tt, tc), F32), pltpu.VMEM((tt, tc), F32), pltpu.VMEM((1, tc), F32)],
        compiler_params=_params(("parallel", "arbitrary"), 32),
        name="rglru",
    )(xr, xr, gate, conv_w, row(conv_b), w_a.astype(BF16), row(b_a), w_i.astype(BF16), row(b_i), row(lam))


def _moe_and_ple(mixed, i, p, ln_ffn_g, ln_ffn_b, moe_w1, moe_w3, moe_w2, ple_w_proj, ple_w_gate, ple_b_gate):
    x1, x1_pairs, route = mixed
    dest, block_expert, n_used = _moe_plan(route)
    xs = _dispatch_rows(x1_pairs, dest, block_expert.shape[0])
    ys = _grouped_ffn(xs, moe_w1, moe_w3, moe_w2, i, block_expert, n_used)
    return _combine_ple(ys, dest, x1, route, ln_ffn_g[i], ln_ffn_b[i], ple_w_gate, ple_b_gate[i], p,
                        ple_w_proj, i)


def kernel(x, p, positions, ev_w_in, ev_conv_w, ev_conv_b, ev_cnorm_g, ev_cnorm_b, ev_qnorm_g, ev_w_uq, ev_kvnorm_g, ev_w_ukv, ev_w_out, od_w_in, od_conv_w, od_conv_b, od_w_a, od_b_a, od_w_i, od_b_i, od_lam, od_w_out, ln_mix_g, ln_mix_b, ln_ffn_g, ln_ffn_b, moe_w_grp, moe_b_grp, moe_w_exp, moe_b_exp, moe_w1, moe_w3, moe_w2, ple_w_proj, ple_w_gate, ple_b_gate):
    x0 = x[0]
    moe =functools.partial(_moe_and_ple, p=p, ln_ffn_g=ln_ffn_g, ln_ffn_b=ln_ffn_b, moe_w1=moe_w1,
                            moe_w3=moe_w3, moe_w2=moe_w2, ple_w_proj=ple_w_proj, ple_w_gate=ple_w_gate,
                            ple_b_gate=ple_b_gate)

    cosm, sinm = _rope_tables(positions[0])
    x0b, qn, qr, kn, kr, v = _mla_proj(x0, ev_w_in[0, :, 2 * CONV_CH:], ev_qnorm_g[0], ev_w_uq[0], ev_kvnorm_g[0],
                                       ev_w_ukv[0], cosm, sinm)
    u =_pair_proj(x0b, ev_w_in[0].T, CONV_CH, mode="glu", transposed=True)
    ub = _conv_gln(u, ev_conv_w[0], ev_conv_b[0], ev_cnorm_g[0], ev_cnorm_b[0], groups=CONV_GROUPS)
    att = _attention(qn, qr, kn, kr, v)
    mixed0 = _out_ln_route(ub, att, ev_w_out[0].astype(BF16), x0, ln_mix_g[0], ln_mix_b[0],
                           moe_w_grp[0], moe_b_grp[0], moe_w_exp[0], moe_b_exp[0])
    x3, x3b = moe(mixed0, 0)

    gate, xr = _pair_proj(x3b, od_w_in.reshape(D_MODEL, -1), RNN_WIDTH, mode="rnn")
    y = _rglru(xr, gate, od_conv_w[0], od_conv_b[0], od_w_a[0], od_b_a[0], od_w_i[0], od_b_i[0], od_lam[0])
    mixed1 = _out_ln_route(y, y, od_w_out[0].astype(BF16), x3, ln_mix_g[1], ln_mix_b[1],
                           moe_w_grp[1], moe_b_grp[1], moe_w_exp[1], moe_b_exp[1], col1=0, col2=1)
    x6, _ = moe(mixed1, 1)
    return x6[None]
```

```python
import functools
import math

import jax
import jax.numpy as jnp
from jax import lax
from jax.experimental import pallas as pl
from jax.experimental.pallas import tpu as pltpu

F32 = jnp.float32
BF16 = jnp.bfloat16

D_MODEL = 2048
SEQ = 8192
DEPTH = 2
CONV_CH = 1024
CONV_GROUPS = 16
CONV_WIDTH = 31
MLA_HEADS = 8
QK_NOPE = 128
QK_ROPE = 64
V_HEAD = 128
Q_RANK = 512
KV_RANK = 256
ROPE_THETA = 10000.0
RNN_WIDTH = 2048
RNN_HEADS = 16
RNN_HEAD_DIM = RNN_WIDTH // RNN_HEADS
RNN_CONV_WIDTH = 4
RG_C = 8.0
N_GROUPS = 8
EXPERTS_PER_GROUP = 8
N_EXPERTS = N_GROUPS * EXPERTS_PER_GROUP
TOP_K = 2
D_EXPERT = 512
PLE_DIM = 256
ALPHA = (2 * DEPTH) ** 0.25
LN_EPS = 1e-5
RMS_EPS = 1e-6

LANES = 128
SUBLANES = 8
CONV_HALO = 32
RNN_HALO = 8
MOE_ROWS = 256
ROUTE_LANES = 128
MIB = 2 ** 20


def _params(semantics, vmem_mib):
    return pltpu.CompilerParams(dimension_semantics=semantics, vmem_limit_bytes=vmem_mib * MIB)


def _layer_norm_rows(z, g, b):
    mu = jnp.mean(z, axis=-1, keepdims=True)
    d = z - mu
    var = jnp.mean(d * d, axis=-1, keepdims=True)
    return d * lax.rsqrt(var + LN_EPS) * g + b


def _pack_bf16_pair(a, b):
    def rounded_bits(v):
        u = lax.bitcast_convert_type(v, jnp.uint32)
        return u + jnp.uint32(0x7FFF) + ((u >> 16) & jnp.uint32(1))

    return (rounded_bits(a) & jnp.uint32(0xFFFF0000)) | (rounded_bits(b) >> 16)


def _unpack_bf16_pair(u):
    a = lax.bitcast_convert_type(u & jnp.uint32(0xFFFF0000), F32)
    b = lax.bitcast_convert_type(u << 16, F32)
    return a, b


def _store_row_tiles(ref, packed):
    n, w = packed.shape
    assert w == SUBLANES * LANES
    for j in range(SUBLANES):
        ref[pl.ds(j, n, stride=SUBLANES), :] = packed[:, j * LANES:(j + 1) * LANES]


def _load_row_tiles(ref):
    n = ref.shape[0] // SUBLANES
    return jnp.concatenate([ref[pl.ds(j, n, stride=SUBLANES), :] for j in range(SUBLANES)], axis=1)


def _split_bf16(v):
    hi = v.astype(BF16)
    lo = (v - hi.astype(F32)).astype(BF16)
    return hi, lo


def _gelu_tanh(x):
    c = math.sqrt(2.0 / math.pi)
    return 0.5 * x * (1.0 + jnp.tanh(c * (x + 0.044715 * (x * x * x))))


def _pair_proj_kernel(x_ref, wa_ref, wb_ref, *refs, mode, transposed):
    out_refs, (wa_bf, wb_bf) = refs[:-2], refs[-2:]

    @pl.when(pl.program_id(1) == 0)
    def _():
        for src, dst in ((wa_ref, wa_bf), (wb_ref, wb_bf)):
            w = src[...]
            dst[...] = (w.T if transposed else w).astype(BF16)

    x = x_ref[...]
    a = jnp.dot(x, wa_bf[...], preferred_element_type=F32)
    b = jnp.dot(x, wb_bf[...], preferred_element_type=F32)
    if mode == "glu":
        out_refs[0][...] = a * jax.nn.sigmoid(b)
    else:
        out_refs[0][...] = _gelu_tanh(a).astype(out_refs[0].dtype)
        out_refs[1][...] = b


def _pair_proj(x, w, n, *, mode, transposed=False, tm=512, tn=512):
    m, k = x.shape
    nb = n // tn
    tile = pl.BlockSpec((tm, tn), lambda j, i: (i, j))
    if transposed:
        w_specs = [pl.BlockSpec((tn, k), lambda j, i: (j, 0)), pl.BlockSpec((tn, k), lambda j, i: (j + nb, 0))]
    else:
        w_specs = [pl.BlockSpec((k, tn), lambda j, i: (0, j)), pl.BlockSpec((k, tn), lambda j, i: (0, j + nb))]
    if mode == "glu":
        out_specs, out_shape = tile, jax.ShapeDtypeStruct((m, n), F32)
    else:
        out_specs = [tile, tile]
        out_shape = [jax.ShapeDtypeStruct((m, n), BF16), jax.ShapeDtypeStruct((m, n), F32)]
    return pl.pallas_call(
        functools.partial(_pair_proj_kernel, mode=mode, transposed=transposed),
        grid=(nb, m // tm),
        in_specs=[pl.BlockSpec((tm, k), lambda j, i: (i, 0))] + w_specs,
        out_specs=out_specs,
        out_shape=out_shape,
        scratch_shapes=[pltpu.VMEM((k, tn), BF16), pltpu.VMEM((k, tn), BF16)],
        compiler_params=_params(("parallel", "arbitrary"), 48),
        name=mode + "_proj",
    )(x, w, w)


def _conv_gln_kernel(cur_ref, prev_ref, w_ref, b_ref, g_ref, beta_ref, gm_ref, o_ref,
                     buf_ref, sh_ref, y_ref, *, tt, ch, width, rc, cc):
    i = pl.program_id(0)
    buf_ref[0:CONV_HALO, :] = jnp.where(i > 0, prev_ref[...], 0.0)
    buf_ref[CONV_HALO:, :] = cur_ref[...]
    off = CONV_HALO - (width - 1)
    sh_rows = sh_ref.shape[1]
    step = 7 * SUBLANES
    assert sh_rows % step == 0
    for b in range(1, SUBLANES):
        for c0 in range(0, ch, cc):
            for r0 in range(0, sh_rows, step):
                sh_ref[b - 1, r0:r0 + step, c0:c0 + cc] = buf_ref[r0 + b:r0 + b + step, c0:c0 + cc]
    for c0 in range(0, ch, cc):
        for r0 in range(0, tt, rc):
            acc = jnp.broadcast_to(b_ref[:, c0:c0 + cc], (rc, cc))
            for k in range(width):
                q, b = divmod(off + k, SUBLANES)
                lo = r0 + q * SUBLANES
                src = buf_ref[lo:lo + rc, c0:c0 + cc] if b == 0 else sh_ref[b - 1, lo:lo + rc, c0:c0 + cc]
                acc = acc + src * w_ref[k:k + 1, c0:c0 + cc]
            y_ref[r0:r0 + rc, c0:c0 + cc] = acc
    gm = gm_ref[...]

    def seg_mean(v):
        hi, lo = _split_bf16(v)
        return (jnp.dot(hi, gm, preferred_element_type=F32)
                + jnp.dot(lo, gm, preferred_element_type=F32))

    for c0 in range(0, ch, LANES):
        y = y_ref[:, c0:c0 + LANES]
        d = y - seg_mean(y)
        var = seg_mean(d * d)
        z = d * lax.rsqrt(var + LN_EPS) * g_ref[:, c0:c0 + LANES] + beta_ref[:, c0:c0 + LANES]
        o_ref[:, c0:c0 + LANES] = (z * jax.nn.sigmoid(z)).astype(o_ref.dtype)


def _conv_gln(u, w, b, g, beta, *, groups, tt=256):
    s, ch = u.shape
    width = w.shape[0]
    gsz = ch // groups
    assert LANES % gsz == 0 and width - 1 <= CONV_HALO and tt % CONV_HALO == 0
    wpad = jnp.zeros((CONV_HALO, ch), F32).at[:width].set(w)
    lane = jnp.arange(LANES)
    gm = jnp.where((lane[:, None] // gsz) == (lane[None, :] // gsz), 1.0 / gsz, 0.0).astype(BF16)
    hb = tt // CONV_HALO
    row = lambda v: v.reshape(1, ch)
    return pl.pallas_call(
        functools.partial(_conv_gln_kernel, tt=tt, ch=ch, width=width, rc=32, cc=256),
        grid=(s // tt,),
        in_specs=[pl.BlockSpec((tt, ch), lambda i: (i, 0)),
                  pl.BlockSpec((CONV_HALO, ch), lambda i: (jnp.maximum(i * hb - 1, 0), 0)),
                  pl.BlockSpec((CONV_HALO, ch), lambda i: (0, 0)),
                  pl.BlockSpec((1, ch), lambda i: (0, 0)),
                  pl.BlockSpec((1, ch), lambda i: (0, 0)),
                  pl.BlockSpec((1, ch), lambda i: (0, 0)),
                  pl.BlockSpec((LANES, LANES), lambda i: (0, 0))],
        out_specs=pl.BlockSpec((tt, ch), lambda i: (i, 0)),
        out_shape=jax.ShapeDtypeStruct((s, ch), BF16),
        scratch_shapes=[pltpu.VMEM((tt + CONV_HALO, ch), F32),
                        pltpu.VMEM((SUBLANES - 1, tt + CONV_HALO - SUBLANES, ch), F32),
                        pltpu.VMEM((tt, ch), F32)],
        compiler_params=_params(("parallel",), 40),
        name="conv_gln",
    )(u, u, wpad, row(b), row(g), row(beta), gm)


def _rope_table_kernel(pos_ref, invf_ref, cos_ref, sin_ref):
    ang = pos_ref[...].astype(F32) * invf_ref[...]
    cos_ref[...] = jnp.cos(ang)
    sin_ref[...] = jnp.sin(ang)


def _rope_tables(positions):
    s = positions.shape[0]
    half = QK_ROPE // 2
    per_row = LANES // half
    inv_freq = 1.0 / (ROPE_THETA ** (jnp.arange(0, QK_ROPE, 2, dtype=F32) / QK_ROPE))
    pos_rep = jnp.repeat(positions.reshape(s // per_row, per_row), half, axis=1)
    invf = jnp.tile(inv_freq, per_row).reshape(1, LANES)
    cos, sin = pl.pallas_call(
        _rope_table_kernel,
        out_shape=[jax.ShapeDtypeStruct((s // per_row, LANES), F32)] * 2,
        name="rope_table",
    )(pos_rep, invf)
    cos = cos.reshape(s, half)
    sin = sin.reshape(s, half)
    zero = jnp.zeros((s, LANES - QK_ROPE), F32)
    return jnp.concatenate([cos, cos, zero], axis=1), jnp.concatenate([-sin, sin, zero], axis=1)


def _mla_proj_kernel(x_ref, wm_ref, qg_ref, kvg_ref, wq_ref, wkv_ref, c_ref, s_ref,
                     xb_ref, qn_ref, qr_ref, kn_ref, kr_ref, v_ref, *, scale):
    nq = MLA_HEADS * QK_NOPE
    xb = x_ref[...].astype(BF16)
    xb_ref[...] = xb
    c = jnp.dot(xb, wm_ref[...], preferred_element_type=F32)
    cq = c[:, :Q_RANK]
    ckv = c[:, Q_RANK:Q_RANK + KV_RANK]
    kr_pad = c[:, Q_RANK + KV_RANK:Q_RANK + KV_RANK + LANES]
    kr_rot = c[:, Q_RANK + KV_RANK + LANES:]
    cqn = cq * lax.rsqrt(jnp.mean(cq * cq, axis=-1, keepdims=True) + RMS_EPS) * qg_ref[...]
    ckvn = ckv * lax.rsqrt(jnp.mean(ckv * ckv, axis=-1, keepdims=True) + RMS_EPS) * kvg_ref[...]
    q = jnp.dot(cqn.astype(BF16), wq_ref[...], preferred_element_type=F32)
    kv = jnp.dot(ckvn.astype(BF16), wkv_ref[...], preferred_element_type=F32)
    cosm = c_ref[...]
    sinm = s_ref[...]
    qn_ref[...] = (q[:, :nq] * scale).astype(BF16)
    for h in range(MLA_HEADS):
        lo = nq + h * LANES
        rot = q[:, lo:lo + LANES] * cosm + q[:, lo + nq:lo + nq + LANES] * sinm
        qr_ref[:, h * LANES:(h + 1) * LANES] = (rot * scale).astype(BF16)
    kn_ref[...] = kv[:, :nq].astype(BF16)
    v_ref[...] = kv[:, nq:].astype(BF16)
    kr_ref[...] = (kr_pad * cosm + kr_rot * sinm).astype(BF16)


def _rope_swap(w):
    half = QK_ROPE // 2
    return jnp.concatenate([w[..., half:], w[..., :half]], axis=-1)


def _mla_proj(x, w_in_mla, qnorm_g, w_uq, kvnorm_g, w_ukv, cosm, sinm, *, tm=256):
    s, d = x.shape
    nq = MLA_HEADS * QK_NOPE
    pad = lambda w: jnp.concatenate([w, jnp.zeros(w.shape[:-1] + (LANES - QK_ROPE,), w.dtype)], axis=-1)
    w_kr = w_in_mla[:, Q_RANK + KV_RANK:]
    wm = jnp.concatenate([w_in_mla[:, :Q_RANK + KV_RANK], pad(w_kr), pad(_rope_swap(w_kr))], axis=1).astype(BF16)
    wq3 = w_uq.reshape(Q_RANK, MLA_HEADS, QK_NOPE + QK_ROPE)
    wq_rope = wq3[:, :, QK_NOPE:]
    wq = jnp.concatenate([wq3[:, :, :QK_NOPE].reshape(Q_RANK, nq),
                          pad(wq_rope).reshape(Q_RANK, MLA_HEADS * LANES),
                          pad(_rope_swap(wq_rope)).reshape(Q_RANK, MLA_HEADS * LANES)], axis=1).astype(BF16)
    wkv3 = w_ukv.reshape(KV_RANK, MLA_HEADS, QK_NOPE + V_HEAD)
    wkv = jnp.concatenate([wkv3[:, :, :QK_NOPE].reshape(KV_RANK, nq),
                           wkv3[:, :, QK_NOPE:].reshape(KV_RANK, MLA_HEADS * V_HEAD)], axis=1).astype(BF16)
    scale = (QK_NOPE + QK_ROPE) ** -0.5 * math.log2(math.e)
    full = lambda a: pl.BlockSpec(a.shape, lambda i: (0,) * a.ndim)
    rows = lambda n: pl.BlockSpec((tm, n), lambda i: (i, 0))
    qg = qnorm_g.reshape(1, Q_RANK)
    kvg = kvnorm_g.reshape(1, KV_RANK)
    return pl.pallas_call(
        functools.partial(_mla_proj_kernel, scale=scale),
        grid=(s // tm,),
        in_specs=[rows(d), full(wm), full(qg), full(kvg), full(wq), full(wkv), rows(LANES), rows(LANES)],
        out_specs=[rows(d), rows(nq), rows(MLA_HEADS * LANES), rows(nq), rows(LANES), rows(MLA_HEADS * V_HEAD)],
        out_shape=[jax.ShapeDtypeStruct((s, d), BF16),
                   jax.ShapeDtypeStruct((s, nq), BF16),
                   jax.ShapeDtypeStruct((s, MLA_HEADS * LANES), BF16),
                   jax.ShapeDtypeStruct((s, nq), BF16),
                   jax.ShapeDtypeStruct((s, LANES), BF16),
                   jax.ShapeDtypeStruct((s, MLA_HEADS * V_HEAD), BF16)],
        compiler_params=_params(("parallel",), 48),
        name="mla_proj",
    )(x, wm, qg, kvg, wq, wkv, cosm, sinm)


def _attn_kernel(qn_ref, qr_ref, kn_ref, kr_ref, v_ref, o_ref, m_ref, acc_ref, *, tq, tk, hb):
    qi = pl.program_id(1)
    per_q = tq // tk
    m_ref[...] = jnp.full(m_ref.shape, -jnp.inf, F32)
    acc_ref[...] = jnp.zeros(acc_ref.shape, F32)
    ones = jnp.ones((tk, LANES), BF16)
    head_cols = [slice(h * LANES, (h + 1) * LANES) for h in range(hb)]
    qs = [jnp.concatenate([qn_ref[:, c], qr_ref[:, c]], axis=1) for c in head_cols]

    def chunk(j, diag):
        rows = pl.ds(pl.multiple_of(j * tk, tk), tk)
        k_rope = kr_ref[rows, :]
        q0 = 0 if diag is None else diag * tk
        for h, c in enumerate(head_cols):
            k = jnp.concatenate([kn_ref[rows, c], k_rope], axis=1)
            s = lax.dot_general(qs[h][q0:], k, (((1,), (1,)), ((), ())), preferred_element_type=F32)
            if diag is not None:
                keep = lax.broadcasted_iota(jnp.int32, s.shape, 0) >= lax.broadcasted_iota(jnp.int32, s.shape, 1)
                s = jnp.where(keep, s, -jnp.inf)
            m_old = m_ref[h, q0:, :]
            m_new = jnp.maximum(m_old, jnp.max(s, axis=-1, keepdims=True))
            alpha = jnp.exp2(m_old - m_new)
            p = jnp.exp2(s - jnp.concatenate([m_new] * (tk // LANES), axis=1))
            v_ext = jnp.concatenate([v_ref[rows, c], ones], axis=1)
            acc_ref[h, q0:, :] = (jnp.concatenate([alpha, alpha], axis=1) * acc_ref[h, q0:, :]
                                  + jnp.dot(p.astype(BF16), v_ext, preferred_element_type=F32))
            m_ref[h, q0:, :] = m_new

    def body(j, carry):
        chunk(j, None)
        return carry

    lax.fori_loop(0, qi * per_q, body, 0)
    for d in range(per_q):
        chunk(qi * per_q + d, d)
    for h, c in enumerate(head_cols):
        acc = acc_ref[h]
        o_ref[:, c] = (acc[:, :V_HEAD] / acc[:, V_HEAD:]).astype(o_ref.dtype)


def _attention(qn, qr, kn, kr, v, *, tq=1024, tk=512, hb=4):
    s = qn.shape[0]
    wide = hb * LANES
    tile = lambda: pl.BlockSpec((tq, wide), lambda h, i: (i, h))
    keys = lambda: pl.BlockSpec((s, wide), lambda h, i: (0, h), pipeline_mode=pl.Buffered(1))
    return pl.pallas_call(
        functools.partial(_attn_kernel, tq=tq, tk=tk, hb=hb),
        grid=(MLA_HEADS // hb, s // tq),
        in_specs=[tile(), tile(), keys(),
                  pl.BlockSpec((s, LANES), lambda h, i: (0, 0), pipeline_mode=pl.Buffered(1)), keys()],
        out_specs=tile(),
        out_shape=jax.ShapeDtypeStruct((s, MLA_HEADS * V_HEAD), BF16),
        scratch_shapes=[pltpu.VMEM((hb, tq, LANES), F32), pltpu.VMEM((hb, tq, 2 * V_HEAD), F32)],
        compiler_params=_params(("parallel", "parallel"), 48),
        name="mla_attention",
    )(qn, qr, kn, kr, v)


def _route(logits):
    lane = lax.broadcasted_iota(jnp.int32, logits.shape, 1).astype(F32)
    big = float(2 * ROUTE_LANES)
    neg = -jnp.inf
    gl = jnp.where(lane < N_GROUPS, logits, neg)
    gmax = jnp.max(gl, axis=-1, keepdims=True)
    gsum = jnp.sum(jnp.exp(gl - gmax), axis=-1, keepdims=True)
    g_prob = 1.0 / gsum
    g_idx = jnp.min(jnp.where(gl == gmax, lane, big), axis=-1, keepdims=True)
    lo = N_GROUPS + g_idx * EXPERTS_PER_GROUP
    el = jnp.where((lane >= lo) & (lane < lo + EXPERTS_PER_GROUP), logits, neg)
    emax = jnp.max(el, axis=-1, keepdims=True)
    esum = jnp.sum(jnp.exp(el - emax), axis=-1, keepdims=True)
    i0 = jnp.min(jnp.where(el == emax, lane, big), axis=-1, keepdims=True)
    el2 = jnp.where(lane == i0, neg, el)
    emax2 = jnp.max(el2, axis=-1, keepdims=True)
    i1 = jnp.min(jnp.where(el2 == emax2, lane, big), axis=-1, keepdims=True)
    p0 = 1.0 / esum
    p1 = jnp.exp(emax2 - emax) / esum
    g0 = g_prob * p0 / (p0 + p1)
    g1 = g_prob * p1 / (p0 + p1)
    e0 = i0 - N_GROUPS
    e1 = i1 - N_GROUPS
    return jnp.where(lane == 0, e0, jnp.where(lane == 1, e1, jnp.where(lane == 2, g0, jnp.where(lane == 3, g1, 0.0))))


def _out_ln_route_kernel(a1_ref, a2_ref, w_ref, x_ref, g_ref, b_ref, wrh_ref, br_ref,
                         x1_ref, x1p_ref, route_ref, *, sub):
    half = a1_ref.shape[1]
    for r0 in range(0, x_ref.shape[0], sub):
        rows = slice(r0, r0 + sub)
        m = (jnp.dot(a1_ref[rows, :], w_ref[:half, :], preferred_element_type=F32)
             + jnp.dot(a2_ref[rows, :], w_ref[half:, :], preferred_element_type=F32))
        x1 = _layer_norm_rows(ALPHA * x_ref[rows, :] + m, g_ref[...], b_ref[...])
        x1_ref[rows, :] = x1
        hd = x1.shape[1] // 2
        _store_row_tiles(x1p_ref.at[pl.ds(r0 * SUBLANES, sub * SUBLANES), :], _pack_bf16_pair(x1[:, :hd], x1[:, hd:]))
        xh, xl = _split_bf16(x1)
        both = jnp.dot(xh, wrh_ref[...], preferred_element_type=F32)
        logits = (both[:, :ROUTE_LANES] + both[:, ROUTE_LANES:]
                  + jnp.dot(xl, wrh_ref[:, :ROUTE_LANES], preferred_element_type=F32) + br_ref[...])
        route_ref[rows, :] = _route(logits)


def _out_ln_route(a1, a2, w, x, ln_g, ln_b, w_grp, b_grp, w_exp, b_exp, *, col1=0, col2=0, tm=512, sub=512):
    s, d = x.shape
    half = w.shape[0] // 2
    pad = ROUTE_LANES - N_GROUPS - N_EXPERTS
    wr = jnp.concatenate([w_grp, w_exp, jnp.zeros((d, pad), F32)], axis=1)
    br = jnp.concatenate([b_grp, b_exp, jnp.zeros((pad,), F32)]).reshape(1, ROUTE_LANES)
    wr_hi = wr.astype(BF16)
    wrh = jnp.concatenate([wr_hi, (wr - wr_hi.astype(F32)).astype(BF16)], axis=1)
    full = lambda a: pl.BlockSpec(a.shape, lambda i: (0,) * a.ndim)
    rows = lambda n: pl.BlockSpec((tm, n), lambda i: (i, 0))
    g2 = ln_g.reshape(1, d)
    b2 = ln_b.reshape(1, d)
    return pl.pallas_call(
        functools.partial(_out_ln_route_kernel, sub=sub),
        grid=(s // tm,),
        in_specs=[pl.BlockSpec((tm, half), lambda i: (i, col1)), pl.BlockSpec((tm, half), lambda i: (i, col2)),
                  pl.BlockSpec(w.shape, lambda i: (0, 0), pipeline_mode=pl.Buffered(1)),
                  rows(d), full(g2), full(b2), full(wrh), full(br)],
        out_specs=[rows(d), pl.BlockSpec((tm * SUBLANES, LANES), lambda i: (i, 0)), rows(ROUTE_LANES)],
        out_shape=[jax.ShapeDtypeStruct((s, d), F32), jax.ShapeDtypeStruct((s * SUBLANES, LANES), jnp.uint32),
                   jax.ShapeDtypeStruct((s, ROUTE_LANES), F32)],
        compiler_params=_params(("parallel",), 48),
        name="out_ln_route",
    )(a1, a2, w, x, g2, b2, wrh, br)


def _rank_kernel(route_ref, tri_ref, rank_ref, cnt_ref, carry_ref):
    @pl.when(pl.program_id(0) == 0)
    def _():
        carry_ref[...] = jnp.zeros(carry_ref.shape, F32)

    route = route_ref[...]
    lane = lax.broadcasted_iota(jnp.int32, route.shape, 1).astype(F32)
    oh0 = jnp.where(lane == route[:, 0:1], 1.0, 0.0)
    oh1 = jnp.where(lane == route[:, 1:2], 1.0, 0.0)
    both = oh0 + oh1
    before = jnp.dot(tri_ref[...], both.astype(BF16), preferred_element_type=F32) + carry_ref[...]
    r0 = jnp.sum(before * oh0, axis=-1, keepdims=True)
    r1 = jnp.sum(before * oh1, axis=-1, keepdims=True)
    rank_ref[...] = jnp.where(lane == 0.0, r0, jnp.where(lane == 1.0, r1, 0.0))
    carry_ref[...] = carry_ref[...] + jnp.sum(both, axis=0, keepdims=True)
    cnt_ref[...] = carry_ref[...]


def _dest_kernel(route_ref, rank_ref, pstart_ref, dest_ref):
    route = route_ref[...]
    rank = rank_ref[...]
    lane = lax.broadcasted_iota(jnp.int32, route.shape, 1).astype(F32)
    ps = pstart_ref[...]
    d0 = jnp.sum(jnp.where(lane == route[:, 0:1], ps, 0.0), axis=-1, keepdims=True) + rank[:, 0:1]
    d1 = jnp.sum(jnp.where(lane == route[:, 1:2], ps, 0.0), axis=-1, keepdims=True) + rank[:, 1:2]
    dest = jnp.where(lane == 0.0, d0, jnp.where(lane == 1.0, d1, 0.0)) * float(SUBLANES)
    dest_ref[...] = dest.astype(jnp.int32)


def _moe_plan(route, *, tb=512):
    s = route.shape[0]
    tri = (jnp.arange(tb)[:, None] > jnp.arange(tb)[None, :]).astype(BF16)
    rows = pl.BlockSpec((tb, ROUTE_LANES), lambda i: (i, 0))
    one = lambda: pl.BlockSpec((1, ROUTE_LANES), lambda i: (0, 0))
    rank, cnt = pl.pallas_call(
        _rank_kernel,
        grid=(s // tb,),
        in_specs=[rows, pl.BlockSpec((tb, tb), lambda i: (0, 0))],
        out_specs=[rows, one()],
        out_shape=[jax.ShapeDtypeStruct((s, ROUTE_LANES), F32), jax.ShapeDtypeStruct((1, ROUTE_LANES), F32)],
        scratch_shapes=[pltpu.VMEM((1, ROUTE_LANES), F32)],
        compiler_params=_params(("arbitrary",), 32),
        name="moe_rank",
    )(route, tri)
    counts = cnt[0, :N_EXPERTS].astype(jnp.int32)
    pcounts = (counts + MOE_ROWS - 1) // MOE_ROWS * MOE_ROWS
    pends = jnp.cumsum(pcounts)
    pstarts = pends - pcounts
    n_blocks = s * TOP_K // MOE_ROWS + N_EXPERTS
    n_used = pends[-1] // MOE_ROWS
    blk = jnp.minimum(jnp.arange(n_blocks, dtype=jnp.int32), n_used - 1) * MOE_ROWS
    block_expert = jnp.minimum(jnp.sum(pends[None, :] <= blk[:, None], axis=1), N_EXPERTS - 1).astype(jnp.int32)
    ps = jnp.zeros((1, ROUTE_LANES), F32).at[0, :N_EXPERTS].set(pstarts.astype(F32))
    dest = pl.pallas_call(
        _dest_kernel,
        grid=(s // tb,),
        in_specs=[rows, rows, one()],
        out_specs=rows,
        out_shape=jax.ShapeDtypeStruct((s, ROUTE_LANES), jnp.int32),
        compiler_params=_params(("parallel",), 32),
        name="moe_dest",
    )(route, rank, ps)
    dest = dest[:, :TOP_K].T.reshape(TOP_K * s)
    return dest, block_expert, n_used.astype(jnp.int32).reshape(1)


ROW_UNROLL = 8


def _row_tile(ref, first_row):
    return ref.at[pl.ds(pl.multiple_of(first_row, SUBLANES), SUBLANES), :]


def _row_source_kernel(dest_ref, src_ref, *, s):
    def body(g, carry):
        for u in range(ROW_UNROLL):
            t = g * ROW_UNROLL + u
            for k in range(TOP_K):
                src_ref[lax.shift_right_logical(dest_ref[k * s + t], SUBLANES.bit_length() - 1)] = t
        return carry

    lax.fori_loop(0, s // ROW_UNROLL, body, 0)


def _row_source(dest, n_rows):
    s = dest.shape[0] // TOP_K
    return pl.pallas_call(
        functools.partial(_row_source_kernel, s=s),
        in_specs=[pl.BlockSpec(memory_space=pltpu.SMEM)],
        out_specs=pl.BlockSpec(memory_space=pltpu.SMEM),
        out_shape=jax.ShapeDtypeStruct((n_rows,), jnp.int32),
        name="moe_row_source",
    )(dest)


FFN_K_CHUNK = 512


def _ffn_kernel(nused_ref, run_ref, first_ref, rexp_ref, nruns_ref, rsrc_ref, x_hbm, w1_hbm, w3_hbm, w2_hbm,
                ys_ref, w1_buf, w3_buf, w2_buf, x_buf, sems, xsems, *, layer, n_tok):
    b = pl.program_id(0)
    n_used = nused_ref[0]

    def weight_copies(run, slot):
        e = rexp_ref[run]
        return [pltpu.make_async_copy(w_hbm.at[layer, e], buf.at[slot], sems.at[slot, j])
                for j, (w_hbm, buf) in enumerate(((w1_hbm, w1_buf), (w3_hbm, w3_buf), (w2_hbm, w2_buf)))]

    def start_rows(block, xslot):
        base = block * MOE_ROWS

        def issue(g, carry):
            for u in range(ROW_UNROLL):
                r = g * ROW_UNROLL + u
                tok = jnp.clip(rsrc_ref[base + r], 0, n_tok - 1)
                pltpu.make_async_copy(_row_tile(x_hbm, tok * SUBLANES), _row_tile(x_buf.at[xslot], r * SUBLANES),
                                      xsems.at[xslot]).start(priority=1)
            return carry

        lax.fori_loop(0, MOE_ROWS // ROW_UNROLL, issue, 0)

    @pl.when(b == 0)
    def _():
        for c in weight_copies(0, 0):
            c.start()
        start_rows(0, 0)

    @pl.when(b < n_used)
    def _():
        run = run_ref[b]
        slot = run % 2
        xslot = b % 2

        @pl.when(b + 1 < n_used)
        def _():
            start_rows(b + 1, 1 - xslot)

        @pl.when(first_ref[b] == 1)
        def _():
            @pl.when(run + 1 < nruns_ref[0])
            def _():
                for c in weight_copies(run + 1, 1 - slot):
                    c.start()

            for c in weight_copies(run, slot):
                c.wait()

        pltpu.make_async_copy(x_hbm.at[pl.ds(0, MOE_ROWS * SUBLANES), :], x_buf.at[xslot], xsems.at[xslot]).wait()
        x_lo, x_hi = _unpack_bf16_pair(_load_row_tiles(x_buf.at[xslot]))
        hd = x_lo.shape[1]
        x = jnp.concatenate([x_lo.astype(BF16), x_hi.astype(BF16)], axis=1)

        def up(buf):
            acc = None
            for k0 in range(0, 2 * hd, FFN_K_CHUNK):
                part = jnp.dot(x[:, k0:k0 + FFN_K_CHUNK], buf[slot, k0:k0 + FFN_K_CHUNK, :].astype(BF16),
                               preferred_element_type=F32)
                acc = part if acc is None else acc + part
            return acc

        h1 = up(w1_buf)
        h3 = up(w3_buf)
        hmid = (h1 * jax.nn.sigmoid(h1) * h3).astype(BF16)
        y = jnp.dot(hmid, w2_buf[slot].astype(BF16), preferred_element_type=F32)
        _store_row_tiles(ys_ref, _pack_bf16_pair(y[:, :hd], y[:, hd:]))


def _grouped_ffn(x_tiles, row_src, w1, w3, w2, layer, block_expert, n_used):
    d = x_tiles.shape[1]
    n_tok = x_tiles.shape[0] // SUBLANES
    dm = w1.shape[2]
    de = w1.shape[3]
    nb = block_expert.shape[0]
    first = jnp.concatenate([jnp.ones((1,), jnp.int32), (block_expert[1:] != block_expert[:-1]).astype(jnp.int32)])
    run = jnp.cumsum(first) - 1
    n_runs = run[-1:] + 1
    hit = (run[None, :] == jnp.arange(nb, dtype=jnp.int32)[:, None]) & (first[None, :] == 1)
    run_expert = jnp.sum(jnp.where(hit, block_expert[None, :], 0), axis=1).astype(jnp.int32)
    blk = lambda b, nu, *_: (jnp.minimum(b, nu[0] - 1), 0)
    return pl.pallas_call(
        functools.partial(_ffn_kernel, layer=layer, n_tok=n_tok),
        grid_spec=pltpu.PrefetchScalarGridSpec(
            num_scalar_prefetch=6,
            grid=(nb,),
            in_specs=[pl.BlockSpec(memory_space=pl.ANY),
                      pl.BlockSpec(memory_space=pl.ANY),
                      pl.BlockSpec(memory_space=pl.ANY),
                      pl.BlockSpec(memory_space=pl.ANY)],
            out_specs=pl.BlockSpec((MOE_ROWS * SUBLANES, d), blk),
            scratch_shapes=[pltpu.VMEM((2, dm, de), F32), pltpu.VMEM((2, dm, de), F32),
                            pltpu.VMEM((2, de, dm), F32), pltpu.VMEM((2, MOE_ROWS * SUBLANES, d), jnp.uint32),
                            pltpu.SemaphoreType.DMA((2, 3)), pltpu.SemaphoreType.DMA((2,))]),
        out_shape=jax.ShapeDtypeStruct((nb * MOE_ROWS * SUBLANES, d), jnp.uint32),
        compiler_params=_params(("arbitrary",), 56),
        name="moe_ffn",
    )(n_used, run.astype(jnp.int32), first, run_expert, n_runs.astype(jnp.int32), row_src, x_tiles, w1, w3, w2)


def _combine_ple_kernel(pos_ref, ys_hbm, x1_ref, route_ref, g_ref, b_ref, wg_ref, bg_ref, p_ref, wp_ref,
                        o_ref, ob_ref, ybuf_a, ybuf_b, wg_bf, wp_bf, sems, *, tm, s):
    i = pl.program_id(0)
    n = pl.num_programs(0)

    def start_row(t0, buf, buf_id, r):
        first = r * SUBLANES if isinstance(r, int) else pl.multiple_of(r * SUBLANES, SUBLANES)
        for k in range(TOP_K):
            pltpu.make_async_copy(_row_tile(ys_hbm, pos_ref[k * s + t0 + r]),
                                  buf.at[k, pl.ds(first, SUBLANES), :], sems.at[buf_id, k]).start(priority=k)

    def wait_buffer(buf, buf_id):
        span = ys_hbm.at[pl.ds(0, tm * SUBLANES), :]
        for k in range(TOP_K):
            pltpu.make_async_copy(span, buf.at[k], sems.at[buf_id, k]).wait()

    @pl.when(i == 0)
    def _():
        def issue(g, carry):
            for u in range(ROW_UNROLL):
                start_row(0, ybuf_a, 0, g * ROW_UNROLL + u)
            return carry

        lax.fori_loop(0, tm // ROW_UNROLL, issue, 0)
        chunk = wg_ref.shape[0] // 8
        for r0 in range(0, wg_ref.shape[0], chunk):
            wg_bf[r0:r0 + chunk, :] = wg_ref[r0:r0 + chunk, :].astype(BF16)
        wp_bf[...] = wp_ref[...].astype(BF16)

    def step(cur, cur_id, nxt_buf, nxt_id):
        wait_buffer(cur, cur_id)
        nxt = jnp.minimum(i + 1, n - 1) * tm
        for r in range(tm):
            start_row(nxt, nxt_buf, nxt_id, r)

        route = route_ref[...]
        f = None
        for k in range(TOP_K):
            lo, hi = _unpack_bf16_pair(_load_row_tiles(cur.at[k]))
            fk = route[:, TOP_K + k:TOP_K + k + 1] * jnp.concatenate([lo, hi], axis=1)
            f = fk if f is None else f + fk
        x2 = _layer_norm_rows(ALPHA * x1_ref[...] + f, g_ref[...], b_ref[...])
        gate = jax.nn.sigmoid(jnp.dot(x2.astype(BF16), wg_bf[...], preferred_element_type=F32) + bg_ref[...])
        emb = jnp.dot(p_ref[...].astype(BF16), wp_bf[...], preferred_element_type=F32)
        out = x2 + gate * emb
        o_ref[...] = out
        ob_ref[...] = out.astype(BF16)

        @pl.when(i == n - 1)
        def _():
            wait_buffer(nxt_buf, nxt_id)

    @pl.when(i % 2 == 0)
    def _():
        step(ybuf_a, 0, ybuf_b, 1)

    @pl.when(i % 2 == 1)
    def _():
        step(ybuf_b, 1, ybuf_a, 0)


def _combine_ple(ys, pos, x1, route, ln_g, ln_b, w_gate, b_gate, p, w_proj, layer, *, tm=256):
    s, d = x1.shape
    pd = p.shape[-1]
    rows = lambda n: pl.BlockSpec((tm, n), lambda i, pos: (i, 0))
    p_rows = pl.BlockSpec((None, None, tm, pd), lambda i, pos: (layer, 0, i, 0))
    full = lambda a: pl.BlockSpec(a.shape, lambda i, pos: (0,) * a.ndim, pipeline_mode=pl.Buffered(1))
    of_layer = lambda a: pl.BlockSpec((None,) + a.shape[1:], lambda i, pos: (layer, 0, 0),
                                      pipeline_mode=pl.Buffered(1))
    g2, b2, bg = ln_g.reshape(1, d), ln_b.reshape(1, d), b_gate.reshape(1, d)
    return pl.pallas_call(
        functools.partial(_combine_ple_kernel, tm=tm, s=s),
        grid_spec=pltpu.PrefetchScalarGridSpec(
            num_scalar_prefetch=1,
            grid=(s // tm,),
            in_specs=[pl.BlockSpec(memory_space=pl.ANY), rows(d), rows(ROUTE_LANES), full(g2), full(b2),
                      of_layer(w_gate), full(bg), p_rows, of_layer(w_proj)],
            out_specs=[rows(d), rows(d)],
            scratch_shapes=[pltpu.VMEM((TOP_K, tm * SUBLANES, LANES), jnp.uint32),
                            pltpu.VMEM((TOP_K, tm * SUBLANES, LANES), jnp.uint32),
                            pltpu.VMEM(w_gate.shape[1:], BF16), pltpu.VMEM(w_proj.shape[1:], BF16),
                            pltpu.SemaphoreType.DMA((2, TOP_K))]),
        out_shape=[jax.ShapeDtypeStruct((s, d), F32), jax.ShapeDtypeStruct((s, d), BF16)],
        compiler_params=_params(("arbitrary",), 56),
        name="moe_combine_ple",
    )(pos, ys, x1, route, g2, b2, w_gate, bg, p, w_proj)


def _rglru_kernel(cur_ref, prev_ref, gate_ref, cw_ref, cb_ref, wa_ref, ba_ref, wi_ref, bi_ref, lam_ref,
                  o_ref, buf_ref, a_ref, b_ref, h_ref, carry_ref, *, tt, tc, width):
    i = pl.program_id(1)

    @pl.when(i == 0)
    def _():
        carry_ref[...] = jnp.zeros(carry_ref.shape, F32)

    buf_ref[0:RNN_HALO, :] = jnp.where(i > 0, prev_ref[...], 0.0)
    buf_ref[RNN_HALO:, :] = cur_ref[...]
    off = RNN_HALO - (width - 1)
    z = -lam_ref[...]
    softplus = jnp.log1p(jnp.exp(-jnp.abs(z))) + jnp.maximum(z, 0.0)

    def sigmoid(v):
        return 0.5 * jnp.tanh(0.5 * v) + 0.5

    for hh in range(tc // RNN_HEAD_DIM):
        cols = slice(hh * RNN_HEAD_DIM, (hh + 1) * RNN_HEAD_DIM)
        xh = jnp.broadcast_to(cb_ref[:, cols], (tt, RNN_HEAD_DIM))
        for k in range(width):
            xh = xh + buf_ref[off + k:off + k + tt, cols] * cw_ref[k:k + 1, cols]
        xhb = xh.astype(BF16)
        r = sigmoid(jnp.dot(xhb, wa_ref[hh], preferred_element_type=F32) + ba_ref[:, cols])
        ig = sigmoid(jnp.dot(xhb, wi_ref[hh], preferred_element_type=F32) + bi_ref[:, cols])
        log_a = (-RG_C) * r * softplus[:, cols]
        a = jnp.exp(log_a)
        bval = jnp.sqrt(-jnp.tanh(log_a) * (a * a + 1.0)) * (ig * xh)
        a_ref[:, cols] = a
        b_ref[:, cols] = bval

    def group(gi, h):
        base = pl.multiple_of(gi * 8, 8)
        for r in range(8):
            row = pl.ds(base + r, 1)
            h = a_ref[row, :] * h + b_ref[row, :]
            h_ref[row, :] = h
        return h

    carry_ref[...] = lax.fori_loop(0, tt // 8, group, carry_ref[...])
    o_ref[...] = (gate_ref[...].astype(F32) * h_ref[...]).astype(o_ref.dtype)


def _rglru(xr, gate, conv_w, conv_b, w_a, b_a, w_i, b_i, lam, *, tt=256, tc=2048):
    s, wd = xr.shape
    width = conv_w.shape[0]
    hpc = tc // RNN_HEAD_DIM
    hb = tt // RNN_HALO
    row = lambda v: v.reshape(1, wd)
    cvec = lambda: pl.BlockSpec((1, tc), lambda c, i: (0, c))
    heads = lambda: pl.BlockSpec((hpc, RNN_HEAD_DIM, RNN_HEAD_DIM), lambda c, i: (c, 0, 0))
    tile = lambda: pl.BlockSpec((tt, tc), lambda c, i: (i, c))
    return pl.pallas_call(
        functools.partial(_rglru_kernel, tt=tt, tc=tc, width=width),
        grid=(wd // tc, s // tt),
        in_specs=[tile(),
                  pl.BlockSpec((RNN_HALO, tc), lambda c, i: (jnp.maximum(i * hb - 1, 0), c)),
                  tile(),
                  pl.BlockSpec((width, tc), lambda c, i: (0, c)),
                  cvec(), heads(), cvec(), heads(), cvec(), cvec()],
        out_specs=tile(),
        out_shape=jax.ShapeDtypeStruct((s, wd), BF16),
        scratch_shapes=[pltpu.VMEM((tt + RNN_HALO, tc), F32), pltpu.VMEM((tt, tc), F32),
                        pltpu.VMEM((tt, tc), F32), pltpu.VMEM((tt, tc), F32), pltpu.VMEM((1, tc), F32)],
        compiler_params=_params(("parallel", "arbitrary"), 32),
        name="rglru",
    )(xr, xr, gate, conv_w, row(conv_b), w_a.astype(BF16), row(b_a), w_i.astype(BF16), row(b_i), row(lam))


def _moe_and_ple(mixed, i, p, ln_ffn_g, ln_ffn_b, moe_w1, moe_w3, moe_w2, ple_w_proj, ple_w_gate, ple_b_gate):
    x1, x1_pairs, route = mixed
    dest, block_expert, n_used = _moe_plan(route)
    row_src = _row_source(dest, block_expert.shape[0] * MOE_ROWS)
    ys = _grouped_ffn(x1_pairs, row_src, moe_w1, moe_w3, moe_w2, i, block_expert, n_used)
    return _combine_ple(ys, dest, x1, route, ln_ffn_g[i], ln_ffn_b[i], ple_w_gate, ple_b_gate[i], p,
                        ple_w_proj, i)


def kernel(x, p, positions, ev_w_in, ev_conv_w, ev_conv_b, ev_cnorm_g, ev_cnorm_b, ev_qnorm_g, ev_w_uq, ev_kvnorm_g, ev_w_ukv, ev_w_out, od_w_in, od_conv_w, od_conv_b, od_w_a, od_b_a, od_w_i, od_b_i, od_lam, od_w_out, ln_mix_g, ln_mix_b, ln_ffn_g, ln_ffn_b, moe_w_grp, moe_b_grp, moe_w_exp, moe_b_exp, moe_w1, moe_w3, moe_w2, ple_w_proj, ple_w_gate, ple_b_gate):
    x0 = x[0]
    moe =functools.partial(_moe_and_ple, p=p, ln_ffn_g=ln_ffn_g, ln_ffn_b=ln_ffn_b, moe_w1=moe_w1,
                            moe_w3=moe_w3, moe_w2=moe_w2, ple_w_proj=ple_w_proj, ple_w_gate=ple_w_gate,
                            ple_b_gate=ple_b_gate)

    cosm, sinm = _rope_tables(positions[0])
    x0b, qn, qr, kn, kr, v = _mla_proj(x0, ev_w_in[0, :, 2 * CONV_CH:], ev_qnorm_g[0], ev_w_uq[0], ev_kvnorm_g[0],
                                       ev_w_ukv[0], cosm, sinm)
    u =_pair_proj(x0b, ev_w_in[0].T, CONV_CH, mode="glu", transposed=True)
    ub = _conv_gln(u, ev_conv_w[0], ev_conv_b[0], ev_cnorm_g[0], ev_cnorm_b[0], groups=CONV_GROUPS)
    att = _attention(qn, qr, kn, kr, v)
    mixed0 = _out_ln_route(ub, att, ev_w_out[0].astype(BF16), x0, ln_mix_g[0], ln_mix_b[0],
                           moe_w_grp[0], moe_b_grp[0], moe_w_exp[0], moe_b_exp[0])
    x3, x3b = moe(mixed0, 0)

    gate, xr = _pair_proj(x3b, od_w_in.reshape(D_MODEL, -1), RNN_WIDTH, mode="rnn")
    y = _rglru(xr, gate, od_conv_w[0], od_conv_b[0], od_w_a[0], od_b_a[0], od_w_i[0], od_b_i[0], od_lam[0])
    mixed1 = _out_ln_route(y, y, od_w_out[0].astype(BF16), x3, ln_mix_g[1], ln_mix_b[1],
                           moe_w_grp[1], moe_b_grp[1], moe_w_exp[1], moe_b_exp[1], col1=0, col2=1)
    x6, _ = moe(mixed1, 1)
    return x6[None]
```

```python
import functools
import math

import jax
import jax.numpy as jnp
from jax import lax
from jax.experimental import pallas as pl
from jax.experimental.pallas import tpu as pltpu

F32 = jnp.float32
BF16 = jnp.bfloat16

D_MODEL = 2048
SEQ = 8192
DEPTH = 2
CONV_CH = 1024
CONV_GROUPS = 16
CONV_WIDTH = 31
MLA_HEADS = 8
QK_NOPE = 128
QK_ROPE = 64
V_HEAD = 128
Q_RANK = 512
KV_RANK = 256
ROPE_THETA = 10000.0
RNN_WIDTH = 2048
RNN_HEADS = 16
RNN_HEAD_DIM = RNN_WIDTH // RNN_HEADS
RNN_CONV_WIDTH = 4
RG_C = 8.0
N_GROUPS = 8
EXPERTS_PER_GROUP = 8
N_EXPERTS = N_GROUPS * EXPERTS_PER_GROUP
TOP_K = 2
D_EXPERT = 512
PLE_DIM = 256
ALPHA = (2 * DEPTH) ** 0.25
LN_EPS = 1e-5
RMS_EPS = 1e-6

LANES = 128
SUBLANES = 8
CONV_HALO = 32
RNN_HALO = 8
MOE_ROWS = 256
ROUTE_LANES = 128
MIB = 2 ** 20


def _params(semantics, vmem_mib):
    return pltpu.CompilerParams(dimension_semantics=semantics, vmem_limit_bytes=vmem_mib * MIB)


def _layer_norm_rows(z, g, b):
    mu = jnp.mean(z, axis=-1, keepdims=True)
    d = z - mu
    var = jnp.mean(d * d, axis=-1, keepdims=True)
    return d * lax.rsqrt(var + LN_EPS) * g + b


def _pack_bf16_pair(a, b):
    def rounded_bits(v):
        u = lax.bitcast_convert_type(v, jnp.uint32)
        return u + jnp.uint32(0x7FFF) + ((u >> 16) & jnp.uint32(1))

    return (rounded_bits(a) & jnp.uint32(0xFFFF0000)) | (rounded_bits(b) >> 16)


def _unpack_bf16_pair(u):
    a = lax.bitcast_convert_type(u & jnp.uint32(0xFFFF0000), F32)
    b = lax.bitcast_convert_type(u << 16, F32)
    return a, b


def _store_row_tiles(ref, packed):
    n, w = packed.shape
    assert w == SUBLANES * LANES
    for j in range(SUBLANES):
        ref[pl.ds(j, n, stride=SUBLANES), :] = packed[:, j * LANES:(j + 1) * LANES]


def _load_row_tiles(ref):
    n = ref.shape[0] // SUBLANES
    return jnp.concatenate([ref[pl.ds(j, n, stride=SUBLANES), :] for j in range(SUBLANES)], axis=1)


def _split_bf16(v):
    hi = v.astype(BF16)
    lo = (v - hi.astype(F32)).astype(BF16)
    return hi, lo


def _gelu_tanh(x):
    c = math.sqrt(2.0 / math.pi)
    return 0.5 * x * (1.0 + jnp.tanh(c * (x + 0.044715 * (x * x * x))))


def _pair_proj_kernel(x_ref, wa_ref, wb_ref, *refs, mode, transposed):
    out_refs, (wa_bf, wb_bf) = refs[:-2], refs[-2:]

    @pl.when(pl.program_id(1) == 0)
    def _():
        for src, dst in ((wa_ref, wa_bf), (wb_ref, wb_bf)):
            w = src[...]
            dst[...] = (w.T if transposed else w).astype(BF16)

    x = x_ref[...]
    a = jnp.dot(x, wa_bf[...], preferred_element_type=F32)
    b = jnp.dot(x, wb_bf[...], preferred_element_type=F32)
    if mode == "glu":
        out_refs[0][...] = a * jax.nn.sigmoid(b)
    else:
        out_refs[0][...] = _gelu_tanh(a).astype(out_refs[0].dtype)
        out_refs[1][...] = b


def _pair_proj(x, w, n, *, mode, transposed=False, tm=512, tn=512):
    m, k = x.shape
    nb = n // tn
    tile = pl.BlockSpec((tm, tn), lambda j, i: (i, j))
    if transposed:
        w_specs = [pl.BlockSpec((tn, k), lambda j, i: (j, 0)), pl.BlockSpec((tn, k), lambda j, i: (j + nb, 0))]
    else:
        w_specs = [pl.BlockSpec((k, tn), lambda j, i: (0, j)), pl.BlockSpec((k, tn), lambda j, i: (0, j + nb))]
    if mode == "glu":
        out_specs, out_shape = tile, jax.ShapeDtypeStruct((m, n), F32)
    else:
        out_specs = [tile, tile]
        out_shape = [jax.ShapeDtypeStruct((m, n), BF16), jax.ShapeDtypeStruct((m, n), F32)]
    return pl.pallas_call(
        functools.partial(_pair_proj_kernel, mode=mode, transposed=transposed),
        grid=(nb, m // tm),
        in_specs=[pl.BlockSpec((tm, k), lambda j, i: (i, 0))] + w_specs,
        out_specs=out_specs,
        out_shape=out_shape,
        scratch_shapes=[pltpu.VMEM((k, tn), BF16), pltpu.VMEM((k, tn), BF16)],
        compiler_params=_params(("parallel", "arbitrary"), 48),
        name=mode + "_proj",
    )(x, w, w)


def _conv_gln_kernel(cur_ref, prev_ref, w_ref, b_ref, g_ref, beta_ref, gm_ref, o_ref,
                     buf_ref, sh_ref, y_ref, *, tt, ch, width, rc, cc):
    i = pl.program_id(0)
    buf_ref[0:CONV_HALO, :] = jnp.where(i > 0, prev_ref[...], 0.0)
    buf_ref[CONV_HALO:, :] = cur_ref[...]
    off = CONV_HALO - (width - 1)
    sh_rows = sh_ref.shape[1]
    step = 7 * SUBLANES
    assert sh_rows % step == 0
    for b in range(1, SUBLANES):
        for c0 in range(0, ch, cc):
            for r0 in range(0, sh_rows, step):
                sh_ref[b - 1, r0:r0 + step, c0:c0 + cc] = buf_ref[r0 + b:r0 + b + step, c0:c0 + cc]
    for c0 in range(0, ch, cc):
        for r0 in range(0, tt, rc):
            acc = jnp.broadcast_to(b_ref[:, c0:c0 + cc], (rc, cc))
            for k in range(width):
                q, b = divmod(off + k, SUBLANES)
                lo = r0 + q * SUBLANES
                src = buf_ref[lo:lo + rc, c0:c0 + cc] if b == 0 else sh_ref[b - 1, lo:lo + rc, c0:c0 + cc]
                acc = acc + src * w_ref[k:k + 1, c0:c0 + cc]
            y_ref[r0:r0 + rc, c0:c0 + cc] = acc
    gm = gm_ref[...]

    def seg_mean(v):
        hi, lo = _split_bf16(v)
        return (jnp.dot(hi, gm, preferred_element_type=F32)
                + jnp.dot(lo, gm, preferred_element_type=F32))

    for c0 in range(0, ch, LANES):
        y = y_ref[:, c0:c0 + LANES]
        d = y - seg_mean(y)
        var = seg_mean(d * d)
        z = d * lax.rsqrt(var + LN_EPS) * g_ref[:, c0:c0 + LANES] + beta_ref[:, c0:c0 + LANES]
        o_ref[:, c0:c0 + LANES] = (z * jax.nn.sigmoid(z)).astype(o_ref.dtype)


def _conv_gln(u, w, b, g, beta, *, groups, tt=256):
    s, ch = u.shape
    width = w.shape[0]
    gsz = ch // groups
    assert LANES % gsz == 0 and width - 1 <= CONV_HALO and tt % CONV_HALO == 0
    wpad = jnp.zeros((CONV_HALO, ch), F32).at[:width].set(w)
    lane = jnp.arange(LANES)
    gm = jnp.where((lane[:, None] // gsz) == (lane[None, :] // gsz), 1.0 / gsz, 0.0).astype(BF16)
    hb = tt // CONV_HALO
    row = lambda v: v.reshape(1, ch)
    return pl.pallas_call(
        functools.partial(_conv_gln_kernel, tt=tt, ch=ch, width=width, rc=32, cc=256),
        grid=(s // tt,),
        in_specs=[pl.BlockSpec((tt, ch), lambda i: (i, 0)),
                  pl.BlockSpec((CONV_HALO, ch), lambda i: (jnp.maximum(i * hb - 1, 0), 0)),
                  pl.BlockSpec((CONV_HALO, ch), lambda i: (0, 0)),
                  pl.BlockSpec((1, ch), lambda i: (0, 0)),
                  pl.BlockSpec((1, ch), lambda i: (0, 0)),
                  pl.BlockSpec((1, ch), lambda i: (0, 0)),
                  pl.BlockSpec((LANES, LANES), lambda i: (0, 0))],
        out_specs=pl.BlockSpec((tt, ch), lambda i: (i, 0)),
        out_shape=jax.ShapeDtypeStruct((s, ch), BF16),
        scratch_shapes=[pltpu.VMEM((tt + CONV_HALO, ch), F32),
                        pltpu.VMEM((SUBLANES - 1, tt + CONV_HALO - SUBLANES, ch), F32),
                        pltpu.VMEM((tt, ch), F32)],
        compiler_params=_params(("parallel",), 40),
        name="conv_gln",
    )(u, u, wpad, row(b), row(g), row(beta), gm)


def _rope_table_kernel(pos_ref, invf_ref, cos_ref, sin_ref):
    ang = pos_ref[...].astype(F32) * invf_ref[...]
    cos_ref[...] = jnp.cos(ang)
    sin_ref[...] = jnp.sin(ang)


def _rope_tables(positions):
    s = positions.shape[0]
    half = QK_ROPE // 2
    per_row = LANES // half
    inv_freq = 1.0 / (ROPE_THETA ** (jnp.arange(0, QK_ROPE, 2, dtype=F32) / QK_ROPE))
    pos_rep = jnp.repeat(positions.reshape(s // per_row, per_row), half, axis=1)
    invf = jnp.tile(inv_freq, per_row).reshape(1, LANES)
    cos, sin = pl.pallas_call(
        _rope_table_kernel,
        out_shape=[jax.ShapeDtypeStruct((s // per_row, LANES), F32)] * 2,
        name="rope_table",
    )(pos_rep, invf)
    cos = cos.reshape(s, half)
    sin = sin.reshape(s, half)
    zero = jnp.zeros((s, LANES - QK_ROPE), F32)
    return jnp.concatenate([cos, cos, zero], axis=1), jnp.concatenate([-sin, sin, zero], axis=1)


def _mla_proj_kernel(x_ref, wm_ref, qg_ref, kvg_ref, wq_ref, wkv_ref, c_ref, s_ref,
                     xb_ref, qn_ref, qr_ref, kn_ref, kr_ref, v_ref, *, scale):
    nq = MLA_HEADS * QK_NOPE
    xb = x_ref[...].astype(BF16)
    xb_ref[...] = xb
    c = jnp.dot(xb, wm_ref[...], preferred_element_type=F32)
    cq = c[:, :Q_RANK]
    ckv = c[:, Q_RANK:Q_RANK + KV_RANK]
    kr_pad = c[:, Q_RANK + KV_RANK:Q_RANK + KV_RANK + LANES]
    kr_rot = c[:, Q_RANK + KV_RANK + LANES:]
    cqn = cq * lax.rsqrt(jnp.mean(cq * cq, axis=-1, keepdims=True) + RMS_EPS) * qg_ref[...]
    ckvn = ckv * lax.rsqrt(jnp.mean(ckv * ckv, axis=-1, keepdims=True) + RMS_EPS) * kvg_ref[...]
    q = jnp.dot(cqn.astype(BF16), wq_ref[...], preferred_element_type=F32)
    kv = jnp.dot(ckvn.astype(BF16), wkv_ref[...], preferred_element_type=F32)
    cosm = c_ref[...]
    sinm = s_ref[...]
    qn_ref[...] = (q[:, :nq] * scale).astype(BF16)
    for h in range(MLA_HEADS):
        lo = nq + h * LANES
        rot = q[:, lo:lo + LANES] * cosm + q[:, lo + nq:lo + nq + LANES] * sinm
        qr_ref[:, h * LANES:(h + 1) * LANES] = (rot * scale).astype(BF16)
    kn_ref[...] = kv[:, :nq].astype(BF16)
    v_ref[...] = kv[:, nq:].astype(BF16)
    kr_ref[...] = (kr_pad * cosm + kr_rot * sinm).astype(BF16)


def _rope_swap(w):
    half = QK_ROPE // 2
    return jnp.concatenate([w[..., half:], w[..., :half]], axis=-1)


def _mla_proj(x, w_in_mla, qnorm_g, w_uq, kvnorm_g, w_ukv, cosm, sinm, *, tm=256):
    s, d = x.shape
    nq = MLA_HEADS * QK_NOPE
    pad = lambda w: jnp.concatenate([w, jnp.zeros(w.shape[:-1] + (LANES - QK_ROPE,), w.dtype)], axis=-1)
    w_kr = w_in_mla[:, Q_RANK + KV_RANK:]
    wm = jnp.concatenate([w_in_mla[:, :Q_RANK + KV_RANK], pad(w_kr), pad(_rope_swap(w_kr))], axis=1).astype(BF16)
    wq3 = w_uq.reshape(Q_RANK, MLA_HEADS, QK_NOPE + QK_ROPE)
    wq_rope = wq3[:, :, QK_NOPE:]
    wq = jnp.concatenate([wq3[:, :, :QK_NOPE].reshape(Q_RANK, nq),
                          pad(wq_rope).reshape(Q_RANK, MLA_HEADS * LANES),
                          pad(_rope_swap(wq_rope)).reshape(Q_RANK, MLA_HEADS * LANES)], axis=1).astype(BF16)
    wkv3 = w_ukv.reshape(KV_RANK, MLA_HEADS, QK_NOPE + V_HEAD)
    wkv = jnp.concatenate([wkv3[:, :, :QK_NOPE].reshape(KV_RANK, nq),
                           wkv3[:, :, QK_NOPE:].reshape(KV_RANK, MLA_HEADS * V_HEAD)], axis=1).astype(BF16)
    scale = (QK_NOPE + QK_ROPE) ** -0.5 * math.log2(math.e)
    full = lambda a: pl.BlockSpec(a.shape, lambda i: (0,) * a.ndim)
    rows = lambda n: pl.BlockSpec((tm, n), lambda i: (i, 0))
    qg = qnorm_g.reshape(1, Q_RANK)
    kvg = kvnorm_g.reshape(1, KV_RANK)
    return pl.pallas_call(
        functools.partial(_mla_proj_kernel, scale=scale),
        grid=(s // tm,),
        in_specs=[rows(d), full(wm), full(qg), full(kvg), full(wq), full(wkv), rows(LANES), rows(LANES)],
        out_specs=[rows(d), rows(nq), rows(MLA_HEADS * LANES), rows(nq), rows(LANES), rows(MLA_HEADS * V_HEAD)],
        out_shape=[jax.ShapeDtypeStruct((s, d), BF16),
                   jax.ShapeDtypeStruct((s, nq), BF16),
                   jax.ShapeDtypeStruct((s, MLA_HEADS * LANES), BF16),
                   jax.ShapeDtypeStruct((s, nq), BF16),
                   jax.ShapeDtypeStruct((s, LANES), BF16),
                   jax.ShapeDtypeStruct((s, MLA_HEADS * V_HEAD), BF16)],
        compiler_params=_params(("parallel",), 48),
        name="mla_proj",
    )(x, wm, qg, kvg, wq, wkv, cosm, sinm)


def _attn_kernel(qn_ref, qr_ref, kn_ref, kr_ref, v_ref, o_ref, m_ref, acc_ref, *, tq, tk, hb):
    qi = pl.program_id(1)
    per_q = tq // tk
    m_ref[...] = jnp.full(m_ref.shape, -jnp.inf, F32)
    acc_ref[...] = jnp.zeros(acc_ref.shape, F32)
    ones = jnp.ones((tk, LANES), BF16)
    head_cols = [slice(h * LANES, (h + 1) * LANES) for h in range(hb)]
    qs = [jnp.concatenate([qn_ref[:, c], qr_ref[:, c]], axis=1) for c in head_cols]

    def chunk(j, diag):
        rows = pl.ds(pl.multiple_of(j * tk, tk), tk)
        k_rope = kr_ref[rows, :]
        q0 = 0 if diag is None else diag * tk
        for h, c in enumerate(head_cols):
            k = jnp.concatenate([kn_ref[rows, c], k_rope], axis=1)
            s = lax.dot_general(qs[h][q0:], k, (((1,), (1,)), ((), ())), preferred_element_type=F32)
            if diag is not None:
                keep = lax.broadcasted_iota(jnp.int32, s.shape, 0) >= lax.broadcasted_iota(jnp.int32, s.shape, 1)
                s = jnp.where(keep, s, -jnp.inf)
            m_old = m_ref[h, q0:, :]
            m_new = jnp.maximum(m_old, jnp.max(s, axis=-1, keepdims=True))
            alpha = jnp.exp2(m_old - m_new)
            p = jnp.exp2(s - jnp.concatenate([m_new] * (tk // LANES), axis=1))
            v_ext = jnp.concatenate([v_ref[rows, c], ones], axis=1)
            acc_ref[h, q0:, :] = (jnp.concatenate([alpha, alpha], axis=1) * acc_ref[h, q0:, :]
                                  + jnp.dot(p.astype(BF16), v_ext, preferred_element_type=F32))
            m_ref[h, q0:, :] = m_new

    def body(j, carry):
        chunk(j, None)
        return carry

    lax.fori_loop(0, qi * per_q, body, 0)
    for d in range(per_q):
        chunk(qi * per_q + d, d)
    for h, c in enumerate(head_cols):
        acc = acc_ref[h]
        o_ref[:, c] = (acc[:, :V_HEAD] / acc[:, V_HEAD:]).astype(o_ref.dtype)


def _attention(qn, qr, kn, kr, v, *, tq=1024, tk=512, hb=4):
    s = qn.shape[0]
    wide = hb * LANES
    tile = lambda: pl.BlockSpec((tq, wide), lambda h, i: (i, h))
    keys = lambda: pl.BlockSpec((s, wide), lambda h, i: (0, h), pipeline_mode=pl.Buffered(1))
    return pl.pallas_call(
        functools.partial(_attn_kernel, tq=tq, tk=tk, hb=hb),
        grid=(MLA_HEADS // hb, s // tq),
        in_specs=[tile(), tile(), keys(),
                  pl.BlockSpec((s, LANES), lambda h, i: (0, 0), pipeline_mode=pl.Buffered(1)), keys()],
        out_specs=tile(),
        out_shape=jax.ShapeDtypeStruct((s, MLA_HEADS * V_HEAD), BF16),
        scratch_shapes=[pltpu.VMEM((hb, tq, LANES), F32), pltpu.VMEM((hb, tq, 2 * V_HEAD), F32)],
        compiler_params=_params(("parallel", "parallel"), 48),
        name="mla_attention",
    )(qn, qr, kn, kr, v)


def _route(logits):
    lane = lax.broadcasted_iota(jnp.int32, logits.shape, 1).astype(F32)
    big = float(2 * ROUTE_LANES)
    neg = -jnp.inf
    gl = jnp.where(lane < N_GROUPS, logits, neg)
    gmax = jnp.max(gl, axis=-1, keepdims=True)
    gsum = jnp.sum(jnp.exp(gl - gmax), axis=-1, keepdims=True)
    g_prob = 1.0 / gsum
    g_idx = jnp.min(jnp.where(gl == gmax, lane, big), axis=-1, keepdims=True)
    lo = N_GROUPS + g_idx * EXPERTS_PER_GROUP
    el = jnp.where((lane >= lo) & (lane < lo + EXPERTS_PER_GROUP), logits, neg)
    emax = jnp.max(el, axis=-1, keepdims=True)
    esum = jnp.sum(jnp.exp(el - emax), axis=-1, keepdims=True)
    i0 = jnp.min(jnp.where(el == emax, lane, big), axis=-1, keepdims=True)
    el2 = jnp.where(lane == i0, neg, el)
    emax2 = jnp.max(el2, axis=-1, keepdims=True)
    i1 = jnp.min(jnp.where(el2 == emax2, lane, big), axis=-1, keepdims=True)
    p0 = 1.0 / esum
    p1 = jnp.exp(emax2 - emax) / esum
    g0 = g_prob * p0 / (p0 + p1)
    g1 = g_prob * p1 / (p0 + p1)
    e0 = i0 - N_GROUPS
    e1 = i1 - N_GROUPS
    return jnp.where(lane == 0, e0, jnp.where(lane == 1, e1, jnp.where(lane == 2, g0, jnp.where(lane == 3, g1, 0.0))))


def _out_ln_route_kernel(a1_ref, a2_ref, w_ref, x_ref, g_ref, b_ref, wrh_ref, br_ref,
                         x1_ref, x1p_ref, route_ref, w_bf, *, sub):
    half = a1_ref.shape[1]

    @pl.when(pl.program_id(0) == 0)
    def _():
        chunk = w_ref.shape[0] // 8
        for r0 in range(0, w_ref.shape[0], chunk):
            w_bf[r0:r0 + chunk, :] = w_ref[r0:r0 + chunk, :].astype(BF16)

    for r0 in range(0, x_ref.shape[0], sub):
        rows = slice(r0, r0 + sub)
        m = (jnp.dot(a1_ref[rows, :], w_bf[:half, :], preferred_element_type=F32)
             + jnp.dot(a2_ref[rows, :], w_bf[half:, :], preferred_element_type=F32))
        x1 = _layer_norm_rows(ALPHA * x_ref[rows, :] + m, g_ref[...], b_ref[...])
        x1_ref[rows, :] = x1
        hd = x1.shape[1] // 2
        _store_row_tiles(x1p_ref.at[pl.ds(r0 * SUBLANES, sub * SUBLANES), :], _pack_bf16_pair(x1[:, :hd], x1[:, hd:]))
        xh, xl = _split_bf16(x1)
        both = jnp.dot(xh, wrh_ref[...], preferred_element_type=F32)
        logits = (both[:, :ROUTE_LANES] + both[:, ROUTE_LANES:]
                  + jnp.dot(xl, wrh_ref[:, :ROUTE_LANES], preferred_element_type=F32) + br_ref[...])
        route_ref[rows, :] = _route(logits)


def _out_ln_route(a1, a2, w, x, ln_g, ln_b, w_grp, b_grp, w_exp, b_exp, *, col1=0, col2=0, tm=512, sub=512):
    s, d = x.shape
    half = w.shape[1] // 2
    pad = ROUTE_LANES - N_GROUPS - N_EXPERTS
    wr = jnp.concatenate([w_grp, w_exp, jnp.zeros((d, pad), F32)], axis=1)
    br = jnp.concatenate([b_grp, b_exp, jnp.zeros((pad,), F32)]).reshape(1, ROUTE_LANES)
    wr_hi = wr.astype(BF16)
    wrh = jnp.concatenate([wr_hi, (wr - wr_hi.astype(F32)).astype(BF16)], axis=1)
    full = lambda a: pl.BlockSpec(a.shape, lambda i: (0,) * a.ndim)
    rows = lambda n: pl.BlockSpec((tm, n), lambda i: (i, 0))
    g2 = ln_g.reshape(1, d)
    b2 = ln_b.reshape(1, d)
    return pl.pallas_call(
        functools.partial(_out_ln_route_kernel, sub=sub),
        grid=(s // tm,),
        in_specs=[pl.BlockSpec((tm, half), lambda i: (i, col1)), pl.BlockSpec((tm, half), lambda i: (i, col2)),
                  pl.BlockSpec((None,) + w.shape[1:], lambda i: (0, 0, 0), pipeline_mode=pl.Buffered(1)),
                  rows(d), full(g2), full(b2), full(wrh), full(br)],
        out_specs=[rows(d), pl.BlockSpec((tm * SUBLANES, LANES), lambda i: (i, 0)), rows(ROUTE_LANES)],
        out_shape=[jax.ShapeDtypeStruct((s, d), F32), jax.ShapeDtypeStruct((s * SUBLANES, LANES), jnp.uint32),
                   jax.ShapeDtypeStruct((s, ROUTE_LANES), F32)],
        scratch_shapes=[pltpu.VMEM(w.shape[1:], BF16)],
        compiler_params=_params(("arbitrary",), 58),
        name="out_ln_route",
    )(a1, a2, w, x, g2, b2, wrh, br)


def _rank_kernel(route_ref, tri_ref, rank_ref, cnt_ref, carry_ref):
    @pl.when(pl.program_id(0) == 0)
    def _():
        carry_ref[...] = jnp.zeros(carry_ref.shape, F32)

    route = route_ref[...]
    lane = lax.broadcasted_iota(jnp.int32, route.shape, 1).astype(F32)
    oh0 = jnp.where(lane == route[:, 0:1], 1.0, 0.0)
    oh1 = jnp.where(lane == route[:, 1:2], 1.0, 0.0)
    both = oh0 + oh1
    before = jnp.dot(tri_ref[...], both.astype(BF16), preferred_element_type=F32) + carry_ref[...]
    r0 = jnp.sum(before * oh0, axis=-1, keepdims=True)
    r1 = jnp.sum(before * oh1, axis=-1, keepdims=True)
    rank_ref[...] = jnp.where(lane == 0.0, r0, jnp.where(lane == 1.0, r1, 0.0))
    carry_ref[...] = carry_ref[...] + jnp.sum(both, axis=0, keepdims=True)
    cnt_ref[...] = carry_ref[...]


def _dest_kernel(route_ref, rank_ref, pstart_ref, dest_ref):
    route = route_ref[...]
    rank = rank_ref[...]
    lane = lax.broadcasted_iota(jnp.int32, route.shape, 1).astype(F32)
    ps = pstart_ref[...]
    d0 = jnp.sum(jnp.where(lane == route[:, 0:1], ps, 0.0), axis=-1, keepdims=True) + rank[:, 0:1]
    d1 = jnp.sum(jnp.where(lane == route[:, 1:2], ps, 0.0), axis=-1, keepdims=True) + rank[:, 1:2]
    dest = jnp.where(lane == 0.0, d0, jnp.where(lane == 1.0, d1, 0.0)) * float(SUBLANES)
    dest_ref[...] = dest.astype(jnp.int32)


def _moe_plan(route, *, tb=512):
    s = route.shape[0]
    tri = (jnp.arange(tb)[:, None] > jnp.arange(tb)[None, :]).astype(BF16)
    rows = pl.BlockSpec((tb, ROUTE_LANES), lambda i: (i, 0))
    one = lambda: pl.BlockSpec((1, ROUTE_LANES), lambda i: (0, 0))
    rank, cnt = pl.pallas_call(
        _rank_kernel,
        grid=(s // tb,),
        in_specs=[rows, pl.BlockSpec((tb, tb), lambda i: (0, 0))],
        out_specs=[rows, one()],
        out_shape=[jax.ShapeDtypeStruct((s, ROUTE_LANES), F32), jax.ShapeDtypeStruct((1, ROUTE_LANES), F32)],
        scratch_shapes=[pltpu.VMEM((1, ROUTE_LANES), F32)],
        compiler_params=_params(("arbitrary",), 32),
        name="moe_rank",
    )(route, tri)
    counts = cnt[0, :N_EXPERTS].astype(jnp.int32)
    pcounts = (counts + MOE_ROWS - 1) // MOE_ROWS * MOE_ROWS
    pends = jnp.cumsum(pcounts)
    pstarts = pends - pcounts
    n_blocks = s * TOP_K // MOE_ROWS + N_EXPERTS
    n_used = pends[-1] // MOE_ROWS
    blk = jnp.minimum(jnp.arange(n_blocks, dtype=jnp.int32), n_used - 1) * MOE_ROWS
    block_expert = jnp.minimum(jnp.sum(pends[None, :] <= blk[:, None], axis=1), N_EXPERTS - 1).astype(jnp.int32)
    ps = jnp.zeros((1, ROUTE_LANES), F32).at[0, :N_EXPERTS].set(pstarts.astype(F32))
    dest = pl.pallas_call(
        _dest_kernel,
        grid=(s // tb,),
        in_specs=[rows, rows, one()],
        out_specs=rows,
        out_shape=jax.ShapeDtypeStruct((s, ROUTE_LANES), jnp.int32),
        compiler_params=_params(("parallel",), 32),
        name="moe_dest",
    )(route, rank, ps)
    dest = dest[:, :TOP_K].T.reshape(TOP_K * s)
    return dest, block_expert, n_used.astype(jnp.int32).reshape(1)


ROW_UNROLL = 8


def _row_tile(ref, first_row):
    return ref.at[pl.ds(pl.multiple_of(first_row, SUBLANES), SUBLANES), :]


def _dispatch_kernel(dest_ref, x_ref, xs_hbm, sems, *, tm, s):
    t0 = pl.program_id(0) * tm

    def issue(g, carry):
        for u in range(ROW_UNROLL):
            r = g * ROW_UNROLL + u
            src = _row_tile(x_ref, r * SUBLANES)
            for k in range(TOP_K):
                pltpu.make_async_copy(src, _row_tile(xs_hbm, dest_ref[k * s + t0 + r]), sems.at[k]).start(priority=k)
        return carry

    lax.fori_loop(0, tm // ROW_UNROLL, issue, 0)
    for k in range(TOP_K):
        pltpu.make_async_copy(x_ref, xs_hbm.at[pl.ds(0, tm * SUBLANES), :], sems.at[k]).wait()


def _dispatch_rows(x, dest, n_blocks, *, tm=256):
    s = x.shape[0] // SUBLANES
    return pl.pallas_call(
        functools.partial(_dispatch_kernel, tm=tm, s=s),
        grid_spec=pltpu.PrefetchScalarGridSpec(
            num_scalar_prefetch=1,
            grid=(s // tm,),
            in_specs=[pl.BlockSpec((tm * SUBLANES, LANES), lambda i, dest: (i, 0))],
            out_specs=pl.BlockSpec(memory_space=pl.ANY),
            scratch_shapes=[pltpu.SemaphoreType.DMA((TOP_K,))]),
        out_shape=jax.ShapeDtypeStruct((n_blocks * MOE_ROWS * SUBLANES, LANES), x.dtype),
        compiler_params=_params(("arbitrary",), 32),
        name="moe_dispatch",
    )(dest, x)


FFN_K_CHUNK = 512


def _ffn_kernel(nused_ref, run_ref, first_ref, rexp_ref, nruns_ref, xs_ref, w1_hbm, w3_hbm, w2_hbm, ys_ref,
                w1_buf, w3_buf, w2_buf, sems, *, layer):
    b = pl.program_id(0)

    def weight_copies(run, slot):
        e = rexp_ref[run]
        return [pltpu.make_async_copy(w_hbm.at[layer, e], buf.at[slot], sems.at[slot, j])
                for j, (w_hbm, buf) in enumerate(((w1_hbm, w1_buf), (w3_hbm, w3_buf), (w2_hbm, w2_buf)))]

    @pl.when(b == 0)
    def _():
        for c in weight_copies(0, 0):
            c.start()

    @pl.when(b < nused_ref[0])
    def _():
        run = run_ref[b]
        slot = run % 2

        @pl.when(first_ref[b] == 1)
        def _():
            @pl.when(run + 1 < nruns_ref[0])
            def _():
                for c in weight_copies(run + 1, 1 - slot):
                    c.start()

            for c in weight_copies(run, slot):
                c.wait()

        x_lo, x_hi = _unpack_bf16_pair(_load_row_tiles(xs_ref))
        hd = x_lo.shape[1]
        x = jnp.concatenate([x_lo.astype(BF16), x_hi.astype(BF16)], axis=1)

        def up(buf):
            acc = None
            for k0 in range(0, 2 * hd, FFN_K_CHUNK):
                part = jnp.dot(x[:, k0:k0 + FFN_K_CHUNK], buf[slot, k0:k0 + FFN_K_CHUNK, :].astype(BF16),
                               preferred_element_type=F32)
                acc = part if acc is None else acc + part
            return acc

        h1 = up(w1_buf)
        h3 = up(w3_buf)
        hmid = (h1 * jax.nn.sigmoid(h1) * h3).astype(BF16)
        y = jnp.dot(hmid, w2_buf[slot].astype(BF16), preferred_element_type=F32)
        _store_row_tiles(ys_ref, _pack_bf16_pair(y[:, :hd], y[:, hd:]))


def _grouped_ffn(xs, w1, w3, w2, layer, block_expert, n_used):
    d = xs.shape[1]
    dm = w1.shape[2]
    de = w1.shape[3]
    nb = block_expert.shape[0]
    first = jnp.concatenate([jnp.ones((1,), jnp.int32), (block_expert[1:] != block_expert[:-1]).astype(jnp.int32)])
    run = jnp.cumsum(first) - 1
    n_runs = run[-1:] + 1
    hit = (run[None, :] == jnp.arange(nb, dtype=jnp.int32)[:, None]) & (first[None, :] == 1)
    run_expert = jnp.sum(jnp.where(hit, block_expert[None, :], 0), axis=1).astype(jnp.int32)
    blk = lambda b, nu, *_: (jnp.minimum(b, nu[0] - 1), 0)
    return pl.pallas_call(
        functools.partial(_ffn_kernel, layer=layer),
        grid_spec=pltpu.PrefetchScalarGridSpec(
            num_scalar_prefetch=5,
            grid=(nb,),
            in_specs=[pl.BlockSpec((MOE_ROWS * SUBLANES, d), blk),
                      pl.BlockSpec(memory_space=pl.ANY),
                      pl.BlockSpec(memory_space=pl.ANY),
                      pl.BlockSpec(memory_space=pl.ANY)],
            out_specs=pl.BlockSpec((MOE_ROWS * SUBLANES, d), blk),
            scratch_shapes=[pltpu.VMEM((2, dm, de), F32), pltpu.VMEM((2, dm, de), F32),
                            pltpu.VMEM((2, de, dm), F32), pltpu.SemaphoreType.DMA((2, 3))]),
        out_shape=jax.ShapeDtypeStruct(xs.shape, jnp.uint32),
        compiler_params=_params(("arbitrary",), 56),
        name="moe_ffn",
    )(n_used, run.astype(jnp.int32), first, run_expert, n_runs.astype(jnp.int32), xs, w1, w3, w2)


def _combine_ple_kernel(pos_ref, ys_hbm, x1_ref, route_ref, g_ref, b_ref, wg_ref, bg_ref, p_ref, wp_ref,
                        o_ref, ob_ref, ybuf_a, ybuf_b, wg_bf, wp_bf, sems, *, tm, s):
    i = pl.program_id(0)
    n = pl.num_programs(0)

    def start_row(t0, buf, buf_id, r):
        first = r * SUBLANES if isinstance(r, int) else pl.multiple_of(r * SUBLANES, SUBLANES)
        for k in range(TOP_K):
            pltpu.make_async_copy(_row_tile(ys_hbm, pos_ref[k * s + t0 + r]),
                                  buf.at[k, pl.ds(first, SUBLANES), :], sems.at[buf_id, k]).start(priority=k)

    def wait_buffer(buf, buf_id):
        span = ys_hbm.at[pl.ds(0, tm * SUBLANES), :]
        for k in range(TOP_K):
            pltpu.make_async_copy(span, buf.at[k], sems.at[buf_id, k]).wait()

    @pl.when(i == 0)
    def _():
        def issue(g, carry):
            for u in range(ROW_UNROLL):
                start_row(0, ybuf_a, 0, g * ROW_UNROLL + u)
            return carry

        lax.fori_loop(0, tm // ROW_UNROLL, issue, 0)
        chunk = wg_ref.shape[0] // 8
        for r0 in range(0, wg_ref.shape[0], chunk):
            wg_bf[r0:r0 + chunk, :] = wg_ref[r0:r0 + chunk, :].astype(BF16)
        wp_bf[...] = wp_ref[...].astype(BF16)

    def step(cur, cur_id, nxt_buf, nxt_id):
        wait_buffer(cur, cur_id)
        nxt = jnp.minimum(i + 1, n - 1) * tm
        for r in range(tm):
            start_row(nxt, nxt_buf, nxt_id, r)

        route = route_ref[...]
        f = None
        for k in range(TOP_K):
            lo, hi = _unpack_bf16_pair(_load_row_tiles(cur.at[k]))
            fk = route[:, TOP_K + k:TOP_K + k + 1] * jnp.concatenate([lo, hi], axis=1)
            f = fk if f is None else f + fk
        x2 = _layer_norm_rows(ALPHA * x1_ref[...] + f, g_ref[...], b_ref[...])
        gate = jax.nn.sigmoid(jnp.dot(x2.astype(BF16), wg_bf[...], preferred_element_type=F32) + bg_ref[...])
        emb = jnp.dot(p_ref[...].astype(BF16), wp_bf[...], preferred_element_type=F32)
        out = x2 + gate * emb
        o_ref[...] = out
        ob_ref[...] = out.astype(BF16)

        @pl.when(i == n - 1)
        def _():
            wait_buffer(nxt_buf, nxt_id)

    @pl.when(i % 2 == 0)
    def _():
        step(ybuf_a, 0, ybuf_b, 1)

    @pl.when(i % 2 == 1)
    def _():
        step(ybuf_b, 1, ybuf_a, 0)


def _combine_ple(ys, pos, x1, route, ln_g, ln_b, w_gate, b_gate, p, w_proj, layer, *, tm=256):
    s, d = x1.shape
    pd = p.shape[-1]
    rows = lambda n: pl.BlockSpec((tm, n), lambda i, pos: (i, 0))
    p_rows = pl.BlockSpec((None, None, tm, pd), lambda i, pos: (layer, 0, i, 0))
    full = lambda a: pl.BlockSpec(a.shape, lambda i, pos: (0,) * a.ndim, pipeline_mode=pl.Buffered(1))
    of_layer = lambda a: pl.BlockSpec((None,) + a.shape[1:], lambda i, pos: (layer, 0, 0),
                                      pipeline_mode=pl.Buffered(1))
    g2, b2, bg = ln_g.reshape(1, d), ln_b.reshape(1, d), b_gate.reshape(1, d)
    return pl.pallas_call(
        functools.partial(_combine_ple_kernel, tm=tm, s=s),
        grid_spec=pltpu.PrefetchScalarGridSpec(
            num_scalar_prefetch=1,
            grid=(s // tm,),
            in_specs=[pl.BlockSpec(memory_space=pl.ANY), rows(d), rows(ROUTE_LANES), full(g2), full(b2),
                      of_layer(w_gate), full(bg), p_rows, of_layer(w_proj)],
            out_specs=[rows(d), rows(d)],
            scratch_shapes=[pltpu.VMEM((TOP_K, tm * SUBLANES, LANES), jnp.uint32),
                            pltpu.VMEM((TOP_K, tm * SUBLANES, LANES), jnp.uint32),
                            pltpu.VMEM(w_gate.shape[1:], BF16), pltpu.VMEM(w_proj.shape[1:], BF16),
                            pltpu.SemaphoreType.DMA((2, TOP_K))]),
        out_shape=[jax.ShapeDtypeStruct((s, d), F32), jax.ShapeDtypeStruct((s, d), BF16)],
        compiler_params=_params(("arbitrary",), 56),
        name="moe_combine_ple",
    )(pos, ys, x1, route, g2, b2, w_gate, bg, p, w_proj)


def _rglru_kernel(cur_ref, prev_ref, gate_ref, cw_ref, cb_ref, wa_ref, ba_ref, wi_ref, bi_ref, lam_ref,
                  o_ref, buf_ref, a_ref, b_ref, h_ref, carry_ref, *, tt, tc, width):
    i = pl.program_id(1)

    @pl.when(i == 0)
    def _():
        carry_ref[...] = jnp.zeros(carry_ref.shape, F32)

    buf_ref[0:RNN_HALO, :] = jnp.where(i > 0, prev_ref[...], 0.0)
    buf_ref[RNN_HALO:, :] = cur_ref[...]
    off = RNN_HALO - (width - 1)
    z = -lam_ref[...]
    log_a_scale = (-RG_C) * (jnp.log1p(jnp.exp(-jnp.abs(z))) + jnp.maximum(z, 0.0))

    def sigmoid(v):
        return 0.5 * jnp.tanh(0.5 * v) + 0.5

    for hh in range(tc // RNN_HEAD_DIM):
        cols = slice(hh * RNN_HEAD_DIM, (hh + 1) * RNN_HEAD_DIM)
        xh = jnp.broadcast_to(cb_ref[:, cols], (tt, RNN_HEAD_DIM))
        for k in range(width):
            xh = xh + buf_ref[off + k:off + k + tt, cols] * cw_ref[k:k + 1, cols]
        xhb = xh.astype(BF16)
        r = sigmoid(jnp.dot(xhb, wa_ref[hh], preferred_element_type=F32) + ba_ref[:, cols])
        ig = sigmoid(jnp.dot(xhb, wi_ref[hh], preferred_element_type=F32) + bi_ref[:, cols])
        log_a = r * log_a_scale[:, cols]
        a = jnp.exp(log_a)
        bval = jnp.sqrt(-jnp.tanh(log_a) * (a * a + 1.0)) * (ig * xh)
        a_ref[:, cols] = a
        b_ref[:, cols] = bval

    def group(gi, h):
        base = pl.multiple_of(gi * 8, 8)
        for r in range(8):
            row = pl.ds(base + r, 1)
            h = a_ref[row, :] * h + b_ref[row, :]
            h_ref[row, :] = h
        return h

    carry_ref[...] = lax.fori_loop(0, tt // 8, group, carry_ref[...])
    o_ref[...] = (gate_ref[...].astype(F32) * h_ref[...]).astype(o_ref.dtype)


def _rglru(xr, gate, conv_w, conv_b, w_a, b_a, w_i, b_i, lam, *, tt=256, tc=2048):
    s, wd = xr.shape
    width = conv_w.shape[0]
    hpc = tc // RNN_HEAD_DIM
    hb = tt // RNN_HALO
    row = lambda v: v.reshape(1, wd)
    cvec = lambda: pl.BlockSpec((1, tc), lambda c, i: (0, c))
    heads = lambda: pl.BlockSpec((hpc, RNN_HEAD_DIM, RNN_HEAD_DIM), lambda c, i: (c, 0, 0))
    tile = lambda: pl.BlockSpec((tt, tc), lambda c, i: (i, c))
    return pl.pallas_call(
        functools.partial(_rglru_kernel, tt=tt, tc=tc, width=width),
        grid=(wd // tc, s // tt),
        in_specs=[tile(),
                  pl.BlockSpec((RNN_HALO, tc), lambda c, i: (jnp.maximum(i * hb - 1, 0), c)),
                  tile(),
                  pl.BlockSpec((width, tc), lambda c, i: (0, c)),
                  cvec(), heads(), cvec(), heads(), cvec(), cvec()],
        out_specs=tile(),
        out_shape=jax.ShapeDtypeStruct((s, wd), BF16),
        scratch_shapes=[pltpu.VMEM((tt + RNN_HALO, tc), F32), pltpu.VMEM((tt, tc), F32),
                        pltpu.VMEM((tt, tc), F32), pltpu.VMEM((tt, tc), F32), pltpu.VMEM((1, tc), F32)],
        compiler_params=_params(("parallel", "arbitrary"), 32),
        name="rglru",
    )(xr, xr, gate, conv_w, row(conv_b), w_a.astype(BF16), row(b_a), w_i.astype(BF16), row(b_i), row(lam))


def _moe_and_ple(mixed, i, p, ln_ffn_g, ln_ffn_b, moe_w1, moe_w3, moe_w2, ple_w_proj, ple_w_gate, ple_b_gate):
    x1, x1_pairs, route = mixed
    dest, block_expert, n_used = _moe_plan(route)
    xs = _dispatch_rows(x1_pairs, dest, block_expert.shape[0])
    ys = _grouped_ffn(xs, moe_w1, moe_w3, moe_w2, i, block_expert, n_used)
    return _combine_ple(ys, dest, x1, route, ln_ffn_g[i], ln_ffn_b[i], ple_w_gate, ple_b_gate[i], p,
                        ple_w_proj, i)


def kernel(x, p, positions, ev_w_in, ev_conv_w, ev_conv_b, ev_cnorm_g, ev_cnorm_b, ev_qnorm_g, ev_w_uq, ev_kvnorm_g, ev_w_ukv, ev_w_out, od_w_in, od_conv_w, od_conv_b, od_w_a, od_b_a, od_w_i, od_b_i, od_lam, od_w_out, ln_mix_g, ln_mix_b, ln_ffn_g, ln_ffn_b, moe_w_grp, moe_b_grp, moe_w_exp, moe_b_exp, moe_w1, moe_w3, moe_w2, ple_w_proj, ple_w_gate, ple_b_gate):
    x0 = x[0]
    moe = functools.partial(_moe_and_ple, p=p, ln_ffn_g=ln_ffn_g, ln_ffn_b=ln_ffn_b, moe_w1=moe_w1,
                            moe_w3=moe_w3, moe_w2=moe_w2, ple_w_proj=ple_w_proj, ple_w_gate=ple_w_gate,
                            ple_b_gate=ple_b_gate)

    cosm, sinm = _rope_tables(positions[0])
    x0b, qn, qr, kn, kr, v = _mla_proj(x0, ev_w_in[0, :, 2 * CONV_CH:], ev_qnorm_g[0], ev_w_uq[0], ev_kvnorm_g[0],
                                       ev_w_ukv[0], cosm, sinm)
    u = _pair_proj(x0b, ev_w_in[0].T, CONV_CH, mode="glu", transposed=True)
    ub = _conv_gln(u, ev_conv_w[0], ev_conv_b[0], ev_cnorm_g[0], ev_cnorm_b[0], groups=CONV_GROUPS)
    att = _attention(qn, qr, kn, kr, v)
    mixed0 = _out_ln_route(ub, att, ev_w_out, x0, ln_mix_g[0], ln_mix_b[0],
                           moe_w_grp[0], moe_b_grp[0], moe_w_exp[0], moe_b_exp[0])
    x3, x3b = moe(mixed0, 0)

    gate, xr = _pair_proj(x3b, od_w_in.reshape(D_MODEL, -1), RNN_WIDTH, mode="rnn")
    y = _rglru(xr, gate, od_conv_w[0], od_conv_b[0], od_w_a[0], od_b_a[0], od_w_i[0], od_b_i[0], od_lam[0])
    mixed1 = _out_ln_route(y, y, od_w_out, x3, ln_mix_g[1], ln_mix_b[1],
                           moe_w_grp[1], moe_b_grp[1], moe_w_exp[1], moe_b_exp[1], col1=0, col2=1)
    x6, _ = moe(mixed1, 1)
    return x6[None]
```

```python
import functools
import math

import jax
import jax.numpy as jnp
from jax import lax
from jax.experimental import pallas as pl
from jax.experimental.pallas import tpu as pltpu

F32 = jnp.float32
BF16 = jnp.bfloat16

D_MODEL = 2048
SEQ = 8192
DEPTH = 2
CONV_CH = 1024
CONV_GROUPS = 16
CONV_WIDTH = 31
MLA_HEADS = 8
QK_NOPE = 128
QK_ROPE = 64
V_HEAD = 128
Q_RANK = 512
KV_RANK = 256
ROPE_THETA = 10000.0
RNN_WIDTH = 2048
RNN_HEADS = 16
RNN_HEAD_DIM = RNN_WIDTH // RNN_HEADS
RNN_CONV_WIDTH = 4
RG_C = 8.0
N_GROUPS = 8
EXPERTS_PER_GROUP = 8
N_EXPERTS = N_GROUPS * EXPERTS_PER_GROUP
TOP_K = 2
D_EXPERT = 512
PLE_DIM = 256
ALPHA = (2 * DEPTH) ** 0.25
LN_EPS = 1e-5
RMS_EPS = 1e-6

LANES = 128
SUBLANES = 8
CONV_HALO = 32
RNN_HALO = 8
MOE_ROWS = 256
ROUTE_LANES = 128
MIB = 2 ** 20


def _params(semantics, vmem_mib):
    return pltpu.CompilerParams(dimension_semantics=semantics, vmem_limit_bytes=vmem_mib * MIB)


def _layer_norm_rows(z, g, b):
    mu = jnp.mean(z, axis=-1, keepdims=True)
    d = z - mu
    var = jnp.mean(d * d, axis=-1, keepdims=True)
    return d * lax.rsqrt(var + LN_EPS) * g + b


def _pack_bf16_pair(a, b):
    def rounded_bits(v):
        u = lax.bitcast_convert_type(v, jnp.uint32)
        return u + jnp.uint32(0x7FFF) + ((u >> 16) & jnp.uint32(1))

    return (rounded_bits(a) & jnp.uint32(0xFFFF0000)) | (rounded_bits(b) >> 16)


def _unpack_bf16_pair(u):
    a = lax.bitcast_convert_type(u & jnp.uint32(0xFFFF0000), F32)
    b = lax.bitcast_convert_type(u << 16, F32)
    return a, b


def _store_row_tiles(ref, packed):
    n, w = packed.shape
    assert w == SUBLANES * LANES
    for j in range(SUBLANES):
        ref[pl.ds(j, n, stride=SUBLANES), :] = packed[:, j * LANES:(j + 1) * LANES]


def _load_row_tiles(ref):
    n = ref.shape[0] // SUBLANES
    return jnp.concatenate([ref[pl.ds(j, n, stride=SUBLANES), :] for j in range(SUBLANES)], axis=1)


def _split_bf16(v):
    hi = v.astype(BF16)
    lo = (v - hi.astype(F32)).astype(BF16)
    return hi, lo


def _gelu_tanh(x):
    c = math.sqrt(2.0 / math.pi)
    return 0.5 * x * (1.0 + jnp.tanh(c * (x + 0.044715 * (x * x * x))))


def _pair_proj_kernel(x_ref, wa_ref, wb_ref, *refs, mode, transposed):
    out_refs, (wa_bf, wb_bf) = refs[:-2], refs[-2:]

    @pl.when(pl.program_id(1) == 0)
    def _():
        for src, dst in ((wa_ref, wa_bf), (wb_ref, wb_bf)):
            w = src[...]
            dst[...] = (w.T if transposed else w).astype(BF16)

    x = x_ref[...]
    a = jnp.dot(x, wa_bf[...], preferred_element_type=F32)
    b = jnp.dot(x, wb_bf[...], preferred_element_type=F32)
    if mode == "glu":
        out_refs[0][...] = a * jax.nn.sigmoid(b)
    else:
        out_refs[0][...] = _gelu_tanh(a).astype(out_refs[0].dtype)
        out_refs[1][...] = b


def _pair_proj(x, w, n, *, mode, transposed=False, tm=512, tn=512):
    m, k = x.shape
    nb = n // tn
    tile = pl.BlockSpec((tm, tn), lambda j, i: (i, j))
    if transposed:
        w_specs = [pl.BlockSpec((tn, k), lambda j, i: (j, 0)), pl.BlockSpec((tn, k), lambda j, i: (j + nb, 0))]
    else:
        w_specs = [pl.BlockSpec((k, tn), lambda j, i: (0, j)), pl.BlockSpec((k, tn), lambda j, i: (0, j + nb))]
    if mode == "glu":
        out_specs, out_shape = tile, jax.ShapeDtypeStruct((m, n), F32)
    else:
        out_specs = [tile, tile]
        out_shape = [jax.ShapeDtypeStruct((m, n), BF16), jax.ShapeDtypeStruct((m, n), F32)]
    return pl.pallas_call(
        functools.partial(_pair_proj_kernel, mode=mode, transposed=transposed),
        grid=(nb, m // tm),
        in_specs=[pl.BlockSpec((tm, k), lambda j, i: (i, 0))] + w_specs,
        out_specs=out_specs,
        out_shape=out_shape,
        scratch_shapes=[pltpu.VMEM((k, tn), BF16), pltpu.VMEM((k, tn), BF16)],
        compiler_params=_params(("parallel", "arbitrary"), 48),
        name=mode + "_proj",
    )(x, w, w)


def _conv_gln_kernel(cur_ref, prev_ref, w_ref, b_ref, g_ref, beta_ref, gm_ref, o_ref,
                     buf_ref, sh_ref, y_ref, *, tt, ch, width, rc, cc):
    i = pl.program_id(0)
    buf_ref[0:CONV_HALO, :] = jnp.where(i > 0, prev_ref[...], 0.0)
    buf_ref[CONV_HALO:, :] = cur_ref[...]
    off = CONV_HALO - (width - 1)
    sh_rows = sh_ref.shape[1]
    step = 7 * SUBLANES
    assert sh_rows % step == 0
    for b in range(1, SUBLANES):
        for c0 in range(0, ch, cc):
            for r0 in range(0, sh_rows, step):
                sh_ref[b - 1, r0:r0 + step, c0:c0 + cc] = buf_ref[r0 + b:r0 + b + step, c0:c0 + cc]
    for c0 in range(0, ch, cc):
        for r0 in range(0, tt, rc):
            acc = jnp.broadcast_to(b_ref[:, c0:c0 + cc], (rc, cc))
            for k in range(width):
                q, b = divmod(off + k, SUBLANES)
                lo = r0 + q * SUBLANES
                src = buf_ref[lo:lo + rc, c0:c0 + cc] if b == 0 else sh_ref[b - 1, lo:lo + rc, c0:c0 + cc]
                acc = acc + src * w_ref[k:k + 1, c0:c0 + cc]
            y_ref[r0:r0 + rc, c0:c0 + cc] = acc
    gm = gm_ref[...]

    def seg_mean(v):
        hi, lo = _split_bf16(v)
        return (jnp.dot(hi, gm, preferred_element_type=F32)
                + jnp.dot(lo, gm, preferred_element_type=F32))

    for c0 in range(0, ch, LANES):
        y = y_ref[:, c0:c0 + LANES]
        d = y - seg_mean(y)
        var = seg_mean(d * d)
        z = d * lax.rsqrt(var + LN_EPS) * g_ref[:, c0:c0 + LANES] + beta_ref[:, c0:c0 + LANES]
        o_ref[:, c0:c0 + LANES] = (z * jax.nn.sigmoid(z)).astype(o_ref.dtype)


def _conv_gln(u, w, b, g, beta, *, groups, tt=256):
    s, ch = u.shape
    width = w.shape[0]
    gsz = ch // groups
    assert LANES % gsz == 0 and width - 1 <= CONV_HALO and tt % CONV_HALO == 0
    wpad = jnp.zeros((CONV_HALO, ch), F32).at[:width].set(w)
    lane = jnp.arange(LANES)
    gm = jnp.where((lane[:, None] // gsz) == (lane[None, :] // gsz), 1.0 / gsz, 0.0).astype(BF16)
    hb = tt // CONV_HALO
    row = lambda v: v.reshape(1, ch)
    return pl.pallas_call(
        functools.partial(_conv_gln_kernel, tt=tt, ch=ch, width=width, rc=64, cc=128),
        grid=(s // tt,),
        in_specs=[pl.BlockSpec((tt, ch), lambda i: (i, 0)),
                  pl.BlockSpec((CONV_HALO, ch), lambda i: (jnp.maximum(i * hb - 1, 0), 0)),
                  pl.BlockSpec((CONV_HALO, ch), lambda i: (0, 0)),
                  pl.BlockSpec((1, ch), lambda i: (0, 0)),
                  pl.BlockSpec((1, ch), lambda i: (0, 0)),
                  pl.BlockSpec((1, ch), lambda i: (0, 0)),
                  pl.BlockSpec((LANES, LANES), lambda i: (0, 0))],
        out_specs=pl.BlockSpec((tt, ch), lambda i: (i, 0)),
        out_shape=jax.ShapeDtypeStruct((s, ch), BF16),
        scratch_shapes=[pltpu.VMEM((tt + CONV_HALO, ch), F32),
                        pltpu.VMEM((SUBLANES - 1, tt + CONV_HALO - SUBLANES, ch), F32),
                        pltpu.VMEM((tt, ch), F32)],
        compiler_params=_params(("parallel",), 40),
        name="conv_gln",
    )(u, u, wpad, row(b), row(g), row(beta), gm)


def _rope_table_kernel(pos_ref, invf_ref, cos_ref, sin_ref):
    ang = pos_ref[...].astype(F32) * invf_ref[...]
    cos_ref[...] = jnp.cos(ang)
    sin_ref[...] = jnp.sin(ang)


def _rope_tables(positions):
    s = positions.shape[0]
    half = QK_ROPE // 2
    per_row = LANES // half
    inv_freq = 1.0 / (ROPE_THETA ** (jnp.arange(0, QK_ROPE, 2, dtype=F32) / QK_ROPE))
    pos_rep = jnp.repeat(positions.reshape(s // per_row, per_row), half, axis=1)
    invf = jnp.tile(inv_freq, per_row).reshape(1, LANES)
    cos, sin = pl.pallas_call(
        _rope_table_kernel,
        out_shape=[jax.ShapeDtypeStruct((s // per_row, LANES), F32)] * 2,
        name="rope_table",
    )(pos_rep, invf)
    cos = cos.reshape(s, half)
    sin = sin.reshape(s, half)
    zero = jnp.zeros((s, LANES - QK_ROPE), F32)
    return jnp.concatenate([cos, cos, zero], axis=1), jnp.concatenate([-sin, sin, zero], axis=1)


def _mla_proj_kernel(x_ref, wm_ref, qg_ref, kvg_ref, wq_ref, wkv_ref, c_ref, s_ref,
                     xb_ref, qn_ref, qr_ref, kn_ref, kr_ref, v_ref, *, scale):
    nq = MLA_HEADS * QK_NOPE
    xb = x_ref[...].astype(BF16)
    xb_ref[...] = xb
    c = jnp.dot(xb, wm_ref[...], preferred_element_type=F32)
    cq = c[:, :Q_RANK]
    ckv = c[:, Q_RANK:Q_RANK + KV_RANK]
    kr_pad = c[:, Q_RANK + KV_RANK:Q_RANK + KV_RANK + LANES]
    kr_rot = c[:, Q_RANK + KV_RANK + LANES:]
    cqn = cq * lax.rsqrt(jnp.mean(cq * cq, axis=-1, keepdims=True) + RMS_EPS) * qg_ref[...]
    ckvn = ckv * lax.rsqrt(jnp.mean(ckv * ckv, axis=-1, keepdims=True) + RMS_EPS) * kvg_ref[...]
    q = jnp.dot(cqn.astype(BF16), wq_ref[...], preferred_element_type=F32)
    kv = jnp.dot(ckvn.astype(BF16), wkv_ref[...], preferred_element_type=F32)
    cosm = c_ref[...]
    sinm = s_ref[...]
    qn_ref[...] = (q[:, :nq] * scale).astype(BF16)
    for h in range(MLA_HEADS):
        lo = nq + h * LANES
        rot = q[:, lo:lo + LANES] * cosm + q[:, lo + nq:lo + nq + LANES] * sinm
        qr_ref[:, h * LANES:(h + 1) * LANES] = (rot * scale).astype(BF16)
    kn_ref[...] = kv[:, :nq].astype(BF16)
    v_ref[...] = kv[:, nq:].astype(BF16)
    kr_ref[...] = (kr_pad * cosm + kr_rot * sinm).astype(BF16)


def _rope_swap(w):
    half = QK_ROPE // 2
    return jnp.concatenate([w[..., half:], w[..., :half]], axis=-1)


def _mla_proj(x, w_in_mla, qnorm_g, w_uq, kvnorm_g, w_ukv, cosm, sinm, *, tm=256):
    s, d = x.shape
    nq = MLA_HEADS * QK_NOPE
    pad = lambda w: jnp.concatenate([w, jnp.zeros(w.shape[:-1] + (LANES - QK_ROPE,), w.dtype)], axis=-1)
    w_kr = w_in_mla[:, Q_RANK + KV_RANK:]
    wm = jnp.concatenate([w_in_mla[:, :Q_RANK + KV_RANK], pad(w_kr), pad(_rope_swap(w_kr))], axis=1).astype(BF16)
    wq3 = w_uq.reshape(Q_RANK, MLA_HEADS, QK_NOPE + QK_ROPE)
    wq_rope = wq3[:, :, QK_NOPE:]
    wq = jnp.concatenate([wq3[:, :, :QK_NOPE].reshape(Q_RANK, nq),
                          pad(wq_rope).reshape(Q_RANK, MLA_HEADS * LANES),
                          pad(_rope_swap(wq_rope)).reshape(Q_RANK, MLA_HEADS * LANES)], axis=1).astype(BF16)
    wkv3 = w_ukv.reshape(KV_RANK, MLA_HEADS, QK_NOPE + V_HEAD)
    wkv = jnp.concatenate([wkv3[:, :, :QK_NOPE].reshape(KV_RANK, nq),
                           wkv3[:, :, QK_NOPE:].reshape(KV_RANK, MLA_HEADS * V_HEAD)], axis=1).astype(BF16)
    scale = (QK_NOPE + QK_ROPE) ** -0.5 * math.log2(math.e)
    full = lambda a: pl.BlockSpec(a.shape, lambda i: (0,) * a.ndim)
    rows = lambda n: pl.BlockSpec((tm, n), lambda i: (i, 0))
    qg = qnorm_g.reshape(1, Q_RANK)
    kvg = kvnorm_g.reshape(1, KV_RANK)
    return pl.pallas_call(
        functools.partial(_mla_proj_kernel, scale=scale),
        grid=(s // tm,),
        in_specs=[rows(d), full(wm), full(qg), full(kvg), full(wq), full(wkv), rows(LANES), rows(LANES)],
        out_specs=[rows(d), rows(nq), rows(MLA_HEADS * LANES), rows(nq), rows(LANES), rows(MLA_HEADS * V_HEAD)],
        out_shape=[jax.ShapeDtypeStruct((s, d), BF16),
                   jax.ShapeDtypeStruct((s, nq), BF16),
                   jax.ShapeDtypeStruct((s, MLA_HEADS * LANES), BF16),
                   jax.ShapeDtypeStruct((s, nq), BF16),
                   jax.ShapeDtypeStruct((s, LANES), BF16),
                   jax.ShapeDtypeStruct((s, MLA_HEADS * V_HEAD), BF16)],
        compiler_params=_params(("parallel",), 48),
        name="mla_proj",
    )(x, wm, qg, kvg, wq, wkv, cosm, sinm)


def _attn_kernel(qn_ref, qr_ref, kn_ref, kr_ref, v_ref, o_ref, m_ref, acc_ref, *, tq, tk, hb):
    qi = pl.program_id(1)
    per_q = tq // tk
    m_ref[...] = jnp.full(m_ref.shape, -jnp.inf, F32)
    acc_ref[...] = jnp.zeros(acc_ref.shape, F32)
    ones = jnp.ones((tk, LANES), BF16)
    head_cols = [slice(h * LANES, (h + 1) * LANES) for h in range(hb)]
    qs = [jnp.concatenate([qn_ref[:, c], qr_ref[:, c]], axis=1) for c in head_cols]

    def chunk(j, diag):
        rows = pl.ds(pl.multiple_of(j * tk, tk), tk)
        k_rope = kr_ref[rows, :]
        q0 = 0 if diag is None else diag * tk
        for h, c in enumerate(head_cols):
            k = jnp.concatenate([kn_ref[rows, c], k_rope], axis=1)
            s = lax.dot_general(qs[h][q0:], k, (((1,), (1,)), ((), ())), preferred_element_type=F32)
            if diag is not None:
                keep = lax.broadcasted_iota(jnp.int32, s.shape, 0) >= lax.broadcasted_iota(jnp.int32, s.shape, 1)
                s = jnp.where(keep, s, -jnp.inf)
            m_old = m_ref[h, q0:, :]
            m_new = jnp.maximum(m_old, jnp.max(s, axis=-1, keepdims=True))
            alpha = jnp.exp2(m_old - m_new)
            p = jnp.exp2(s - jnp.concatenate([m_new] * (tk // LANES), axis=1))
            v_ext = jnp.concatenate([v_ref[rows, c], ones], axis=1)
            acc_ref[h, q0:, :] = (jnp.concatenate([alpha, alpha], axis=1) * acc_ref[h, q0:, :]
                                  + jnp.dot(p.astype(BF16), v_ext, preferred_element_type=F32))
            m_ref[h, q0:, :] = m_new

    def body(j, carry):
        chunk(j, None)
        return carry

    lax.fori_loop(0, qi * per_q, body, 0)
    for d in range(per_q):
        chunk(qi * per_q + d, d)
    for h, c in enumerate(head_cols):
        acc = acc_ref[h]
        o_ref[:, c] = (acc[:, :V_HEAD] / acc[:, V_HEAD:]).astype(o_ref.dtype)


def _attention(qn, qr, kn, kr, v, *, tq=1024, tk=512, hb=4):
    s = qn.shape[0]
    wide = hb * LANES
    tile = lambda: pl.BlockSpec((tq, wide), lambda h, i: (i, h))
    keys = lambda: pl.BlockSpec((s, wide), lambda h, i: (0, h), pipeline_mode=pl.Buffered(1))
    return pl.pallas_call(
        functools.partial(_attn_kernel, tq=tq, tk=tk, hb=hb),
        grid=(MLA_HEADS // hb, s // tq),
        in_specs=[tile(), tile(), keys(),
                  pl.BlockSpec((s, LANES), lambda h, i: (0, 0), pipeline_mode=pl.Buffered(1)), keys()],
        out_specs=tile(),
        out_shape=jax.ShapeDtypeStruct((s, MLA_HEADS * V_HEAD), BF16),
        scratch_shapes=[pltpu.VMEM((hb, tq, LANES), F32), pltpu.VMEM((hb, tq, 2 * V_HEAD), F32)],
        compiler_params=_params(("parallel", "parallel"), 48),
        name="mla_attention",
    )(qn, qr, kn, kr, v)


def _route(logits):
    lane = lax.broadcasted_iota(jnp.int32, logits.shape, 1).astype(F32)
    big = float(2 * ROUTE_LANES)
    neg = -jnp.inf
    gl = jnp.where(lane < N_GROUPS, logits, neg)
    gmax = jnp.max(gl, axis=-1, keepdims=True)
    gsum = jnp.sum(jnp.exp(gl - gmax), axis=-1, keepdims=True)
    g_prob = 1.0 / gsum
    g_idx = jnp.min(jnp.where(gl == gmax, lane, big), axis=-1, keepdims=True)
    lo = N_GROUPS + g_idx * EXPERTS_PER_GROUP
    el = jnp.where((lane >= lo) & (lane < lo + EXPERTS_PER_GROUP), logits, neg)
    emax = jnp.max(el, axis=-1, keepdims=True)
    esum = jnp.sum(jnp.exp(el - emax), axis=-1, keepdims=True)
    i0 = jnp.min(jnp.where(el == emax, lane, big), axis=-1, keepdims=True)
    el2 = jnp.where(lane == i0, neg, el)
    emax2 = jnp.max(el2, axis=-1, keepdims=True)
    i1 = jnp.min(jnp.where(el2 == emax2, lane, big), axis=-1, keepdims=True)
    p0 = 1.0 / esum
    p1 = jnp.exp(emax2 - emax) / esum
    g0 = g_prob * p0 / (p0 + p1)
    g1 = g_prob * p1 / (p0 + p1)
    e0 = i0 - N_GROUPS
    e1 = i1 - N_GROUPS
    return jnp.where(lane == 0, e0, jnp.where(lane == 1, e1, jnp.where(lane == 2, g0, jnp.where(lane == 3, g1, 0.0))))


def _out_ln_route_kernel(a1_ref, a2_ref, w_ref, x_ref, g_ref, b_ref, wrh_ref, br_ref,
                         x1_ref, x1p_ref, route_ref, w_bf, *, sub):
    half = a1_ref.shape[1]

    @pl.when(pl.program_id(0) == 0)
    def _():
        chunk = w_ref.shape[0] // 8
        for r0 in range(0, w_ref.shape[0], chunk):
            w_bf[r0:r0 + chunk, :] = w_ref[r0:r0 + chunk, :].astype(BF16)

    for r0 in range(0, x_ref.shape[0], sub):
        rows = slice(r0, r0 + sub)
        m = (jnp.dot(a1_ref[rows, :], w_bf[:half, :], preferred_element_type=F32)
             + jnp.dot(a2_ref[rows, :], w_bf[half:, :], preferred_element_type=F32))
        x1 = _layer_norm_rows(ALPHA * x_ref[rows, :] + m, g_ref[...], b_ref[...])
        x1_ref[rows, :] = x1
        hd = x1.shape[1] // 2
        _store_row_tiles(x1p_ref.at[pl.ds(r0 * SUBLANES, sub * SUBLANES), :], _pack_bf16_pair(x1[:, :hd], x1[:, hd:]))
        xh, xl = _split_bf16(x1)
        both = jnp.dot(xh, wrh_ref[...], preferred_element_type=F32)
        logits = (both[:, :ROUTE_LANES] + both[:, ROUTE_LANES:]
                  + jnp.dot(xl, wrh_ref[:, :ROUTE_LANES], preferred_element_type=F32) + br_ref[...])
        route_ref[rows, :] = _route(logits)


def _out_ln_route(a1, a2, w, x, ln_g, ln_b, w_grp, b_grp, w_exp, b_exp, *, col1=0, col2=0, tm=512, sub=512):
    s, d = x.shape
    half = w.shape[1] // 2
    pad = ROUTE_LANES - N_GROUPS - N_EXPERTS
    wr = jnp.concatenate([w_grp, w_exp, jnp.zeros((d, pad), F32)], axis=1)
    br = jnp.concatenate([b_grp, b_exp, jnp.zeros((pad,), F32)]).reshape(1, ROUTE_LANES)
    wr_hi = wr.astype(BF16)
    wrh = jnp.concatenate([wr_hi, (wr - wr_hi.astype(F32)).astype(BF16)], axis=1)
    full = lambda a: pl.BlockSpec(a.shape, lambda i: (0,) * a.ndim)
    rows = lambda n: pl.BlockSpec((tm, n), lambda i: (i, 0))
    g2 = ln_g.reshape(1, d)
    b2 = ln_b.reshape(1, d)
    return pl.pallas_call(
        functools.partial(_out_ln_route_kernel, sub=sub),
        grid=(s // tm,),
        in_specs=[pl.BlockSpec((tm, half), lambda i: (i, col1)), pl.BlockSpec((tm, half), lambda i: (i, col2)),
                  pl.BlockSpec((None,) + w.shape[1:], lambda i: (0, 0, 0), pipeline_mode=pl.Buffered(1)),
                  rows(d), full(g2), full(b2), full(wrh), full(br)],
        out_specs=[rows(d), pl.BlockSpec((tm * SUBLANES, LANES), lambda i: (i, 0)), rows(ROUTE_LANES)],
        out_shape=[jax.ShapeDtypeStruct((s, d), F32), jax.ShapeDtypeStruct((s * SUBLANES, LANES), jnp.uint32),
                   jax.ShapeDtypeStruct((s, ROUTE_LANES), F32)],
        scratch_shapes=[pltpu.VMEM(w.shape[1:], BF16)],
        compiler_params=_params(("arbitrary",), 58),
        name="out_ln_route",
    )(a1, a2, w, x, g2, b2, wrh, br)


def _rank_kernel(route_ref, tri_ref, rank_ref, cnt_ref, carry_ref):
    @pl.when(pl.program_id(0) == 0)
    def _():
        carry_ref[...] = jnp.zeros(carry_ref.shape, F32)

    route = route_ref[...]
    lane = lax.broadcasted_iota(jnp.int32, route.shape, 1).astype(F32)
    oh0 = jnp.where(lane == route[:, 0:1], 1.0, 0.0)
    oh1 = jnp.where(lane == route[:, 1:2], 1.0, 0.0)
    both = oh0 + oh1
    before = jnp.dot(tri_ref[...], both.astype(BF16), preferred_element_type=F32) + carry_ref[...]
    r0 = jnp.sum(before * oh0, axis=-1, keepdims=True)
    r1 = jnp.sum(before * oh1, axis=-1, keepdims=True)
    rank_ref[...] = jnp.where(lane == 0.0, r0, jnp.where(lane == 1.0, r1, 0.0))
    carry_ref[...] = carry_ref[...] + jnp.sum(both, axis=0, keepdims=True)
    cnt_ref[...] = carry_ref[...]


def _dest_kernel(route_ref, rank_ref, pstart_ref, dest_ref):
    route = route_ref[...]
    rank = rank_ref[...]
    lane = lax.broadcasted_iota(jnp.int32, route.shape, 1).astype(F32)
    ps = pstart_ref[...]
    d0 = jnp.sum(jnp.where(lane == route[:, 0:1], ps, 0.0), axis=-1, keepdims=True) + rank[:, 0:1]
    d1 = jnp.sum(jnp.where(lane == route[:, 1:2], ps, 0.0), axis=-1, keepdims=True) + rank[:, 1:2]
    dest = jnp.where(lane == 0.0, d0, jnp.where(lane == 1.0, d1, 0.0)) * float(SUBLANES)
    dest_ref[...] = dest.astype(jnp.int32)


def _moe_plan(route, *, tb=512):
    s = route.shape[0]
    tri = (jnp.arange(tb)[:, None] > jnp.arange(tb)[None, :]).astype(BF16)
    rows = pl.BlockSpec((tb, ROUTE_LANES), lambda i: (i, 0))
    one = lambda: pl.BlockSpec((1, ROUTE_LANES), lambda i: (0, 0))
    rank, cnt = pl.pallas_call(
        _rank_kernel,
        grid=(s // tb,),
        in_specs=[rows, pl.BlockSpec((tb, tb), lambda i: (0, 0))],
        out_specs=[rows, one()],
        out_shape=[jax.ShapeDtypeStruct((s, ROUTE_LANES), F32), jax.ShapeDtypeStruct((1, ROUTE_LANES), F32)],
        scratch_shapes=[pltpu.VMEM((1, ROUTE_LANES), F32)],
        compiler_params=_params(("arbitrary",), 32),
        name="moe_rank",
    )(route, tri)
    counts = cnt[0, :N_EXPERTS].astype(jnp.int32)
    pcounts = (counts + MOE_ROWS - 1) // MOE_ROWS * MOE_ROWS
    pends = jnp.cumsum(pcounts)
    pstarts = pends - pcounts
    n_blocks = s * TOP_K // MOE_ROWS + N_EXPERTS
    n_used = pends[-1] // MOE_ROWS
    blk = jnp.minimum(jnp.arange(n_blocks, dtype=jnp.int32), n_used - 1) * MOE_ROWS
    block_expert = jnp.minimum(jnp.sum(pends[None, :] <= blk[:, None], axis=1), N_EXPERTS - 1).astype(jnp.int32)
    ps = jnp.zeros((1, ROUTE_LANES), F32).at[0, :N_EXPERTS].set(pstarts.astype(F32))
    dest = pl.pallas_call(
        _dest_kernel,
        grid=(s // tb,),
        in_specs=[rows, rows, one()],
        out_specs=rows,
        out_shape=jax.ShapeDtypeStruct((s, ROUTE_LANES), jnp.int32),
        compiler_params=_params(("parallel",), 32),
        name="moe_dest",
    )(route, rank, ps)
    dest = dest[:, :TOP_K].T.reshape(TOP_K * s)
    return dest, block_expert, n_used.astype(jnp.int32).reshape(1)


ROW_UNROLL = 8


def _row_tile(ref, first_row):
    return ref.at[pl.ds(pl.multiple_of(first_row, SUBLANES), SUBLANES), :]


def _dispatch_kernel(dest_ref, x_ref, xs_hbm, sems, *, tm, s):
    t0 = pl.program_id(0) * tm

    def issue(g, carry):
        for u in range(ROW_UNROLL):
            r = g * ROW_UNROLL + u
            src = _row_tile(x_ref, r * SUBLANES)
            for k in range(TOP_K):
                pltpu.make_async_copy(src, _row_tile(xs_hbm, dest_ref[k * s + t0 + r]), sems.at[k]).start(priority=k)
        return carry

    lax.fori_loop(0, tm // ROW_UNROLL, issue, 0)
    for k in range(TOP_K):
        pltpu.make_async_copy(x_ref, xs_hbm.at[pl.ds(0, tm * SUBLANES), :], sems.at[k]).wait()


def _dispatch_rows(x, dest, n_blocks, *, tm=256):
    s = x.shape[0] // SUBLANES
    return pl.pallas_call(
        functools.partial(_dispatch_kernel, tm=tm, s=s),
        grid_spec=pltpu.PrefetchScalarGridSpec(
            num_scalar_prefetch=1,
            grid=(s // tm,),
            in_specs=[pl.BlockSpec((tm * SUBLANES, LANES), lambda i, dest: (i, 0))],
            out_specs=pl.BlockSpec(memory_space=pl.ANY),
            scratch_shapes=[pltpu.SemaphoreType.DMA((TOP_K,))]),
        out_shape=jax.ShapeDtypeStruct((n_blocks * MOE_ROWS * SUBLANES, LANES), x.dtype),
        compiler_params=_params(("arbitrary",), 32),
        name="moe_dispatch",
    )(dest, x)


FFN_K_CHUNK = 512


def _ffn_kernel(nused_ref, run_ref, first_ref, rexp_ref, nruns_ref, xs_ref, w1_hbm, w3_hbm, w2_hbm, ys_ref,
                w1_buf, w3_buf, w2_buf, sems, *, layer):
    b = pl.program_id(0)

    def weight_copies(run, slot):
        e = rexp_ref[run]
        return [pltpu.make_async_copy(w_hbm.at[layer, e], buf.at[slot], sems.at[slot, j])
                for j, (w_hbm, buf) in enumerate(((w1_hbm, w1_buf), (w3_hbm, w3_buf), (w2_hbm, w2_buf)))]

    @pl.when(b == 0)
    def _():
        for c in weight_copies(0, 0):
            c.start()

    @pl.when(b < nused_ref[0])
    def _():
        run = run_ref[b]
        slot = run % 2

        @pl.when(first_ref[b] == 1)
        def _():
            @pl.when(run + 1 < nruns_ref[0])
            def _():
                for c in weight_copies(run + 1, 1 - slot):
                    c.start()

            for c in weight_copies(run, slot):
                c.wait()

        x_lo, x_hi = _unpack_bf16_pair(_load_row_tiles(xs_ref))
        hd = x_lo.shape[1]
        x = jnp.concatenate([x_lo.astype(BF16), x_hi.astype(BF16)], axis=1)

        def up(buf):
            acc = None
            for k0 in range(0, 2 * hd, FFN_K_CHUNK):
                part = jnp.dot(x[:, k0:k0 + FFN_K_CHUNK], buf[slot, k0:k0 + FFN_K_CHUNK, :].astype(BF16),
                               preferred_element_type=F32)
                acc = part if acc is None else acc + part
            return acc

        h1 = up(w1_buf)
        h3 = up(w3_buf)
        hmid = (h1 * jax.nn.sigmoid(h1) * h3).astype(BF16)
        y = jnp.dot(hmid, w2_buf[slot].astype(BF16), preferred_element_type=F32)
        _store_row_tiles(ys_ref, _pack_bf16_pair(y[:, :hd], y[:, hd:]))


def _grouped_ffn(xs, w1, w3, w2, layer, block_expert, n_used):
    d = xs.shape[1]
    dm = w1.shape[2]
    de = w1.shape[3]
    nb = block_expert.shape[0]
    first = jnp.concatenate([jnp.ones((1,), jnp.int32), (block_expert[1:] != block_expert[:-1]).astype(jnp.int32)])
    run = jnp.cumsum(first) - 1
    n_runs = run[-1:] + 1
    hit = (run[None, :] == jnp.arange(nb, dtype=jnp.int32)[:, None]) & (first[None, :] == 1)
    run_expert = jnp.sum(jnp.where(hit, block_expert[None, :], 0), axis=1).astype(jnp.int32)
    blk = lambda b, nu, *_: (jnp.minimum(b, nu[0] - 1), 0)
    return pl.pallas_call(
        functools.partial(_ffn_kernel, layer=layer),
        grid_spec=pltpu.PrefetchScalarGridSpec(
            num_scalar_prefetch=5,
            grid=(nb,),
            in_specs=[pl.BlockSpec((MOE_ROWS * SUBLANES, d), blk),
                      pl.BlockSpec(memory_space=pl.ANY),
                      pl.BlockSpec(memory_space=pl.ANY),
                      pl.BlockSpec(memory_space=pl.ANY)],
            out_specs=pl.BlockSpec((MOE_ROWS * SUBLANES, d), blk),
            scratch_shapes=[pltpu.VMEM((2, dm, de), F32), pltpu.VMEM((2, dm, de), F32),
                            pltpu.VMEM((2, de, dm), F32), pltpu.SemaphoreType.DMA((2, 3))]),
        out_shape=jax.ShapeDtypeStruct(xs.shape, jnp.uint32),
        compiler_params=_params(("arbitrary",), 56),
        name="moe_ffn",
    )(n_used, run.astype(jnp.int32), first, run_expert, n_runs.astype(jnp.int32), xs, w1, w3, w2)


def _combine_ple_kernel(pos_ref, ys_hbm, x1_ref, route_ref, g_ref, b_ref, wg_ref, bg_ref, p_ref, wp_ref,
                        o_ref, ob_ref, ybuf_a, ybuf_b, wg_bf, wp_bf, sems, *, tm, s):
    i = pl.program_id(0)
    n = pl.num_programs(0)

    def start_row(t0, buf, buf_id, r):
        first = r * SUBLANES if isinstance(r, int) else pl.multiple_of(r * SUBLANES, SUBLANES)
        for k in range(TOP_K):
            pltpu.make_async_copy(_row_tile(ys_hbm, pos_ref[k * s + t0 + r]),
                                  buf.at[k, pl.ds(first, SUBLANES), :], sems.at[buf_id, k]).start(priority=k)

    def wait_buffer(buf, buf_id):
        span = ys_hbm.at[pl.ds(0, tm * SUBLANES), :]
        for k in range(TOP_K):
            pltpu.make_async_copy(span, buf.at[k], sems.at[buf_id, k]).wait()

    @pl.when(i == 0)
    def _():
        def issue(g, carry):
            for u in range(ROW_UNROLL):
                start_row(0, ybuf_a, 0, g * ROW_UNROLL + u)
            return carry

        lax.fori_loop(0, tm // ROW_UNROLL, issue, 0)
        chunk = wg_ref.shape[0] // 8
        for r0 in range(0, wg_ref.shape[0], chunk):
            wg_bf[r0:r0 + chunk, :] = wg_ref[r0:r0 + chunk, :].astype(BF16)
        wp_bf[...] = wp_ref[...].astype(BF16)

    def step(cur, cur_id, nxt_buf, nxt_id):
        wait_buffer(cur, cur_id)
        nxt = jnp.minimum(i + 1, n - 1) * tm
        for r in range(tm):
            start_row(nxt, nxt_buf, nxt_id, r)

        route = route_ref[...]
        f = None
        for k in range(TOP_K):
            lo, hi = _unpack_bf16_pair(_load_row_tiles(cur.at[k]))
            fk = route[:, TOP_K + k:TOP_K + k + 1] * jnp.concatenate([lo, hi], axis=1)
            f = fk if f is None else f + fk
        x2 = _layer_norm_rows(ALPHA * x1_ref[...] + f, g_ref[...], b_ref[...])
        gate = jax.nn.sigmoid(jnp.dot(x2.astype(BF16), wg_bf[...], preferred_element_type=F32) + bg_ref[...])
        emb = jnp.dot(p_ref[...].astype(BF16), wp_bf[...], preferred_element_type=F32)
        out = x2 + gate * emb
        o_ref[...] = out
        ob_ref[...] = out.astype(BF16)

        @pl.when(i == n - 1)
        def _():
            wait_buffer(nxt_buf, nxt_id)

    @pl.when(i % 2 == 0)
    def _():
        step(ybuf_a, 0, ybuf_b, 1)

    @pl.when(i % 2 == 1)
    def _():
        step(ybuf_b, 1, ybuf_a, 0)


def _combine_ple(ys, pos, x1, route, ln_g, ln_b, w_gate, b_gate, p, w_proj, layer, *, tm=256):
    s, d = x1.shape
    pd = p.shape[-1]
    rows = lambda n: pl.BlockSpec((tm, n), lambda i, pos: (i, 0))
    p_rows = pl.BlockSpec((None, None, tm, pd), lambda i, pos: (layer, 0, i, 0))
    full = lambda a: pl.BlockSpec(a.shape, lambda i, pos: (0,) * a.ndim, pipeline_mode=pl.Buffered(1))
    of_layer = lambda a: pl.BlockSpec((None,) + a.shape[1:], lambda i, pos: (layer, 0, 0),
                                      pipeline_mode=pl.Buffered(1))
    g2, b2, bg = ln_g.reshape(1, d), ln_b.reshape(1, d), b_gate.reshape(1, d)
    return pl.pallas_call(
        functools.partial(_combine_ple_kernel, tm=tm, s=s),
        grid_spec=pltpu.PrefetchScalarGridSpec(
            num_scalar_prefetch=1,
            grid=(s // tm,),
            in_specs=[pl.BlockSpec(memory_space=pl.ANY), rows(d), rows(ROUTE_LANES), full(g2), full(b2),
                      of_layer(w_gate), full(bg), p_rows, of_layer(w_proj)],
            out_specs=[rows(d), rows(d)],
            scratch_shapes=[pltpu.VMEM((TOP_K, tm * SUBLANES, LANES), jnp.uint32),
                            pltpu.VMEM((TOP_K, tm * SUBLANES, LANES), jnp.uint32),
                            pltpu.VMEM(w_gate.shape[1:], BF16), pltpu.VMEM(w_proj.shape[1:], BF16),
                            pltpu.SemaphoreType.DMA((2, TOP_K))]),
        out_shape=[jax.ShapeDtypeStruct((s, d), F32), jax.ShapeDtypeStruct((s, d), BF16)],
        compiler_params=_params(("arbitrary",), 56),
        name="moe_combine_ple",
    )(pos, ys, x1, route, g2, b2, w_gate, bg, p, w_proj)


def _rglru_kernel(cur_ref, prev_ref, gate_ref, cw_ref, cb_ref, wa_ref, ba_ref, wi_ref, bi_ref, lam_ref,
                  o_ref, buf_ref, a_ref, b_ref, h_ref, carry_ref, *, tt, tc, width):
    i = pl.program_id(1)

    @pl.when(i == 0)
    def _():
        carry_ref[...] = jnp.zeros(carry_ref.shape, F32)

    buf_ref[0:RNN_HALO, :] = jnp.where(i > 0, prev_ref[...], 0.0)
    buf_ref[RNN_HALO:, :] = cur_ref[...]
    off = RNN_HALO - (width - 1)
    z = -lam_ref[...]
    log_a_scale = (-RG_C) * (jnp.log1p(jnp.exp(-jnp.abs(z))) + jnp.maximum(z, 0.0))

    def sigmoid(v):
        return 0.5 * jnp.tanh(0.5 * v) + 0.5

    for hh in range(tc // RNN_HEAD_DIM):
        cols = slice(hh * RNN_HEAD_DIM, (hh + 1) * RNN_HEAD_DIM)
        xh = jnp.broadcast_to(cb_ref[:, cols], (tt, RNN_HEAD_DIM))
        for k in range(width):
            xh = xh + buf_ref[off + k:off + k + tt, cols] * cw_ref[k:k + 1, cols]
        xhb = xh.astype(BF16)
        r = sigmoid(jnp.dot(xhb, wa_ref[hh], preferred_element_type=F32) + ba_ref[:, cols])
        ig = sigmoid(jnp.dot(xhb, wi_ref[hh], preferred_element_type=F32) + bi_ref[:, cols])
        log_a = r * log_a_scale[:, cols]
        a = jnp.exp(log_a)
        bval = jnp.sqrt(-jnp.tanh(log_a) * (a * a + 1.0)) * (ig * xh)
        a_ref[:, cols] = a
        b_ref[:, cols] = bval

    def group(gi, h):
        base = pl.multiple_of(gi * 8, 8)
        for r in range(8):
            row = pl.ds(base + r, 1)
            h = a_ref[row, :] * h + b_ref[row, :]
            h_ref[row, :] = h
        return h

    carry_ref[...] = lax.fori_loop(0, tt // 8, group, carry_ref[...])
    o_ref[...] = (gate_ref[...].astype(F32) * h_ref[...]).astype(o_ref.dtype)


def _rglru(xr, gate, conv_w, conv_b, w_a, b_a, w_i, b_i, lam, *, tt=256, tc=2048):
    s, wd = xr.shape
    width = conv_w.shape[0]
    hpc = tc // RNN_HEAD_DIM
    hb = tt // RNN_HALO
    row = lambda v: v.reshape(1, wd)
    cvec = lambda: pl.BlockSpec((1, tc), lambda c, i: (0, c))
    heads = lambda: pl.BlockSpec((hpc, RNN_HEAD_DIM, RNN_HEAD_DIM), lambda c, i: (c, 0, 0))
    tile = lambda: pl.BlockSpec((tt, tc), lambda c, i: (i, c))
    return pl.pallas_call(
        functools.partial(_rglru_kernel, tt=tt, tc=tc, width=width),
        grid=(wd // tc, s // tt),
        in_specs=[tile(),
                  pl.BlockSpec((RNN_HALO, tc), lambda c, i: (jnp.maximum(i * hb - 1, 0), c)),
                  tile(),
                  pl.BlockSpec((width, tc), lambda c, i: (0, c)),
                  cvec(), heads(), cvec(), heads(), cvec(), cvec()],
        out_specs=tile(),
        out_shape=jax.ShapeDtypeStruct((s, wd), BF16),
        scratch_shapes=[pltpu.VMEM((tt + RNN_HALO, tc), F32), pltpu.VMEM((tt, tc), F32),
                        pltpu.VMEM((tt, tc), F32), pltpu.VMEM((tt, tc), F32), pltpu.VMEM((1, tc), F32)],
        compiler_params=_params(("parallel", "arbitrary"), 32),
        name="rglru",
    )(xr, xr, gate, conv_w, row(conv_b), w_a.astype(BF16), row(b_a), w_i.astype(BF16), row(b_i), row(lam))


def _moe_and_ple(mixed, i, p, ln_ffn_g, ln_ffn_b, moe_w1, moe_w3, moe_w2, ple_w_proj, ple_w_gate, ple_b_gate):
    x1, x1_pairs, route = mixed
    dest, block_expert, n_used = _moe_plan(route)
    xs = _dispatch_rows(x1_pairs, dest, block_expert.shape[0])
    ys = _grouped_ffn(xs, moe_w1, moe_w3, moe_w2, i, block_expert, n_used)
    return _combine_ple(ys, dest, x1, route, ln_ffn_g[i], ln_ffn_b[i], ple_w_gate, ple_b_gate[i], p,
                        ple_w_proj, i)


def kernel(x, p, positions, ev_w_in, ev_conv_w, ev_conv_b, ev_cnorm_g, ev_cnorm_b, ev_qnorm_g, ev_w_uq, ev_kvnorm_g, ev_w_ukv, ev_w_out, od_w_in, od_conv_w, od_conv_b, od_w_a, od_b_a, od_w_i, od_b_i, od_lam, od_w_out, ln_mix_g, ln_mix_b, ln_ffn_g, ln_ffn_b, moe_w_grp, moe_b_grp, moe_w_exp, moe_b_exp, moe_w1, moe_w3, moe_w2, ple_w_proj, ple_w_gate, ple_b_gate):
    x0 = x[0]
    moe = functools.partial(_moe_and_ple, p=p, ln_ffn_g=ln_ffn_g, ln_ffn_b=ln_ffn_b, moe_w1=moe_w1,
                            moe_w3=moe_w3, moe_w2=moe_w2, ple_w_proj=ple_w_proj, ple_w_gate=ple_w_gate,
                            ple_b_gate=ple_b_gate)

    cosm, sinm = _rope_tables(positions[0])
    x0b, qn, qr, kn, kr, v = _mla_proj(x0, ev_w_in[0, :, 2 * CONV_CH:], ev_qnorm_g[0], ev_w_uq[0], ev_kvnorm_g[0],
                                       ev_w_ukv[0], cosm, sinm)
    u = _pair_proj(x0b, ev_w_in[0].T, CONV_CH, mode="glu", transposed=True)
    ub = _conv_gln(u, ev_conv_w[0], ev_conv_b[0], ev_cnorm_g[0], ev_cnorm_b[0], groups=CONV_GROUPS)
    att = _attention(qn, qr, kn, kr, v)
    mixed0 = _out_ln_route(ub, att, ev_w_out, x0, ln_mix_g[0], ln_mix_b[0],
                           moe_w_grp[0], moe_b_grp[0], moe_w_exp[0], moe_b_exp[0])
    x3, x3b = moe(mixed0, 0)

    gate, xr = _pair_proj(x3b, od_w_in.reshape(D_MODEL, -1), RNN_WIDTH, mode="rnn")
    y = _rglru(xr, gate, od_conv_w[0], od_conv_b[0], od_w_a[0], od_b_a[0], od_w_i[0], od_b_i[0], od_lam[0])
    mixed1 = _out_ln_route(y, y, od_w_out, x3, ln_mix_g[1], ln_mix_b[1],
                           moe_w_grp[1], moe_b_grp[1], moe_w_exp[1], moe_b_exp[1], col1=0, col2=1)
    x6, _ = moe(mixed1, 1)
    return x6[None]
```

```python
import functools
import math

import jax
import jax.numpy as jnp
from jax import lax
from jax.experimental import pallas as pl
from jax.experimental.pallas import tpu as pltpu

F32 = jnp.float32
BF16 = jnp.bfloat16

D_MODEL = 2048
SEQ = 8192
DEPTH = 2
CONV_CH = 1024
CONV_GROUPS = 16
CONV_WIDTH = 31
MLA_HEADS = 8
QK_NOPE = 128
QK_ROPE = 64
V_HEAD = 128
Q_RANK = 512
KV_RANK = 256
ROPE_THETA = 10000.0
RNN_WIDTH = 2048
RNN_HEADS = 16
RNN_HEAD_DIM = RNN_WIDTH // RNN_HEADS
RNN_CONV_WIDTH = 4
RG_C = 8.0
N_GROUPS = 8
EXPERTS_PER_GROUP = 8
N_EXPERTS = N_GROUPS * EXPERTS_PER_GROUP
TOP_K = 2
D_EXPERT = 512
PLE_DIM = 256
ALPHA = (2 * DEPTH) ** 0.25
LN_EPS = 1e-5
RMS_EPS = 1e-6

LANES = 128
SUBLANES = 8
CONV_HALO = 32
RNN_HALO = 8
MOE_ROWS = 256
ROUTE_LANES = 128
MIB = 2 ** 20


def _params(semantics, vmem_mib):
    return pltpu.CompilerParams(dimension_semantics=semantics, vmem_limit_bytes=vmem_mib * MIB)


def _layer_norm_rows(z, g, b):
    mu = jnp.mean(z, axis=-1, keepdims=True)
    d = z - mu
    var = jnp.mean(d * d, axis=-1, keepdims=True)
    return d * lax.rsqrt(var + LN_EPS) * g + b


def _pack_bf16_pair(a, b):
    def rounded_bits(v):
        u = lax.bitcast_convert_type(v, jnp.uint32)
        return u + jnp.uint32(0x7FFF) + ((u >> 16) & jnp.uint32(1))

    return (rounded_bits(a) & jnp.uint32(0xFFFF0000)) | (rounded_bits(b) >> 16)


def _unpack_bf16_pair(u):
    a = lax.bitcast_convert_type(u & jnp.uint32(0xFFFF0000), F32)
    b = lax.bitcast_convert_type(u << 16, F32)
    return a, b


def _store_row_tiles(ref, packed):
    n, w = packed.shape
    assert w == SUBLANES * LANES
    for j in range(SUBLANES):
        ref[pl.ds(j, n, stride=SUBLANES), :] = packed[:, j * LANES:(j + 1) * LANES]


def _load_row_tiles(ref):
    n = ref.shape[0] // SUBLANES
    return jnp.concatenate([ref[pl.ds(j, n, stride=SUBLANES), :] for j in range(SUBLANES)], axis=1)


def _split_bf16(v):
    hi = v.astype(BF16)
    lo = (v - hi.astype(F32)).astype(BF16)
    return hi, lo


def _gelu_tanh(x):
    c = math.sqrt(2.0 / math.pi)
    return 0.5 * x * (1.0 + jnp.tanh(c * (x + 0.044715 * (x * x * x))))


def _pair_proj_kernel(x_ref, wa_ref, wb_ref, *refs, mode, transposed):
    out_refs, (wa_bf, wb_bf) = refs[:-2], refs[-2:]

    @pl.when(pl.program_id(1) == 0)
    def _():
        for src, dst in ((wa_ref, wa_bf), (wb_ref, wb_bf)):
            w = src[...]
            dst[...] = (w.T if transposed else w).astype(BF16)

    x = x_ref[...]
    a = jnp.dot(x, wa_bf[...], preferred_element_type=F32)
    b = jnp.dot(x, wb_bf[...], preferred_element_type=F32)
    if mode == "glu":
        out_refs[0][...] = a * jax.nn.sigmoid(b)
    else:
        out_refs[0][...] = _gelu_tanh(a).astype(out_refs[0].dtype)
        out_refs[1][...] = b


def _pair_proj(x, w, n, *, mode, transposed=False, tm=1024, tn=512):
    m, k = x.shape
    nb = n // tn
    tile = pl.BlockSpec((tm, tn), lambda j, i: (i, j))
    if transposed:
        w_specs = [pl.BlockSpec((tn, k), lambda j, i: (j, 0)), pl.BlockSpec((tn, k), lambda j, i: (j + nb, 0))]
    else:
        w_specs = [pl.BlockSpec((k, tn), lambda j, i: (0, j)), pl.BlockSpec((k, tn), lambda j, i: (0, j + nb))]
    if mode == "glu":
        out_specs, out_shape = tile, jax.ShapeDtypeStruct((m, n), F32)
    else:
        out_specs = [tile, tile]
        out_shape = [jax.ShapeDtypeStruct((m, n), BF16), jax.ShapeDtypeStruct((m, n), F32)]
    return pl.pallas_call(
        functools.partial(_pair_proj_kernel, mode=mode, transposed=transposed),
        grid=(nb, m // tm),
        in_specs=[pl.BlockSpec((tm, k), lambda j, i: (i, 0))] + w_specs,
        out_specs=out_specs,
        out_shape=out_shape,
        scratch_shapes=[pltpu.VMEM((k, tn), BF16), pltpu.VMEM((k, tn), BF16)],
        compiler_params=_params(("parallel", "arbitrary"), 48),
        name=mode + "_proj",
    )(x, w, w)


def _conv_gln_kernel(cur_ref, prev_ref, w_ref, b_ref, g_ref, beta_ref, gm_ref, o_ref,
                     buf_ref, sh_ref, y_ref, *, tt, ch, width, rc, cc):
    i = pl.program_id(0)
    buf_ref[0:CONV_HALO, :] = jnp.where(i > 0, prev_ref[...], 0.0)
    buf_ref[CONV_HALO:, :] = cur_ref[...]
    off = CONV_HALO - (width - 1)
    sh_rows = sh_ref.shape[1]
    step = 7 * SUBLANES
    assert sh_rows % step == 0
    for b in range(1, SUBLANES):
        for c0 in range(0, ch, cc):
            for r0 in range(0, sh_rows, step):
                sh_ref[b - 1, r0:r0 + step, c0:c0 + cc] = buf_ref[r0 + b:r0 + b + step, c0:c0 + cc]
    for c0 in range(0, ch, cc):
        for r0 in range(0, tt, rc):
            acc = jnp.broadcast_to(b_ref[:, c0:c0 + cc], (rc, cc))
            for k in range(width):
                q, b = divmod(off + k, SUBLANES)
                lo = r0 + q * SUBLANES
                src = buf_ref[lo:lo + rc, c0:c0 + cc] if b == 0 else sh_ref[b - 1, lo:lo + rc, c0:c0 + cc]
                acc = acc + src * w_ref[k:k + 1, c0:c0 + cc]
            y_ref[r0:r0 + rc, c0:c0 + cc] = acc
    gm = gm_ref[...]

    def seg_mean(v):
        hi, lo = _split_bf16(v)
        return (jnp.dot(hi, gm, preferred_element_type=F32)
                + jnp.dot(lo, gm, preferred_element_type=F32))

    for c0 in range(0, ch, LANES):
        y = y_ref[:, c0:c0 + LANES]
        d = y - seg_mean(y)
        var = seg_mean(d * d)
        z = d * lax.rsqrt(var + LN_EPS) * g_ref[:, c0:c0 + LANES] + beta_ref[:, c0:c0 + LANES]
        o_ref[:, c0:c0 + LANES] = (z * jax.nn.sigmoid(z)).astype(o_ref.dtype)


def _conv_gln(u, w, b, g, beta, *, groups, tt=256):
    s, ch = u.shape
    width = w.shape[0]
    gsz = ch // groups
    assert LANES % gsz == 0 and width - 1 <= CONV_HALO and tt % CONV_HALO == 0
    wpad = jnp.zeros((CONV_HALO, ch), F32).at[:width].set(w)
    lane = jnp.arange(LANES)
    gm = jnp.where((lane[:, None] // gsz) == (lane[None, :] // gsz), 1.0 / gsz, 0.0).astype(BF16)
    hb = tt // CONV_HALO
    row = lambda v: v.reshape(1, ch)
    return pl.pallas_call(
        functools.partial(_conv_gln_kernel, tt=tt, ch=ch, width=width, rc=64, cc=128),
        grid=(s // tt,),
        in_specs=[pl.BlockSpec((tt, ch), lambda i: (i, 0)),
                  pl.BlockSpec((CONV_HALO, ch), lambda i: (jnp.maximum(i * hb - 1, 0), 0)),
                  pl.BlockSpec((CONV_HALO, ch), lambda i: (0, 0)),
                  pl.BlockSpec((1, ch), lambda i: (0, 0)),
                  pl.BlockSpec((1, ch), lambda i: (0, 0)),
                  pl.BlockSpec((1, ch), lambda i: (0, 0)),
                  pl.BlockSpec((LANES, LANES), lambda i: (0, 0))],
        out_specs=pl.BlockSpec((tt, ch), lambda i: (i, 0)),
        out_shape=jax.ShapeDtypeStruct((s, ch), BF16),
        scratch_shapes=[pltpu.VMEM((tt + CONV_HALO, ch), F32),
                        pltpu.VMEM((SUBLANES - 1, tt + CONV_HALO - SUBLANES, ch), F32),
                        pltpu.VMEM((tt, ch), F32)],
        compiler_params=_params(("parallel",), 40),
        name="conv_gln",
    )(u, u, wpad, row(b), row(g), row(beta), gm)


def _rope_table_kernel(pos_ref, invf_ref, cos_ref, sin_ref):
    ang = pos_ref[...].astype(F32) * invf_ref[...]
    cos_ref[...] = jnp.cos(ang)
    sin_ref[...] = jnp.sin(ang)


def _rope_tables(positions):
    s = positions.shape[0]
    half = QK_ROPE // 2
    per_row = LANES // half
    inv_freq = 1.0 / (ROPE_THETA ** (jnp.arange(0, QK_ROPE, 2, dtype=F32) / QK_ROPE))
    pos_rep = jnp.repeat(positions.reshape(s // per_row, per_row), half, axis=1)
    invf = jnp.tile(inv_freq, per_row).reshape(1, LANES)
    cos, sin = pl.pallas_call(
        _rope_table_kernel,
        out_shape=[jax.ShapeDtypeStruct((s // per_row, LANES), F32)] * 2,
        name="rope_table",
    )(pos_rep, invf)
    cos = cos.reshape(s, half)
    sin = sin.reshape(s, half)
    zero = jnp.zeros((s, LANES - QK_ROPE), F32)
    return jnp.concatenate([cos, cos, zero], axis=1), jnp.concatenate([-sin, sin, zero], axis=1)


def _mla_proj_kernel(x_ref, wm_ref, qg_ref, kvg_ref, wq_ref, wkv_ref, c_ref, s_ref,
                     xb_ref, qn_ref, qr_ref, kn_ref, kr_ref, v_ref, *, scale):
    nq = MLA_HEADS * QK_NOPE
    xb = x_ref[...].astype(BF16)
    xb_ref[...] = xb
    c = jnp.dot(xb, wm_ref[...], preferred_element_type=F32)
    cq = c[:, :Q_RANK]
    ckv = c[:, Q_RANK:Q_RANK + KV_RANK]
    kr_pad = c[:, Q_RANK + KV_RANK:Q_RANK + KV_RANK + LANES]
    kr_rot = c[:, Q_RANK + KV_RANK + LANES:]
    cqn = cq * lax.rsqrt(jnp.mean(cq * cq, axis=-1, keepdims=True) + RMS_EPS) * qg_ref[...]
    ckvn = ckv * lax.rsqrt(jnp.mean(ckv * ckv, axis=-1, keepdims=True) + RMS_EPS) * kvg_ref[...]
    q = jnp.dot(cqn.astype(BF16), wq_ref[...], preferred_element_type=F32)
    kv = jnp.dot(ckvn.astype(BF16), wkv_ref[...], preferred_element_type=F32)
    cosm = c_ref[...]
    sinm = s_ref[...]
    qn_ref[...] = (q[:, :nq] * scale).astype(BF16)
    for h in range(MLA_HEADS):
        lo = nq + h * LANES
        rot = q[:, lo:lo + LANES] * cosm + q[:, lo + nq:lo + nq + LANES] * sinm
        qr_ref[:, h * LANES:(h + 1) * LANES] = (rot * scale).astype(BF16)
    kn_ref[...] = kv[:, :nq].astype(BF16)
    v_ref[...] = kv[:, nq:].astype(BF16)
    kr_ref[...] = (kr_pad * cosm + kr_rot * sinm).astype(BF16)


def _rope_swap(w):
    half = QK_ROPE // 2
    return jnp.concatenate([w[..., half:], w[..., :half]], axis=-1)


def _mla_proj(x, w_in_mla, qnorm_g, w_uq, kvnorm_g, w_ukv, cosm, sinm, *, tm=256):
    s, d = x.shape
    nq = MLA_HEADS * QK_NOPE
    pad = lambda w: jnp.concatenate([w, jnp.zeros(w.shape[:-1] + (LANES - QK_ROPE,), w.dtype)], axis=-1)
    w_kr = w_in_mla[:, Q_RANK + KV_RANK:]
    wm = jnp.concatenate([w_in_mla[:, :Q_RANK + KV_RANK], pad(w_kr), pad(_rope_swap(w_kr))], axis=1).astype(BF16)
    wq3 = w_uq.reshape(Q_RANK, MLA_HEADS, QK_NOPE + QK_ROPE)
    wq_rope = wq3[:, :, QK_NOPE:]
    wq = jnp.concatenate([wq3[:, :, :QK_NOPE].reshape(Q_RANK, nq),
                          pad(wq_rope).reshape(Q_RANK, MLA_HEADS * LANES),
                          pad(_rope_swap(wq_rope)).reshape(Q_RANK, MLA_HEADS * LANES)], axis=1).astype(BF16)
    wkv3 = w_ukv.reshape(KV_RANK, MLA_HEADS, QK_NOPE + V_HEAD)
    wkv = jnp.concatenate([wkv3[:, :, :QK_NOPE].reshape(KV_RANK, nq),
                           wkv3[:, :, QK_NOPE:].reshape(KV_RANK, MLA_HEADS * V_HEAD)], axis=1).astype(BF16)
    scale = (QK_NOPE + QK_ROPE) ** -0.5 * math.log2(math.e)
    full = lambda a: pl.BlockSpec(a.shape, lambda i: (0,) * a.ndim)
    rows = lambda n: pl.BlockSpec((tm, n), lambda i: (i, 0))
    qg = qnorm_g.reshape(1, Q_RANK)
    kvg = kvnorm_g.reshape(1, KV_RANK)
    return pl.pallas_call(
        functools.partial(_mla_proj_kernel, scale=scale),
        grid=(s // tm,),
        in_specs=[rows(d), full(wm), full(qg), full(kvg), full(wq), full(wkv), rows(LANES), rows(LANES)],
        out_specs=[rows(d), rows(nq), rows(MLA_HEADS * LANES), rows(nq), rows(LANES), rows(MLA_HEADS * V_HEAD)],
        out_shape=[jax.ShapeDtypeStruct((s, d), BF16),
                   jax.ShapeDtypeStruct((s, nq), BF16),
                   jax.ShapeDtypeStruct((s, MLA_HEADS * LANES), BF16),
                   jax.ShapeDtypeStruct((s, nq), BF16),
                   jax.ShapeDtypeStruct((s, LANES), BF16),
                   jax.ShapeDtypeStruct((s, MLA_HEADS * V_HEAD), BF16)],
        compiler_params=_params(("parallel",), 48),
        name="mla_proj",
    )(x, wm, qg, kvg, wq, wkv, cosm, sinm)


def _attn_kernel(qn_ref, qr_ref, kn_ref, kr_ref, v_ref, o_ref, m_ref, acc_ref, *, tq, tk, hb):
    qi = pl.program_id(1)
    per_q = tq // tk
    m_ref[...] = jnp.full(m_ref.shape, -jnp.inf, F32)
    acc_ref[...] = jnp.zeros(acc_ref.shape, F32)
    ones = jnp.ones((tk, LANES), BF16)
    head_cols = [slice(h * LANES, (h + 1) * LANES) for h in range(hb)]
    qs = [jnp.concatenate([qn_ref[:, c], qr_ref[:, c]], axis=1) for c in head_cols]

    def chunk(j, diag):
        rows = pl.ds(pl.multiple_of(j * tk, tk), tk)
        k_rope = kr_ref[rows, :]
        q0 = 0 if diag is None else diag * tk
        for h, c in enumerate(head_cols):
            k = jnp.concatenate([kn_ref[rows, c], k_rope], axis=1)
            s = lax.dot_general(qs[h][q0:], k, (((1,), (1,)), ((), ())), preferred_element_type=F32)
            if diag is not None:
                keep = lax.broadcasted_iota(jnp.int32, s.shape, 0) >= lax.broadcasted_iota(jnp.int32, s.shape, 1)
                s = jnp.where(keep, s, -jnp.inf)
            m_old = m_ref[h, q0:, :]
            m_new = jnp.maximum(m_old, jnp.max(s, axis=-1, keepdims=True))
            alpha = jnp.exp2(m_old - m_new)
            p = jnp.exp2(s - jnp.concatenate([m_new] * (tk // LANES), axis=1))
            v_ext = jnp.concatenate([v_ref[rows, c], ones], axis=1)
            acc_ref[h, q0:, :] = (jnp.concatenate([alpha, alpha], axis=1) * acc_ref[h, q0:, :]
                                  + jnp.dot(p.astype(BF16), v_ext, preferred_element_type=F32))
            m_ref[h, q0:, :] = m_new

    def body(j, carry):
        chunk(j, None)
        return carry

    lax.fori_loop(0, qi * per_q, body, 0)
    for d in range(per_q):
        chunk(qi * per_q + d, d)
    for h, c in enumerate(head_cols):
        acc = acc_ref[h]
        o_ref[:, c] = (acc[:, :V_HEAD] / acc[:, V_HEAD:]).astype(o_ref.dtype)


def _attention(qn, qr, kn, kr, v, *, tq=1024, tk=512, hb=4):
    s = qn.shape[0]
    wide = hb * LANES
    tile = lambda: pl.BlockSpec((tq, wide), lambda h, i: (i, h))
    keys = lambda: pl.BlockSpec((s, wide), lambda h, i: (0, h), pipeline_mode=pl.Buffered(1))
    return pl.pallas_call(
        functools.partial(_attn_kernel, tq=tq, tk=tk, hb=hb),
        grid=(MLA_HEADS // hb, s // tq),
        in_specs=[tile(), tile(), keys(),
                  pl.BlockSpec((s, LANES), lambda h, i: (0, 0), pipeline_mode=pl.Buffered(1)), keys()],
        out_specs=tile(),
        out_shape=jax.ShapeDtypeStruct((s, MLA_HEADS * V_HEAD), BF16),
        scratch_shapes=[pltpu.VMEM((hb, tq, LANES), F32), pltpu.VMEM((hb, tq, 2 * V_HEAD), F32)],
        compiler_params=_params(("parallel", "parallel"), 48),
        name="mla_attention",
    )(qn, qr, kn, kr, v)


def _route(logits):
    lane = lax.broadcasted_iota(jnp.int32, logits.shape, 1).astype(F32)
    big = float(2 * ROUTE_LANES)
    neg = -jnp.inf
    gl = jnp.where(lane < N_GROUPS, logits, neg)
    gmax = jnp.max(gl, axis=-1, keepdims=True)
    gsum = jnp.sum(jnp.exp(gl - gmax), axis=-1, keepdims=True)
    g_prob = 1.0 / gsum
    g_idx = jnp.min(jnp.where(gl == gmax, lane, big), axis=-1, keepdims=True)
    lo = N_GROUPS + g_idx * EXPERTS_PER_GROUP
    el = jnp.where((lane >= lo) & (lane < lo + EXPERTS_PER_GROUP), logits, neg)
    emax = jnp.max(el, axis=-1, keepdims=True)
    esum = jnp.sum(jnp.exp(el - emax), axis=-1, keepdims=True)
    i0 = jnp.min(jnp.where(el == emax, lane, big), axis=-1, keepdims=True)
    el2 = jnp.where(lane == i0, neg, el)
    emax2 = jnp.max(el2, axis=-1, keepdims=True)
    i1 = jnp.min(jnp.where(el2 == emax2, lane, big), axis=-1, keepdims=True)
    p0 = 1.0 / esum
    p1 = jnp.exp(emax2 - emax) / esum
    g0 = g_prob * p0 / (p0 + p1)
    g1 = g_prob * p1 / (p0 + p1)
    e0 = i0 - N_GROUPS
    e1 = i1 - N_GROUPS
    return jnp.where(lane == 0, e0, jnp.where(lane == 1, e1, jnp.where(lane == 2, g0, jnp.where(lane == 3, g1, 0.0))))


def _out_ln_route_kernel(a1_ref, a2_ref, w_ref, x_ref, g_ref, b_ref, wrh_ref, br_ref,
                         x1_ref, x1p_ref, route_ref, w_bf, *, sub):
    half = a1_ref.shape[1]

    @pl.when(pl.program_id(0) == 0)
    def _():
        chunk = w_ref.shape[0] // 8
        for r0 in range(0, w_ref.shape[0], chunk):
            w_bf[r0:r0 + chunk, :] = w_ref[r0:r0 + chunk, :].astype(BF16)

    for r0 in range(0, x_ref.shape[0], sub):
        rows = slice(r0, r0 + sub)
        m = (jnp.dot(a1_ref[rows, :], w_bf[:half, :], preferred_element_type=F32)
             + jnp.dot(a2_ref[rows, :], w_bf[half:, :], preferred_element_type=F32))
        x1 = _layer_norm_rows(ALPHA * x_ref[rows, :] + m, g_ref[...], b_ref[...])
        x1_ref[rows, :] = x1
        hd = x1.shape[1] // 2
        _store_row_tiles(x1p_ref.at[pl.ds(r0 * SUBLANES, sub * SUBLANES), :], _pack_bf16_pair(x1[:, :hd], x1[:, hd:]))
        xh, xl = _split_bf16(x1)
        both = jnp.dot(xh, wrh_ref[...], preferred_element_type=F32)
        logits = (both[:, :ROUTE_LANES] + both[:, ROUTE_LANES:]
                  + jnp.dot(xl, wrh_ref[:, :ROUTE_LANES], preferred_element_type=F32) + br_ref[...])
        route_ref[rows, :] = _route(logits)


def _out_ln_route(a1, a2, w, x, ln_g, ln_b, w_grp, b_grp, w_exp, b_exp, *, col1=0, col2=0, tm=512, sub=512):
    s, d = x.shape
    half = w.shape[1] // 2
    pad = ROUTE_LANES - N_GROUPS - N_EXPERTS
    wr = jnp.concatenate([w_grp, w_exp, jnp.zeros((d, pad), F32)], axis=1)
    br = jnp.concatenate([b_grp, b_exp, jnp.zeros((pad,), F32)]).reshape(1, ROUTE_LANES)
    wr_hi = wr.astype(BF16)
    wrh = jnp.concatenate([wr_hi, (wr - wr_hi.astype(F32)).astype(BF16)], axis=1)
    full = lambda a: pl.BlockSpec(a.shape, lambda i: (0,) * a.ndim)
    rows = lambda n: pl.BlockSpec((tm, n), lambda i: (i, 0))
    g2 = ln_g.reshape(1, d)
    b2 = ln_b.reshape(1, d)
    return pl.pallas_call(
        functools.partial(_out_ln_route_kernel, sub=sub),
        grid=(s // tm,),
        in_specs=[pl.BlockSpec((tm, half), lambda i: (i, col1)), pl.BlockSpec((tm, half), lambda i: (i, col2)),
                  pl.BlockSpec((None,) + w.shape[1:], lambda i: (0, 0, 0), pipeline_mode=pl.Buffered(1)),
                  rows(d), full(g2), full(b2), full(wrh), full(br)],
        out_specs=[rows(d), pl.BlockSpec((tm * SUBLANES, LANES), lambda i: (i, 0)), rows(ROUTE_LANES)],
        out_shape=[jax.ShapeDtypeStruct((s, d), F32), jax.ShapeDtypeStruct((s * SUBLANES, LANES), jnp.uint32),
                   jax.ShapeDtypeStruct((s, ROUTE_LANES), F32)],
        scratch_shapes=[pltpu.VMEM(w.shape[1:], BF16)],
        compiler_params=_params(("arbitrary",), 58),
        name="out_ln_route",
    )(a1, a2, w, x, g2, b2, wrh, br)


def _rank_kernel(route_ref, tri_ref, rank_ref, cnt_ref, carry_ref):
    @pl.when(pl.program_id(0) == 0)
    def _():
        carry_ref[...] = jnp.zeros(carry_ref.shape, F32)

    route = route_ref[...]
    lane = lax.broadcasted_iota(jnp.int32, route.shape, 1).astype(F32)
    oh0 = jnp.where(lane == route[:, 0:1], 1.0, 0.0)
    oh1 = jnp.where(lane == route[:, 1:2], 1.0, 0.0)
    both = oh0 + oh1
    before = jnp.dot(tri_ref[...], both.astype(BF16), preferred_element_type=F32) + carry_ref[...]
    r0 = jnp.sum(before * oh0, axis=-1, keepdims=True)
    r1 = jnp.sum(before * oh1, axis=-1, keepdims=True)
    rank_ref[...] = jnp.where(lane == 0.0, r0, jnp.where(lane == 1.0, r1, 0.0))
    carry_ref[...] = carry_ref[...] + jnp.sum(both, axis=0, keepdims=True)
    cnt_ref[...] = carry_ref[...]


def _dest_kernel(route_ref, rank_ref, pstart_ref, dest_ref):
    route = route_ref[...]
    rank = rank_ref[...]
    lane = lax.broadcasted_iota(jnp.int32, route.shape, 1).astype(F32)
    ps = pstart_ref[...]
    d0 = jnp.sum(jnp.where(lane == route[:, 0:1], ps, 0.0), axis=-1, keepdims=True) + rank[:, 0:1]
    d1 = jnp.sum(jnp.where(lane == route[:, 1:2], ps, 0.0), axis=-1, keepdims=True) + rank[:, 1:2]
    dest = jnp.where(lane == 0.0, d0, jnp.where(lane == 1.0, d1, 0.0)) * float(SUBLANES)
    dest_ref[...] = dest.astype(jnp.int32)


def _moe_plan(route, *, tb=512):
    s = route.shape[0]
    tri = (jnp.arange(tb)[:, None] > jnp.arange(tb)[None, :]).astype(BF16)
    rows = pl.BlockSpec((tb, ROUTE_LANES), lambda i: (i, 0))
    one = lambda: pl.BlockSpec((1, ROUTE_LANES), lambda i: (0, 0))
    rank, cnt = pl.pallas_call(
        _rank_kernel,
        grid=(s // tb,),
        in_specs=[rows, pl.BlockSpec((tb, tb), lambda i: (0, 0))],
        out_specs=[rows, one()],
        out_shape=[jax.ShapeDtypeStruct((s, ROUTE_LANES), F32), jax.ShapeDtypeStruct((1, ROUTE_LANES), F32)],
        scratch_shapes=[pltpu.VMEM((1, ROUTE_LANES), F32)],
        compiler_params=_params(("arbitrary",), 32),
        name="moe_rank",
    )(route, tri)
    counts = cnt[0, :N_EXPERTS].astype(jnp.int32)
    pcounts = (counts + MOE_ROWS - 1) // MOE_ROWS * MOE_ROWS
    pends = jnp.cumsum(pcounts)
    pstarts = pends - pcounts
    n_blocks = s * TOP_K // MOE_ROWS + N_EXPERTS
    n_used = pends[-1] // MOE_ROWS
    blk = jnp.minimum(jnp.arange(n_blocks, dtype=jnp.int32), n_used - 1) * MOE_ROWS
    block_expert = jnp.minimum(jnp.sum(pends[None, :] <= blk[:, None], axis=1), N_EXPERTS - 1).astype(jnp.int32)
    ps = jnp.zeros((1, ROUTE_LANES), F32).at[0, :N_EXPERTS].set(pstarts.astype(F32))
    dest = pl.pallas_call(
        _dest_kernel,
        grid=(s // tb,),
        in_specs=[rows, rows, one()],
        out_specs=rows,
        out_shape=jax.ShapeDtypeStruct((s, ROUTE_LANES), jnp.int32),
        compiler_params=_params(("parallel",), 32),
        name="moe_dest",
    )(route, rank, ps)
    dest = dest[:, :TOP_K].T.reshape(TOP_K * s)
    return dest, block_expert, n_used.astype(jnp.int32).reshape(1)


ROW_UNROLL = 8


def _row_tile(ref, first_row):
    return ref.at[pl.ds(pl.multiple_of(first_row, SUBLANES), SUBLANES), :]


def _dispatch_kernel(dest_ref, x_ref, xs_hbm, sems, *, tm, s):
    t0 = pl.program_id(0) * tm

    def issue(g, carry):
        for u in range(ROW_UNROLL):
            r = g * ROW_UNROLL + u
            src = _row_tile(x_ref, r * SUBLANES)
            for k in range(TOP_K):
                pltpu.make_async_copy(src, _row_tile(xs_hbm, dest_ref[k * s + t0 + r]), sems.at[k]).start(priority=k)
        return carry

    lax.fori_loop(0, tm // ROW_UNROLL, issue, 0)
    for k in range(TOP_K):
        pltpu.make_async_copy(x_ref, xs_hbm.at[pl.ds(0, tm * SUBLANES), :], sems.at[k]).wait()


def _dispatch_rows(x, dest, n_blocks, *, tm=256):
    s = x.shape[0] // SUBLANES
    return pl.pallas_call(
        functools.partial(_dispatch_kernel, tm=tm, s=s),
        grid_spec=pltpu.PrefetchScalarGridSpec(
            num_scalar_prefetch=1,
            grid=(s // tm,),
            in_specs=[pl.BlockSpec((tm * SUBLANES, LANES), lambda i, dest: (i, 0))],
            out_specs=pl.BlockSpec(memory_space=pl.ANY),
            scratch_shapes=[pltpu.SemaphoreType.DMA((TOP_K,))]),
        out_shape=jax.ShapeDtypeStruct((n_blocks * MOE_ROWS * SUBLANES, LANES), x.dtype),
        compiler_params=_params(("arbitrary",), 32),
        name="moe_dispatch",
    )(dest, x)


FFN_K_CHUNK = 512


def _ffn_kernel(nused_ref, run_ref, first_ref, rexp_ref, nruns_ref, xs_ref, w1_hbm, w3_hbm, w2_hbm, ys_ref,
                w1_buf, w3_buf, w2_buf, sems, *, layer):
    b = pl.program_id(0)

    def weight_copies(run, slot):
        e = rexp_ref[run]
        return [pltpu.make_async_copy(w_hbm.at[layer, e], buf.at[slot], sems.at[slot, j])
                for j, (w_hbm, buf) in enumerate(((w1_hbm, w1_buf), (w3_hbm, w3_buf), (w2_hbm, w2_buf)))]

    @pl.when(b == 0)
    def _():
        for c in weight_copies(0, 0):
            c.start()

    @pl.when(b < nused_ref[0])
    def _():
        run = run_ref[b]
        slot = run % 2

        @pl.when(first_ref[b] == 1)
        def _():
            @pl.when(run + 1 < nruns_ref[0])
            def _():
                for c in weight_copies(run + 1, 1 - slot):
                    c.start()

            for c in weight_copies(run, slot):
                c.wait()

        x_lo, x_hi = _unpack_bf16_pair(_load_row_tiles(xs_ref))
        hd = x_lo.shape[1]
        x = jnp.concatenate([x_lo.astype(BF16), x_hi.astype(BF16)], axis=1)

        def up(buf):
            acc = None
            for k0 in range(0, 2 * hd, FFN_K_CHUNK):
                part = jnp.dot(x[:, k0:k0 + FFN_K_CHUNK], buf[slot, k0:k0 + FFN_K_CHUNK, :].astype(BF16),
                               preferred_element_type=F32)
                acc = part if acc is None else acc + part
            return acc

        h1 = up(w1_buf)
        h3 = up(w3_buf)
        hmid = (h1 * jax.nn.sigmoid(h1) * h3).astype(BF16)
        y = jnp.dot(hmid, w2_buf[slot].astype(BF16), preferred_element_type=F32)
        _store_row_tiles(ys_ref, _pack_bf16_pair(y[:, :hd], y[:, hd:]))


def _grouped_ffn(xs, w1, w3, w2, layer, block_expert, n_used):
    d = xs.shape[1]
    dm = w1.shape[2]
    de = w1.shape[3]
    nb = block_expert.shape[0]
    first = jnp.concatenate([jnp.ones((1,), jnp.int32), (block_expert[1:] != block_expert[:-1]).astype(jnp.int32)])
    run = jnp.cumsum(first) - 1
    n_runs = run[-1:] + 1
    hit = (run[None, :] == jnp.arange(nb, dtype=jnp.int32)[:, None]) & (first[None, :] == 1)
    run_expert = jnp.sum(jnp.where(hit, block_expert[None, :], 0), axis=1).astype(jnp.int32)
    blk = lambda b, nu, *_: (jnp.minimum(b, nu[0] - 1), 0)
    return pl.pallas_call(
        functools.partial(_ffn_kernel, layer=layer),
        grid_spec=pltpu.PrefetchScalarGridSpec(
            num_scalar_prefetch=5,
            grid=(nb,),
            in_specs=[pl.BlockSpec((MOE_ROWS * SUBLANES, d), blk),
                      pl.BlockSpec(memory_space=pl.ANY),
                      pl.BlockSpec(memory_space=pl.ANY),
                      pl.BlockSpec(memory_space=pl.ANY)],
            out_specs=pl.BlockSpec((MOE_ROWS * SUBLANES, d), blk),
            scratch_shapes=[pltpu.VMEM((2, dm, de), F32), pltpu.VMEM((2, dm, de), F32),
                            pltpu.VMEM((2, de, dm), F32), pltpu.SemaphoreType.DMA((2, 3))]),
        out_shape=jax.ShapeDtypeStruct(xs.shape, jnp.uint32),
        compiler_params=_params(("arbitrary",), 56),
        name="moe_ffn",
    )(n_used, run.astype(jnp.int32), first, run_expert, n_runs.astype(jnp.int32), xs, w1, w3, w2)


def _combine_ple_kernel(pos_ref, ys_hbm, x1_ref, route_ref, g_ref, b_ref, wg_ref, bg_ref, p_ref, wp_ref,
                        o_ref, ob_ref, ybuf_a, ybuf_b, wg_bf, wp_bf, sems, *, tm, s):
    i = pl.program_id(0)
    n = pl.num_programs(0)

    def start_row(t0, buf, buf_id, r):
        first = r * SUBLANES if isinstance(r, int) else pl.multiple_of(r * SUBLANES, SUBLANES)
        for k in range(TOP_K):
            pltpu.make_async_copy(_row_tile(ys_hbm, pos_ref[k * s + t0 + r]),
                                  buf.at[k, pl.ds(first, SUBLANES), :], sems.at[buf_id, k]).start(priority=k)

    def wait_buffer(buf, buf_id):
        span = ys_hbm.at[pl.ds(0, tm * SUBLANES), :]
        for k in range(TOP_K):
            pltpu.make_async_copy(span, buf.at[k], sems.at[buf_id, k]).wait()

    @pl.when(i == 0)
    def _():
        def issue(g, carry):
            for u in range(ROW_UNROLL):
                start_row(0, ybuf_a, 0, g * ROW_UNROLL + u)
            return carry

        lax.fori_loop(0, tm // ROW_UNROLL, issue, 0)
        chunk = wg_ref.shape[0] // 8
        for r0 in range(0, wg_ref.shape[0], chunk):
            wg_bf[r0:r0 + chunk, :] = wg_ref[r0:r0 + chunk, :].astype(BF16)
        wp_bf[...] = wp_ref[...].astype(BF16)

    def step(cur, cur_id, nxt_buf, nxt_id):
        wait_buffer(cur, cur_id)
        nxt = jnp.minimum(i + 1, n - 1) * tm
        for r in range(tm):
            start_row(nxt, nxt_buf, nxt_id, r)

        route = route_ref[...]
        f = None
        for k in range(TOP_K):
            lo, hi = _unpack_bf16_pair(_load_row_tiles(cur.at[k]))
            fk = route[:, TOP_K + k:TOP_K + k + 1] * jnp.concatenate([lo, hi], axis=1)
            f = fk if f is None else f + fk
        x2 = _layer_norm_rows(ALPHA * x1_ref[...] + f, g_ref[...], b_ref[...])
        gate = jax.nn.sigmoid(jnp.dot(x2.astype(BF16), wg_bf[...], preferred_element_type=F32) + bg_ref[...])
        emb = jnp.dot(p_ref[...].astype(BF16), wp_bf[...], preferred_element_type=F32)
        out = x2 + gate * emb
        o_ref[...] = out
        ob_ref[...] = out.astype(BF16)

        @pl.when(i == n - 1)
        def _():
            wait_buffer(nxt_buf, nxt_id)

    @pl.when(i % 2 == 0)
    def _():
        step(ybuf_a, 0, ybuf_b, 1)

    @pl.when(i % 2 == 1)
    def _():
        step(ybuf_b, 1, ybuf_a, 0)


def _combine_ple(ys, pos, x1, route, ln_g, ln_b, w_gate, b_gate, p, w_proj, layer, *, tm=256):
    s, d = x1.shape
    pd = p.shape[-1]
    rows = lambda n: pl.BlockSpec((tm, n), lambda i, pos: (i, 0))
    p_rows = pl.BlockSpec((None, None, tm, pd), lambda i, pos: (layer, 0, i, 0))
    full = lambda a: pl.BlockSpec(a.shape, lambda i, pos: (0,) * a.ndim, pipeline_mode=pl.Buffered(1))
    of_layer = lambda a: pl.BlockSpec((None,) + a.shape[1:], lambda i, pos: (layer, 0, 0),
                                      pipeline_mode=pl.Buffered(1))
    g2, b2, bg = ln_g.reshape(1, d), ln_b.reshape(1, d), b_gate.reshape(1, d)
    return pl.pallas_call(
        functools.partial(_combine_ple_kernel, tm=tm, s=s),
        grid_spec=pltpu.PrefetchScalarGridSpec(
            num_scalar_prefetch=1,
            grid=(s // tm,),
            in_specs=[pl.BlockSpec(memory_space=pl.ANY), rows(d), rows(ROUTE_LANES), full(g2), full(b2),
                      of_layer(w_gate), full(bg), p_rows, of_layer(w_proj)],
            out_specs=[rows(d), rows(d)],
            scratch_shapes=[pltpu.VMEM((TOP_K, tm * SUBLANES, LANES), jnp.uint32),
                            pltpu.VMEM((TOP_K, tm * SUBLANES, LANES), jnp.uint32),
                            pltpu.VMEM(w_gate.shape[1:], BF16), pltpu.VMEM(w_proj.shape[1:], BF16),
                            pltpu.SemaphoreType.DMA((2, TOP_K))]),
        out_shape=[jax.ShapeDtypeStruct((s, d), F32), jax.ShapeDtypeStruct((s, d), BF16)],
        compiler_params=_params(("arbitrary",), 56),
        name="moe_combine_ple",
    )(pos, ys, x1, route, g2, b2, w_gate, bg, p, w_proj)


def _rglru_kernel(cur_ref, prev_ref, gate_ref, cw_ref, cb_ref, wa_ref, ba_ref, wi_ref, bi_ref, lam_ref,
                  o_ref, buf_ref, a_ref, b_ref, h_ref, carry_ref, *, tt, tc, width):
    i = pl.program_id(1)

    @pl.when(i == 0)
    def _():
        carry_ref[...] = jnp.zeros(carry_ref.shape, F32)

    buf_ref[0:RNN_HALO, :] = jnp.where(i > 0, prev_ref[...], 0.0)
    buf_ref[RNN_HALO:, :] = cur_ref[...]
    off = RNN_HALO - (width - 1)
    z = -lam_ref[...]
    log_a_scale = (-RG_C) * (jnp.log1p(jnp.exp(-jnp.abs(z))) + jnp.maximum(z, 0.0))

    def sigmoid(v):
        return 0.5 * jnp.tanh(0.5 * v) + 0.5

    for hh in range(tc // RNN_HEAD_DIM):
        cols = slice(hh * RNN_HEAD_DIM, (hh + 1) * RNN_HEAD_DIM)
        xh = jnp.broadcast_to(cb_ref[:, cols], (tt, RNN_HEAD_DIM))
        for k in range(width):
            xh = xh + buf_ref[off + k:off + k + tt, cols] * cw_ref[k:k + 1, cols]
        xhb = xh.astype(BF16)
        r = sigmoid(jnp.dot(xhb, wa_ref[hh], preferred_element_type=F32) + ba_ref[:, cols])
        ig = sigmoid(jnp.dot(xhb, wi_ref[hh], preferred_element_type=F32) + bi_ref[:, cols])
        log_a = r * log_a_scale[:, cols]
        a = jnp.exp(log_a)
        bval = jnp.sqrt(-jnp.tanh(log_a) * (a * a + 1.0)) * (ig * xh)
        a_ref[:, cols] = a
        b_ref[:, cols] = bval

    def group(gi, h):
        base = pl.multiple_of(gi * 8, 8)
        for r in range(8):
            row = pl.ds(base + r, 1)
            h = a_ref[row, :] * h + b_ref[row, :]
            h_ref[row, :] = h
        return h

    carry_ref[...] = lax.fori_loop(0, tt // 8, group, carry_ref[...])
    o_ref[...] = (gate_ref[...].astype(F32) * h_ref[...]).astype(o_ref.dtype)


def _rglru(xr, gate, conv_w, conv_b, w_a, b_a, w_i, b_i, lam, *, tt=256, tc=2048):
    s, wd = xr.shape
    width = conv_w.shape[0]
    hpc = tc // RNN_HEAD_DIM
    hb = tt // RNN_HALO
    row = lambda v: v.reshape(1, wd)
    cvec = lambda: pl.BlockSpec((1, tc), lambda c, i: (0, c))
    heads = lambda: pl.BlockSpec((hpc, RNN_HEAD_DIM, RNN_HEAD_DIM), lambda c, i: (c, 0, 0))
    tile = lambda: pl.BlockSpec((tt, tc), lambda c, i: (i, c))
    return pl.pallas_call(
        functools.partial(_rglru_kernel, tt=tt, tc=tc, width=width),
        grid=(wd // tc, s // tt),
        in_specs=[tile(),
                  pl.BlockSpec((RNN_HALO, tc), lambda c, i: (jnp.maximum(i * hb - 1, 0), c)),
                  tile(),
                  pl.BlockSpec((width, tc), lambda c, i: (0, c)),
                  cvec(), heads(), cvec(), heads(), cvec(), cvec()],
        out_specs=tile(),
        out_shape=jax.ShapeDtypeStruct((s, wd), BF16),
        scratch_shapes=[pltpu.VMEM((tt + RNN_HALO, tc), F32), pltpu.VMEM((tt, tc), F32),
                        pltpu.VMEM((tt, tc), F32), pltpu.VMEM((tt, tc), F32), pltpu.VMEM((1, tc), F32)],
        compiler_params=_params(("parallel", "arbitrary"), 32),
        name="rglru",
    )(xr, xr, gate, conv_w, row(conv_b), w_a.astype(BF16), row(b_a), w_i.astype(BF16), row(b_i), row(lam))


def _moe_and_ple(mixed, i, p, ln_ffn_g, ln_ffn_b, moe_w1, moe_w3, moe_w2, ple_w_proj, ple_w_gate, ple_b_gate):
    x1, x1_pairs, route = mixed
    dest, block_expert, n_used = _moe_plan(route)
    xs = _dispatch_rows(x1_pairs, dest, block_expert.shape[0])
    ys = _grouped_ffn(xs, moe_w1, moe_w3, moe_w2, i, block_expert, n_used)
    return _combine_ple(ys, dest, x1, route, ln_ffn_g[i], ln_ffn_b[i], ple_w_gate, ple_b_gate[i], p,
                        ple_w_proj, i)


def kernel(x, p, positions, ev_w_in, ev_conv_w, ev_conv_b, ev_cnorm_g, ev_cnorm_b, ev_qnorm_g, ev_w_uq, ev_kvnorm_g, ev_w_ukv, ev_w_out, od_w_in, od_conv_w, od_conv_b, od_w_a, od_b_a, od_w_i, od_b_i, od_lam, od_w_out, ln_mix_g, ln_mix_b, ln_ffn_g, ln_ffn_b, moe_w_grp, moe_b_grp, moe_w_exp, moe_b_exp, moe_w1, moe_w3, moe_w2, ple_w_proj, ple_w_gate, ple_b_gate):
    x0 = x[0]
    moe = functools.partial(_moe_and_ple, p=p, ln_ffn_g=ln_ffn_g, ln_ffn_b=ln_ffn_b, moe_w1=moe_w1,
                            moe_w3=moe_w3, moe_w2=moe_w2, ple_w_proj=ple_w_proj, ple_w_gate=ple_w_gate,
                            ple_b_gate=ple_b_gate)

    cosm, sinm = _rope_tables(positions[0])
    x0b, qn, qr, kn, kr, v = _mla_proj(x0, ev_w_in[0, :, 2 * CONV_CH:], ev_qnorm_g[0], ev_w_uq[0], ev_kvnorm_g[0],
                                       ev_w_ukv[0], cosm, sinm)
    u = _pair_proj(x0b, ev_w_in[0].T, CONV_CH, mode="glu", transposed=True)
    ub = _conv_gln(u, ev_conv_w[0], ev_conv_b[0], ev_cnorm_g[0], ev_cnorm_b[0], groups=CONV_GROUPS)
    att = _attention(qn, qr, kn, kr, v)
    mixed0 = _out_ln_route(ub, att, ev_w_out, x0, ln_mix_g[0], ln_mix_b[0],
                           moe_w_grp[0], moe_b_grp[0], moe_w_exp[0], moe_b_exp[0])
    x3, x3b = moe(mixed0, 0)

    gate, xr = _pair_proj(x3b, od_w_in.reshape(D_MODEL, -1), RNN_WIDTH, mode="rnn")
    y = _rglru(xr, gate, od_conv_w[0], od_conv_b[0], od_w_a[0], od_b_a[0], od_w_i[0], od_b_i[0], od_lam[0])
    mixed1 = _out_ln_route(y, y, od_w_out, x3, ln_mix_g[1], ln_mix_b[1],
                           moe_w_grp[1], moe_b_grp[1], moe_w_exp[1], moe_b_exp[1], col1=0, col2=1)
    x6, _ = moe(mixed1, 1)
    return x6[None]
```

```python
import functools
import math

import jax
import jax.numpy as jnp
from jax import lax
from jax.experimental import pallas as pl
from jax.experimental.pallas import tpu as pltpu

F32 = jnp.float32
BF16 = jnp.bfloat16

D_MODEL = 2048
SEQ = 8192
DEPTH = 2
CONV_CH = 1024
CONV_GROUPS = 16
CONV_WIDTH = 31
MLA_HEADS = 8
QK_NOPE = 128
QK_ROPE = 64
V_HEAD = 128
Q_RANK = 512
KV_RANK = 256
ROPE_THETA = 10000.0
RNN_WIDTH = 2048
RNN_HEADS = 16
RNN_HEAD_DIM = RNN_WIDTH // RNN_HEADS
RNN_CONV_WIDTH = 4
RG_C = 8.0
N_GROUPS = 8
EXPERTS_PER_GROUP = 8
N_EXPERTS = N_GROUPS * EXPERTS_PER_GROUP
TOP_K = 2
D_EXPERT = 512
PLE_DIM = 256
ALPHA = (2 * DEPTH) ** 0.25
LN_EPS = 1e-5
RMS_EPS = 1e-6

LANES = 128
SUBLANES = 8
CONV_HALO = 32
RNN_HALO = 8
MOE_ROWS = 256
ROUTE_LANES = 128
MIB = 2 ** 20


def _params(semantics, vmem_mib):
    return pltpu.CompilerParams(dimension_semantics=semantics, vmem_limit_bytes=vmem_mib * MIB)


def _layer_norm_rows(z, g, b):
    mu = jnp.mean(z, axis=-1, keepdims=True)
    d = z - mu
    var = jnp.mean(d * d, axis=-1, keepdims=True)
    return d * lax.rsqrt(var + LN_EPS) * g + b


def _pack_bf16_pair(a, b):
    def rounded_bits(v):
        u = lax.bitcast_convert_type(v, jnp.uint32)
        return u + jnp.uint32(0x7FFF) + ((u >> 16) & jnp.uint32(1))

    return (rounded_bits(a) & jnp.uint32(0xFFFF0000)) | (rounded_bits(b) >> 16)


def _unpack_bf16_pair(u):
    a = lax.bitcast_convert_type(u & jnp.uint32(0xFFFF0000), F32)
    b = lax.bitcast_convert_type(u << 16, F32)
    return a, b


def _store_row_tiles(ref, packed):
    n, w = packed.shape
    assert w == SUBLANES * LANES
    for j in range(SUBLANES):
        ref[pl.ds(j, n, stride=SUBLANES), :] = packed[:, j * LANES:(j + 1) * LANES]


def _load_row_tiles(ref):
    n = ref.shape[0] // SUBLANES
    return jnp.concatenate([ref[pl.ds(j, n, stride=SUBLANES), :] for j in range(SUBLANES)], axis=1)


def _split_bf16(v):
    hi = v.astype(BF16)
    lo = (v - hi.astype(F32)).astype(BF16)
    return hi, lo


def _gelu_tanh(x):
    c = math.sqrt(2.0 / math.pi)
    return 0.5 * x * (1.0 + jnp.tanh(c * (x + 0.044715 * (x * x * x))))


def _pair_proj_kernel(x_ref, wa_ref, wb_ref, *refs, mode, transposed):
    out_refs, (wa_bf, wb_bf) = refs[:-2], refs[-2:]

    @pl.when(pl.program_id(1) == 0)
    def _():
        for src, dst in ((wa_ref, wa_bf), (wb_ref, wb_bf)):
            w = src[...]
            dst[...] = (w.T if transposed else w).astype(BF16)

    x = x_ref[...]
    a = jnp.dot(x, wa_bf[...], preferred_element_type=F32)
    b = jnp.dot(x, wb_bf[...], preferred_element_type=F32)
    if mode == "glu":
        out_refs[0][...] = a * jax.nn.sigmoid(b)
    else:
        out_refs[0][...] = _gelu_tanh(a).astype(out_refs[0].dtype)
        out_refs[1][...] = b


def _pair_proj(x, w, n, *, mode, transposed=False, tm=1024, tn=512):
    m, k = x.shape
    nb = n // tn
    tile = pl.BlockSpec((tm, tn), lambda j, i: (i, j))
    if transposed:
        w_specs = [pl.BlockSpec((tn, k), lambda j, i: (j, 0)), pl.BlockSpec((tn, k), lambda j, i: (j + nb, 0))]
    else:
        w_specs = [pl.BlockSpec((k, tn), lambda j, i: (0, j)), pl.BlockSpec((k, tn), lambda j, i: (0, j + nb))]
    if mode == "glu":
        out_specs, out_shape = tile, jax.ShapeDtypeStruct((m, n), F32)
    else:
        out_specs = [tile, tile]
        out_shape = [jax.ShapeDtypeStruct((m, n), BF16), jax.ShapeDtypeStruct((m, n), F32)]
    return pl.pallas_call(
        functools.partial(_pair_proj_kernel, mode=mode, transposed=transposed),
        grid=(nb, m // tm),
        in_specs=[pl.BlockSpec((tm, k), lambda j, i: (i, 0))] + w_specs,
        out_specs=out_specs,
        out_shape=out_shape,
        scratch_shapes=[pltpu.VMEM((k, tn), BF16), pltpu.VMEM((k, tn), BF16)],
        compiler_params=_params(("parallel", "arbitrary"), 48),
        name=mode + "_proj",
    )(x, w, w)


def _conv_gln_kernel(cur_ref, prev_ref, w_ref, b_ref, g_ref, beta_ref, gm_ref, o_ref,
                     buf_ref, sh_ref, y_ref, *, tt, ch, width, rc, cc):
    i = pl.program_id(0)
    buf_ref[0:CONV_HALO, :] = jnp.where(i > 0, prev_ref[...], 0.0)
    buf_ref[CONV_HALO:, :] = cur_ref[...]
    off = CONV_HALO - (width - 1)
    sh_rows = sh_ref.shape[1]
    step = 7 * SUBLANES
    assert sh_rows % step == 0
    for b in range(1, SUBLANES):
        for c0 in range(0, ch, cc):
            for r0 in range(0, sh_rows, step):
                sh_ref[b - 1, r0:r0 + step, c0:c0 + cc] = buf_ref[r0 + b:r0 + b + step, c0:c0 + cc]
    for c0 in range(0, ch, cc):
        for r0 in range(0, tt, rc):
            acc = jnp.broadcast_to(b_ref[:, c0:c0 + cc], (rc, cc))
            for k in range(width):
                q, b = divmod(off + k, SUBLANES)
                lo = r0 + q * SUBLANES
                src = buf_ref[lo:lo + rc, c0:c0 + cc] if b == 0 else sh_ref[b - 1, lo:lo + rc, c0:c0 + cc]
                acc = acc + src * w_ref[k:k + 1, c0:c0 + cc]
            y_ref[r0:r0 + rc, c0:c0 + cc] = acc
    gm = gm_ref[...]

    def seg_mean(v):
        hi, lo = _split_bf16(v)
        return (jnp.dot(hi, gm, preferred_element_type=F32)
                + jnp.dot(lo, gm, preferred_element_type=F32))

    for c0 in range(0, ch, LANES):
        y = y_ref[:, c0:c0 + LANES]
        d = y - seg_mean(y)
        var = seg_mean(d * d)
        z = d * lax.rsqrt(var + LN_EPS) * g_ref[:, c0:c0 + LANES] + beta_ref[:, c0:c0 + LANES]
        o_ref[:, c0:c0 + LANES] = (z * jax.nn.sigmoid(z)).astype(o_ref.dtype)


def _conv_gln(u, w, b, g, beta, *, groups, tt=256):
    s, ch = u.shape
    width = w.shape[0]
    gsz = ch // groups
    assert LANES % gsz == 0 and width - 1 <= CONV_HALO and tt % CONV_HALO == 0
    wpad = jnp.zeros((CONV_HALO, ch), F32).at[:width].set(w)
    lane = jnp.arange(LANES)
    gm = jnp.where((lane[:, None] // gsz) == (lane[None, :] // gsz), 1.0 / gsz, 0.0).astype(BF16)
    hb = tt // CONV_HALO
    row = lambda v: v.reshape(1, ch)
    return pl.pallas_call(
        functools.partial(_conv_gln_kernel, tt=tt, ch=ch, width=width, rc=64, cc=128),
        grid=(s // tt,),
        in_specs=[pl.BlockSpec((tt, ch), lambda i: (i, 0)),
                  pl.BlockSpec((CONV_HALO, ch), lambda i: (jnp.maximum(i * hb - 1, 0), 0)),
                  pl.BlockSpec((CONV_HALO, ch), lambda i: (0, 0)),
                  pl.BlockSpec((1, ch), lambda i: (0, 0)),
                  pl.BlockSpec((1, ch), lambda i: (0, 0)),
                  pl.BlockSpec((1, ch), lambda i: (0, 0)),
                  pl.BlockSpec((LANES, LANES), lambda i: (0, 0))],
        out_specs=pl.BlockSpec((tt, ch), lambda i: (i, 0)),
        out_shape=jax.ShapeDtypeStruct((s, ch), BF16),
        scratch_shapes=[pltpu.VMEM((tt + CONV_HALO, ch), F32),
                        pltpu.VMEM((SUBLANES - 1, tt + CONV_HALO - SUBLANES, ch), F32),
                        pltpu.VMEM((tt, ch), F32)],
        compiler_params=_params(("parallel",), 40),
        name="conv_gln",
    )(u, u, wpad, row(b), row(g), row(beta), gm)


def _rope_table_kernel(pos_ref, invf_ref, cos_ref, sin_ref):
    ang = pos_ref[...].astype(F32) * invf_ref[...]
    cos_ref[...] = jnp.cos(ang)
    sin_ref[...] = jnp.sin(ang)


def _rope_tables(positions):
    s = positions.shape[0]
    half = QK_ROPE // 2
    per_row = LANES // half
    inv_freq = 1.0 / (ROPE_THETA ** (jnp.arange(0, QK_ROPE, 2, dtype=F32) / QK_ROPE))
    pos_rep = jnp.repeat(positions.reshape(s // per_row, per_row), half, axis=1)
    invf = jnp.tile(inv_freq, per_row).reshape(1, LANES)
    cos, sin = pl.pallas_call(
        _rope_table_kernel,
        out_shape=[jax.ShapeDtypeStruct((s // per_row, LANES), F32)] * 2,
        name="rope_table",
    )(pos_rep, invf)
    cos = cos.reshape(s, half)
    sin = sin.reshape(s, half)
    zero = jnp.zeros((s, LANES - QK_ROPE), F32)
    return jnp.concatenate([cos, cos, zero], axis=1), jnp.concatenate([-sin, sin, zero], axis=1)


def _mla_proj_kernel(x_ref, wm_ref, qg_ref, kvg_ref, wq_ref, wkv_ref, c_ref, s_ref,
                     xb_ref, qn_ref, qr_ref, kn_ref, kr_ref, v_ref, *, scale):
    nq = MLA_HEADS * QK_NOPE
    xb = x_ref[...].astype(BF16)
    xb_ref[...] = xb
    c = jnp.dot(xb, wm_ref[...], preferred_element_type=F32)
    cq = c[:, :Q_RANK]
    ckv = c[:, Q_RANK:Q_RANK + KV_RANK]
    kr_pad = c[:, Q_RANK + KV_RANK:Q_RANK + KV_RANK + LANES]
    kr_rot = c[:, Q_RANK + KV_RANK + LANES:]
    cqn = cq * lax.rsqrt(jnp.mean(cq * cq, axis=-1, keepdims=True) + RMS_EPS) * qg_ref[...]
    ckvn = ckv * lax.rsqrt(jnp.mean(ckv * ckv, axis=-1, keepdims=True) + RMS_EPS) * kvg_ref[...]
    q = jnp.dot(cqn.astype(BF16), wq_ref[...], preferred_element_type=F32)
    kv = jnp.dot(ckvn.astype(BF16), wkv_ref[...], preferred_element_type=F32)
    cosm = c_ref[...]
    sinm = s_ref[...]
    qn_ref[...] = (q[:, :nq] * scale).astype(BF16)
    for h in range(MLA_HEADS):
        lo = nq + h * LANES
        rot = q[:, lo:lo + LANES] * cosm + q[:, lo + nq:lo + nq + LANES] * sinm
        qr_ref[:, h * LANES:(h + 1) * LANES] = (rot * scale).astype(BF16)
    kn_ref[...] = kv[:, :nq].astype(BF16)
    v_ref[...] = kv[:, nq:].astype(BF16)
    kr_ref[...] = (kr_pad * cosm + kr_rot * sinm).astype(BF16)


def _rope_swap(w):
    half = QK_ROPE // 2
    return jnp.concatenate([w[..., half:], w[..., :half]], axis=-1)


def _mla_proj(x, w_in_mla, qnorm_g, w_uq, kvnorm_g, w_ukv, cosm, sinm, *, tm=512):
    s, d = x.shape
    nq = MLA_HEADS * QK_NOPE
    pad = lambda w: jnp.concatenate([w, jnp.zeros(w.shape[:-1] + (LANES - QK_ROPE,), w.dtype)], axis=-1)
    w_kr = w_in_mla[:, Q_RANK + KV_RANK:]
    wm = jnp.concatenate([w_in_mla[:, :Q_RANK + KV_RANK], pad(w_kr), pad(_rope_swap(w_kr))], axis=1).astype(BF16)
    wq3 = w_uq.reshape(Q_RANK, MLA_HEADS, QK_NOPE + QK_ROPE)
    wq_rope = wq3[:, :, QK_NOPE:]
    wq = jnp.concatenate([wq3[:, :, :QK_NOPE].reshape(Q_RANK, nq),
                          pad(wq_rope).reshape(Q_RANK, MLA_HEADS * LANES),
                          pad(_rope_swap(wq_rope)).reshape(Q_RANK, MLA_HEADS * LANES)], axis=1).astype(BF16)
    wkv3 = w_ukv.reshape(KV_RANK, MLA_HEADS, QK_NOPE + V_HEAD)
    wkv = jnp.concatenate([wkv3[:, :, :QK_NOPE].reshape(KV_RANK, nq),
                           wkv3[:, :, QK_NOPE:].reshape(KV_RANK, MLA_HEADS * V_HEAD)], axis=1).astype(BF16)
    scale = (QK_NOPE + QK_ROPE) ** -0.5 * math.log2(math.e)
    full = lambda a: pl.BlockSpec(a.shape, lambda i: (0,) * a.ndim)
    rows = lambda n: pl.BlockSpec((tm, n), lambda i: (i, 0))
    qg = qnorm_g.reshape(1, Q_RANK)
    kvg = kvnorm_g.reshape(1, KV_RANK)
    return pl.pallas_call(
        functools.partial(_mla_proj_kernel, scale=scale),
        grid=(s // tm,),
        in_specs=[rows(d), full(wm), full(qg), full(kvg), full(wq), full(wkv), rows(LANES), rows(LANES)],
        out_specs=[rows(d), rows(nq), rows(MLA_HEADS * LANES), rows(nq), rows(LANES), rows(MLA_HEADS * V_HEAD)],
        out_shape=[jax.ShapeDtypeStruct((s, d), BF16),
                   jax.ShapeDtypeStruct((s, nq), BF16),
                   jax.ShapeDtypeStruct((s, MLA_HEADS * LANES), BF16),
                   jax.ShapeDtypeStruct((s, nq), BF16),
                   jax.ShapeDtypeStruct((s, LANES), BF16),
                   jax.ShapeDtypeStruct((s, MLA_HEADS * V_HEAD), BF16)],
        compiler_params=_params(("parallel",), 48),
        name="mla_proj",
    )(x, wm, qg, kvg, wq, wkv, cosm, sinm)


def _attn_kernel(qn_ref, qr_ref, kn_ref, kr_ref, v_ref, o_ref, m_ref, acc_ref, *, tq, tk, hb):
    qi = pl.program_id(1)
    per_q = tq // tk
    m_ref[...] = jnp.full(m_ref.shape, -jnp.inf, F32)
    acc_ref[...] = jnp.zeros(acc_ref.shape, F32)
    ones = jnp.ones((tk, LANES), BF16)
    head_cols = [slice(h * LANES, (h + 1) * LANES) for h in range(hb)]
    qs = [jnp.concatenate([qn_ref[:, c], qr_ref[:, c]], axis=1) for c in head_cols]

    def chunk(j, diag):
        rows = pl.ds(pl.multiple_of(j * tk, tk), tk)
        k_rope = kr_ref[rows, :]
        q0 = 0 if diag is None else diag * tk
        for h, c in enumerate(head_cols):
            k = jnp.concatenate([kn_ref[rows, c], k_rope], axis=1)
            s = lax.dot_general(qs[h][q0:], k, (((1,), (1,)), ((), ())), preferred_element_type=F32)
            if diag is not None:
                keep = lax.broadcasted_iota(jnp.int32, s.shape, 0) >= lax.broadcasted_iota(jnp.int32, s.shape, 1)
                s = jnp.where(keep, s, -jnp.inf)
            m_old = m_ref[h, q0:, :]
            m_new = jnp.maximum(m_old, jnp.max(s, axis=-1, keepdims=True))
            alpha = jnp.exp2(m_old - m_new)
            p = jnp.exp2(s - jnp.concatenate([m_new] * (tk // LANES), axis=1))
            v_ext = jnp.concatenate([v_ref[rows, c], ones], axis=1)
            acc_ref[h, q0:, :] = (jnp.concatenate([alpha, alpha], axis=1) * acc_ref[h, q0:, :]
                                  + jnp.dot(p.astype(BF16), v_ext, preferred_element_type=F32))
            m_ref[h, q0:, :] = m_new

    def body(j, carry):
        chunk(j, None)
        return carry

    lax.fori_loop(0, qi * per_q, body, 0)
    for d in range(per_q):
        chunk(qi * per_q + d, d)
    for h, c in enumerate(head_cols):
        acc = acc_ref[h]
        o_ref[:, c] = (acc[:, :V_HEAD] / acc[:, V_HEAD:]).astype(o_ref.dtype)


def _attention(qn, qr, kn, kr, v, *, tq=1024, tk=512, hb=4):
    s = qn.shape[0]
    wide = hb * LANES
    tile = lambda: pl.BlockSpec((tq, wide), lambda h, i: (i, h))
    keys = lambda: pl.BlockSpec((s, wide), lambda h, i: (0, h), pipeline_mode=pl.Buffered(1))
    return pl.pallas_call(
        functools.partial(_attn_kernel, tq=tq, tk=tk, hb=hb),
        grid=(MLA_HEADS // hb, s // tq),
        in_specs=[tile(), tile(), keys(),
                  pl.BlockSpec((s, LANES), lambda h, i: (0, 0), pipeline_mode=pl.Buffered(1)), keys()],
        out_specs=tile(),
        out_shape=jax.ShapeDtypeStruct((s, MLA_HEADS * V_HEAD), BF16),
        scratch_shapes=[pltpu.VMEM((hb, tq, LANES), F32), pltpu.VMEM((hb, tq, 2 * V_HEAD), F32)],
        compiler_params=_params(("parallel", "parallel"), 48),
        name="mla_attention",
    )(qn, qr, kn, kr, v)


def _route(logits):
    lane = lax.broadcasted_iota(jnp.int32, logits.shape, 1).astype(F32)
    big = float(2 * ROUTE_LANES)
    neg = -jnp.inf
    gl = jnp.where(lane < N_GROUPS, logits, neg)
    gmax = jnp.max(gl, axis=-1, keepdims=True)
    gsum = jnp.sum(jnp.exp(gl - gmax), axis=-1, keepdims=True)
    g_prob = 1.0 / gsum
    g_idx = jnp.min(jnp.where(gl == gmax, lane, big), axis=-1, keepdims=True)
    lo = N_GROUPS + g_idx * EXPERTS_PER_GROUP
    el = jnp.where((lane >= lo) & (lane < lo + EXPERTS_PER_GROUP), logits, neg)
    emax = jnp.max(el, axis=-1, keepdims=True)
    esum = jnp.sum(jnp.exp(el - emax), axis=-1, keepdims=True)
    i0 = jnp.min(jnp.where(el == emax, lane, big), axis=-1, keepdims=True)
    el2 = jnp.where(lane == i0, neg, el)
    emax2 = jnp.max(el2, axis=-1, keepdims=True)
    i1 = jnp.min(jnp.where(el2 == emax2, lane, big), axis=-1, keepdims=True)
    p0 = 1.0 / esum
    p1 = jnp.exp(emax2 - emax) / esum
    g0 = g_prob * p0 / (p0 + p1)
    g1 = g_prob * p1 / (p0 + p1)
    e0 = i0 - N_GROUPS
    e1 = i1 - N_GROUPS
    return jnp.where(lane == 0, e0, jnp.where(lane == 1, e1, jnp.where(lane == 2, g0, jnp.where(lane == 3, g1, 0.0))))


def _out_ln_route_kernel(a1_ref, a2_ref, w_ref, x_ref, g_ref, b_ref, wrh_ref, br_ref,
                         x1_ref, x1p_ref, route_ref, w_bf, *, sub):
    half = a1_ref.shape[1]

    @pl.when(pl.program_id(0) == 0)
    def _():
        chunk = w_ref.shape[0] // 8
        for r0 in range(0, w_ref.shape[0], chunk):
            w_bf[r0:r0 + chunk, :] = w_ref[r0:r0 + chunk, :].astype(BF16)

    for r0 in range(0, x_ref.shape[0], sub):
        rows = slice(r0, r0 + sub)
        m = (jnp.dot(a1_ref[rows, :], w_bf[:half, :], preferred_element_type=F32)
             + jnp.dot(a2_ref[rows, :], w_bf[half:, :], preferred_element_type=F32))
        x1 = _layer_norm_rows(ALPHA * x_ref[rows, :] + m, g_ref[...], b_ref[...])
        x1_ref[rows, :] = x1
        hd = x1.shape[1] // 2
        _store_row_tiles(x1p_ref.at[pl.ds(r0 * SUBLANES, sub * SUBLANES), :], _pack_bf16_pair(x1[:, :hd], x1[:, hd:]))
        xh, xl = _split_bf16(x1)
        both = jnp.dot(xh, wrh_ref[...], preferred_element_type=F32)
        logits = (both[:, :ROUTE_LANES] + both[:, ROUTE_LANES:]
                  + jnp.dot(xl, wrh_ref[:, :ROUTE_LANES], preferred_element_type=F32) + br_ref[...])
        route_ref[rows, :] = _route(logits)


def _out_ln_route(a1, a2, w, x, ln_g, ln_b, w_grp, b_grp, w_exp, b_exp, *, col1=0, col2=0, tm=512, sub=512):
    s, d = x.shape
    half = w.shape[1] // 2
    pad = ROUTE_LANES - N_GROUPS - N_EXPERTS
    wr = jnp.concatenate([w_grp, w_exp, jnp.zeros((d, pad), F32)], axis=1)
    br = jnp.concatenate([b_grp, b_exp, jnp.zeros((pad,), F32)]).reshape(1, ROUTE_LANES)
    wr_hi = wr.astype(BF16)
    wrh = jnp.concatenate([wr_hi, (wr - wr_hi.astype(F32)).astype(BF16)], axis=1)
    full = lambda a: pl.BlockSpec(a.shape, lambda i: (0,) * a.ndim)
    rows = lambda n: pl.BlockSpec((tm, n), lambda i: (i, 0))
    g2 = ln_g.reshape(1, d)
    b2 = ln_b.reshape(1, d)
    return pl.pallas_call(
        functools.partial(_out_ln_route_kernel, sub=sub),
        grid=(s // tm,),
        in_specs=[pl.BlockSpec((tm, half), lambda i: (i, col1)), pl.BlockSpec((tm, half), lambda i: (i, col2)),
                  pl.BlockSpec((None,) + w.shape[1:], lambda i: (0, 0, 0), pipeline_mode=pl.Buffered(1)),
                  rows(d), full(g2), full(b2), full(wrh), full(br)],
        out_specs=[rows(d), pl.BlockSpec((tm * SUBLANES, LANES), lambda i: (i, 0)), rows(ROUTE_LANES)],
        out_shape=[jax.ShapeDtypeStruct((s, d), F32), jax.ShapeDtypeStruct((s * SUBLANES, LANES), jnp.uint32),
                   jax.ShapeDtypeStruct((s, ROUTE_LANES), F32)],
        scratch_shapes=[pltpu.VMEM(w.shape[1:], BF16)],
        compiler_params=_params(("arbitrary",), 58),
        name="out_ln_route",
    )(a1, a2, w, x, g2, b2, wrh, br)


def _rank_kernel(route_ref, tri_ref, rank_ref, cnt_ref, carry_ref):
    @pl.when(pl.program_id(0) == 0)
    def _():
        carry_ref[...] = jnp.zeros(carry_ref.shape, F32)

    route = route_ref[...]
    lane = lax.broadcasted_iota(jnp.int32, route.shape, 1).astype(F32)
    oh0 = jnp.where(lane == route[:, 0:1], 1.0, 0.0)
    oh1 = jnp.where(lane == route[:, 1:2], 1.0, 0.0)
    both = oh0 + oh1
    before = jnp.dot(tri_ref[...], both.astype(BF16), preferred_element_type=F32) + carry_ref[...]
    r0 = jnp.sum(before * oh0, axis=-1, keepdims=True)
    r1 = jnp.sum(before * oh1, axis=-1, keepdims=True)
    rank_ref[...] = jnp.where(lane == 0.0, r0, jnp.where(lane == 1.0, r1, 0.0))
    carry_ref[...] = carry_ref[...] + jnp.sum(both, axis=0, keepdims=True)
    cnt_ref[...] = carry_ref[...]


def _dest_kernel(route_ref, rank_ref, pstart_ref, dest_ref):
    route = route_ref[...]
    rank = rank_ref[...]
    lane = lax.broadcasted_iota(jnp.int32, route.shape, 1).astype(F32)
    ps = pstart_ref[...]
    d0 = jnp.sum(jnp.where(lane == route[:, 0:1], ps, 0.0), axis=-1, keepdims=True) + rank[:, 0:1]
    d1 = jnp.sum(jnp.where(lane == route[:, 1:2], ps, 0.0), axis=-1, keepdims=True) + rank[:, 1:2]
    dest = jnp.where(lane == 0.0, d0, jnp.where(lane == 1.0, d1, 0.0)) * float(SUBLANES)
    dest_ref[...] = dest.astype(jnp.int32)


def _moe_plan(route, *, tb=512):
    s = route.shape[0]
    tri = (jnp.arange(tb)[:, None] > jnp.arange(tb)[None, :]).astype(BF16)
    rows = pl.BlockSpec((tb, ROUTE_LANES), lambda i: (i, 0))
    one = lambda: pl.BlockSpec((1, ROUTE_LANES), lambda i: (0, 0))
    rank, cnt = pl.pallas_call(
        _rank_kernel,
        grid=(s // tb,),
        in_specs=[rows, pl.BlockSpec((tb, tb), lambda i: (0, 0))],
        out_specs=[rows, one()],
        out_shape=[jax.ShapeDtypeStruct((s, ROUTE_LANES), F32), jax.ShapeDtypeStruct((1, ROUTE_LANES), F32)],
        scratch_shapes=[pltpu.VMEM((1, ROUTE_LANES), F32)],
        compiler_params=_params(("arbitrary",), 32),
        name="moe_rank",
    )(route, tri)
    counts = cnt[0, :N_EXPERTS].astype(jnp.int32)
    pcounts = (counts + MOE_ROWS - 1) // MOE_ROWS * MOE_ROWS
    pends = jnp.cumsum(pcounts)
    pstarts = pends - pcounts
    n_blocks = s * TOP_K // MOE_ROWS + N_EXPERTS
    n_used = pends[-1] // MOE_ROWS
    blk = jnp.minimum(jnp.arange(n_blocks, dtype=jnp.int32), n_used - 1) * MOE_ROWS
    block_expert = jnp.minimum(jnp.sum(pends[None, :] <= blk[:, None], axis=1), N_EXPERTS - 1).astype(jnp.int32)
    ps = jnp.zeros((1, ROUTE_LANES), F32).at[0, :N_EXPERTS].set(pstarts.astype(F32))
    dest = pl.pallas_call(
        _dest_kernel,
        grid=(s // tb,),
        in_specs=[rows, rows, one()],
        out_specs=rows,
        out_shape=jax.ShapeDtypeStruct((s, ROUTE_LANES), jnp.int32),
        compiler_params=_params(("parallel",), 32),
        name="moe_dest",
    )(route, rank, ps)
    dest = dest[:, :TOP_K].T.reshape(TOP_K * s)
    return dest, block_expert, n_used.astype(jnp.int32).reshape(1)


ROW_UNROLL = 8


def _row_tile(ref, first_row):
    return ref.at[pl.ds(pl.multiple_of(first_row, SUBLANES), SUBLANES), :]


def _dispatch_kernel(dest_ref, x_ref, xs_hbm, sems, *, tm, s):
    t0 = pl.program_id(0) * tm

    def issue(g, carry):
        for u in range(ROW_UNROLL):
            r = g * ROW_UNROLL + u
            src = _row_tile(x_ref, r * SUBLANES)
            for k in range(TOP_K):
                pltpu.make_async_copy(src, _row_tile(xs_hbm, dest_ref[k * s + t0 + r]), sems.at[k]).start(priority=k)
        return carry

    lax.fori_loop(0, tm // ROW_UNROLL, issue, 0)
    for k in range(TOP_K):
        pltpu.make_async_copy(x_ref, xs_hbm.at[pl.ds(0, tm * SUBLANES), :], sems.at[k]).wait()


def _dispatch_rows(x, dest, n_blocks, *, tm=256):
    s = x.shape[0] // SUBLANES
    return pl.pallas_call(
        functools.partial(_dispatch_kernel, tm=tm, s=s),
        grid_spec=pltpu.PrefetchScalarGridSpec(
            num_scalar_prefetch=1,
            grid=(s // tm,),
            in_specs=[pl.BlockSpec((tm * SUBLANES, LANES), lambda i, dest: (i, 0))],
            out_specs=pl.BlockSpec(memory_space=pl.ANY),
            scratch_shapes=[pltpu.SemaphoreType.DMA((TOP_K,))]),
        out_shape=jax.ShapeDtypeStruct((n_blocks * MOE_ROWS * SUBLANES, LANES), x.dtype),
        compiler_params=_params(("arbitrary",), 32),
        name="moe_dispatch",
    )(dest, x)


FFN_K_CHUNK = 512


def _ffn_kernel(nused_ref, run_ref, first_ref, rexp_ref, nruns_ref, xs_ref, w1_hbm, w3_hbm, w2_hbm, ys_ref,
                w1_buf, w3_buf, w2_buf, sems, *, layer):
    b = pl.program_id(0)

    def weight_copies(run, slot):
        e = rexp_ref[run]
        return [pltpu.make_async_copy(w_hbm.at[layer, e], buf.at[slot], sems.at[slot, j])
                for j, (w_hbm, buf) in enumerate(((w1_hbm, w1_buf), (w3_hbm, w3_buf), (w2_hbm, w2_buf)))]

    @pl.when(b == 0)
    def _():
        for c in weight_copies(0, 0):
            c.start()

    @pl.when(b < nused_ref[0])
    def _():
        run = run_ref[b]
        slot = run % 2

        @pl.when(first_ref[b] == 1)
        def _():
            @pl.when(run + 1 < nruns_ref[0])
            def _():
                for c in weight_copies(run + 1, 1 - slot):
                    c.start()

            for c in weight_copies(run, slot):
                c.wait()

        x_lo, x_hi = _unpack_bf16_pair(_load_row_tiles(xs_ref))
        hd = x_lo.shape[1]
        x = jnp.concatenate([x_lo.astype(BF16), x_hi.astype(BF16)], axis=1)

        def up(buf):
            acc = None
            for k0 in range(0, 2 * hd, FFN_K_CHUNK):
                part = jnp.dot(x[:, k0:k0 + FFN_K_CHUNK], buf[slot, k0:k0 + FFN_K_CHUNK, :].astype(BF16),
                               preferred_element_type=F32)
                acc = part if acc is None else acc + part
            return acc

        h1 = up(w1_buf)
        h3 = up(w3_buf)
        hmid = (h1 * jax.nn.sigmoid(h1) * h3).astype(BF16)
        y = jnp.dot(hmid, w2_buf[slot].astype(BF16), preferred_element_type=F32)
        _store_row_tiles(ys_ref, _pack_bf16_pair(y[:, :hd], y[:, hd:]))


def _grouped_ffn(xs, w1, w3, w2, layer, block_expert, n_used):
    d = xs.shape[1]
    dm = w1.shape[2]
    de = w1.shape[3]
    nb = block_expert.shape[0]
    first = jnp.concatenate([jnp.ones((1,), jnp.int32), (block_expert[1:] != block_expert[:-1]).astype(jnp.int32)])
    run = jnp.cumsum(first) - 1
    n_runs = run[-1:] + 1
    hit = (run[None, :] == jnp.arange(nb, dtype=jnp.int32)[:, None]) & (first[None, :] == 1)
    run_expert = jnp.sum(jnp.where(hit, block_expert[None, :], 0), axis=1).astype(jnp.int32)
    blk = lambda b, nu, *_: (jnp.minimum(b, nu[0] - 1), 0)
    return pl.pallas_call(
        functools.partial(_ffn_kernel, layer=layer),
        grid_spec=pltpu.PrefetchScalarGridSpec(
            num_scalar_prefetch=5,
            grid=(nb,),
            in_specs=[pl.BlockSpec((MOE_ROWS * SUBLANES, d), blk),
                      pl.BlockSpec(memory_space=pl.ANY),
                      pl.BlockSpec(memory_space=pl.ANY),
                      pl.BlockSpec(memory_space=pl.ANY)],
            out_specs=pl.BlockSpec((MOE_ROWS * SUBLANES, d), blk),
            scratch_shapes=[pltpu.VMEM((2, dm, de), F32), pltpu.VMEM((2, dm, de), F32),
                            pltpu.VMEM((2, de, dm), F32), pltpu.SemaphoreType.DMA((2, 3))]),
        out_shape=jax.ShapeDtypeStruct(xs.shape, jnp.uint32),
        compiler_params=_params(("arbitrary",), 56),
        name="moe_ffn",
    )(n_used, run.astype(jnp.int32), first, run_expert, n_runs.astype(jnp.int32), xs, w1, w3, w2)


def _combine_ple_kernel(pos_ref, ys_hbm, x1_ref, route_ref, g_ref, b_ref, wg_ref, bg_ref, p_ref, wp_ref,
                        o_ref, ob_ref, ybuf_a, ybuf_b, wg_bf, wp_bf, sems, *, tm, s):
    i = pl.program_id(0)
    n = pl.num_programs(0)

    def start_row(t0, buf, buf_id, r):
        first = r * SUBLANES if isinstance(r, int) else pl.multiple_of(r * SUBLANES, SUBLANES)
        for k in range(TOP_K):
            pltpu.make_async_copy(_row_tile(ys_hbm, pos_ref[k * s + t0 + r]),
                                  buf.at[k, pl.ds(first, SUBLANES), :], sems.at[buf_id, k]).start(priority=k)

    def wait_buffer(buf, buf_id):
        span = ys_hbm.at[pl.ds(0, tm * SUBLANES), :]
        for k in range(TOP_K):
            pltpu.make_async_copy(span, buf.at[k], sems.at[buf_id, k]).wait()

    @pl.when(i == 0)
    def _():
        def issue(g, carry):
            for u in range(ROW_UNROLL):
                start_row(0, ybuf_a, 0, g * ROW_UNROLL + u)
            return carry

        lax.fori_loop(0, tm // ROW_UNROLL, issue, 0)
        chunk = wg_ref.shape[0] // 8
        for r0 in range(0, wg_ref.shape[0], chunk):
            wg_bf[r0:r0 + chunk, :] = wg_ref[r0:r0 + chunk, :].astype(BF16)
        wp_bf[...] = wp_ref[...].astype(BF16)

    def step(cur, cur_id, nxt_buf, nxt_id):
        wait_buffer(cur, cur_id)
        nxt = jnp.minimum(i + 1, n - 1) * tm
        for r in range(tm):
            start_row(nxt, nxt_buf, nxt_id, r)

        route = route_ref[...]
        f = None
        for k in range(TOP_K):
            lo, hi = _unpack_bf16_pair(_load_row_tiles(cur.at[k]))
            fk = route[:, TOP_K + k:TOP_K + k + 1] * jnp.concatenate([lo, hi], axis=1)
            f = fk if f is None else f + fk
        x2 = _layer_norm_rows(ALPHA * x1_ref[...] + f, g_ref[...], b_ref[...])
        gate = jax.nn.sigmoid(jnp.dot(x2.astype(BF16), wg_bf[...], preferred_element_type=F32) + bg_ref[...])
        emb = jnp.dot(p_ref[...].astype(BF16), wp_bf[...], preferred_element_type=F32)
        out = x2 + gate * emb
        o_ref[...] = out
        ob_ref[...] = out.astype(BF16)

        @pl.when(i == n - 1)
        def _():
            wait_buffer(nxt_buf, nxt_id)

    @pl.when(i % 2 == 0)
    def _():
        step(ybuf_a, 0, ybuf_b, 1)

    @pl.when(i % 2 == 1)
    def _():
        step(ybuf_b, 1, ybuf_a, 0)


def _combine_ple(ys, pos, x1, route, ln_g, ln_b, w_gate, b_gate, p, w_proj, layer, *, tm=256):
    s, d = x1.shape
    pd = p.shape[-1]
    rows = lambda n: pl.BlockSpec((tm, n), lambda i, pos: (i, 0))
    p_rows = pl.BlockSpec((None, None, tm, pd), lambda i, pos: (layer, 0, i, 0))
    full = lambda a: pl.BlockSpec(a.shape, lambda i, pos: (0,) * a.ndim, pipeline_mode=pl.Buffered(1))
    of_layer = lambda a: pl.BlockSpec((None,) + a.shape[1:], lambda i, pos: (layer, 0, 0),
                                      pipeline_mode=pl.Buffered(1))
    g2, b2, bg = ln_g.reshape(1, d), ln_b.reshape(1, d), b_gate.reshape(1, d)
    return pl.pallas_call(
        functools.partial(_combine_ple_kernel, tm=tm, s=s),
        grid_spec=pltpu.PrefetchScalarGridSpec(
            num_scalar_prefetch=1,
            grid=(s // tm,),
            in_specs=[pl.BlockSpec(memory_space=pl.ANY), rows(d), rows(ROUTE_LANES), full(g2), full(b2),
                      of_layer(w_gate), full(bg), p_rows, of_layer(w_proj)],
            out_specs=[rows(d), rows(d)],
            scratch_shapes=[pltpu.VMEM((TOP_K, tm * SUBLANES, LANES), jnp.uint32),
                            pltpu.VMEM((TOP_K, tm * SUBLANES, LANES), jnp.uint32),
                            pltpu.VMEM(w_gate.shape[1:], BF16), pltpu.VMEM(w_proj.shape[1:], BF16),
                            pltpu.SemaphoreType.DMA((2, TOP_K))]),
        out_shape=[jax.ShapeDtypeStruct((s, d), F32), jax.ShapeDtypeStruct((s, d), BF16)],
        compiler_params=_params(("arbitrary",), 56),
        name="moe_combine_ple",
    )(pos, ys, x1, route, g2, b2, w_gate, bg, p, w_proj)


def _rglru_kernel(cur_ref, prev_ref, gate_ref, cw_ref, cb_ref, wa_ref, ba_ref, wi_ref, bi_ref, lam_ref,
                  o_ref, buf_ref, a_ref, b_ref, h_ref, carry_ref, *, tt, tc, width):
    i = pl.program_id(1)

    @pl.when(i == 0)
    def _():
        carry_ref[...] = jnp.zeros(carry_ref.shape, F32)

    buf_ref[0:RNN_HALO, :] = jnp.where(i > 0, prev_ref[...], 0.0)
    buf_ref[RNN_HALO:, :] = cur_ref[...]
    off = RNN_HALO - (width - 1)
    z = -lam_ref[...]
    log_a_scale = (-RG_C) * (jnp.log1p(jnp.exp(-jnp.abs(z))) + jnp.maximum(z, 0.0))

    def sigmoid(v):
        return 0.5 * jnp.tanh(0.5 * v) + 0.5

    for hh in range(tc // RNN_HEAD_DIM):
        cols = slice(hh * RNN_HEAD_DIM, (hh + 1) * RNN_HEAD_DIM)
        xh = jnp.broadcast_to(cb_ref[:, cols], (tt, RNN_HEAD_DIM))
        for k in range(width):
            xh = xh + buf_ref[off + k:off + k + tt, cols] * cw_ref[k:k + 1, cols]
        xhb = xh.astype(BF16)
        r = sigmoid(jnp.dot(xhb, wa_ref[hh], preferred_element_type=F32) + ba_ref[:, cols])
        ig = sigmoid(jnp.dot(xhb, wi_ref[hh], preferred_element_type=F32) + bi_ref[:, cols])
        log_a = r * log_a_scale[:, cols]
        a = jnp.exp(log_a)
        bval = jnp.sqrt(-jnp.tanh(log_a) * (a * a + 1.0)) * (ig * xh)
        a_ref[:, cols] = a
        b_ref[:, cols] = bval

    def group(gi, h):
        base = pl.multiple_of(gi * 8, 8)
        for r in range(8):
            row = pl.ds(base + r, 1)
            h = a_ref[row, :] * h + b_ref[row, :]
            h_ref[row, :] = h
        return h

    carry_ref[...] = lax.fori_loop(0, tt // 8, group, carry_ref[...])
    o_ref[...] = (gate_ref[...].astype(F32) * h_ref[...]).astype(o_ref.dtype)


def _rglru(xr, gate, conv_w, conv_b, w_a, b_a, w_i, b_i, lam, *, tt=256, tc=2048):
    s, wd = xr.shape
    width = conv_w.shape[0]
    hpc = tc // RNN_HEAD_DIM
    hb = tt // RNN_HALO
    row = lambda v: v.reshape(1, wd)
    cvec = lambda: pl.BlockSpec((1, tc), lambda c, i: (0, c))
    heads = lambda: pl.BlockSpec((hpc, RNN_HEAD_DIM, RNN_HEAD_DIM), lambda c, i: (c, 0, 0))
    tile = lambda: pl.BlockSpec((tt, tc), lambda c, i: (i, c))
    return pl.pallas_call(
        functools.partial(_rglru_kernel, tt=tt, tc=tc, width=width),
        grid=(wd // tc, s // tt),
        in_specs=[tile(),
                  pl.BlockSpec((RNN_HALO, tc), lambda c, i: (jnp.maximum(i * hb - 1, 0), c)),
                  tile(),
                  pl.BlockSpec((width, tc), lambda c, i: (0, c)),
                  cvec(), heads(), cvec(), heads(), cvec(), cvec()],
        out_specs=tile(),
        out_shape=jax.ShapeDtypeStruct((s, wd), BF16),
        scratch_shapes=[pltpu.VMEM((tt + RNN_HALO, tc), F32), pltpu.VMEM((tt, tc), F32),
                        pltpu.VMEM((tt, tc), F32), pltpu.VMEM((tt, tc), F32), pltpu.VMEM((1, tc), F32)],
        compiler_params=_params(("parallel", "arbitrary"), 32),
        name="rglru",
    )(xr, xr, gate, conv_w, row(conv_b), w_a.astype(BF16), row(b_a), w_i.astype(BF16), row(b_i), row(lam))


def _moe_and_ple(mixed, i, p, ln_ffn_g, ln_ffn_b, moe_w1, moe_w3, moe_w2, ple_w_proj, ple_w_gate, ple_b_gate):
    x1, x1_pairs, route = mixed
    dest, block_expert, n_used = _moe_plan(route)
    xs = _dispatch_rows(x1_pairs, dest, block_expert.shape[0])
    ys = _grouped_ffn(xs, moe_w1, moe_w3, moe_w2, i, block_expert, n_used)
    return _combine_ple(ys, dest, x1, route, ln_ffn_g[i], ln_ffn_b[i], ple_w_gate, ple_b_gate[i], p,
                        ple_w_proj, i)


def kernel(x, p, positions, ev_w_in, ev_conv_w, ev_conv_b, ev_cnorm_g, ev_cnorm_b, ev_qnorm_g, ev_w_uq, ev_kvnorm_g, ev_w_ukv, ev_w_out, od_w_in, od_conv_w, od_conv_b, od_w_a, od_b_a, od_w_i, od_b_i, od_lam, od_w_out, ln_mix_g, ln_mix_b, ln_ffn_g, ln_ffn_b, moe_w_grp, moe_b_grp, moe_w_exp, moe_b_exp, moe_w1, moe_w3, moe_w2, ple_w_proj, ple_w_gate, ple_b_gate):
    x0 = x[0]
    moe = functools.partial(_moe_and_ple, p=p, ln_ffn_g=ln_ffn_g, ln_ffn_b=ln_ffn_b, moe_w1=moe_w1,
                            moe_w3=moe_w3, moe_w2=moe_w2, ple_w_proj=ple_w_proj, ple_w_gate=ple_w_gate,
                            ple_b_gate=ple_b_gate)

    cosm, sinm = _rope_tables(positions[0])
    x0b, qn, qr, kn, kr, v = _mla_proj(x0, ev_w_in[0, :, 2 * CONV_CH:], ev_qnorm_g[0], ev_w_uq[0], ev_kvnorm_g[0],
                                       ev_w_ukv[0], cosm, sinm)
    u = _pair_proj(x0b, ev_w_in[0].T, CONV_CH, mode="glu", transposed=True)
    ub = _conv_gln(u, ev_conv_w[0], ev_conv_b[0], ev_cnorm_g[0], ev_cnorm_b[0], groups=CONV_GROUPS)
    att = _attention(qn, qr, kn, kr, v)
    mixed0 = _out_ln_route(ub, att, ev_w_out, x0, ln_mix_g[0], ln_mix_b[0],
                           moe_w_grp[0], moe_b_grp[0], moe_w_exp[0], moe_b_exp[0])
    x3, x3b = moe(mixed0, 0)

    gate, xr = _pair_proj(x3b, od_w_in.reshape(D_MODEL, -1), RNN_WIDTH, mode="rnn")
    y = _rglru(xr, gate, od_conv_w[0], od_conv_b[0], od_w_a[0], od_b_a[0], od_w_i[0], od_b_i[0], od_lam[0])
    mixed1 = _out_ln_route(y, y, od_w_out, x3, ln_mix_g[1], ln_mix_b[1],
                           moe_w_grp[1], moe_b_grp[1], moe_w_exp[1], moe_b_exp[1], col1=0, col2=1)
    x6, _ = moe(mixed1, 1)
    return x6[None]
```
